```python
import math
import jax, jax.numpy as jnp
from jax import lax
import numpy as np

D_MODEL = 1024
BATCH = 8
SEQ = 2048
DEPTH = 1
DEC_BATCH = 128
DEC_SEQ = 4
PAST_LEN = 16384
PAGE_SIZE = 128

GM_WIDTH = D_MODEL
GM_GROUPS = 8
GM_GROUP_DIM = GM_WIDTH // GM_GROUPS
GM_CHUNK = 128
DN_HEADS = 8
DN_HEAD_DIM = D_MODEL // DN_HEADS
DN_WIDTH = DN_HEADS * DN_HEAD_DIM
DN_CONV = 4
DN_CHUNK = 64
D_FF = 2816
IN_COLS = 2 * GM_WIDTH + 4 * DN_WIDTH + 2 * DN_HEADS + 2 * D_MODEL
ALPHA = (2.0 * DEPTH) ** 0.25
BETA_INIT = (8.0 * DEPTH) ** -0.25
LN_EPS = 1e-5
RMS_EPS = 1e-6

kernel_name = 'hybrid_gmlp_gdn_macaron_deepnorm_step'


def layer_norm(x, g, b):
    x32 = x.astype(jnp.float32)
    mu = jnp.mean(x32, axis=-1, keepdims=True)
    var = jnp.mean(jnp.square(x32 - mu), axis=-1, keepdims=True)
    return ((x32 - mu) * lax.rsqrt(var + LN_EPS) * g + b).astype(x.dtype)


def rms_norm(x, w):
    x32 = x.astype(jnp.float32)
    return x32 * lax.rsqrt(jnp.mean(jnp.square(x32), axis=-1, keepdims=True) + RMS_EPS) * w


def l2_normalize(x):
    return x * lax.rsqrt(jnp.sum(jnp.square(x), axis=-1, keepdims=True) + RMS_EPS)


def swiglu_ffn(x, w_up, w_down):
    a, gt = jnp.split(x @ w_up, 2, axis=-1)
    return (jax.nn.silu(a) * gt) @ w_down


def chunk_spatial_mix(v, w_s, b_s):
    B, T, _ = v.shape
    L = min(GM_CHUNK, T)
    vc = v.reshape(B, T // L, L, GM_GROUPS, GM_GROUP_DIM)
    w = jnp.tril(w_s[:, :L, :L])
    mixed = jnp.einsum('gts,bcsgd->bctgd', w, vc) + b_s[:, :L].T[None, None, :, :, None]
    return mixed.reshape(B, T, GM_WIDTH)


def _to_chunks(a, n, L, pad):
    a = jnp.pad(a, [(0, 0), (0, pad)] + [(0, 0)] * (a.ndim - 2))
    a = a.reshape((a.shape[0], n, L) + a.shape[2:])
    return jnp.swapaxes(jnp.swapaxes(a, 0, 1), 2, 3)


def gated_delta_rule(q, k, v, g_log, beta, s0):
    B, T, H, _ = q.shape
    dv = v.shape[-1]
    L = min(DN_CHUNK, T)
    n = -(-T // L)
    pad = n * L - T
    q, k, v = _to_chunks(q, n, L, pad), _to_chunks(k, n, L, pad), _to_chunks(v, n, L, pad)
    g_log, beta = _to_chunks(g_log, n, L, pad), _to_chunks(beta, n, L, pad)
    G = jnp.cumsum(g_log, axis=-1)
    causal = jnp.tril(jnp.ones((L, L), dtype=bool))
    strict = jnp.tril(jnp.ones((L, L), dtype=bool), -1)
    diff = G[..., :, None] - G[..., None, :]
    decay = jnp.where(causal, jnp.exp(jnp.where(causal, diff, 0.0)), 0.0)
    kb = k * beta[..., None]
    a_mat = jnp.where(strict, jnp.einsum('nbhid,nbhjd->nbhij', kb, k) * decay, 0.0)
    eye = jnp.eye(L, dtype=q.dtype)
    t_inv = lax.linalg.triangular_solve(eye + a_mat, jnp.broadcast_to(eye, a_mat.shape),
                                        left_side=True, lower=True, unit_diagonal=True)
    u_vals = jnp.einsum('nbhij,nbhjv->nbhiv', t_inv, v * beta[..., None])
    w_vals = jnp.einsum('nbhij,nbhjd->nbhid', t_inv, kb * jnp.exp(G)[..., None])
    qk = jnp.where(causal, jnp.einsum('nbhid,nbhjd->nbhij', q, k) * decay, 0.0)

    def step(S, xs):
        q_c, k_c, u_c, w_c, g_c, qk_c = xs
        v_new = u_c - jnp.einsum('bhld,bhdv->bhlv', w_c, S)
        o = (jnp.einsum('bhld,bhdv->bhlv', q_c * jnp.exp(g_c)[..., None], S)
             + jnp.einsum('bhij,bhjv->bhiv', qk_c, v_new))
        g_last = g_c[..., -1:]
        S = (S * jnp.exp(g_last)[..., None]
             + jnp.einsum('bhld,bhlv->bhdv', k_c * jnp.exp(g_last - g_c)[..., None], v_new))
        return S, o

    s_final, o = lax.scan(step, s0, (q, k, u_vals, w_vals, G, qk))
    o = jnp.swapaxes(jnp.swapaxes(o, 2, 3), 0, 1).reshape(B, n * L, H, dv)[:, :T]
    return o, s_final


def token_mixing(h, conv_state, ssm_state, p):
    B, T, _ = h.shape
    proj = h @ p['w_in'] + p['b_in']
    offs = np.cumsum([GM_WIDTH, GM_WIDTH, 3 * DN_WIDTH, DN_WIDTH, DN_HEADS, DN_HEADS]).tolist()
    u, v, qkv, z, beta_logit, decay_logit, gates = jnp.split(proj, offs, axis=-1)

    u = jax.nn.gelu(u)
    v = layer_norm(jax.nn.gelu(v), p['gm_v_g'], p['gm_v_b'])
    y_a = u * chunk_spatial_mix(v, p['gm_w_s'], p['gm_b_s']).astype(u.dtype)

    xc = jnp.concatenate([conv_state.astype(qkv.dtype), qkv], axis=1)
    conv_new = xc[:, T:]
    w_c = p['dn_conv_w']
    acc = xc[:, 0:T] * w_c[0]
    for i in range(1, DN_CONV):
        acc = acc + xc[:, i:i + T] * w_c[i]
    qkv_c = jax.nn.silu(acc).astype(jnp.float32)
    q, k, vd = jnp.split(qkv_c, 3, axis=-1)
    q = l2_normalize(q.reshape(B, T, DN_HEADS, DN_HEAD_DIM)) * (DN_HEAD_DIM ** -0.5)
    k = l2_normalize(k.reshape(B, T, DN_HEADS, DN_HEAD_DIM))
    vd = vd.reshape(B, T, DN_HEADS, DN_HEAD_DIM)
    beta = jax.nn.sigmoid(beta_logit.astype(jnp.float32))
    g_log = (-jnp.exp(p['dn_a_log'].astype(jnp.float32))
             * jax.nn.softplus(decay_logit.astype(jnp.float32) + p['dn_dt_bias'].astype(jnp.float32)))
    o, ssm_new = gated_delta_rule(q, k, vd, g_log, beta, ssm_state.astype(jnp.float32))
    o = rms_norm(o, p['dn_norm_w']) * jax.nn.silu(z.astype(jnp.float32).reshape(B, T, DN_HEADS, DN_HEAD_DIM))
    y_b = o.reshape(B, T, DN_WIDTH).astype(h.dtype)

    gate_a, gate_b = jnp.split(gates, 2, axis=-1)
    merged = (jax.nn.sigmoid(gate_a) * (y_a @ p['w_branch_a'])
              + jax.nn.sigmoid(gate_b) * (y_b @ p['w_branch_b']))
    return merged @ p['w_out'], v, conv_new, ssm_new.astype(ssm_state.dtype)


def trunk_layer(x, conv_state, ssm_state, p):
    x = layer_norm(ALPHA * x + 0.5 * swiglu_ffn(x, p['ffn1_w_up'], p['ffn1_w_down']), p['ln1_g'], p['ln1_b'])
    mix, v_rows, conv_new, ssm_new = token_mixing(x, conv_state, ssm_state, p)
    x = layer_norm(ALPHA * x + mix, p['ln2_g'], p['ln2_b'])
    x = layer_norm(ALPHA * x + 0.5 * swiglu_ffn(x, p['ffn2_w_up'], p['ffn2_w_down']), p['ln3_g'], p['ln3_b'])
    return x, v_rows, conv_new, ssm_new


def setup_inputs(seed: int = 0) -> dict:
    key = jax.random.key(seed)
    ks = jax.random.split(key, 32)
    f32 = jnp.float32

    def nrm(k, shape, scale):
        return jax.random.normal(k, shape, f32) * scale

    def gain(k, n):
        return 1.0 + 0.02 * jax.random.normal(k, (DEPTH, n), f32)

    def bias(k, n):
        return 0.02 * jax.random.normal(k, (DEPTH, n), f32)

    dt = jnp.exp(jax.random.uniform(ks[17], (DEPTH, DN_HEADS), f32, math.log(1e-3), math.log(1e-1)))
    return {
        'x_prompt': nrm(ks[0], (BATCH, SEQ, D_MODEL), 1.0),
        'x_sample': nrm(ks[1], (DEC_BATCH, DEC_SEQ, D_MODEL), 1.0),
        'state_conv': nrm(ks[2], (DEPTH, DEC_BATCH, DN_CONV - 1, 3 * DN_WIDTH), 1.0),
        'state_ssm': nrm(ks[3], (DEPTH, DEC_BATCH, DN_HEADS, DN_HEAD_DIM, DN_HEAD_DIM), 0.1),
        'ffn1_w_up': nrm(ks[4], (DEPTH, D_MODEL, 2 * D_FF), D_MODEL ** -0.5),
        'ffn1_w_down': nrm(ks[5], (DEPTH, D_FF, D_MODEL), BETA_INIT * D_FF ** -0.5),
        'ln1_g': gain(ks[6], D_MODEL),
        'ln1_b': bias(ks[7], D_MODEL),
        'w_in': nrm(ks[8], (DEPTH, D_MODEL, IN_COLS), D_MODEL ** -0.5),
        'b_in': bias(ks[9], IN_COLS),
        'gm_v_g': gain(ks[10], GM_WIDTH),
        'gm_v_b': bias(ks[11], GM_WIDTH),
        'gm_w_s': nrm(ks[12], (DEPTH, GM_GROUPS, GM_CHUNK, GM_CHUNK), GM_CHUNK ** -0.5),
        'gm_b_s': 1.0 + 0.02 * jax.random.normal(ks[13], (DEPTH, GM_GROUPS, GM_CHUNK), f32),
        'dn_conv_w': nrm(ks[14], (DEPTH, DN_CONV, 3 * DN_WIDTH), DN_CONV ** -0.5),
        'dn_a_log': jnp.log(jax.random.uniform(ks[15], (DEPTH, DN_HEADS), f32, 1.0, 16.0)),
        'dn_dt_bias': dt + jnp.log(-jnp.expm1(-dt)),
        'dn_norm_w': gain(ks[16], DN_HEAD_DIM),
        'w_branch_a': nrm(ks[18], (DEPTH, GM_WIDTH, D_MODEL), GM_WIDTH ** -0.5),
        'w_branch_b': nrm(ks[19], (DEPTH, DN_WIDTH, D_MODEL), DN_WIDTH ** -0.5),
        'w_out': nrm(ks[20], (DEPTH, D_MODEL, D_MODEL), BETA_INIT * D_MODEL ** -0.5),
        'ln2_g': gain(ks[21], D_MODEL),
        'ln2_b': bias(ks[22], D_MODEL),
        'ffn2_w_up': nrm(ks[23], (DEPTH, D_MODEL, 2 * D_FF), D_MODEL ** -0.5),
        'ffn2_w_down': nrm(ks[24], (DEPTH, D_FF, D_MODEL), BETA_INIT * D_FF ** -0.5),
        'ln3_g': gain(ks[25], D_MODEL),
        'ln3_b': bias(ks[26], D_MODEL),
    }


def reference(x_prompt, x_sample, state_conv, state_ssm, ffn1_w_up, ffn1_w_down, ln1_g, ln1_b,
              w_in, b_in, gm_v_g, gm_v_b, gm_w_s, gm_b_s, dn_conv_w, dn_a_log, dn_dt_bias, dn_norm_w,
              w_branch_a, w_branch_b, w_out, ln2_g, ln2_b, ffn2_w_up, ffn2_w_down, ln3_g, ln3_b):
    y_p, y_s = x_prompt, x_sample
    conv_p, ssm_p, conv_s, ssm_s, v_s = [], [], [], [], []
    for l in range(DEPTH):
        p = {
            'ffn1_w_up': ffn1_w_up[l], 'ffn1_w_down': ffn1_w_down[l], 'ln1_g': ln1_g[l], 'ln1_b': ln1_b[l],
            'w_in': w_in[l], 'b_in': b_in[l], 'gm_v_g': gm_v_g[l], 'gm_v_b': gm_v_b[l],
            'gm_w_s': gm_w_s[l], 'gm_b_s': gm_b_s[l], 'dn_conv_w': dn_conv_w[l], 'dn_a_log': dn_a_log[l],
            'dn_dt_bias': dn_dt_bias[l], 'dn_norm_w': dn_norm_w[l], 'w_branch_a': w_branch_a[l],
            'w_branch_b': w_branch_b[l], 'w_out': w_out[l], 'ln2_g': ln2_g[l], 'ln2_b': ln2_b[l],
            'ffn2_w_up': ffn2_w_up[l], 'ffn2_w_down': ffn2_w_down[l], 'ln3_g': ln3_g[l], 'ln3_b': ln3_b[l],
        }
        zero_conv = jnp.zeros((x_prompt.shape[0], DN_CONV - 1, 3 * DN_WIDTH), x_prompt.dtype)
        zero_ssm = jnp.zeros((x_prompt.shape[0], DN_HEADS, DN_HEAD_DIM, DN_HEAD_DIM), state_ssm.dtype)
        y_p, _, c_p, s_p = trunk_layer(y_p, zero_conv, zero_ssm, p)
        y_s, v_rows, c_s, s_s = trunk_layer(y_s, state_conv[l], state_ssm[l], p)
        conv_p.append(c_p)
        ssm_p.append(s_p)
        conv_s.append(c_s)
        ssm_s.append(s_s)
        v_s.append(v_rows)
    return (y_p, y_s, jnp.stack(conv_p), jnp.stack(ssm_p), jnp.stack(conv_s), jnp.stack(ssm_s), jnp.stack(v_s))
```

```python
import functools
import math

import jax
import jax.numpy as jnp
from jax import lax
from jax.experimental import pallas as pl
from jax.experimental.pallas import tpu as pltpu

F32 = jnp.float32
BF16 = jnp.bfloat16

D_MODEL = 1024
D_FF = 2816
HEADS = 8
HEAD_DIM = 128
GROUPS = 8
GROUP_DIM = 128
GM_CHUNK = 128
DN_CHUNK = 64
DN_CONV = 4
QKV = 3 * D_MODEL
MAIN_COLS = 6 * D_MODEL
LN_EPS = 1e-5
RMS_EPS = 1e-6

SUB = 128
ROW_TILE = 8
VMEM_LIMIT = 56 * 1024 * 1024


def _sigmoid(x):
    return 1.0 / (1.0 + jnp.exp(-x))


def _silu(x):
    return x * _sigmoid(x)


def _gelu_tanh(x):
    c = math.sqrt(2.0 / math.pi)
    return x * (0.5 * (1.0 + jnp.tanh(c * (x + 0.044715 * (x * x * x)))))


def _softplus(x):
    return jnp.maximum(x, 0.0) + jnp.log(1.0 + jnp.exp(-jnp.abs(x)))


def _layer_norm(y, g, b):
    mu = jnp.mean(y, axis=-1, keepdims=True)
    yc = y - mu
    var = jnp.mean(yc * yc, axis=-1, keepdims=True)
    return yc * lax.rsqrt(var + LN_EPS) * g + b


def _dot(a, b):
    return jnp.dot(a.astype(BF16), b.astype(BF16), preferred_element_type=F32)


def _dot_nt(a, b):
    return lax.dot_general(a.astype(BF16), b.astype(BF16), (((1,), (1,)), ((), ())),
                           preferred_element_type=F32)


def _dot_exact_lhs(m01, x):
    hi = x.astype(BF16)
    r1 = x - hi.astype(F32)
    mid = r1.astype(BF16)
    lo = (r1 - mid.astype(F32)).astype(BF16)
    m = m01.astype(BF16)
    return (jnp.dot(m, hi, preferred_element_type=F32)
            + jnp.dot(m, mid, preferred_element_type=F32)
            + jnp.dot(m, lo, preferred_element_type=F32))


def _block_masks(n, blk):
    row = lax.broadcasted_iota(jnp.int32, (n, n), 0)
    col = lax.broadcasted_iota(jnp.int32, (n, n), 1)
    same = (row // blk) == (col // blk)
    return same & (row >= col), same & (row > col), row == col


def _inv_unit_lower(a, eye, n_iter):
    b = -a
    p = eye + b
    for _ in range(n_iter):
        b = _dot(b, b)
        p = p + _dot(p, b)
    return p


def _dn_intra(q, k, v, beta, g_col, g_row, masks, n_iter):
    causal, strict, diag = masks
    decay = jnp.where(causal, jnp.exp(jnp.where(causal, g_col - g_row, 0.0)), 0.0)
    kb = k * beta
    kk = _dot_nt(kb, k)
    qk = _dot_nt(q, k) * decay
    a = jnp.where(strict, kk * decay, 0.0)
    eye = jnp.where(diag, 1.0, 0.0).astype(F32)
    t_inv = _inv_unit_lower(a, eye, n_iter)
    e_g = jnp.exp(g_col)
    uw = _dot(t_inv, jnp.concatenate([v * beta, kb * e_g], axis=1))
    return uw[:, :HEAD_DIM], uw[:, HEAD_DIM:], qk, q * e_g


def _gated_rms(o, norm_w, z_act):
    return o * lax.rsqrt(jnp.mean(o * o, axis=-1, keepdims=True) + RMS_EPS) * norm_w * z_act


def _l2n(x):
    return x * lax.rsqrt(jnp.sum(x * x, axis=-1, keepdims=True) + RMS_EPS)


def _ffn_ln_kernel(x_ref, wa_ref, wg_ref, wd_ref, g_ref, b_ref, o_ref, *, alpha, n_split):
    x = x_ref[...]
    xb = x.astype(BF16)
    cw = D_FF // n_split
    acc = None
    for c in range(n_split):
        a = jnp.dot(xb, wa_ref[:, c * cw:(c + 1) * cw], preferred_element_type=F32)
        gt = jnp.dot(xb, wg_ref[:, c * cw:(c + 1) * cw], preferred_element_type=F32)
        h = (_silu(a) * gt).astype(BF16)
        f = jnp.dot(h, wd_ref[c * cw:(c + 1) * cw, :], preferred_element_type=F32)
        acc = f if acc is None else acc + f
    o_ref[...] = _layer_norm(alpha * x + 0.5 * acc, g_ref[...], b_ref[...])


def _const_spec(shape):
    nd = len(shape)
    return pl.BlockSpec(shape, lambda *_: (0,) * nd, pipeline_mode=pl.Buffered(1))


def _ffn_ln(x2d, wa, wg, wd, g, b, alpha, tm):
    n = x2d.shape[0]
    assert n % tm == 0
    return pl.pallas_call(
        functools.partial(_ffn_ln_kernel, alpha=alpha, n_split=2),
        grid=(n // tm,),
        in_specs=[pl.BlockSpec((tm, D_MODEL), lambda i: (i, 0)),
                  _const_spec(wa.shape), _const_spec(wg.shape), _const_spec(wd.shape),
                  _const_spec(g.shape), _const_spec(b.shape)],
        out_specs=pl.BlockSpec((tm, D_MODEL), lambda i: (i, 0)),
        out_shape=jax.ShapeDtypeStruct((n, D_MODEL), F32),
        compiler_params=pltpu.CompilerParams(dimension_semantics=("arbitrary",),
                                             vmem_limit_bytes=VMEM_LIMIT),
        name="ffn_ln",
    )(x2d, wa, wg, wd, g, b)


def _branch_gates_and_z(hb, w_main, b_main):
    z = jnp.dot(hb, w_main[:, 5 * D_MODEL:6 * D_MODEL], preferred_element_type=F32) \
        + b_main[:, 5 * D_MODEL:6 * D_MODEL]
    return _silu(z)


def _beta_and_logdecay(hb, w_bd, b_bd, alog, dtb):
    bd = jnp.dot(hb, w_bd[...], preferred_element_type=F32) + b_bd[...]
    beta = _sigmoid(bd[:, :128])
    g = -jnp.exp(alog[...]) * _softplus(bd[:, 128:] + dtb[...])
    return beta, g


def _merge_out_ln(x, hb, a_part, yb, w_gates, b_gates, wb_ref, wo_ref, ln_g, ln_b, alpha):
    gate_b = _sigmoid(jnp.dot(hb, w_gates[:, D_MODEL:], preferred_element_type=F32)
                      + b_gates[:, D_MODEL:])
    merged = a_part + gate_b * jnp.dot(yb, wb_ref[...], preferred_element_type=F32)
    mix = jnp.dot(merged.astype(BF16), wo_ref[...], preferred_element_type=F32)
    return _layer_norm(alpha * x + mix, ln_g[...], ln_b[...])


def _mix_prompt_kernel(x_ref, w_main, b_main, w_bd, b_bd, w_gates, b_gates, vg_ref, vb_ref,
                       ws_ref, bst_ref, convw_ref, alog_ref, dtb_ref, normw_ref,
                       wa_ref, wb_ref, wo_ref, ln_g, ln_b,
                       x2_ref, conv_out_ref, ssm_out_ref,
                       s_ref, xc_ref, q_s, k_s, v_s, z_s, g_s, beta_s, yb_s, *, alpha, tt):
    t = pl.program_id(1)
    nt = pl.num_programs(1)

    @pl.when(t == 0)
    def _():
        s_ref[...] = jnp.zeros(s_ref.shape, F32)
        xc_ref[0:ROW_TILE, :] = jnp.zeros((ROW_TILE, QKV), F32)

    x = x_ref[...]
    hb = x.astype(BF16)

    u = _gelu_tanh(jnp.dot(hb, w_main[:, 0:D_MODEL], preferred_element_type=F32)
                   + b_main[:, 0:D_MODEL])
    v = _gelu_tanh(jnp.dot(hb, w_main[:, D_MODEL:2 * D_MODEL], preferred_element_type=F32)
                   + b_main[:, D_MODEL:2 * D_MODEL])
    vn = _layer_norm(v, vg_ref[...], vb_ref[...]).astype(BF16)
    r128 = lax.broadcasted_iota(jnp.int32, (GM_CHUNK, GM_CHUNK), 0)
    c128 = lax.broadcasted_iota(jnp.int32, (GM_CHUNK, GM_CHUNK), 1)
    tril = r128 >= c128
    w_tril = [jnp.where(tril, ws_ref[g], 0.0).astype(BF16) for g in range(GROUPS)]
    rows = []
    for c in range(tt // GM_CHUNK):
        cols = []
        for g in range(GROUPS):
            blk = vn[c * GM_CHUNK:(c + 1) * GM_CHUNK, g * GROUP_DIM:(g + 1) * GROUP_DIM]
            cols.append(jnp.dot(w_tril[g], blk, preferred_element_type=F32) + bst_ref[:, g:g + 1])
        rows.append(jnp.concatenate(cols, axis=1))
    mixed = jnp.concatenate(rows, axis=0)
    ya = (u * mixed).astype(BF16)
    gate_a = _sigmoid(jnp.dot(hb, w_gates[:, :D_MODEL], preferred_element_type=F32)
                      + b_gates[:, :D_MODEL])
    a_part = gate_a * jnp.dot(ya, wa_ref[...], preferred_element_type=F32)

    xc_ref[ROW_TILE:, :] = (jnp.dot(hb, w_main[:, 2 * D_MODEL:5 * D_MODEL],
                                    preferred_element_type=F32)
                            + b_main[:, 2 * D_MODEL:5 * D_MODEL])
    acc = xc_ref[ROW_TILE:, :] * convw_ref[DN_CONV - 1:DN_CONV, :]
    for j in range(1, DN_CONV):
        acc = acc + xc_ref[ROW_TILE - j:ROW_TILE - j + tt, :] * convw_ref[DN_CONV - 1 - j:DN_CONV - j, :]

    @pl.when(t == nt - 1)
    def _():
        conv_out_ref[...] = xc_ref[ROW_TILE + tt - (DN_CONV - 1):ROW_TILE + tt, :]

    xc_ref[0:ROW_TILE, :] = xc_ref[tt:tt + ROW_TILE, :]
    sact = _silu(acc)
    scale = HEAD_DIM ** -0.5
    for h in range(HEADS):
        sl = slice(h * HEAD_DIM, (h + 1) * HEAD_DIM)
        q_s[:, sl] = _l2n(sact[:, sl]) * scale
        k_s[:, sl] = _l2n(sact[:, D_MODEL + h * HEAD_DIM:D_MODEL + (h + 1) * HEAD_DIM])
    v_s[...] = sact[:, 2 * D_MODEL:]
    z_s[...] = _branch_gates_and_z(hb, w_main, b_main)
    beta, g_log = _beta_and_logdecay(hb, w_bd, b_bd, alog_ref, dtb_ref)
    beta_s[...] = beta
    rt = lax.broadcasted_iota(jnp.int32, (tt, tt), 0)
    ct = lax.broadcasted_iota(jnp.int32, (tt, tt), 1)
    cum = jnp.where(((rt // DN_CHUNK) == (ct // DN_CHUNK)) & (rt >= ct), 1.0, 0.0)
    g_s[...] = _dot_exact_lhs(cum, g_log)

    masks = _block_masks(SUB, DN_CHUNK)
    norm_w = normw_ref[...]

    def sub_body(s, carry):
        r0 = pl.multiple_of(s * SUB, SUB)
        g_sub = g_s[pl.ds(r0, SUB), :]
        g_t = g_sub.T
        b_sub = beta_s[pl.ds(r0, SUB), :]
        for h in range(HEADS):
            sl = slice(h * HEAD_DIM, (h + 1) * HEAD_DIM)
            q = q_s[pl.ds(r0, SUB), sl]
            k = k_s[pl.ds(r0, SUB), sl]
            vv = v_s[pl.ds(r0, SUB), sl]
            g_col = jnp.broadcast_to(g_sub[:, h:h + 1], (SUB, HEAD_DIM))
            g_row = jnp.broadcast_to(g_t[h:h + 1, :], (SUB, SUB))
            beta_h = jnp.broadcast_to(b_sub[:, h:h + 1], (SUB, HEAD_DIM))
            u_h, w_h, qk, qe = _dn_intra(q, k, vv, beta_h, g_col, g_row, masks, 5)
            state = s_ref[h]
            outs = []
            zeros = jnp.zeros((DN_CHUNK, HEAD_DIM), F32)
            for c in range(SUB // DN_CHUNK):
                rs = slice(c * DN_CHUNK, (c + 1) * DN_CHUNK)
                r = _dot(jnp.concatenate([w_h[rs], qe[rs]], axis=0), state)
                v_new = u_h[rs] - r[:DN_CHUNK]
                v_pad = (jnp.concatenate([v_new, zeros], axis=0) if c == 0
                         else jnp.concatenate([zeros, v_new], axis=0))
                outs.append(r[DN_CHUNK:] + _dot(qk[rs], v_pad))
                g_last = g_col[(c + 1) * DN_CHUNK - 1:(c + 1) * DN_CHUNK, :]
                in_chunk = (lax.broadcasted_iota(jnp.int32, (SUB, HEAD_DIM), 0) // DN_CHUNK) == c
                k_dec = jnp.where(in_chunk, k * jnp.exp(jnp.where(in_chunk, g_last - g_col, 0.0)), 0.0)
                state = state * jnp.exp(g_last) + _dot(k_dec.T, v_pad)
            s_ref[h] = state
            o = jnp.concatenate(outs, axis=0)
            yb_s[pl.ds(r0, SUB), sl] = _gated_rms(o, norm_w, z_s[pl.ds(r0, SUB), sl]).astype(BF16)
        return carry

    lax.fori_loop(0, tt // SUB, sub_body, 0)

    @pl.when(t == nt - 1)
    def _():
        ssm_out_ref[...] = s_ref[...]

    x2_ref[...] = _merge_out_ln(x, hb, a_part, yb_s[...], w_gates, b_gates, wb_ref, wo_ref,
                                ln_g, ln_b, alpha)


def _mix_prompt(x1, p, alpha, tt):
    b, t, _ = x1.shape
    assert t % tt == 0 and tt % SUB == 0
    consts = [p['w_main'], p['b_main'], p['w_bd'], p['b_bd'], p['w_gates'], p['b_gates'],
              p['gm_v_g'], p['gm_v_b'], p['gm_w_s'], p['gm_b_s_t'], p['conv_w'], p['a_log'],
              p['dt_bias'], p['norm_w'], p['w_a'], p['w_b'], p['w_o'], p['ln2_g'], p['ln2_b']]
    return pl.pallas_call(
        functools.partial(_mix_prompt_kernel, alpha=alpha, tt=tt),
        grid=(b, t // tt),
        in_specs=[pl.BlockSpec((None, tt, D_MODEL), lambda i, j: (i, j, 0))]
                 + [_const_spec(c.shape) for c in consts],
        out_specs=[pl.BlockSpec((None, tt, D_MODEL), lambda i, j: (i, j, 0)),
                   pl.BlockSpec((None, DN_CONV - 1, QKV), lambda i, j: (i, 0, 0)),
                   pl.BlockSpec((None, HEADS, HEAD_DIM, HEAD_DIM), lambda i, j: (i, 0, 0, 0))],
        out_shape=[jax.ShapeDtypeStruct((b, t, D_MODEL), F32),
                   jax.ShapeDtypeStruct((b, DN_CONV - 1, QKV), F32),
                   jax.ShapeDtypeStruct((b, HEADS, HEAD_DIM, HEAD_DIM), F32)],
        scratch_shapes=[pltpu.VMEM((HEADS, HEAD_DIM, HEAD_DIM), F32),
                        pltpu.VMEM((tt + ROW_TILE, QKV), F32),
                        pltpu.VMEM((tt, D_MODEL), F32),
                        pltpu.VMEM((tt, D_MODEL), F32),
                        pltpu.VMEM((tt, D_MODEL), F32),
                        pltpu.VMEM((tt, D_MODEL), F32),
                        pltpu.VMEM((tt, 128), F32),
                        pltpu.VMEM((tt, 128), F32),
                        pltpu.VMEM((tt, D_MODEL), BF16)],
        compiler_params=pltpu.CompilerParams(dimension_semantics=("arbitrary", "arbitrary"),
                                             vmem_limit_bytes=VMEM_LIMIT),
        name="mix_prompt",
    )(x1, *consts)


def _mix_sample_kernel(x_ref, cs_ref, s_in_ref, w_main, b_main, w_bd, b_bd, w_gates, b_gates,
                       vg_ref, vb_ref, coef_ref, bias_ref, convw_ref, alog_ref, dtb_ref, normw_ref,
                       wa_ref, wb_ref, wo_ref, ln_g, ln_b,
                       x2_ref, vrow_ref, z_out_ref, s_out_ref, *, alpha, nb):
    rows = nb * ROW_TILE
    x = x_ref[...]
    hb = x.astype(BF16)
    valid = (lax.broadcasted_iota(jnp.int32, (rows, 1), 0) % ROW_TILE) >= (ROW_TILE - DN_CONV)
    validf = jnp.where(valid, 1.0, 0.0).astype(F32)

    u = _gelu_tanh(jnp.dot(hb, w_main[:, 0:D_MODEL], preferred_element_type=F32)
                   + b_main[:, 0:D_MODEL])
    v = _gelu_tanh(jnp.dot(hb, w_main[:, D_MODEL:2 * D_MODEL], preferred_element_type=F32)
                   + b_main[:, D_MODEL:2 * D_MODEL])
    vn = _layer_norm(v, vg_ref[...], vb_ref[...])
    vrow_ref[...] = vn
    vn3 = vn.reshape(nb, ROW_TILE, D_MODEL)
    mixed = vn3 * coef_ref[0][None] + bias_ref[...][None]
    for j in range(1, DN_CONV):
        mixed = mixed + pltpu.roll(vn3, j, 1) * coef_ref[j][None]
    ya = (u * mixed.reshape(rows, D_MODEL)).astype(BF16)
    gate_a = _sigmoid(jnp.dot(hb, w_gates[:, :D_MODEL], preferred_element_type=F32)
                      + b_gates[:, :D_MODEL])
    a_part = gate_a * jnp.dot(ya, wa_ref[...], preferred_element_type=F32)

    qkv = jnp.dot(hb, w_main[:, 2 * D_MODEL:5 * D_MODEL], preferred_element_type=F32) \
        + b_main[:, 2 * D_MODEL:5 * D_MODEL]
    zfull = jnp.where(valid, qkv, 0.0) + cs_ref[...]
    z_out_ref[...] = zfull
    z3 = zfull.reshape(nb, ROW_TILE, QKV)
    acc = z3 * convw_ref[DN_CONV - 1:DN_CONV, :][None]
    for j in range(1, DN_CONV):
        acc = acc + pltpu.roll(z3, j, 1) * convw_ref[DN_CONV - 1 - j:DN_CONV - j, :][None]
    sact = _silu(acc.reshape(rows, QKV)) * validf
    z_act = _branch_gates_and_z(hb, w_main, b_main)
    beta, g_log = _beta_and_logdecay(hb, w_bd, b_bd, alog_ref, dtb_ref)
    beta = beta * validf
    g_log = g_log * validf
    rt = lax.broadcasted_iota(jnp.int32, (rows, rows), 0)
    ct = lax.broadcasted_iota(jnp.int32, (rows, rows), 1)
    cum = jnp.where(((rt // ROW_TILE) == (ct // ROW_TILE)) & (rt >= ct), 1.0, 0.0)
    g_cum = _dot_exact_lhs(cum, g_log)

    pad_rows = SUB - rows
    def pad(a):
        return jnp.concatenate([a, jnp.zeros((pad_rows, a.shape[1]), a.dtype)], axis=0)

    g_pad = pad(g_cum)
    g_t = g_pad.T
    beta_pad = pad(beta)
    masks = _block_masks(SUB, ROW_TILE)
    norm_w = normw_ref[...]
    scale = HEAD_DIM ** -0.5
    row_id = lax.broadcasted_iota(jnp.int32, (SUB, HEAD_DIM), 0)
    ybs = []
    for h in range(HEADS):
        sl = slice(h * HEAD_DIM, (h + 1) * HEAD_DIM)
        q = pad(_l2n(sact[:, sl]) * scale * validf)
        k = pad(_l2n(sact[:, D_MODEL + h * HEAD_DIM:D_MODEL + (h + 1) * HEAD_DIM]) * validf)
        vv = pad(sact[:, 2 * D_MODEL + h * HEAD_DIM:2 * D_MODEL + (h + 1) * HEAD_DIM])
        g_col = jnp.broadcast_to(g_pad[:, h:h + 1], (SUB, HEAD_DIM))
        g_row = jnp.broadcast_to(g_t[h:h + 1, :], (SUB, SUB))
        beta_h = jnp.broadcast_to(beta_pad[:, h:h + 1], (SUB, HEAD_DIM))
        u_h, w_h, qk, qe = _dn_intra(q, k, vv, beta_h, g_col, g_row, masks, 2)
        v_news, q_states = [], []
        for i in range(nb):
            rs = slice(i * ROW_TILE, (i + 1) * ROW_TILE)
            r = _dot(jnp.concatenate([w_h[rs], qe[rs]], axis=0), s_in_ref[i, h])
            v_news.append(u_h[rs] - r[:ROW_TILE])
            q_states.append(r[ROW_TILE:])
        v_new = pad(jnp.concatenate(v_news, axis=0))
        o = jnp.concatenate(q_states, axis=0) + _dot(qk, v_new)[:rows]
        for i in range(nb):
            g_last = g_col[(i + 1) * ROW_TILE - 1:(i + 1) * ROW_TILE, :]
            in_seq = (row_id // ROW_TILE) == i
            k_dec = jnp.where(in_seq, k * jnp.exp(jnp.where(in_seq, g_last - g_col, 0.0)), 0.0)
            s_out_ref[i, h] = (s_in_ref[i, h] * jnp.exp(g_last)
                               + _dot(k_dec.T, jnp.where(in_seq, v_new, 0.0)))
        ybs.append(_gated_rms(o, norm_w, z_act[:, sl]).astype(BF16))
    yb = jnp.concatenate(ybs, axis=1)
    x2_ref[...] = _merge_out_ln(x, hb, a_part, yb, w_gates, b_gates, wb_ref, wo_ref,
                                ln_g, ln_b, alpha)


def _mix_sample(x1, cs_pad, s_in, p, alpha, nb):
    n = x1.shape[0]
    nseq = n // ROW_TILE
    assert nseq % nb == 0 and nb * ROW_TILE <= SUB
    rows = nb * ROW_TILE
    consts = [p['w_main'], p['b_main'], p['w_bd'], p['b_bd'], p['w_gates'], p['b_gates'],
              p['gm_v_g'], p['gm_v_b'], p['mix_coef'], p['mix_bias'], p['conv_w'], p['a_log'],
              p['dt_bias'], p['norm_w'], p['w_a'], p['w_b'], p['w_o'], p['ln2_g'], p['ln2_b']]
    state_spec = pl.BlockSpec((nb, HEADS, HEAD_DIM, HEAD_DIM), lambda i: (i, 0, 0, 0))
    return pl.pallas_call(
        functools.partial(_mix_sample_kernel, alpha=alpha, nb=nb),
        grid=(nseq // nb,),
        in_specs=[pl.BlockSpec((rows, D_MODEL), lambda i: (i, 0)),
                  pl.BlockSpec((rows, QKV), lambda i: (i, 0)),
                  state_spec] + [_const_spec(c.shape) for c in consts],
        out_specs=[pl.BlockSpec((rows, D_MODEL), lambda i: (i, 0)),
                   pl.BlockSpec((rows, D_MODEL), lambda i: (i, 0)),
                   pl.BlockSpec((rows, QKV), lambda i: (i, 0)),
                   state_spec],
        out_shape=[jax.ShapeDtypeStruct((n, D_MODEL), F32),
                   jax.ShapeDtypeStruct((n, D_MODEL), F32),
                   jax.ShapeDtypeStruct((n, QKV), F32),
                   jax.ShapeDtypeStruct(s_in.shape, F32)],
        compiler_params=pltpu.CompilerParams(dimension_semantics=("arbitrary",),
                                             vmem_limit_bytes=VMEM_LIMIT),
        name="mix_sample",
    )(x1, cs_pad, s_in, *consts)


def _pad_lanes(a, n=128):
    return jnp.pad(a, [(0, 0)] * (a.ndim - 1) + [(0, n - a.shape[-1])])


def _layer_params(l, ffn1_w_up, ffn1_w_down, ln1_g, ln1_b, w_in, b_in, gm_v_g, gm_v_b, gm_w_s,
                  gm_b_s, dn_conv_w, dn_a_log, dn_dt_bias, dn_norm_w, w_branch_a, w_branch_b,
                  w_out, ln2_g, ln2_b, ffn2_w_up, ffn2_w_down, ln3_g, ln3_b):
    row = lambda a: a[l][None, :].astype(F32)
    wi, bi = w_in[l], b_in[l]
    o_beta = MAIN_COLS
    o_dec = o_beta + HEADS
    o_gate = o_dec + HEADS
    ws = gm_w_s[l]
    lsm = DN_CONV
    coef = jnp.zeros((DN_CONV, ROW_TILE, GROUPS), F32)
    for j in range(lsm):
        for t in range(j, lsm):
            coef = coef.at[j, ROW_TILE - lsm + t, :].set(ws[:, t, t - j])
    bias = jnp.zeros((ROW_TILE, GROUPS), F32).at[ROW_TILE - lsm:, :].set(gm_b_s[l][:, :lsm].T)
    return {
        'ffn1': (ffn1_w_up[l][:, :D_FF].astype(BF16), ffn1_w_up[l][:, D_FF:].astype(BF16),
                 ffn1_w_down[l].astype(BF16), row(ln1_g), row(ln1_b)),
        'ffn2': (ffn2_w_up[l][:, :D_FF].astype(BF16), ffn2_w_up[l][:, D_FF:].astype(BF16),
                 ffn2_w_down[l].astype(BF16), row(ln3_g), row(ln3_b)),
        'w_main': wi[:, :MAIN_COLS].astype(BF16),
        'b_main': bi[None, :MAIN_COLS],
        'w_bd': jnp.concatenate([_pad_lanes(wi[:, o_beta:o_dec]), _pad_lanes(wi[:, o_dec:o_gate])],
                                axis=1).astype(BF16),
        'b_bd': jnp.concatenate([_pad_lanes(bi[None, o_beta:o_dec]), _pad_lanes(bi[None, o_dec:o_gate])],
                                axis=1),
        'w_gates': wi[:, o_gate:].astype(BF16),
        'b_gates': bi[None, o_gate:],
        'gm_v_g': row(gm_v_g), 'gm_v_b': row(gm_v_b),
        'gm_w_s': ws, 'gm_b_s_t': gm_b_s[l].T,
        'mix_coef': jnp.repeat(coef, GROUP_DIM, axis=-1), 'mix_bias': jnp.repeat(bias, GROUP_DIM, axis=-1),
        'conv_w': dn_conv_w[l],
        'a_log': _pad_lanes(dn_a_log[l][None, :].astype(F32)),
        'dt_bias': _pad_lanes(dn_dt_bias[l][None, :].astype(F32)),
        'norm_w': row(dn_norm_w),
        'w_a': w_branch_a[l].astype(BF16), 'w_b': w_branch_b[l].astype(BF16),
        'w_o': w_out[l].astype(BF16),
        'ln2_g': row(ln2_g), 'ln2_b': row(ln2_b),
    }


def kernel(x_prompt, x_sample, state_conv, state_ssm, ffn1_w_up, ffn1_w_down, ln1_g, ln1_b, w_in, b_in, gm_v_g, gm_v_b, gm_w_s, gm_b_s, dn_conv_w, dn_a_log, dn_dt_bias, dn_norm_w, w_branch_a, w_branch_b, w_out, ln2_g, ln2_b, ffn2_w_up, ffn2_w_down, ln3_g, ln3_b):
    depth = ffn1_w_up.shape[0]
    alpha = (2.0 * depth) ** 0.25
    bp, tp, _ = x_prompt.shape
    bs, ts, _ = x_sample.shape
    assert ts == DN_CONV and ts <= ROW_TILE - (DN_CONV - 1)
    lead = ROW_TILE - ts
    y_p = x_prompt
    y_s = jnp.pad(x_sample, ((0, 0), (lead, 0), (0, 0))).reshape(bs * ROW_TILE, D_MODEL)
    conv_p, ssm_p, conv_s, ssm_s, v_s = [], [], [], [], []
    for l in range(depth):
        p = _layer_params(l, ffn1_w_up, ffn1_w_down, ln1_g, ln1_b, w_in, b_in, gm_v_g, gm_v_b,
                          gm_w_s, gm_b_s, dn_conv_w, dn_a_log, dn_dt_bias, dn_norm_w, w_branch_a,
                          w_branch_b, w_out, ln2_g, ln2_b, ffn2_w_up, ffn2_w_down, ln3_g, ln3_b)
        x1 = _ffn_ln(y_p.reshape(bp * tp, D_MODEL), *p['ffn1'], alpha, 512).reshape(bp, tp, D_MODEL)
        x2, c_p, s_p = _mix_prompt(x1, p, alpha, 256)
        y_p = _ffn_ln(x2.reshape(bp * tp, D_MODEL), *p['ffn2'], alpha, 512).reshape(bp, tp, D_MODEL)
        x1s = _ffn_ln(y_s, *p['ffn1'], alpha, 512)
        cs_pad = jnp.pad(state_conv[l], ((0, 0), (lead - (DN_CONV - 1), ts), (0, 0)))
        x2s, vrows, zfull, s_s = _mix_sample(x1s, cs_pad.reshape(bs * ROW_TILE, QKV),
                                             state_ssm[l], p, alpha, 8)
        y_s = _ffn_ln(x2s, *p['ffn2'], alpha, 512)
        conv_p.append(c_p)
        ssm_p.append(s_p)
        conv_s.append(zfull.reshape(bs, ROW_TILE, QKV)[:, ROW_TILE - (DN_CONV - 1):])
        ssm_s.append(s_s)
        v_s.append(vrows.reshape(bs, ROW_TILE, D_MODEL)[:, lead:])
    y_s_out = y_s.reshape(bs, ROW_TILE, D_MODEL)[:, lead:]
    return (y_p, y_s_out, jnp.stack(conv_p), jnp.stack(ssm_p), jnp.stack(conv_s), jnp.stack(ssm_s),
            jnp.stack(v_s))
```

```python
import functools
import math

import jax
import jax.numpy as jnp
from jax import lax
from jax.experimental import pallas as pl
from jax.experimental.pallas import tpu as pltpu

F32 = jnp.float32
BF16 = jnp.bfloat16

D_MODEL = 1024
D_FF = 2816
HEADS = 8
HEAD_DIM = 128
GROUPS = 8
GROUP_DIM = 128
GM_CHUNK = 128
DN_CHUNK = 64
DN_CONV = 4
QKV = 3 * D_MODEL
MAIN_COLS = 6 * D_MODEL
LN_EPS = 1e-5
RMS_EPS = 1e-6

SUB = 128
ROW_TILE = 8
VMEM_LIMIT = 56 * 1024 * 1024


def _sigmoid(x):
    return 1.0 / (1.0 + jnp.exp(-x))


def _silu(x):
    return x * _sigmoid(x)


def _gelu_tanh(x):
    c = math.sqrt(2.0 / math.pi)
    return x * (0.5 * (1.0 + jnp.tanh(c * (x + 0.044715 * (x * x * x)))))


def _softplus(x):
    return jnp.maximum(x, 0.0) + jnp.log(1.0 + jnp.exp(-jnp.abs(x)))


def _layer_norm(y, g, b):
    mu = jnp.mean(y, axis=-1, keepdims=True)
    yc = y - mu
    var = jnp.mean(yc * yc, axis=-1, keepdims=True)
    return yc * lax.rsqrt(var + LN_EPS) * g + b


def _dot(a, b):
    return jnp.dot(a.astype(BF16), b.astype(BF16), preferred_element_type=F32)


def _dot_nt(a, b):
    return lax.dot_general(a.astype(BF16), b.astype(BF16), (((1,), (1,)), ((), ())),
                           preferred_element_type=F32)


def _dot_exact_lhs(m01, x):
    hi = x.astype(BF16)
    r1 = x - hi.astype(F32)
    mid = r1.astype(BF16)
    lo = (r1 - mid.astype(F32)).astype(BF16)
    m = m01.astype(BF16)
    return (jnp.dot(m, hi, preferred_element_type=F32)
            + jnp.dot(m, mid, preferred_element_type=F32)
            + jnp.dot(m, lo, preferred_element_type=F32))


def _block_masks(n, blk):
    row = lax.broadcasted_iota(jnp.int32, (n, n), 0)
    col = lax.broadcasted_iota(jnp.int32, (n, n), 1)
    same = (row // blk) == (col // blk)
    return same & (row >= col), same & (row > col), row == col


def _inv_unit_lower(a, eye, n_iter):
    n = eye.shape[0]
    b = [-x for x in a]
    p = [eye + x for x in b]
    b = [_dot(x, x) for x in b]
    for _ in range(n_iter - 1):
        pb = [_dot(jnp.concatenate([pi, bi], axis=0), bi) for pi, bi in zip(p, b)]
        p = [pi + x[:n] for pi, x in zip(p, pb)]
        b = [x[n:] for x in pb]
    return [pi + _dot(pi, bi) for pi, bi in zip(p, b)]


def _dn_intra(q, k, v, beta, g_col, g_row, masks, n_iter):
    causal, strict, diag = masks
    heads = range(len(q))
    decay = [jnp.where(causal, jnp.exp(jnp.where(causal, g_col[h] - g_row[h], 0.0)), 0.0) for h in heads]
    kb = [k[h] * beta[h] for h in heads]
    kq = [_dot_nt(jnp.concatenate([kb[h], q[h]], axis=0), k[h]) for h in heads]
    a = [jnp.where(strict, kq[h][:SUB] * decay[h], 0.0) for h in heads]
    qk = [kq[h][SUB:] * decay[h] for h in heads]
    eye = jnp.where(diag, 1.0, 0.0).astype(F32)
    t_inv = _inv_unit_lower(a, eye, n_iter)
    e_g = [jnp.exp(g_col[h]) for h in heads]
    uw = [_dot(t_inv[h], jnp.concatenate([v[h] * beta[h], kb[h] * e_g[h]], axis=1)) for h in heads]
    return ([x[:, :HEAD_DIM] for x in uw], [x[:, HEAD_DIM:] for x in uw], qk,
            [q[h] * e_g[h] for h in heads])


def _gated_rms(o, norm_w, z_act):
    return o * lax.rsqrt(jnp.mean(o * o, axis=-1, keepdims=True) + RMS_EPS) * norm_w * z_act


def _l2n(x):
    return x * lax.rsqrt(jnp.sum(x * x, axis=-1, keepdims=True) + RMS_EPS)


def _ffn_ln_kernel(x_ref, wa_ref, wg_ref, wd_ref, g_ref, b_ref, o_ref, *, alpha, n_split):
    x = x_ref[...]
    xb = x.astype(BF16)
    cw = D_FF // n_split
    acc = None
    for c in range(n_split):
        a = jnp.dot(xb, wa_ref[:, c * cw:(c + 1) * cw], preferred_element_type=F32)
        gt = jnp.dot(xb, wg_ref[:, c * cw:(c + 1) * cw], preferred_element_type=F32)
        h = (_silu(a) * gt).astype(BF16)
        f = jnp.dot(h, wd_ref[c * cw:(c + 1) * cw, :], preferred_element_type=F32)
        acc = f if acc is None else acc + f
    o_ref[...] = _layer_norm(alpha * x + 0.5 * acc, g_ref[...], b_ref[...])


def _const_spec(shape):
    nd = len(shape)
    return pl.BlockSpec(shape, lambda *_: (0,) * nd, pipeline_mode=pl.Buffered(1))


def _ffn_ln(x2d, wa, wg, wd, g, b, alpha, tm):
    n = x2d.shape[0]
    assert n % tm == 0
    return pl.pallas_call(
        functools.partial(_ffn_ln_kernel, alpha=alpha, n_split=2),
        grid=(n // tm,),
        in_specs=[pl.BlockSpec((tm, D_MODEL), lambda i: (i, 0)),
                  _const_spec(wa.shape), _const_spec(wg.shape), _const_spec(wd.shape),
                  _const_spec(g.shape), _const_spec(b.shape)],
        out_specs=pl.BlockSpec((tm, D_MODEL), lambda i: (i, 0)),
        out_shape=jax.ShapeDtypeStruct((n, D_MODEL), F32),
        compiler_params=pltpu.CompilerParams(dimension_semantics=("arbitrary",),
                                             vmem_limit_bytes=VMEM_LIMIT),
        name="ffn_ln",
    )(x2d, wa, wg, wd, g, b)


def _branch_gates_and_z(hb, w_main, b_main):
    z = jnp.dot(hb, w_main[:, 5 * D_MODEL:6 * D_MODEL], preferred_element_type=F32) \
        + b_main[:, 5 * D_MODEL:6 * D_MODEL]
    return _silu(z)


def _beta_and_logdecay(hb, w_bd, b_bd, alog, dtb):
    bd = jnp.dot(hb, w_bd[...], preferred_element_type=F32) + b_bd[...]
    beta = _sigmoid(bd[:, :128])
    g = -jnp.exp(alog[...]) * _softplus(bd[:, 128:] + dtb[...])
    return beta, g


def _merge_out_ln(x, hb, a_part, yb, w_gates, b_gates, wb_ref, wo_ref, ln_g, ln_b, alpha):
    gate_b = _sigmoid(jnp.dot(hb, w_gates[:, D_MODEL:], preferred_element_type=F32)
                      + b_gates[:, D_MODEL:])
    merged = a_part + gate_b * jnp.dot(yb, wb_ref[...], preferred_element_type=F32)
    mix = jnp.dot(merged.astype(BF16), wo_ref[...], preferred_element_type=F32)
    return _layer_norm(alpha * x + mix, ln_g[...], ln_b[...])


def _mix_prompt_kernel(x_ref, w_main, b_main, w_bd, b_bd, w_gates, b_gates, vg_ref, vb_ref,
                       ws_ref, bst_ref, convw_ref, alog_ref, dtb_ref, normw_ref,
                       wa_ref, wb_ref, wo_ref, ln_g, ln_b,
                       x2_ref, conv_out_ref, ssm_out_ref,
                       s_ref, xc_ref, q_s, k_s, v_s, z_s, g_s, beta_s, yb_s, *, alpha, tt):
    t = pl.program_id(1)
    nt = pl.num_programs(1)

    @pl.when(t == 0)
    def _():
        s_ref[...] = jnp.zeros(s_ref.shape, F32)
        xc_ref[0:ROW_TILE, :] = jnp.zeros((ROW_TILE, QKV), F32)

    x = x_ref[...]
    hb = x.astype(BF16)

    u = _gelu_tanh(jnp.dot(hb, w_main[:, 0:D_MODEL], preferred_element_type=F32)
                   + b_main[:, 0:D_MODEL])
    v = _gelu_tanh(jnp.dot(hb, w_main[:, D_MODEL:2 * D_MODEL], preferred_element_type=F32)
                   + b_main[:, D_MODEL:2 * D_MODEL])
    vn = _layer_norm(v, vg_ref[...], vb_ref[...]).astype(BF16)
    r128 = lax.broadcasted_iota(jnp.int32, (GM_CHUNK, GM_CHUNK), 0)
    c128 = lax.broadcasted_iota(jnp.int32, (GM_CHUNK, GM_CHUNK), 1)
    tril = r128 >= c128
    w_tril = [jnp.where(tril, ws_ref[g], 0.0).astype(BF16) for g in range(GROUPS)]
    rows = []
    for c in range(tt // GM_CHUNK):
        cols = []
        for g in range(GROUPS):
            blk = vn[c * GM_CHUNK:(c + 1) * GM_CHUNK, g * GROUP_DIM:(g + 1) * GROUP_DIM]
            cols.append(jnp.dot(w_tril[g], blk, preferred_element_type=F32) + bst_ref[:, g:g + 1])
        rows.append(jnp.concatenate(cols, axis=1))
    mixed = jnp.concatenate(rows, axis=0)
    ya = (u * mixed).astype(BF16)
    gate_a = _sigmoid(jnp.dot(hb, w_gates[:, :D_MODEL], preferred_element_type=F32)
                      + b_gates[:, :D_MODEL])
    a_part = gate_a * jnp.dot(ya, wa_ref[...], preferred_element_type=F32)

    xc_ref[ROW_TILE:, :] = (jnp.dot(hb, w_main[:, 2 * D_MODEL:5 * D_MODEL],
                                    preferred_element_type=F32)
                            + b_main[:, 2 * D_MODEL:5 * D_MODEL])
    acc = xc_ref[ROW_TILE:, :] * convw_ref[DN_CONV - 1:DN_CONV, :]
    for j in range(1, DN_CONV):
        acc = acc + xc_ref[ROW_TILE - j:ROW_TILE - j + tt, :] * convw_ref[DN_CONV - 1 - j:DN_CONV - j, :]

    @pl.when(t == nt - 1)
    def _():
        conv_out_ref[...] = xc_ref[ROW_TILE + tt - (DN_CONV - 1):ROW_TILE + tt, :]

    xc_ref[0:ROW_TILE, :] = xc_ref[tt:tt + ROW_TILE, :]
    sact = _silu(acc)
    scale = HEAD_DIM ** -0.5
    for h in range(HEADS):
        sl = slice(h * HEAD_DIM, (h + 1) * HEAD_DIM)
        q_s[:, sl] = _l2n(sact[:, sl]) * scale
        k_s[:, sl] = _l2n(sact[:, D_MODEL + h * HEAD_DIM:D_MODEL + (h + 1) * HEAD_DIM])
    v_s[...] = sact[:, 2 * D_MODEL:]
    z_s[...] = _branch_gates_and_z(hb, w_main, b_main)
    beta, g_log = _beta_and_logdecay(hb, w_bd, b_bd, alog_ref, dtb_ref)
    beta_s[...] = beta
    rt = lax.broadcasted_iota(jnp.int32, (tt, tt), 0)
    ct = lax.broadcasted_iota(jnp.int32, (tt, tt), 1)
    cum = jnp.where(((rt // DN_CHUNK) == (ct // DN_CHUNK)) & (rt >= ct), 1.0, 0.0)
    g_s[...] = _dot_exact_lhs(cum, g_log)

    masks = _block_masks(SUB, DN_CHUNK)
    norm_w = normw_ref[...]

    def sub_body(s, carry):
        r0 = pl.multiple_of(s * SUB, SUB)
        g_sub = g_s[pl.ds(r0, SUB), :]
        g_t = g_sub.T
        b_sub = beta_s[pl.ds(r0, SUB), :]
        heads = range(HEADS)
        sls = [slice(h * HEAD_DIM, (h + 1) * HEAD_DIM) for h in heads]
        q = [q_s[pl.ds(r0, SUB), sl] for sl in sls]
        k = [k_s[pl.ds(r0, SUB), sl] for sl in sls]
        vv = [v_s[pl.ds(r0, SUB), sl] for sl in sls]
        z_act = [z_s[pl.ds(r0, SUB), sl] for sl in sls]
        state = [s_ref[h] for h in heads]
        g_col = [jnp.broadcast_to(g_sub[:, h:h + 1], (SUB, HEAD_DIM)) for h in heads]
        g_row = [jnp.broadcast_to(g_t[h:h + 1, :], (SUB, SUB)) for h in heads]
        beta_h = [jnp.broadcast_to(b_sub[:, h:h + 1], (SUB, HEAD_DIM)) for h in heads]
        u_h, w_h, qk, qe = _dn_intra(q, k, vv, beta_h, g_col, g_row, masks, 5)
        n_chunks = SUB // DN_CHUNK
        g_last = [[g_col[h][(c + 1) * DN_CHUNK - 1:(c + 1) * DN_CHUNK, :] for c in range(n_chunks)]
                  for h in heads]
        row_chunk = lax.broadcasted_iota(jnp.int32, (SUB, HEAD_DIM), 0) // DN_CHUNK
        k_dec_t = []
        for h in heads:
            g_end = g_last[h][n_chunks - 1]
            for c in range(n_chunks - 2, -1, -1):
                g_end = jnp.where(row_chunk == c, g_last[h][c], g_end)
            k_dec_t.append((k[h] * jnp.exp(g_end - g_col[h])).T)
        zeros = jnp.zeros((DN_CHUNK, HEAD_DIM), F32)
        outs = [[] for _ in heads]
        for c in range(n_chunks):
            rs = slice(c * DN_CHUNK, (c + 1) * DN_CHUNK)
            r = [_dot(jnp.concatenate([w_h[h][rs], qe[h][rs]], axis=0), state[h]) for h in heads]
            v_new = [u_h[h][rs] - r[h][:DN_CHUNK] for h in heads]
            v_pad = [jnp.concatenate([zeros] * c + [v_new[h]] + [zeros] * (n_chunks - 1 - c), axis=0)
                     for h in heads]
            m = [_dot(jnp.concatenate([qk[h][rs], k_dec_t[h]], axis=0), v_pad[h]) for h in heads]
            for h in heads:
                outs[h].append(r[h][DN_CHUNK:] + m[h][:DN_CHUNK])
            state = [state[h] * jnp.exp(g_last[h][c]) + m[h][DN_CHUNK:] for h in heads]
        for h in heads:
            s_ref[h] = state[h]
            o = jnp.concatenate(outs[h], axis=0)
            yb_s[pl.ds(r0, SUB), sls[h]] = _gated_rms(o, norm_w, z_act[h]).astype(BF16)
        return carry

    lax.fori_loop(0, tt // SUB, sub_body, 0)

    @pl.when(t == nt - 1)
    def _():
        ssm_out_ref[...] = s_ref[...]

    x2_ref[...] = _merge_out_ln(x, hb, a_part, yb_s[...], w_gates, b_gates, wb_ref, wo_ref,
                                ln_g, ln_b, alpha)


def _mix_prompt(x1, p, alpha, tt):
    b, t, _ = x1.shape
    assert t % tt == 0 and tt % SUB == 0
    consts = [p['w_main'], p['b_main'], p['w_bd'], p['b_bd'], p['w_gates'], p['b_gates'],
              p['gm_v_g'], p['gm_v_b'], p['gm_w_s'], p['gm_b_s_t'], p['conv_w'], p['a_log'],
              p['dt_bias'], p['norm_w'], p['w_a'], p['w_b'], p['w_o'], p['ln2_g'], p['ln2_b']]
    return pl.pallas_call(
        functools.partial(_mix_prompt_kernel, alpha=alpha, tt=tt),
        grid=(b, t // tt),
        in_specs=[pl.BlockSpec((None, tt, D_MODEL), lambda i, j: (i, j, 0))]
                 + [_const_spec(c.shape) for c in consts],
        out_specs=[pl.BlockSpec((None, tt, D_MODEL), lambda i, j: (i, j, 0)),
                   pl.BlockSpec((None, DN_CONV - 1, QKV), lambda i, j: (i, 0, 0)),
                   pl.BlockSpec((None, HEADS, HEAD_DIM, HEAD_DIM), lambda i, j: (i, 0, 0, 0))],
        out_shape=[jax.ShapeDtypeStruct((b, t, D_MODEL), F32),
                   jax.ShapeDtypeStruct((b, DN_CONV - 1, QKV), F32),
                   jax.ShapeDtypeStruct((b, HEADS, HEAD_DIM, HEAD_DIM), F32)],
        scratch_shapes=[pltpu.VMEM((HEADS, HEAD_DIM, HEAD_DIM), F32),
                        pltpu.VMEM((tt + ROW_TILE, QKV), F32),
                        pltpu.VMEM((tt, D_MODEL), F32),
                        pltpu.VMEM((tt, D_MODEL), F32),
                        pltpu.VMEM((tt, D_MODEL), F32),
                        pltpu.VMEM((tt, D_MODEL), F32),
                        pltpu.VMEM((tt, 128), F32),
                        pltpu.VMEM((tt, 128), F32),
                        pltpu.VMEM((tt, D_MODEL), BF16)],
        compiler_params=pltpu.CompilerParams(dimension_semantics=("arbitrary", "arbitrary"),
                                             vmem_limit_bytes=VMEM_LIMIT),
        name="mix_prompt",
    )(x1, *consts)


def _mix_sample_kernel(x_ref, cs_ref, s_in_ref, w_main, b_main, w_bd, b_bd, w_gates, b_gates,
                       vg_ref, vb_ref, coef_ref, bias_ref, convw_ref, alog_ref, dtb_ref, normw_ref,
                       wa_ref, wb_ref, wo_ref, ln_g, ln_b,
                       x2_ref, vrow_ref, z_out_ref, s_out_ref, *, alpha, nb):
    rows = nb * ROW_TILE
    x = x_ref[...]
    hb = x.astype(BF16)
    valid = (lax.broadcasted_iota(jnp.int32, (rows, 1), 0) % ROW_TILE) >= (ROW_TILE - DN_CONV)
    validf = jnp.where(valid, 1.0, 0.0).astype(F32)

    u = _gelu_tanh(jnp.dot(hb, w_main[:, 0:D_MODEL], preferred_element_type=F32)
                   + b_main[:, 0:D_MODEL])
    v = _gelu_tanh(jnp.dot(hb, w_main[:, D_MODEL:2 * D_MODEL], preferred_element_type=F32)
                   + b_main[:, D_MODEL:2 * D_MODEL])
    vn = _layer_norm(v, vg_ref[...], vb_ref[...])
    vrow_ref[...] = vn
    vn3 = vn.reshape(nb, ROW_TILE, D_MODEL)
    mixed = vn3 * coef_ref[0][None] + bias_ref[...][None]
    for j in range(1, DN_CONV):
        mixed = mixed + pltpu.roll(vn3, j, 1) * coef_ref[j][None]
    ya = (u * mixed.reshape(rows, D_MODEL)).astype(BF16)
    gate_a = _sigmoid(jnp.dot(hb, w_gates[:, :D_MODEL], preferred_element_type=F32)
                      + b_gates[:, :D_MODEL])
    a_part = gate_a * jnp.dot(ya, wa_ref[...], preferred_element_type=F32)

    qkv = jnp.dot(hb, w_main[:, 2 * D_MODEL:5 * D_MODEL], preferred_element_type=F32) \
        + b_main[:, 2 * D_MODEL:5 * D_MODEL]
    zfull = jnp.where(valid, qkv, 0.0) + cs_ref[...]
    z_out_ref[...] = zfull
    z3 = zfull.reshape(nb, ROW_TILE, QKV)
    acc = z3 * convw_ref[DN_CONV - 1:DN_CONV, :][None]
    for j in range(1, DN_CONV):
        acc = acc + pltpu.roll(z3, j, 1) * convw_ref[DN_CONV - 1 - j:DN_CONV - j, :][None]
    sact = _silu(acc.reshape(rows, QKV)) * validf
    z_act = _branch_gates_and_z(hb, w_main, b_main)
    beta, g_log = _beta_and_logdecay(hb, w_bd, b_bd, alog_ref, dtb_ref)
    beta = beta * validf
    g_log = g_log * validf
    rt = lax.broadcasted_iota(jnp.int32, (rows, rows), 0)
    ct = lax.broadcasted_iota(jnp.int32, (rows, rows), 1)
    cum = jnp.where(((rt // ROW_TILE) == (ct // ROW_TILE)) & (rt >= ct), 1.0, 0.0)
    g_cum = _dot_exact_lhs(cum, g_log)

    pad_rows = SUB - rows
    def pad(a):
        return jnp.concatenate([a, jnp.zeros((pad_rows, a.shape[1]), a.dtype)], axis=0)

    g_pad = pad(g_cum)
    g_t = g_pad.T
    beta_pad = pad(beta)
    masks = _block_masks(SUB, ROW_TILE)
    norm_w = normw_ref[...]
    scale = HEAD_DIM ** -0.5
    row_id = lax.broadcasted_iota(jnp.int32, (SUB, HEAD_DIM), 0)
    heads = range(HEADS)
    seqs = range(nb)
    sls = [slice(h * HEAD_DIM, (h + 1) * HEAD_DIM) for h in heads]
    q = [pad(_l2n(sact[:, sl]) * scale * validf) for sl in sls]
    k = [pad(_l2n(sact[:, D_MODEL + h * HEAD_DIM:D_MODEL + (h + 1) * HEAD_DIM]) * validf) for h in heads]
    vv = [pad(sact[:, 2 * D_MODEL + h * HEAD_DIM:2 * D_MODEL + (h + 1) * HEAD_DIM]) for h in heads]
    g_col = [jnp.broadcast_to(g_pad[:, h:h + 1], (SUB, HEAD_DIM)) for h in heads]
    g_row = [jnp.broadcast_to(g_t[h:h + 1, :], (SUB, SUB)) for h in heads]
    beta_h = [jnp.broadcast_to(beta_pad[:, h:h + 1], (SUB, HEAD_DIM)) for h in heads]
    u_h, w_h, qk, qe = _dn_intra(q, k, vv, beta_h, g_col, g_row, masks, 2)
    tiles = [slice(i * ROW_TILE, (i + 1) * ROW_TILE) for i in seqs]
    r = [[_dot(jnp.concatenate([w_h[h][rs], qe[h][rs]], axis=0), s_in_ref[i, h]) for i, rs in enumerate(tiles)]
         for h in heads]
    v_new = [pad(jnp.concatenate([u_h[h][rs] - r[h][i][:ROW_TILE] for i, rs in enumerate(tiles)], axis=0))
             for h in heads]
    qkv_new = [_dot(qk[h], v_new[h]) for h in heads]
    g_end = [jnp.broadcast_to(g_col[h].reshape(SUB // ROW_TILE, ROW_TILE, HEAD_DIM)[:, ROW_TILE - 1:, :],
                              (SUB // ROW_TILE, ROW_TILE, HEAD_DIM)).reshape(SUB, HEAD_DIM) for h in heads]
    k_dec_t = [(k[h] * jnp.exp(g_end[h] - g_col[h])).T for h in heads]
    seq_of_row = row_id // ROW_TILE
    for i in seqs:
        for h in heads:
            g_last = g_col[h][(i + 1) * ROW_TILE - 1:(i + 1) * ROW_TILE, :]
            s_out_ref[i, h] = (s_in_ref[i, h] * jnp.exp(g_last)
                               + _dot(k_dec_t[h], jnp.where(seq_of_row == i, v_new[h], 0.0)))
    ybs = []
    for h in heads:
        o = jnp.concatenate([r[h][i][ROW_TILE:] for i in seqs], axis=0) + qkv_new[h][:rows]
        ybs.append(_gated_rms(o, norm_w, z_act[:, sls[h]]).astype(BF16))
    yb = jnp.concatenate(ybs, axis=1)
    x2_ref[...] = _merge_out_ln(x, hb, a_part, yb, w_gates, b_gates, wb_ref, wo_ref,
                                ln_g, ln_b, alpha)


def _mix_sample(x1, cs_pad, s_in, p, alpha, nb):
    n = x1.shape[0]
    nseq = n // ROW_TILE
    assert nseq % nb == 0 and nb * ROW_TILE <= SUB
    rows = nb * ROW_TILE
    consts = [p['w_main'], p['b_main'], p['w_bd'], p['b_bd'], p['w_gates'], p['b_gates'],
              p['gm_v_g'], p['gm_v_b'], p['mix_coef'], p['mix_bias'], p['conv_w'], p['a_log'],
              p['dt_bias'], p['norm_w'], p['w_a'], p['w_b'], p['w_o'], p['ln2_g'], p['ln2_b']]
    state_spec = pl.BlockSpec((nb, HEADS, HEAD_DIM, HEAD_DIM), lambda i: (i, 0, 0, 0))
    return pl.pallas_call(
        functools.partial(_mix_sample_kernel, alpha=alpha, nb=nb),
        grid=(nseq // nb,),
        in_specs=[pl.BlockSpec((rows, D_MODEL), lambda i: (i, 0)),
                  pl.BlockSpec((rows, QKV), lambda i: (i, 0)),
                  state_spec] + [_const_spec(c.shape) for c in consts],
        out_specs=[pl.BlockSpec((rows, D_MODEL), lambda i: (i, 0)),
                   pl.BlockSpec((rows, D_MODEL), lambda i: (i, 0)),
                   pl.BlockSpec((rows, QKV), lambda i: (i, 0)),
                   state_spec],
        out_shape=[jax.ShapeDtypeStruct((n, D_MODEL), F32),
                   jax.ShapeDtypeStruct((n, D_MODEL), F32),
                   jax.ShapeDtypeStruct((n, QKV), F32),
                   jax.ShapeDtypeStruct(s_in.shape, F32)],
        compiler_params=pltpu.CompilerParams(dimension_semantics=("arbitrary",),
                                             vmem_limit_bytes=VMEM_LIMIT),
        name="mix_sample",
    )(x1, cs_pad, s_in, *consts)


def _pad_lanes(a, n=128):
    return jnp.pad(a, [(0, 0)] * (a.ndim - 1) + [(0, n - a.shape[-1])])


def _layer_params(l, ffn1_w_up, ffn1_w_down, ln1_g, ln1_b, w_in, b_in, gm_v_g, gm_v_b, gm_w_s,
                  gm_b_s, dn_conv_w, dn_a_log, dn_dt_bias, dn_norm_w, w_branch_a, w_branch_b,
                  w_out, ln2_g, ln2_b, ffn2_w_up, ffn2_w_down, ln3_g, ln3_b):
    row = lambda a: a[l][None, :].astype(F32)
    wi, bi = w_in[l], b_in[l]
    o_beta = MAIN_COLS
    o_dec = o_beta + HEADS
    o_gate = o_dec + HEADS
    ws = gm_w_s[l]
    lsm = DN_CONV
    coef = jnp.zeros((DN_CONV, ROW_TILE, GROUPS), F32)
    for j in range(lsm):
        for t in range(j, lsm):
            coef = coef.at[j, ROW_TILE - lsm + t, :].set(ws[:, t, t - j])
    bias = jnp.zeros((ROW_TILE, GROUPS), F32).at[ROW_TILE - lsm:, :].set(gm_b_s[l][:, :lsm].T)
    return {
        'ffn1': (ffn1_w_up[l][:, :D_FF].astype(BF16), ffn1_w_up[l][:, D_FF:].astype(BF16),
                 ffn1_w_down[l].astype(BF16), row(ln1_g), row(ln1_b)),
        'ffn2': (ffn2_w_up[l][:, :D_FF].astype(BF16), ffn2_w_up[l][:, D_FF:].astype(BF16),
                 ffn2_w_down[l].astype(BF16), row(ln3_g), row(ln3_b)),
        'w_main': wi[:, :MAIN_COLS].astype(BF16),
        'b_main': bi[None, :MAIN_COLS],
        'w_bd': jnp.concatenate([_pad_lanes(wi[:, o_beta:o_dec]), _pad_lanes(wi[:, o_dec:o_gate])],
                                axis=1).astype(BF16),
        'b_bd': jnp.concatenate([_pad_lanes(bi[None, o_beta:o_dec]), _pad_lanes(bi[None, o_dec:o_gate])],
                                axis=1),
        'w_gates': wi[:, o_gate:].astype(BF16),
        'b_gates': bi[None, o_gate:],
        'gm_v_g': row(gm_v_g), 'gm_v_b': row(gm_v_b),
        'gm_w_s': ws, 'gm_b_s_t': gm_b_s[l].T,
        'mix_coef': jnp.repeat(coef, GROUP_DIM, axis=-1), 'mix_bias': jnp.repeat(bias, GROUP_DIM, axis=-1),
        'conv_w': dn_conv_w[l],
        'a_log': _pad_lanes(dn_a_log[l][None, :].astype(F32)),
        'dt_bias': _pad_lanes(dn_dt_bias[l][None, :].astype(F32)),
        'norm_w': row(dn_norm_w),
        'w_a': w_branch_a[l].astype(BF16), 'w_b': w_branch_b[l].astype(BF16),
        'w_o': w_out[l].astype(BF16),
        'ln2_g': row(ln2_g), 'ln2_b': row(ln2_b),
    }


def kernel(x_prompt, x_sample, state_conv, state_ssm, ffn1_w_up, ffn1_w_down, ln1_g, ln1_b, w_in, b_in, gm_v_g, gm_v_b, gm_w_s, gm_b_s, dn_conv_w, dn_a_log, dn_dt_bias, dn_norm_w, w_branch_a, w_branch_b, w_out, ln2_g, ln2_b, ffn2_w_up, ffn2_w_down, ln3_g, ln3_b):
    depth = ffn1_w_up.shape[0]
    alpha = (2.0 * depth) ** 0.25
    bp, tp, _ = x_prompt.shape
    bs, ts, _ = x_sample.shape
    assert ts == DN_CONV and ts <= ROW_TILE - (DN_CONV - 1)
    lead = ROW_TILE - ts
    y_p = x_prompt
    y_s = jnp.pad(x_sample, ((0, 0), (lead, 0), (0, 0))).reshape(bs * ROW_TILE, D_MODEL)
    conv_p, ssm_p, conv_s, ssm_s, v_s = [], [], [], [], []
    for l in range(depth):
        p = _layer_params(l, ffn1_w_up, ffn1_w_down, ln1_g, ln1_b, w_in, b_in, gm_v_g, gm_v_b,
                          gm_w_s, gm_b_s, dn_conv_w, dn_a_log, dn_dt_bias, dn_norm_w, w_branch_a,
                          w_branch_b, w_out, ln2_g, ln2_b, ffn2_w_up, ffn2_w_down, ln3_g, ln3_b)
        x1 = _ffn_ln(y_p.reshape(bp * tp, D_MODEL), *p['ffn1'], alpha, 512).reshape(bp, tp, D_MODEL)
        x2, c_p, s_p = _mix_prompt(x1, p, alpha, 256)
        y_p = _ffn_ln(x2.reshape(bp * tp, D_MODEL), *p['ffn2'], alpha, 512).reshape(bp, tp, D_MODEL)
        x1s = _ffn_ln(y_s, *p['ffn1'], alpha, 512)
        cs_pad = jnp.pad(state_conv[l], ((0, 0), (lead - (DN_CONV - 1), ts), (0, 0)))
        x2s, vrows, zfull, s_s = _mix_sample(x1s, cs_pad.reshape(bs * ROW_TILE, QKV),
                                             state_ssm[l], p, alpha, 8)
        y_s = _ffn_ln(x2s, *p['ffn2'], alpha, 512)
        conv_p.append(c_p)
        ssm_p.append(s_p)
        conv_s.append(zfull.reshape(bs, ROW_TILE, QKV)[:, ROW_TILE - (DN_CONV - 1):])
        ssm_s.append(s_s)
        v_s.append(vrows.reshape(bs, ROW_TILE, D_MODEL)[:, lead:])
    y_s_out = y_s.reshape(bs, ROW_TILE, D_MODEL)[:, lead:]
    return (y_p, y_s_out, jnp.stack(conv_p), jnp.stack(ssm_p), jnp.stack(conv_s), jnp.stack(ssm_s),
            jnp.stack(v_s))
```

```python
import functools
import math

import jax
import jax.numpy as jnp
import numpy as np
from jax import lax
from jax.experimental import pallas as pl
from jax.experimental.pallas import tpu as pltpu

F32 = jnp.float32
BF16 = jnp.bfloat16

D_MODEL = 1024
D_FF = 2816
HEADS = 8
HEAD_DIM = 128
GROUPS = 8
GROUP_DIM = 128
GM_CHUNK = 128
DN_CHUNK = 64
DN_CONV = 4
QKV = 3 * D_MODEL
MAIN_COLS = 6 * D_MODEL
LN_EPS = 1e-5
RMS_EPS = 1e-6

MXU_DIM = 256
SUB = 128
ROW_TILE = 8
LANES = 128
VMEM_LIMIT = 56 * 1024 * 1024


def _sigmoid(x):
    return 0.5 * jnp.tanh(0.5 * x) + 0.5


def _silu(x):
    return x * _sigmoid(x)


def _gelu_tanh(x):
    c = math.sqrt(2.0 / math.pi)
    return x * (0.5 * (1.0 + jnp.tanh(c * (x + 0.044715 * (x * x * x)))))


def _softplus(x):
    return jnp.maximum(x, 0.0) + jnp.log(1.0 + jnp.exp(-jnp.abs(x)))


def _layer_norm(y, g, b):
    mu = jnp.mean(y, axis=-1, keepdims=True)
    yc = y - mu
    var = jnp.mean(yc * yc, axis=-1, keepdims=True)
    return yc * lax.rsqrt(var + LN_EPS) * g + b


def _dot(a, b):
    return jnp.dot(a.astype(BF16), b.astype(BF16), preferred_element_type=F32)


def _dot_nt(a, b):
    return lax.dot_general(a.astype(BF16), b.astype(BF16), (((1,), (1,)), ((), ())),
                           preferred_element_type=F32)


def _dot_exact_lhs(m01, x):
    hi = x.astype(BF16)
    r1 = x - hi.astype(F32)
    mid = r1.astype(BF16)
    lo = (r1 - mid.astype(F32)).astype(BF16)
    m = m01.astype(BF16)
    return (jnp.dot(m, hi, preferred_element_type=F32)
            + jnp.dot(m, mid, preferred_element_type=F32)
            + jnp.dot(m, lo, preferred_element_type=F32))


def _block_masks(n, blk):
    row = lax.broadcasted_iota(jnp.int32, (n, n), 0)
    col = lax.broadcasted_iota(jnp.int32, (n, n), 1)
    same = (row // blk) == (col // blk)
    return same & (row >= col), same & (row > col), row == col


def _inv_unit_lower(a, eye, n_iter):
    n = eye.shape[0]
    b = [-x for x in a]
    p = [eye + x for x in b]
    b = [_dot(x, x) for x in b]
    for _ in range(n_iter - 1):
        pb = [_dot(jnp.concatenate([pi, bi], axis=0), bi) for pi, bi in zip(p, b)]
        p = [pi + x[:n] for pi, x in zip(p, pb)]
        b = [x[n:] for x in pb]
    return [pi + _dot(pi, bi) for pi, bi in zip(p, b)]


def _dn_intra(q, k, v, beta, g_col, g_row, masks, n_iter):
    causal, strict, diag = masks
    heads = range(len(q))
    decay = [jnp.where(causal, jnp.exp(jnp.where(causal, g_col[h] - g_row[h], 0.0)), 0.0) for h in heads]
    kb = [k[h] * beta[h] for h in heads]
    kq = [_dot_nt(jnp.concatenate([kb[h], q[h]], axis=0), k[h]) for h in heads]
    a = [jnp.where(strict, kq[h][:SUB] * decay[h], 0.0) for h in heads]
    qk = [kq[h][SUB:] * decay[h] for h in heads]
    eye = jnp.where(diag, 1.0, 0.0).astype(F32)
    t_inv = _inv_unit_lower(a, eye, n_iter)
    e_g = [jnp.exp(g_col[h]) for h in heads]
    uw = [_dot(t_inv[h], jnp.concatenate([v[h] * beta[h], kb[h] * e_g[h]], axis=1)) for h in heads]
    return ([x[:, :HEAD_DIM] for x in uw], [x[:, HEAD_DIM:] for x in uw], qk,
            [q[h] * e_g[h] for h in heads])


def _gated_rms(o, norm_w, z_act):
    return o * lax.rsqrt(jnp.mean(o * o, axis=-1, keepdims=True) + RMS_EPS) * norm_w * z_act


def _l2n(x):
    return x * lax.rsqrt(jnp.sum(x * x, axis=-1, keepdims=True) + RMS_EPS)


def _ffn_chunks():
    n_tiles = D_FF // MXU_DIM
    first = (n_tiles + 1) // 2 * MXU_DIM
    return ((0, first), (first, D_FF))


def _ffn_ln_kernel(x_ref, wu_ref, wd_ref, g_ref, b_ref, o_ref, *, alpha):
    x = x_ref[...]
    xb = x.astype(BF16)
    acc = None
    for lo, hi in _ffn_chunks():
        a = jnp.dot(xb, wu_ref[:, lo:hi], preferred_element_type=F32)
        gt = jnp.dot(xb, wu_ref[:, D_FF + lo:D_FF + hi], preferred_element_type=F32)
        h = (_silu(a) * gt).astype(BF16)
        f = jnp.dot(h, wd_ref[lo:hi, :], preferred_element_type=F32)
        acc = f if acc is None else acc + f
    o_ref[...] = _layer_norm(alpha * x + 0.5 * acc, g_ref[...], b_ref[...])


def _const_spec(shape):
    nd = len(shape)
    return pl.BlockSpec(shape, lambda *_: (0,) * nd, pipeline_mode=pl.Buffered(1))


def _ffn_ln(x2d, wu, wd, g, b, alpha, tm):
    n = x2d.shape[0]
    assert n % tm == 0 and D_FF % MXU_DIM == 0
    return pl.pallas_call(
        functools.partial(_ffn_ln_kernel, alpha=alpha),
        grid=(n // tm,),
        in_specs=[pl.BlockSpec((tm, D_MODEL), lambda i: (i, 0)),
                  _const_spec(wu.shape), _const_spec(wd.shape),
                  _const_spec(g.shape), _const_spec(b.shape)],
        out_specs=pl.BlockSpec((tm, D_MODEL), lambda i: (i, 0)),
        out_shape=jax.ShapeDtypeStruct((n, D_MODEL), F32),
        compiler_params=pltpu.CompilerParams(dimension_semantics=("arbitrary",),
                                             vmem_limit_bytes=VMEM_LIMIT),
        name="ffn_ln",
    )(x2d, wu, wd, g, b)


def _proj_views(w_ref, b_ref):
    g0, g1 = MAIN_COLS, MAIN_COLS + 2 * D_MODEL
    return (w_ref.at[:, 0:g0], w_ref.at[:, g0:g1], w_ref.at[:, g1:g1 + 2 * LANES],
            b_ref.at[:, 0:g0], b_ref.at[:, g0:g1], b_ref.at[:, g1:g1 + 2 * LANES])


def _branch_gates_and_z(hb, w_main, b_main):
    z = jnp.dot(hb, w_main[:, 5 * D_MODEL:6 * D_MODEL], preferred_element_type=F32) \
        + b_main[:, 5 * D_MODEL:6 * D_MODEL]
    return _silu(z)


def _beta_and_logdecay(hb, w_bd, b_bd, alog, dtb):
    bd = jnp.dot(hb, w_bd[...], preferred_element_type=F32) + b_bd[...]
    beta = _sigmoid(bd[:, :128])
    g = -jnp.exp(alog[...]) * _softplus(bd[:, 128:] + dtb[...])
    return beta, g


def _merge_out_ln(x, hb, a_part, yb, w_gates, b_gates, wb_ref, wo_ref, ln_g, ln_b, alpha):
    gate_b = _sigmoid(jnp.dot(hb, w_gates[:, D_MODEL:], preferred_element_type=F32)
                      + b_gates[:, D_MODEL:])
    merged = a_part + gate_b * jnp.dot(yb, wb_ref[...], preferred_element_type=F32)
    mix = jnp.dot(merged.astype(BF16), wo_ref[...], preferred_element_type=F32)
    return _layer_norm(alpha * x + mix, ln_g[...], ln_b[...])


def _mix_prompt_kernel(x_ref, w_ref, b_ref, vg_ref, vb_ref,
                       ws_ref, bst_ref, convw_ref, alog_ref, dtb_ref, normw_ref,
                       wa_ref, wb_ref, wo_ref, ln_g, ln_b,
                       x2_ref, conv_out_ref, ssm_out_ref,
                       s_ref, xc_ref, q_s, k_s, v_s, z_s, g_s, beta_s, yb_s, *, alpha, tt):
    t = pl.program_id(1)
    nt = pl.num_programs(1)
    w_main, w_gates, w_bd, b_main, b_gates, b_bd = _proj_views(w_ref, b_ref)

    @pl.when(t == 0)
    def _():
        s_ref[...] = jnp.zeros(s_ref.shape, F32)
        xc_ref[...] = jnp.zeros((ROW_TILE, QKV), F32)

    x = x_ref[...]
    hb = x.astype(BF16)

    u = _gelu_tanh(jnp.dot(hb, w_main[:, 0:D_MODEL], preferred_element_type=F32)
                   + b_main[:, 0:D_MODEL])
    v = _gelu_tanh(jnp.dot(hb, w_main[:, D_MODEL:2 * D_MODEL], preferred_element_type=F32)
                   + b_main[:, D_MODEL:2 * D_MODEL])
    vn = _layer_norm(v, vg_ref[...], vb_ref[...]).astype(BF16)
    r128 = lax.broadcasted_iota(jnp.int32, (GM_CHUNK, GM_CHUNK), 0)
    c128 = lax.broadcasted_iota(jnp.int32, (GM_CHUNK, GM_CHUNK), 1)
    tril = r128 >= c128
    w_tril = [jnp.where(tril, ws_ref[g], 0.0).astype(BF16) for g in range(GROUPS)]
    rows = []
    for c in range(tt // GM_CHUNK):
        cols = []
        for g in range(GROUPS):
            blk = vn[c * GM_CHUNK:(c + 1) * GM_CHUNK, g * GROUP_DIM:(g + 1) * GROUP_DIM]
            cols.append(jnp.dot(w_tril[g], blk, preferred_element_type=F32) + bst_ref[:, g:g + 1])
        rows.append(jnp.concatenate(cols, axis=1))
    mixed = jnp.concatenate(rows, axis=0)
    ya = (u * mixed).astype(BF16)
    gate_a = _sigmoid(jnp.dot(hb, w_gates[:, :D_MODEL], preferred_element_type=F32)
                      + b_gates[:, :D_MODEL])
    a_part = gate_a * jnp.dot(ya, wa_ref[...], preferred_element_type=F32)

    qkv = (jnp.dot(hb, w_main[:, 2 * D_MODEL:5 * D_MODEL], preferred_element_type=F32)
           + b_main[:, 2 * D_MODEL:5 * D_MODEL])
    xfull = jnp.concatenate([xc_ref[...], qkv], axis=0)
    acc = qkv * convw_ref[DN_CONV - 1:DN_CONV, :]
    for j in range(1, DN_CONV):
        acc = acc + pltpu.roll(xfull, j, 0)[ROW_TILE:] * convw_ref[DN_CONV - 1 - j:DN_CONV - j, :]

    @pl.when(t == nt - 1)
    def _():
        conv_out_ref[...] = qkv[tt - (DN_CONV - 1):, :]

    xc_ref[...] = qkv[tt - ROW_TILE:, :]
    sact = _silu(acc)
    scale = HEAD_DIM ** -0.5
    for h in range(HEADS):
        sl = slice(h * HEAD_DIM, (h + 1) * HEAD_DIM)
        q_s[:, sl] = _l2n(sact[:, sl]) * scale
        k_s[:, sl] = _l2n(sact[:, D_MODEL + h * HEAD_DIM:D_MODEL + (h + 1) * HEAD_DIM])
    v_s[...] = sact[:, 2 * D_MODEL:]
    z_s[...] = _branch_gates_and_z(hb, w_main, b_main)
    beta, g_log = _beta_and_logdecay(hb, w_bd, b_bd, alog_ref, dtb_ref)
    beta_s[...] = beta
    rt = lax.broadcasted_iota(jnp.int32, (tt, tt), 0)
    ct = lax.broadcasted_iota(jnp.int32, (tt, tt), 1)
    cum = jnp.where(((rt // DN_CHUNK) == (ct // DN_CHUNK)) & (rt >= ct), 1.0, 0.0)
    g_s[...] = _dot_exact_lhs(cum, g_log)

    masks = _block_masks(SUB, DN_CHUNK)
    norm_w = normw_ref[...]

    def sub_body(s, carry):
        r0 = pl.multiple_of(s * SUB, SUB)
        g_sub = g_s[pl.ds(r0, SUB), :]
        g_t = g_sub.T
        b_sub = beta_s[pl.ds(r0, SUB), :]
        heads = range(HEADS)
        sls = [slice(h * HEAD_DIM, (h + 1) * HEAD_DIM) for h in heads]
        q = [q_s[pl.ds(r0, SUB), sl] for sl in sls]
        k = [k_s[pl.ds(r0, SUB), sl] for sl in sls]
        vv = [v_s[pl.ds(r0, SUB), sl] for sl in sls]
        z_act = [z_s[pl.ds(r0, SUB), sl] for sl in sls]
        state = [s_ref[h] for h in heads]
        g_col = [jnp.broadcast_to(g_sub[:, h:h + 1], (SUB, HEAD_DIM)) for h in heads]
        g_row = [jnp.broadcast_to(g_t[h:h + 1, :], (SUB, SUB)) for h in heads]
        beta_h = [jnp.broadcast_to(b_sub[:, h:h + 1], (SUB, HEAD_DIM)) for h in heads]
        u_h, w_h, qk, qe = _dn_intra(q, k, vv, beta_h, g_col, g_row, masks, 5)
        n_chunks = SUB // DN_CHUNK
        g_last = [[g_col[h][(c + 1) * DN_CHUNK - 1:(c + 1) * DN_CHUNK, :] for c in range(n_chunks)]
                  for h in heads]
        row_chunk = lax.broadcasted_iota(jnp.int32, (SUB, HEAD_DIM), 0) // DN_CHUNK
        k_dec_t = []
        for h in heads:
            g_end = g_last[h][n_chunks - 1]
            for c in range(n_chunks - 2, -1, -1):
                g_end = jnp.where(row_chunk == c, g_last[h][c], g_end)
            k_dec_t.append((k[h] * jnp.exp(g_end - g_col[h])).T)
        zeros = jnp.zeros((DN_CHUNK, HEAD_DIM), F32)
        outs = [[] for _ in heads]
        for c in range(n_chunks):
            rs = slice(c * DN_CHUNK, (c + 1) * DN_CHUNK)
            r = [_dot(jnp.concatenate([w_h[h][rs], qe[h][rs]], axis=0), state[h]) for h in heads]
            v_new = [u_h[h][rs] - r[h][:DN_CHUNK] for h in heads]
            v_pad = [jnp.concatenate([zeros] * c + [v_new[h]] + [zeros] * (n_chunks - 1 - c), axis=0)
                     for h in heads]
            m = [_dot(jnp.concatenate([qk[h][rs], k_dec_t[h]], axis=0), v_pad[h]) for h in heads]
            for h in heads:
                outs[h].append(r[h][DN_CHUNK:] + m[h][:DN_CHUNK])
            state = [state[h] * jnp.exp(g_last[h][c]) + m[h][DN_CHUNK:] for h in heads]
        for h in heads:
            s_ref[h] = state[h]
            o = jnp.concatenate(outs[h], axis=0)
            yb_s[pl.ds(r0, SUB), sls[h]] = _gated_rms(o, norm_w, z_act[h]).astype(BF16)
        return carry

    lax.fori_loop(0, tt // SUB, sub_body, 0)

    @pl.when(t == nt - 1)
    def _():
        ssm_out_ref[...] = s_ref[...]

    x2_ref[...] = _merge_out_ln(x, hb, a_part, yb_s[...], w_gates, b_gates, wb_ref, wo_ref,
                                ln_g, ln_b, alpha)


def _mix_prompt(x1, p, alpha, tt):
    b, t, _ = x1.shape
    assert t % tt == 0 and tt % SUB == 0
    consts = [p['w_proj'], p['b_proj'],
              p['gm_v_g'], p['gm_v_b'], p['gm_w_s'], p['gm_b_s_t'], p['conv_w'], p['a_log'],
              p['dt_bias'], p['norm_w'], p['w_a'], p['w_b'], p['w_o'], p['ln2_g'], p['ln2_b']]
    return pl.pallas_call(
        functools.partial(_mix_prompt_kernel, alpha=alpha, tt=tt),
        grid=(b, t // tt),
        in_specs=[pl.BlockSpec((None, tt, D_MODEL), lambda i, j: (i, j, 0))]
                 + [_const_spec(c.shape) for c in consts],
        out_specs=[pl.BlockSpec((None, tt, D_MODEL), lambda i, j: (i, j, 0)),
                   pl.BlockSpec((None, DN_CONV - 1, QKV), lambda i, j: (i, 0, 0)),
                   pl.BlockSpec((None, HEADS, HEAD_DIM, HEAD_DIM), lambda i, j: (i, 0, 0, 0))],
        out_shape=[jax.ShapeDtypeStruct((b, t, D_MODEL), F32),
                   jax.ShapeDtypeStruct((b, DN_CONV - 1, QKV), F32),
                   jax.ShapeDtypeStruct((b, HEADS, HEAD_DIM, HEAD_DIM), F32)],
        scratch_shapes=[pltpu.VMEM((HEADS, HEAD_DIM, HEAD_DIM), F32),
                        pltpu.VMEM((ROW_TILE, QKV), F32),
                        pltpu.VMEM((tt, D_MODEL), F32),
                        pltpu.VMEM((tt, D_MODEL), F32),
                        pltpu.VMEM((tt, D_MODEL), F32),
                        pltpu.VMEM((tt, D_MODEL), F32),
                        pltpu.VMEM((tt, 128), F32),
                        pltpu.VMEM((tt, 128), F32),
                        pltpu.VMEM((tt, D_MODEL), BF16)],
        compiler_params=pltpu.CompilerParams(dimension_semantics=("arbitrary", "arbitrary"),
                                             vmem_limit_bytes=VMEM_LIMIT),
        name="mix_prompt",
    )(x1, *consts)


def _mix_sample_kernel(x_ref, cs_ref, s_in_ref, w_ref, b_ref,
                       vg_ref, vb_ref, coef_ref, bias_ref, convw_ref, alog_ref, dtb_ref, normw_ref,
                       wa_ref, wb_ref, wo_ref, ln_g, ln_b,
                       x2_ref, vrow_ref, z_out_ref, s_out_ref, *, alpha, nb):
    rows = nb * ROW_TILE
    w_main, w_gates, w_bd, b_main, b_gates, b_bd = _proj_views(w_ref, b_ref)
    x = x_ref[...]
    hb = x.astype(BF16)
    valid = (lax.broadcasted_iota(jnp.int32, (rows, 1), 0) % ROW_TILE) >= (ROW_TILE - DN_CONV)
    validf = jnp.where(valid, 1.0, 0.0).astype(F32)

    u = _gelu_tanh(jnp.dot(hb, w_main[:, 0:D_MODEL], preferred_element_type=F32)
                   + b_main[:, 0:D_MODEL])
    v = _gelu_tanh(jnp.dot(hb, w_main[:, D_MODEL:2 * D_MODEL], preferred_element_type=F32)
                   + b_main[:, D_MODEL:2 * D_MODEL])
    vn = _layer_norm(v, vg_ref[...], vb_ref[...])
    vrow_ref[...] = vn
    vn3 = vn.reshape(nb, ROW_TILE, D_MODEL)
    mixed = vn3 * coef_ref[0][None] + bias_ref[...][None]
    for j in range(1, DN_CONV):
        mixed = mixed + pltpu.roll(vn3, j, 1) * coef_ref[j][None]
    ya = (u * mixed.reshape(rows, D_MODEL)).astype(BF16)
    gate_a = _sigmoid(jnp.dot(hb, w_gates[:, :D_MODEL], preferred_element_type=F32)
                      + b_gates[:, :D_MODEL])
    a_part = gate_a * jnp.dot(ya, wa_ref[...], preferred_element_type=F32)

    qkv = jnp.dot(hb, w_main[:, 2 * D_MODEL:5 * D_MODEL], preferred_element_type=F32) \
        + b_main[:, 2 * D_MODEL:5 * D_MODEL]
    zfull = jnp.where(valid, qkv, 0.0) + cs_ref[...]
    z_out_ref[...] = zfull
    z3 = zfull.reshape(nb, ROW_TILE, QKV)
    acc = z3 * convw_ref[DN_CONV - 1:DN_CONV, :][None]
    for j in range(1, DN_CONV):
        acc = acc + pltpu.roll(z3, j, 1) * convw_ref[DN_CONV - 1 - j:DN_CONV - j, :][None]
    sact = _silu(acc.reshape(rows, QKV)) * validf
    z_act = _branch_gates_and_z(hb, w_main, b_main)
    beta, g_log = _beta_and_logdecay(hb, w_bd, b_bd, alog_ref, dtb_ref)
    beta = beta * validf
    g_log = g_log * validf
    rt = lax.broadcasted_iota(jnp.int32, (rows, rows), 0)
    ct = lax.broadcasted_iota(jnp.int32, (rows, rows), 1)
    cum = jnp.where(((rt // ROW_TILE) == (ct // ROW_TILE)) & (rt >= ct), 1.0, 0.0)
    g_cum = _dot_exact_lhs(cum, g_log)

    pad_rows = SUB - rows
    def pad(a):
        return jnp.concatenate([a, jnp.zeros((pad_rows, a.shape[1]), a.dtype)], axis=0)

    g_pad = pad(g_cum)
    g_t = g_pad.T
    beta_pad = pad(beta)
    masks = _block_masks(SUB, ROW_TILE)
    norm_w = normw_ref[...]
    scale = HEAD_DIM ** -0.5
    row_id = lax.broadcasted_iota(jnp.int32, (SUB, HEAD_DIM), 0)
    heads = range(HEADS)
    seqs = range(nb)
    sls = [slice(h * HEAD_DIM, (h + 1) * HEAD_DIM) for h in heads]
    q = [pad(_l2n(sact[:, sl]) * scale * validf) for sl in sls]
    k = [pad(_l2n(sact[:, D_MODEL + h * HEAD_DIM:D_MODEL + (h + 1) * HEAD_DIM]) * validf) for h in heads]
    vv = [pad(sact[:, 2 * D_MODEL + h * HEAD_DIM:2 * D_MODEL + (h + 1) * HEAD_DIM]) for h in heads]
    g_col = [jnp.broadcast_to(g_pad[:, h:h + 1], (SUB, HEAD_DIM)) for h in heads]
    g_row = [jnp.broadcast_to(g_t[h:h + 1, :], (SUB, SUB)) for h in heads]
    beta_h = [jnp.broadcast_to(beta_pad[:, h:h + 1], (SUB, HEAD_DIM)) for h in heads]
    u_h, w_h, qk, qe = _dn_intra(q, k, vv, beta_h, g_col, g_row, masks, 2)
    tiles = [slice(i * ROW_TILE, (i + 1) * ROW_TILE) for i in seqs]
    r = [[_dot(jnp.concatenate([w_h[h][rs], qe[h][rs]], axis=0), s_in_ref[i, h]) for i, rs in enumerate(tiles)]
         for h in heads]
    v_new = [pad(jnp.concatenate([u_h[h][rs] - r[h][i][:ROW_TILE] for i, rs in enumerate(tiles)], axis=0))
             for h in heads]
    qkv_new = [_dot(qk[h], v_new[h]) for h in heads]
    g_end = [jnp.broadcast_to(g_col[h].reshape(SUB // ROW_TILE, ROW_TILE, HEAD_DIM)[:, ROW_TILE - 1:, :],
                              (SUB // ROW_TILE, ROW_TILE, HEAD_DIM)).reshape(SUB, HEAD_DIM) for h in heads]
    k_dec_t = [(k[h] * jnp.exp(g_end[h] - g_col[h])).T for h in heads]
    seq_of_row = row_id // ROW_TILE
    for i in seqs:
        for h in heads:
            g_last = g_col[h][(i + 1) * ROW_TILE - 1:(i + 1) * ROW_TILE, :]
            s_out_ref[i, h] = (s_in_ref[i, h] * jnp.exp(g_last)
                               + _dot(k_dec_t[h], jnp.where(seq_of_row == i, v_new[h], 0.0)))
    ybs = []
    for h in heads:
        o = jnp.concatenate([r[h][i][ROW_TILE:] for i in seqs], axis=0) + qkv_new[h][:rows]
        ybs.append(_gated_rms(o, norm_w, z_act[:, sls[h]]).astype(BF16))
    yb = jnp.concatenate(ybs, axis=1)
    x2_ref[...] = _merge_out_ln(x, hb, a_part, yb, w_gates, b_gates, wb_ref, wo_ref,
                                ln_g, ln_b, alpha)


def _mix_sample(x1, cs_pad, s_in, p, alpha, nb):
    n = x1.shape[0]
    nseq = n // ROW_TILE
    assert nseq % nb == 0 and nb * ROW_TILE <= SUB
    rows = nb * ROW_TILE
    consts = [p['w_proj'], p['b_proj'],
              p['gm_v_g'], p['gm_v_b'], p['mix_coef'], p['mix_bias'], p['conv_w'], p['a_log'],
              p['dt_bias'], p['norm_w'], p['w_a'], p['w_b'], p['w_o'], p['ln2_g'], p['ln2_b']]
    state_spec = pl.BlockSpec((nb, HEADS, HEAD_DIM, HEAD_DIM), lambda i: (i, 0, 0, 0))
    return pl.pallas_call(
        functools.partial(_mix_sample_kernel, alpha=alpha, nb=nb),
        grid=(nseq // nb,),
        in_specs=[pl.BlockSpec((rows, D_MODEL), lambda i: (i, 0)),
                  pl.BlockSpec((rows, QKV), lambda i: (i, 0)),
                  state_spec] + [_const_spec(c.shape) for c in consts],
        out_specs=[pl.BlockSpec((rows, D_MODEL), lambda i: (i, 0)),
                   pl.BlockSpec((rows, D_MODEL), lambda i: (i, 0)),
                   pl.BlockSpec((rows, QKV), lambda i: (i, 0)),
                   state_spec],
        out_shape=[jax.ShapeDtypeStruct((n, D_MODEL), F32),
                   jax.ShapeDtypeStruct((n, D_MODEL), F32),
                   jax.ShapeDtypeStruct((n, QKV), F32),
                   jax.ShapeDtypeStruct(s_in.shape, F32)],
        compiler_params=pltpu.CompilerParams(dimension_semantics=("arbitrary",),
                                             vmem_limit_bytes=VMEM_LIMIT),
        name="mix_sample",
    )(x1, cs_pad, s_in, *consts)


def _pad_lanes(a, n=128):
    return jnp.pad(a, [(0, 0)] * (a.ndim - 1) + [(0, n - a.shape[-1])])


def _layer_params(l, ffn1_w_up, ffn1_w_down, ln1_g, ln1_b, w_in, b_in, gm_v_g, gm_v_b, gm_w_s,
                  gm_b_s, dn_conv_w, dn_a_log, dn_dt_bias, dn_norm_w, w_branch_a, w_branch_b,
                  w_out, ln2_g, ln2_b, ffn2_w_up, ffn2_w_down, ln3_g, ln3_b):
    row = lambda a: a[l][None, :].astype(F32)
    wi, bi = w_in[l], b_in[l]
    o_beta = MAIN_COLS
    o_dec = o_beta + HEADS
    o_gate = o_dec + HEADS
    ws = gm_w_s[l]
    lsm = DN_CONV
    shift = np.arange(lsm)[:, None]
    pos = np.arange(ROW_TILE)[None, :] - (ROW_TILE - lsm)
    live = (pos >= shift)
    coef = jnp.where(live[:, :, None],
                     jnp.transpose(ws[:, np.clip(pos + 0 * shift, 0, lsm - 1),
                                      np.clip(pos - shift, 0, lsm - 1)], (1, 2, 0)), 0.0)
    bias = jnp.pad(gm_b_s[l][:, :lsm].T, ((ROW_TILE - lsm, 0), (0, 0)))
    return {
        'ffn1': (ffn1_w_up[l].astype(BF16), ffn1_w_down[l].astype(BF16), row(ln1_g), row(ln1_b)),
        'ffn2': (ffn2_w_up[l].astype(BF16), ffn2_w_down[l].astype(BF16), row(ln3_g), row(ln3_b)),
        'w_proj': jnp.concatenate([wi[:, :MAIN_COLS], wi[:, o_gate:], _pad_lanes(wi[:, o_beta:o_dec]),
                                   _pad_lanes(wi[:, o_dec:o_gate])], axis=1).astype(BF16),
        'b_proj': jnp.concatenate([bi[:MAIN_COLS], bi[o_gate:], _pad_lanes(bi[o_beta:o_dec]),
                                   _pad_lanes(bi[o_dec:o_gate])])[None, :],
        'gm_v_g': row(gm_v_g), 'gm_v_b': row(gm_v_b),
        'gm_w_s': ws, 'gm_b_s_t': gm_b_s[l].T,
        'mix_coef': jnp.repeat(coef, GROUP_DIM, axis=-1), 'mix_bias': jnp.repeat(bias, GROUP_DIM, axis=-1),
        'conv_w': dn_conv_w[l],
        'a_log': _pad_lanes(dn_a_log[l][None, :].astype(F32)),
        'dt_bias': _pad_lanes(dn_dt_bias[l][None, :].astype(F32)),
        'norm_w': row(dn_norm_w),
        'w_a': w_branch_a[l].astype(BF16), 'w_b': w_branch_b[l].astype(BF16),
        'w_o': w_out[l].astype(BF16),
        'ln2_g': row(ln2_g), 'ln2_b': row(ln2_b),
    }


def kernel(x_prompt, x_sample, state_conv, state_ssm, ffn1_w_up, ffn1_w_down, ln1_g, ln1_b, w_in, b_in, gm_v_g, gm_v_b, gm_w_s, gm_b_s, dn_conv_w, dn_a_log, dn_dt_bias, dn_norm_w, w_branch_a, w_branch_b, w_out, ln2_g, ln2_b, ffn2_w_up, ffn2_w_down, ln3_g, ln3_b):
    depth = ffn1_w_up.shape[0]
    alpha = (2.0 * depth) ** 0.25
    bp, tp, _ = x_prompt.shape
    bs, ts, _ = x_sample.shape
    assert ts == DN_CONV and ts <= ROW_TILE - (DN_CONV - 1)
    lead = ROW_TILE - ts
    y_p = x_prompt
    y_s = jnp.pad(x_sample, ((0, 0), (lead, 0), (0, 0))).reshape(bs * ROW_TILE, D_MODEL)
    conv_p, ssm_p, conv_s, ssm_s, v_s = [], [], [], [], []
    for l in range(depth):
        p = _layer_params(l, ffn1_w_up, ffn1_w_down, ln1_g, ln1_b, w_in, b_in, gm_v_g, gm_v_b,
                          gm_w_s, gm_b_s, dn_conv_w, dn_a_log, dn_dt_bias, dn_norm_w, w_branch_a,
                          w_branch_b, w_out, ln2_g, ln2_b, ffn2_w_up, ffn2_w_down, ln3_g, ln3_b)
        x1 = _ffn_ln(y_p.reshape(bp * tp, D_MODEL), *p['ffn1'], alpha, 512).reshape(bp, tp, D_MODEL)
        x2, c_p, s_p = _mix_prompt(x1, p, alpha, 256)
        y_p = _ffn_ln(x2.reshape(bp * tp, D_MODEL), *p['ffn2'], alpha, 512).reshape(bp, tp, D_MODEL)
        x1s = _ffn_ln(y_s, *p['ffn1'], alpha, 512)
        cs_pad = jnp.pad(state_conv[l], ((0, 0), (lead - (DN_CONV - 1), ts), (0, 0)))
        x2s, vrows, zfull, s_s = _mix_sample(x1s, cs_pad.reshape(bs * ROW_TILE, QKV),
                                             state_ssm[l], p, alpha, 8)
        y_s = _ffn_ln(x2s, *p['ffn2'], alpha, 512)
        conv_p.append(c_p)
        ssm_p.append(s_p)
        conv_s.append(zfull.reshape(bs, ROW_TILE, QKV)[:, ROW_TILE - (DN_CONV - 1):])
        ssm_s.append(s_s)
        v_s.append(vrows.reshape(bs, ROW_TILE, D_MODEL)[:, lead:])
    y_s_out = y_s.reshape(bs, ROW_TILE, D_MODEL)[:, lead:]
    return (y_p, y_s_out, jnp.stack(conv_p), jnp.stack(ssm_p), jnp.stack(conv_s), jnp.stack(ssm_s),
            jnp.stack(v_s))
```

```python
import functools
import math

import jax
import jax.numpy as jnp
import numpy as np
from jax import lax
from jax.experimental import pallas as pl
from jax.experimental.pallas import tpu as pltpu

F32 = jnp.float32
BF16 = jnp.bfloat16

D_MODEL = 1024
D_FF = 2816
HEADS = 8
HEAD_DIM = 128
GROUPS = 8
GROUP_DIM = 128
GM_CHUNK = 128
DN_CHUNK = 64
DN_CONV = 4
QKV = 3 * D_MODEL
MAIN_COLS = 6 * D_MODEL
LN_EPS = 1e-5
RMS_EPS = 1e-6

MXU_DIM = 256
SUB = 128
ROW_TILE = 8
LANES = 128
SLAB = 16
VMEM_LIMIT = 56 * 1024 * 1024


def _sigmoid(x):
    return 0.5 * jnp.tanh(0.5 * x) + 0.5


def _silu(x):
    return x * _sigmoid(x)


def _gelu_tanh(x):
    c = math.sqrt(2.0 / math.pi)
    return x * (0.5 * (1.0 + jnp.tanh(c * (x + 0.044715 * (x * x * x)))))


def _softplus(x):
    return jnp.maximum(x, 0.0) + jnp.log(1.0 + jnp.exp(-jnp.abs(x)))


def _layer_norm(y, g, b):
    mu = jnp.mean(y, axis=-1, keepdims=True)
    yc = y - mu
    var = jnp.mean(yc * yc, axis=-1, keepdims=True)
    return yc * lax.rsqrt(var + LN_EPS) * g + b


def _dot(a, b):
    return jnp.dot(a.astype(BF16), b.astype(BF16), preferred_element_type=F32)


def _dot_nt(a, b):
    return lax.dot_general(a.astype(BF16), b.astype(BF16), (((1,), (1,)), ((), ())),
                           preferred_element_type=F32)


def _dot_exact_lhs(m01, x):
    hi = x.astype(BF16)
    r1 = x - hi.astype(F32)
    mid = r1.astype(BF16)
    lo = (r1 - mid.astype(F32)).astype(BF16)
    m = m01.astype(BF16)
    return (jnp.dot(m, hi, preferred_element_type=F32)
            + jnp.dot(m, mid, preferred_element_type=F32)
            + jnp.dot(m, lo, preferred_element_type=F32))


def _block_masks(n, blk):
    row = lax.broadcasted_iota(jnp.int32, (n, n), 0)
    col = lax.broadcasted_iota(jnp.int32, (n, n), 1)
    same = (row // blk) == (col // blk)
    return same & (row >= col), same & (row > col), row == col


def _inv_unit_lower(a, eye, n_iter):
    n = eye.shape[0]
    b = [-x for x in a]
    p = [eye + x for x in b]
    b = [_dot(x, x) for x in b]
    for _ in range(n_iter - 1):
        pb = [_dot(jnp.concatenate([pi, bi], axis=0), bi) for pi, bi in zip(p, b)]
        p = [pi + x[:n] for pi, x in zip(p, pb)]
        b = [x[n:] for x in pb]
    return [pi + _dot(pi, bi) for pi, bi in zip(p, b)]


def _dn_intra(q, k, v, beta, g_col, g_row, masks, n_iter):
    causal, strict, diag = masks
    heads = range(len(q))
    decay = [jnp.where(causal, jnp.exp(jnp.where(causal, g_col[h] - g_row[h], 0.0)), 0.0) for h in heads]
    kb = [k[h] * beta[h] for h in heads]
    kq = [_dot_nt(jnp.concatenate([kb[h], q[h]], axis=0), k[h]) for h in heads]
    a = [jnp.where(strict, kq[h][:SUB] * decay[h], 0.0) for h in heads]
    qk = [kq[h][SUB:] * decay[h] for h in heads]
    eye = jnp.where(diag, 1.0, 0.0).astype(F32)
    t_inv = _inv_unit_lower(a, eye, n_iter)
    e_g = [jnp.exp(g_col[h]) for h in heads]
    uw = [_dot(t_inv[h], jnp.concatenate([v[h] * beta[h], kb[h] * e_g[h]], axis=1)) for h in heads]
    return ([x[:, :HEAD_DIM] for x in uw], [x[:, HEAD_DIM:] for x in uw], qk,
            [q[h] * e_g[h] for h in heads])


class _Slabs:
    def __init__(self, fn, *xs, slab):
        self._outs = []
        self.thunks = [functools.partial(self._run, fn, xs, r, slab)
                       for r in range(0, xs[0].shape[0], slab)]

    def _run(self, fn, xs, r, slab):
        self._outs.append(fn(*[x[r:r + slab] for x in xs]))

    def result(self):
        assert len(self._outs) == len(self.thunks)
        return jnp.concatenate(self._outs, axis=0)


def _by_rows(fn, *xs, slab):
    job = _Slabs(fn, *xs, slab=slab)
    for th in job.thunks:
        th()
    return job.result()


def _piped_dot(lhs, w, lo, hi, work, pieces=4):
    step = (hi - lo) // pieces
    n = len(work)
    outs = []
    for p in range(pieces):
        outs.append(jnp.dot(lhs, w[:, lo + p * step:lo + (p + 1) * step], preferred_element_type=F32))
        for th in work[p * n // pieces:(p + 1) * n // pieces]:
            th()
    return jnp.concatenate(outs, axis=1)


def _conv_job(x, b, tail, w, post):
    c = x.shape[1]
    taps = w.shape[0]
    sub = lax.broadcasted_iota(jnp.int32, (ROW_TILE, c), 0)
    b_rows = jnp.broadcast_to(b, (ROW_TILE, c))
    wj = [jnp.broadcast_to(w[taps - 1 - j:taps - j, :], (ROW_TILE, c)) for j in range(taps)]
    state = {'prev': [pltpu.roll(tail, j, 0) for j in range(1, taps)], 'cur': None}

    def tile(raw):
        cur = raw + b_rows
        rolled = [pltpu.roll(cur, j, 0) for j in range(1, taps)]
        acc = cur * wj[0]
        for j in range(1, taps):
            acc = acc + jnp.where(sub < j, state['prev'][j - 1], rolled[j - 1]) * wj[j]
        state['prev'], state['cur'] = rolled, cur
        return post(acc)

    return _Slabs(tile, x, slab=ROW_TILE), state


def _gated_rms(o, norm_w, z_act):
    return o * lax.rsqrt(jnp.mean(o * o, axis=-1, keepdims=True) + RMS_EPS) * norm_w * z_act


def _l2n(x):
    return x * lax.rsqrt(jnp.sum(x * x, axis=-1, keepdims=True) + RMS_EPS)


def _ffn_chunks():
    n_tiles = D_FF // MXU_DIM
    first = (n_tiles + 1) // 2 * MXU_DIM
    return ((0, first), (first, D_FF))


def _ffn_ln_kernel(x_ref, wu_ref, wd_ref, g_ref, b_ref, o_ref, *, alpha):
    x = x_ref[...]
    xb = x.astype(BF16)
    acc = None
    for lo, hi in _ffn_chunks():
        a = jnp.dot(xb, wu_ref[:, lo:hi], preferred_element_type=F32)
        gt = jnp.dot(xb, wu_ref[:, D_FF + lo:D_FF + hi], preferred_element_type=F32)
        h = (_silu(a) * gt).astype(BF16)
        f = jnp.dot(h, wd_ref[lo:hi, :], preferred_element_type=F32)
        acc = f if acc is None else acc + f
    o_ref[...] = _layer_norm(alpha * x + 0.5 * acc, g_ref[...], b_ref[...])


def _const_spec(shape):
    nd = len(shape)
    return pl.BlockSpec(shape, lambda *_: (0,) * nd, pipeline_mode=pl.Buffered(1))


def _ffn_ln(x2d, wu, wd, g, b, alpha, tm):
    n = x2d.shape[0]
    assert n % tm == 0 and D_FF % MXU_DIM == 0
    return pl.pallas_call(
        functools.partial(_ffn_ln_kernel, alpha=alpha),
        grid=(n // tm,),
        in_specs=[pl.BlockSpec((tm, D_MODEL), lambda i: (i, 0)),
                  _const_spec(wu.shape), _const_spec(wd.shape),
                  _const_spec(g.shape), _const_spec(b.shape)],
        out_specs=pl.BlockSpec((tm, D_MODEL), lambda i: (i, 0)),
        out_shape=jax.ShapeDtypeStruct((n, D_MODEL), F32),
        compiler_params=pltpu.CompilerParams(dimension_semantics=("arbitrary",),
                                             vmem_limit_bytes=VMEM_LIMIT),
        name="ffn_ln",
    )(x2d, wu, wd, g, b)


def _proj_views(w_ref, b_ref):
    g0, g1 = MAIN_COLS, MAIN_COLS + 2 * D_MODEL
    return (w_ref.at[:, 0:g0], w_ref.at[:, g0:g1], w_ref.at[:, g1:g1 + 2 * LANES],
            b_ref.at[:, 0:g0], b_ref.at[:, g0:g1], b_ref.at[:, g1:g1 + 2 * LANES])


def _branch_gates_and_z(hb, w_main, b_main):
    z = jnp.dot(hb, w_main[:, 5 * D_MODEL:6 * D_MODEL], preferred_element_type=F32) \
        + b_main[:, 5 * D_MODEL:6 * D_MODEL]
    return _silu(z)


def _beta_and_logdecay(hb, w_bd, b_bd, alog, dtb):
    bd = jnp.dot(hb, w_bd[...], preferred_element_type=F32) + b_bd[...]
    beta = _sigmoid(bd[:, :128])
    g = -jnp.exp(alog[...]) * _softplus(bd[:, 128:] + dtb[...])
    return beta, g


def _merge_out_ln(x, hb, a_part, yb, w_gates, b_gates, wb_ref, wo_ref, ln_g, ln_b, alpha):
    gate_b = _sigmoid(jnp.dot(hb, w_gates[:, D_MODEL:], preferred_element_type=F32)
                      + b_gates[:, D_MODEL:])
    merged = a_part + gate_b * jnp.dot(yb, wb_ref[...], preferred_element_type=F32)
    mix = jnp.dot(merged.astype(BF16), wo_ref[...], preferred_element_type=F32)
    return _layer_norm(alpha * x + mix, ln_g[...], ln_b[...])


def _mix_prompt_kernel(x_ref, w_ref, b_ref, vg_ref, vb_ref,
                       ws_ref, bst_ref, convw_ref, alog_ref, dtb_ref, normw_ref,
                       wa_ref, wb_ref, wo_ref, ln_g, ln_b,
                       x2_ref, conv_out_ref, ssm_out_ref,
                       s_ref, xc_ref, q_s, k_s, v_s, z_s, g_s, beta_s, yb_s, *, alpha, tt):
    t = pl.program_id(1)
    nt = pl.num_programs(1)
    w_main, w_gates, w_bd, b_main, b_gates, b_bd = _proj_views(w_ref, b_ref)

    @pl.when(t == 0)
    def _():
        s_ref[...] = jnp.zeros(s_ref.shape, F32)
        xc_ref[...] = jnp.zeros((ROW_TILE, QKV), F32)

    x = x_ref[...]
    hb = x.astype(BF16)

    def proj(part, work):
        return _piped_dot(hb, w_main, part * D_MODEL, (part + 1) * D_MODEL, work)

    b_u = b_main[:, 0:D_MODEL]
    b_v, vg, vb = b_main[:, D_MODEL:2 * D_MODEL], vg_ref[...], vb_ref[...]
    pu = proj(0, [])
    job_u = _Slabs(lambda a: _gelu_tanh(a + b_u), pu, slab=SLAB)
    pv = proj(1, job_u.thunks)
    job_v = _Slabs(lambda a: _layer_norm(_gelu_tanh(a + b_v), vg, vb).astype(BF16), pv, slab=SLAB)
    pq = proj(2, job_v.thunks)
    u = job_u.result()
    vn = job_v.result()

    r128 = lax.broadcasted_iota(jnp.int32, (GM_CHUNK, GM_CHUNK), 0)
    c128 = lax.broadcasted_iota(jnp.int32, (GM_CHUNK, GM_CHUNK), 1)
    tril = r128 >= c128
    w_tril = [jnp.where(tril, ws_ref[g], 0.0).astype(BF16) for g in range(GROUPS)]
    rows = []
    for c in range(tt // GM_CHUNK):
        cols = []
        for g in range(GROUPS):
            blk = vn[c * GM_CHUNK:(c + 1) * GM_CHUNK, g * GROUP_DIM:(g + 1) * GROUP_DIM]
            cols.append(jnp.dot(w_tril[g], blk, preferred_element_type=F32) + bst_ref[:, g:g + 1])
        rows.append(jnp.concatenate(cols, axis=1))
    mixed = jnp.concatenate(rows, axis=0)

    scale = HEAD_DIM ** -0.5
    heads = [slice(h * HEAD_DIM, (h + 1) * HEAD_DIM) for h in range(HEADS)]

    def conv_job(raw, part, post):
        cols = slice(part * D_MODEL, (part + 1) * D_MODEL)
        b_p = b_main[:, (2 + part) * D_MODEL:(3 + part) * D_MODEL]
        job, state = _conv_job(raw, b_p, xc_ref[:, cols], convw_ref[:, cols],
                               lambda acc: post(_silu(acc)))

        def finish():
            last = state['cur']

            @pl.when(t == nt - 1)
            def _():
                conv_out_ref[:, cols] = last[ROW_TILE - (DN_CONV - 1):, :]

            xc_ref[:, cols] = last
            return job.result()

        return job, finish

    def l2n_heads(a, mul):
        return jnp.concatenate([_l2n(a[:, sl]) * mul for sl in heads], axis=1)

    job_q, finish_q = conv_job(pq, 0, lambda a: l2n_heads(a, scale))
    pk = proj(3, job_q.thunks)
    q_s[...] = finish_q()
    job_k, finish_k = conv_job(pk, 1, lambda a: l2n_heads(a, 1.0))
    pvv = proj(4, job_k.thunks)
    k_s[...] = finish_k()
    job_vv, finish_vv = conv_job(pvv, 2, lambda a: a)
    pz = proj(5, job_vv.thunks)
    v_s[...] = finish_vv()
    b_z = b_main[:, 5 * D_MODEL:6 * D_MODEL]
    job_z = _Slabs(lambda a: _silu(a + b_z), pz, slab=SLAB)
    job_ya = _Slabs(lambda a, m: (a * m).astype(BF16), u, mixed, slab=SLAB)
    pga = _piped_dot(hb, w_gates, 0, D_MODEL, job_ya.thunks + job_z.thunks)
    z_s[...] = job_z.result()
    pa = _piped_dot(job_ya.result(), wa_ref, 0, D_MODEL, [])
    b_ga = b_gates[:, :D_MODEL]
    job_a = _Slabs(lambda g, p_: _sigmoid(g + b_ga) * p_, pga, pa, slab=SLAB)
    pgb = _piped_dot(hb, w_gates, D_MODEL, 2 * D_MODEL, job_a.thunks)
    a_part = job_a.result()
    beta, g_log = _beta_and_logdecay(hb, w_bd, b_bd, alog_ref, dtb_ref)
    beta_s[...] = beta
    rt = lax.broadcasted_iota(jnp.int32, (tt, tt), 0)
    ct = lax.broadcasted_iota(jnp.int32, (tt, tt), 1)
    cum = jnp.where(((rt // DN_CHUNK) == (ct // DN_CHUNK)) & (rt >= ct), 1.0, 0.0)
    g_s[...] = _dot_exact_lhs(cum, g_log)

    masks = _block_masks(SUB, DN_CHUNK)
    norm_w = normw_ref[...]

    def sub_body(s, carry):
        r0 = pl.multiple_of(s * SUB, SUB)
        g_sub = g_s[pl.ds(r0, SUB), :]
        g_t = g_sub.T
        b_sub = beta_s[pl.ds(r0, SUB), :]
        heads = range(HEADS)
        sls = [slice(h * HEAD_DIM, (h + 1) * HEAD_DIM) for h in heads]
        q = [q_s[pl.ds(r0, SUB), sl] for sl in sls]
        k = [k_s[pl.ds(r0, SUB), sl] for sl in sls]
        vv = [v_s[pl.ds(r0, SUB), sl] for sl in sls]
        z_act = [z_s[pl.ds(r0, SUB), sl] for sl in sls]
        state = [s_ref[h] for h in heads]
        g_col = [jnp.broadcast_to(g_sub[:, h:h + 1], (SUB, HEAD_DIM)) for h in heads]
        g_row = [jnp.broadcast_to(g_t[h:h + 1, :], (SUB, SUB)) for h in heads]
        beta_h = [jnp.broadcast_to(b_sub[:, h:h + 1], (SUB, HEAD_DIM)) for h in heads]
        u_h, w_h, qk, qe = _dn_intra(q, k, vv, beta_h, g_col, g_row, masks, 5)
        n_chunks = SUB // DN_CHUNK
        g_last = [[g_col[h][(c + 1) * DN_CHUNK - 1:(c + 1) * DN_CHUNK, :] for c in range(n_chunks)]
                  for h in heads]
        row_chunk = lax.broadcasted_iota(jnp.int32, (SUB, HEAD_DIM), 0) // DN_CHUNK
        k_dec_t = []
        for h in heads:
            g_end = g_last[h][n_chunks - 1]
            for c in range(n_chunks - 2, -1, -1):
                g_end = jnp.where(row_chunk == c, g_last[h][c], g_end)
            k_dec_t.append((k[h] * jnp.exp(g_end - g_col[h])).T)
        zeros = jnp.zeros((DN_CHUNK, HEAD_DIM), F32)
        outs = [[] for _ in heads]
        for c in range(n_chunks):
            rs = slice(c * DN_CHUNK, (c + 1) * DN_CHUNK)
            r = [_dot(jnp.concatenate([w_h[h][rs], qe[h][rs]], axis=0), state[h]) for h in heads]
            v_new = [u_h[h][rs] - r[h][:DN_CHUNK] for h in heads]
            v_pad = [jnp.concatenate([zeros] * c + [v_new[h]] + [zeros] * (n_chunks - 1 - c), axis=0)
                     for h in heads]
            m = [_dot(jnp.concatenate([qk[h][rs], k_dec_t[h]], axis=0), v_pad[h]) for h in heads]
            for h in heads:
                outs[h].append(r[h][DN_CHUNK:] + m[h][:DN_CHUNK])
            state = [state[h] * jnp.exp(g_last[h][c]) + m[h][DN_CHUNK:] for h in heads]
        for h in heads:
            s_ref[h] = state[h]
            o = jnp.concatenate(outs[h], axis=0)
            yb_s[pl.ds(r0, SUB), sls[h]] = _gated_rms(o, norm_w, z_act[h]).astype(BF16)
        return carry

    lax.fori_loop(0, tt // SUB, sub_body, 0)

    @pl.when(t == nt - 1)
    def _():
        ssm_out_ref[...] = s_ref[...]

    pb = jnp.dot(yb_s[...], wb_ref[...], preferred_element_type=F32)
    b_gb = b_gates[:, D_MODEL:]
    merged = _by_rows(lambda a, g, p_: (a + _sigmoid(g + b_gb) * p_).astype(BF16), a_part, pgb, pb, slab=SLAB)
    mix = jnp.dot(merged, wo_ref[...], preferred_element_type=F32)
    ln_gain, ln_bias = ln_g[...], ln_b[...]
    x2_ref[...] = _by_rows(lambda xx, m: _layer_norm(alpha * xx + m, ln_gain, ln_bias), x, mix, slab=SLAB)


def _mix_prompt(x1, p, alpha, tt):
    b, t, _ = x1.shape
    assert t % tt == 0 and tt % SUB == 0
    consts = [p['w_proj'], p['b_proj'],
              p['gm_v_g'], p['gm_v_b'], p['gm_w_s'], p['gm_b_s_t'], p['conv_w'], p['a_log'],
              p['dt_bias'], p['norm_w'], p['w_a'], p['w_b'], p['w_o'], p['ln2_g'], p['ln2_b']]
    return pl.pallas_call(
        functools.partial(_mix_prompt_kernel, alpha=alpha, tt=tt),
        grid=(b, t // tt),
        in_specs=[pl.BlockSpec((None, tt, D_MODEL), lambda i, j: (i, j, 0))]
                 + [_const_spec(c.shape) for c in consts],
        out_specs=[pl.BlockSpec((None, tt, D_MODEL), lambda i, j: (i, j, 0)),
                   pl.BlockSpec((None, DN_CONV - 1, QKV), lambda i, j: (i, 0, 0)),
                   pl.BlockSpec((None, HEADS, HEAD_DIM, HEAD_DIM), lambda i, j: (i, 0, 0, 0))],
        out_shape=[jax.ShapeDtypeStruct((b, t, D_MODEL), F32),
                   jax.ShapeDtypeStruct((b, DN_CONV - 1, QKV), F32),
                   jax.ShapeDtypeStruct((b, HEADS, HEAD_DIM, HEAD_DIM), F32)],
        scratch_shapes=[pltpu.VMEM((HEADS, HEAD_DIM, HEAD_DIM), F32),
                        pltpu.VMEM((ROW_TILE, QKV), F32),
                        pltpu.VMEM((tt, D_MODEL), F32),
                        pltpu.VMEM((tt, D_MODEL), F32),
                        pltpu.VMEM((tt, D_MODEL), F32),
                        pltpu.VMEM((tt, D_MODEL), F32),
                        pltpu.VMEM((tt, 128), F32),
                        pltpu.VMEM((tt, 128), F32),
                        pltpu.VMEM((tt, D_MODEL), BF16)],
        compiler_params=pltpu.CompilerParams(dimension_semantics=("arbitrary", "arbitrary"),
                                             vmem_limit_bytes=VMEM_LIMIT),
        name="mix_prompt",
    )(x1, *consts)


def _mix_sample_kernel(x_ref, cs_ref, s_in_ref, w_ref, b_ref,
                       vg_ref, vb_ref, coef_ref, bias_ref, convw_ref, alog_ref, dtb_ref, normw_ref,
                       wa_ref, wb_ref, wo_ref, ln_g, ln_b,
                       x2_ref, vrow_ref, z_out_ref, s_out_ref, *, alpha, nb):
    rows = nb * ROW_TILE
    w_main, w_gates, w_bd, b_main, b_gates, b_bd = _proj_views(w_ref, b_ref)
    x = x_ref[...]
    hb = x.astype(BF16)
    valid = (lax.broadcasted_iota(jnp.int32, (rows, 1), 0) % ROW_TILE) >= (ROW_TILE - DN_CONV)
    validf = jnp.where(valid, 1.0, 0.0).astype(F32)

    u = _gelu_tanh(jnp.dot(hb, w_main[:, 0:D_MODEL], preferred_element_type=F32)
                   + b_main[:, 0:D_MODEL])
    v = _gelu_tanh(jnp.dot(hb, w_main[:, D_MODEL:2 * D_MODEL], preferred_element_type=F32)
                   + b_main[:, D_MODEL:2 * D_MODEL])
    vn = _layer_norm(v, vg_ref[...], vb_ref[...])
    vrow_ref[...] = vn
    vn3 = vn.reshape(nb, ROW_TILE, D_MODEL)
    mixed = vn3 * coef_ref[0][None] + bias_ref[...][None]
    for j in range(1, DN_CONV):
        mixed = mixed + pltpu.roll(vn3, j, 1) * coef_ref[j][None]
    ya = (u * mixed.reshape(rows, D_MODEL)).astype(BF16)
    gate_a = _sigmoid(jnp.dot(hb, w_gates[:, :D_MODEL], preferred_element_type=F32)
                      + b_gates[:, :D_MODEL])
    a_part = gate_a * jnp.dot(ya, wa_ref[...], preferred_element_type=F32)

    qkv = jnp.dot(hb, w_main[:, 2 * D_MODEL:5 * D_MODEL], preferred_element_type=F32) \
        + b_main[:, 2 * D_MODEL:5 * D_MODEL]
    zfull = jnp.where(valid, qkv, 0.0) + cs_ref[...]
    z_out_ref[...] = zfull
    z3 = zfull.reshape(nb, ROW_TILE, QKV)
    acc = z3 * convw_ref[DN_CONV - 1:DN_CONV, :][None]
    for j in range(1, DN_CONV):
        acc = acc + pltpu.roll(z3, j, 1) * convw_ref[DN_CONV - 1 - j:DN_CONV - j, :][None]
    sact = _silu(acc.reshape(rows, QKV)) * validf
    z_act = _branch_gates_and_z(hb, w_main, b_main)
    beta, g_log = _beta_and_logdecay(hb, w_bd, b_bd, alog_ref, dtb_ref)
    beta = beta * validf
    g_log = g_log * validf
    rt = lax.broadcasted_iota(jnp.int32, (rows, rows), 0)
    ct = lax.broadcasted_iota(jnp.int32, (rows, rows), 1)
    cum = jnp.where(((rt // ROW_TILE) == (ct // ROW_TILE)) & (rt >= ct), 1.0, 0.0)
    g_cum = _dot_exact_lhs(cum, g_log)

    pad_rows = SUB - rows
    def pad(a):
        return jnp.concatenate([a, jnp.zeros((pad_rows, a.shape[1]), a.dtype)], axis=0)

    g_pad = pad(g_cum)
    g_t = g_pad.T
    beta_pad = pad(beta)
    masks = _block_masks(SUB, ROW_TILE)
    norm_w = normw_ref[...]
    scale = HEAD_DIM ** -0.5
    row_id = lax.broadcasted_iota(jnp.int32, (SUB, HEAD_DIM), 0)
    heads = range(HEADS)
    seqs = range(nb)
    sls = [slice(h * HEAD_DIM, (h + 1) * HEAD_DIM) for h in heads]
    q = [pad(_l2n(sact[:, sl]) * scale * validf) for sl in sls]
    k = [pad(_l2n(sact[:, D_MODEL + h * HEAD_DIM:D_MODEL + (h + 1) * HEAD_DIM]) * validf) for h in heads]
    vv = [pad(sact[:, 2 * D_MODEL + h * HEAD_DIM:2 * D_MODEL + (h + 1) * HEAD_DIM]) for h in heads]
    g_col = [jnp.broadcast_to(g_pad[:, h:h + 1], (SUB, HEAD_DIM)) for h in heads]
    g_row = [jnp.broadcast_to(g_t[h:h + 1, :], (SUB, SUB)) for h in heads]
    beta_h = [jnp.broadcast_to(beta_pad[:, h:h + 1], (SUB, HEAD_DIM)) for h in heads]
    u_h, w_h, qk, qe = _dn_intra(q, k, vv, beta_h, g_col, g_row, masks, 2)
    tiles = [slice(i * ROW_TILE, (i + 1) * ROW_TILE) for i in seqs]
    r = [[_dot(jnp.concatenate([w_h[h][rs], qe[h][rs]], axis=0), s_in_ref[i, h]) for i, rs in enumerate(tiles)]
         for h in heads]
    v_new = [pad(jnp.concatenate([u_h[h][rs] - r[h][i][:ROW_TILE] for i, rs in enumerate(tiles)], axis=0))
             for h in heads]
    qkv_new = [_dot(qk[h], v_new[h]) for h in heads]
    g_end = [jnp.broadcast_to(g_col[h].reshape(SUB // ROW_TILE, ROW_TILE, HEAD_DIM)[:, ROW_TILE - 1:, :],
                              (SUB // ROW_TILE, ROW_TILE, HEAD_DIM)).reshape(SUB, HEAD_DIM) for h in heads]
    k_dec_t = [(k[h] * jnp.exp(g_end[h] - g_col[h])).T for h in heads]
    seq_of_row = row_id // ROW_TILE
    for i in seqs:
        for h in heads:
            g_last = g_col[h][(i + 1) * ROW_TILE - 1:(i + 1) * ROW_TILE, :]
            s_out_ref[i, h] = (s_in_ref[i, h] * jnp.exp(g_last)
                               + _dot(k_dec_t[h], jnp.where(seq_of_row == i, v_new[h], 0.0)))
    ybs = []
    for h in heads:
        o = jnp.concatenate([r[h][i][ROW_TILE:] for i in seqs], axis=0) + qkv_new[h][:rows]
        ybs.append(_gated_rms(o, norm_w, z_act[:, sls[h]]).astype(BF16))
    yb = jnp.concatenate(ybs, axis=1)
    x2_ref[...] = _merge_out_ln(x, hb, a_part, yb, w_gates, b_gates, wb_ref, wo_ref,
                                ln_g, ln_b, alpha)


def _mix_sample(x1, cs_pad, s_in, p, alpha, nb):
    n = x1.shape[0]
    nseq = n // ROW_TILE
    assert nseq % nb == 0 and nb * ROW_TILE <= SUB
    rows = nb * ROW_TILE
    consts = [p['w_proj'], p['b_proj'],
              p['gm_v_g'], p['gm_v_b'], p['mix_coef'], p['mix_bias'], p['conv_w'], p['a_log'],
              p['dt_bias'], p['norm_w'], p['w_a'], p['w_b'], p['w_o'], p['ln2_g'], p['ln2_b']]
    state_spec = pl.BlockSpec((nb, HEADS, HEAD_DIM, HEAD_DIM), lambda i: (i, 0, 0, 0))
    return pl.pallas_call(
        functools.partial(_mix_sample_kernel, alpha=alpha, nb=nb),
        grid=(nseq // nb,),
        in_specs=[pl.BlockSpec((rows, D_MODEL), lambda i: (i, 0)),
                  pl.BlockSpec((rows, QKV), lambda i: (i, 0)),
                  state_spec] + [_const_spec(c.shape) for c in consts],
        out_specs=[pl.BlockSpec((rows, D_MODEL), lambda i: (i, 0)),
                   pl.BlockSpec((rows, D_MODEL), lambda i: (i, 0)),
                   pl.BlockSpec((rows, QKV), lambda i: (i, 0)),
                   state_spec],
        out_shape=[jax.ShapeDtypeStruct((n, D_MODEL), F32),
                   jax.ShapeDtypeStruct((n, D_MODEL), F32),
                   jax.ShapeDtypeStruct((n, QKV), F32),
                   jax.ShapeDtypeStruct(s_in.shape, F32)],
        compiler_params=pltpu.CompilerParams(dimension_semantics=("arbitrary",),
                                             vmem_limit_bytes=VMEM_LIMIT),
        name="mix_sample",
    )(x1, cs_pad, s_in, *consts)


def _pad_lanes(a, n=128):
    return jnp.pad(a, [(0, 0)] * (a.ndim - 1) + [(0, n - a.shape[-1])])


def _layer_params(l, ffn1_w_up, ffn1_w_down, ln1_g, ln1_b, w_in, b_in, gm_v_g, gm_v_b, gm_w_s,
                  gm_b_s, dn_conv_w, dn_a_log, dn_dt_bias, dn_norm_w, w_branch_a, w_branch_b,
                  w_out, ln2_g, ln2_b, ffn2_w_up, ffn2_w_down, ln3_g, ln3_b):
    row = lambda a: a[l][None, :].astype(F32)
    wi, bi = w_in[l], b_in[l]
    o_beta = MAIN_COLS
    o_dec = o_beta + HEADS
    o_gate = o_dec + HEADS
    ws = gm_w_s[l]
    lsm = DN_CONV
    shift = np.arange(lsm)[:, None]
    pos = np.arange(ROW_TILE)[None, :] - (ROW_TILE - lsm)
    live = (pos >= shift)
    coef = jnp.where(live[:, :, None],
                     jnp.transpose(ws[:, np.clip(pos + 0 * shift, 0, lsm - 1),
                                      np.clip(pos - shift, 0, lsm - 1)], (1, 2, 0)), 0.0)
    bias = jnp.pad(gm_b_s[l][:, :lsm].T, ((ROW_TILE - lsm, 0), (0, 0)))
    return {
        'ffn1': (ffn1_w_up[l].astype(BF16), ffn1_w_down[l].astype(BF16), row(ln1_g), row(ln1_b)),
        'ffn2': (ffn2_w_up[l].astype(BF16), ffn2_w_down[l].astype(BF16), row(ln3_g), row(ln3_b)),
        'w_proj': jnp.concatenate([wi[:, :MAIN_COLS], wi[:, o_gate:], _pad_lanes(wi[:, o_beta:o_dec]),
                                   _pad_lanes(wi[:, o_dec:o_gate])], axis=1).astype(BF16),
        'b_proj': jnp.concatenate([bi[:MAIN_COLS], bi[o_gate:], _pad_lanes(bi[o_beta:o_dec]),
                                   _pad_lanes(bi[o_dec:o_gate])])[None, :],
        'gm_v_g': row(gm_v_g), 'gm_v_b': row(gm_v_b),
        'gm_w_s': ws, 'gm_b_s_t': gm_b_s[l].T,
        'mix_coef': jnp.repeat(coef, GROUP_DIM, axis=-1), 'mix_bias': jnp.repeat(bias, GROUP_DIM, axis=-1),
        'conv_w': dn_conv_w[l],
        'a_log': _pad_lanes(dn_a_log[l][None, :].astype(F32)),
        'dt_bias': _pad_lanes(dn_dt_bias[l][None, :].astype(F32)),
        'norm_w': row(dn_norm_w),
        'w_a': w_branch_a[l].astype(BF16), 'w_b': w_branch_b[l].astype(BF16),
        'w_o': w_out[l].astype(BF16),
        'ln2_g': row(ln2_g), 'ln2_b': row(ln2_b),
    }


def kernel(x_prompt, x_sample, state_conv, state_ssm, ffn1_w_up, ffn1_w_down, ln1_g, ln1_b, w_in, b_in, gm_v_g, gm_v_b, gm_w_s, gm_b_s, dn_conv_w, dn_a_log, dn_dt_bias, dn_norm_w, w_branch_a, w_branch_b, w_out, ln2_g, ln2_b, ffn2_w_up, ffn2_w_down, ln3_g, ln3_b):
    depth = ffn1_w_up.shape[0]
    alpha = (2.0 * depth) ** 0.25
    bp, tp, _ = x_prompt.shape
    bs, ts, _ = x_sample.shape
    assert ts == DN_CONV and ts <= ROW_TILE - (DN_CONV - 1)
    lead = ROW_TILE - ts
    y_p = x_prompt
    y_s = jnp.pad(x_sample, ((0, 0), (lead, 0), (0, 0))).reshape(bs * ROW_TILE, D_MODEL)
    conv_p, ssm_p, conv_s, ssm_s, v_s = [], [], [], [], []
    for l in range(depth):
        p = _layer_params(l, ffn1_w_up, ffn1_w_down, ln1_g, ln1_b, w_in, b_in, gm_v_g, gm_v_b,
                          gm_w_s, gm_b_s, dn_conv_w, dn_a_log, dn_dt_bias, dn_norm_w, w_branch_a,
                          w_branch_b, w_out, ln2_g, ln2_b, ffn2_w_up, ffn2_w_down, ln3_g, ln3_b)
        x1 = _ffn_ln(y_p.reshape(bp * tp, D_MODEL), *p['ffn1'], alpha, 512).reshape(bp, tp, D_MODEL)
        x2, c_p, s_p = _mix_prompt(x1, p, alpha, 256)
        y_p = _ffn_ln(x2.reshape(bp * tp, D_MODEL), *p['ffn2'], alpha, 512).reshape(bp, tp, D_MODEL)
        x1s = _ffn_ln(y_s, *p['ffn1'], alpha, 512)
        cs_pad = jnp.pad(state_conv[l], ((0, 0), (lead - (DN_CONV - 1), ts), (0, 0)))
        x2s, vrows, zfull, s_s = _mix_sample(x1s, cs_pad.reshape(bs * ROW_TILE, QKV),
                                             state_ssm[l], p, alpha, 8)
        y_s = _ffn_ln(x2s, *p['ffn2'], alpha, 512)
        conv_p.append(c_p)
        ssm_p.append(s_p)
        conv_s.append(zfull.reshape(bs, ROW_TILE, QKV)[:, ROW_TILE - (DN_CONV - 1):])
        ssm_s.append(s_s)
        v_s.append(vrows.reshape(bs, ROW_TILE, D_MODEL)[:, lead:])
    y_s_out = y_s.reshape(bs, ROW_TILE, D_MODEL)[:, lead:]
    return (y_p, y_s_out, jnp.stack(conv_p), jnp.stack(ssm_p), jnp.stack(conv_s), jnp.stack(ssm_s),
            jnp.stack(v_s))
```

```python
import functools
import math

import jax
import jax.numpy as jnp
import numpy as np
from jax import lax
from jax.experimental import pallas as pl
from jax.experimental.pallas import tpu as pltpu

F32 = jnp.float32
BF16 = jnp.bfloat16

D_MODEL = 1024
D_FF = 2816
HEADS = 8
HEAD_DIM = 128
GROUPS = 8
GROUP_DIM = 128
GM_CHUNK = 128
DN_CHUNK = 64
DN_CONV = 4
QKV = 3 * D_MODEL
MAIN_COLS = 6 * D_MODEL
LN_EPS = 1e-5
RMS_EPS = 1e-6

MXU_DIM = 256
SUB = 128
ROW_TILE = 8
LANES = 128
SLAB = 16
VMEM_LIMIT = 56 * 1024 * 1024


def _sigmoid(x):
    return 0.5 * jnp.tanh(0.5 * x) + 0.5


def _silu(x):
    return x * _sigmoid(x)


def _gelu_tanh(x):
    c = math.sqrt(2.0 / math.pi)
    return x * (0.5 * (1.0 + jnp.tanh(c * (x + 0.044715 * (x * x * x)))))


def _softplus(x):
    return jnp.maximum(x, 0.0) + jnp.log(1.0 + jnp.exp(-jnp.abs(x)))


def _layer_norm(y, g, b):
    mu = jnp.mean(y, axis=-1, keepdims=True)
    yc = y - mu
    var = jnp.mean(yc * yc, axis=-1, keepdims=True)
    return yc * lax.rsqrt(var + LN_EPS) * g + b


def _dot(a, b):
    return jnp.dot(a.astype(BF16), b.astype(BF16), preferred_element_type=F32)


def _dot_nt(a, b):
    return lax.dot_general(a.astype(BF16), b.astype(BF16), (((1,), (1,)), ((), ())),
                           preferred_element_type=F32)


def _dot_exact_lhs(m01, x):
    hi = x.astype(BF16)
    r1 = x - hi.astype(F32)
    mid = r1.astype(BF16)
    lo = (r1 - mid.astype(F32)).astype(BF16)
    m = m01.astype(BF16)
    return (jnp.dot(m, hi, preferred_element_type=F32)
            + jnp.dot(m, mid, preferred_element_type=F32)
            + jnp.dot(m, lo, preferred_element_type=F32))


def _block_masks(n, blk):
    row = lax.broadcasted_iota(jnp.int32, (n, n), 0)
    col = lax.broadcasted_iota(jnp.int32, (n, n), 1)
    same = (row // blk) == (col // blk)
    return same & (row >= col), same & (row > col), row == col


def _inv_unit_lower(a, eye, n_iter):
    n = eye.shape[0]
    b = [-x for x in a]
    p = [eye + x for x in b]
    b = [_dot(x, x) for x in b]
    for _ in range(n_iter - 1):
        pb = [_dot(jnp.concatenate([pi, bi], axis=0), bi) for pi, bi in zip(p, b)]
        p = [pi + x[:n] for pi, x in zip(p, pb)]
        b = [x[n:] for x in pb]
    return [pi + _dot(pi, bi) for pi, bi in zip(p, b)]


def _dn_intra(q, k, v, beta, g_col, g_row, masks, n_iter):
    causal, strict, diag = masks
    heads = range(len(q))
    decay = [jnp.where(causal, jnp.exp(jnp.where(causal, g_col[h] - g_row[h], 0.0)), 0.0) for h in heads]
    kb = [k[h] * beta[h] for h in heads]
    kq = [_dot_nt(jnp.concatenate([kb[h], q[h]], axis=0), k[h]) for h in heads]
    a = [jnp.where(strict, kq[h][:SUB] * decay[h], 0.0) for h in heads]
    qk = [kq[h][SUB:] * decay[h] for h in heads]
    eye = jnp.where(diag, 1.0, 0.0).astype(F32)
    t_inv = _inv_unit_lower(a, eye, n_iter)
    e_g = [jnp.exp(g_col[h]) for h in heads]
    uw = [_dot(t_inv[h], jnp.concatenate([v[h] * beta[h], kb[h] * e_g[h]], axis=1)) for h in heads]
    return ([x[:, :HEAD_DIM] for x in uw], [x[:, HEAD_DIM:] for x in uw], qk,
            [q[h] * e_g[h] for h in heads])


class _Slabs:
    def __init__(self, fn, *xs, slab):
        self._outs = []
        self.thunks = [functools.partial(self._run, fn, xs, r, slab)
                       for r in range(0, xs[0].shape[0], slab)]

    def _run(self, fn, xs, r, slab):
        self._outs.append(fn(*[x[r:r + slab] for x in xs]))

    def result(self):
        assert len(self._outs) == len(self.thunks)
        return jnp.concatenate(self._outs, axis=0)


def _by_rows(fn, *xs, slab):
    job = _Slabs(fn, *xs, slab=slab)
    for th in job.thunks:
        th()
    return job.result()


def _piped_dot(lhs, w, lo, hi, work, pieces=4):
    step = (hi - lo) // pieces
    n = len(work)
    outs = []
    for p in range(pieces):
        outs.append(jnp.dot(lhs, w[:, lo + p * step:lo + (p + 1) * step], preferred_element_type=F32))
        for th in work[p * n // pieces:(p + 1) * n // pieces]:
            th()
    return jnp.concatenate(outs, axis=1)


def _conv_job(x, b, tail, w, post):
    c = x.shape[1]
    taps = w.shape[0]
    sub = lax.broadcasted_iota(jnp.int32, (ROW_TILE, c), 0)
    b_rows = jnp.broadcast_to(b, (ROW_TILE, c))
    wj = [jnp.broadcast_to(w[taps - 1 - j:taps - j, :], (ROW_TILE, c)) for j in range(taps)]
    state = {'prev': [pltpu.roll(tail, j, 0) for j in range(1, taps)], 'cur': None}

    def tile(raw):
        cur = raw + b_rows
        rolled = [pltpu.roll(cur, j, 0) for j in range(1, taps)]
        acc = cur * wj[0]
        for j in range(1, taps):
            acc = acc + jnp.where(sub < j, state['prev'][j - 1], rolled[j - 1]) * wj[j]
        state['prev'], state['cur'] = rolled, cur
        return post(acc)

    return _Slabs(tile, x, slab=ROW_TILE), state


def _gated_rms(o, norm_w, z_act):
    return o * lax.rsqrt(jnp.mean(o * o, axis=-1, keepdims=True) + RMS_EPS) * norm_w * z_act


def _l2n(x):
    return x * lax.rsqrt(jnp.sum(x * x, axis=-1, keepdims=True) + RMS_EPS)


def _ffn_chunks():
    n_tiles = D_FF // MXU_DIM
    first = (n_tiles + 1) // 2 * MXU_DIM
    return ((0, first), (first, D_FF))


def _ffn_ln_kernel(x_ref, wu_ref, wd_ref, g_ref, b_ref, o_ref, *, alpha):
    x = x_ref[...]
    xb = x.astype(BF16)
    acc = None
    for lo, hi in _ffn_chunks():
        a = jnp.dot(xb, wu_ref[:, lo:hi], preferred_element_type=F32)
        gt = jnp.dot(xb, wu_ref[:, D_FF + lo:D_FF + hi], preferred_element_type=F32)
        h = (_silu(a) * gt).astype(BF16)
        f = jnp.dot(h, wd_ref[lo:hi, :], preferred_element_type=F32)
        acc = f if acc is None else acc + f
    o_ref[...] = _layer_norm(alpha * x + 0.5 * acc, g_ref[...], b_ref[...])


def _const_spec(shape):
    nd = len(shape)
    return pl.BlockSpec(shape, lambda *_: (0,) * nd, pipeline_mode=pl.Buffered(1))


def _ffn_ln(x2d, wu, wd, g, b, alpha, tm):
    n = x2d.shape[0]
    assert n % tm == 0 and D_FF % MXU_DIM == 0
    return pl.pallas_call(
        functools.partial(_ffn_ln_kernel, alpha=alpha),
        grid=(n // tm,),
        in_specs=[pl.BlockSpec((tm, D_MODEL), lambda i: (i, 0)),
                  _const_spec(wu.shape), _const_spec(wd.shape),
                  _const_spec(g.shape), _const_spec(b.shape)],
        out_specs=pl.BlockSpec((tm, D_MODEL), lambda i: (i, 0)),
        out_shape=jax.ShapeDtypeStruct((n, D_MODEL), F32),
        compiler_params=pltpu.CompilerParams(dimension_semantics=("arbitrary",),
                                             vmem_limit_bytes=VMEM_LIMIT),
        name="ffn_ln",
    )(x2d, wu, wd, g, b)


def _proj_views(w_ref, b_ref):
    g0, g1 = MAIN_COLS, MAIN_COLS + 2 * D_MODEL
    return (w_ref.at[:, 0:g0], w_ref.at[:, g0:g1], w_ref.at[:, g1:g1 + 2 * LANES],
            b_ref.at[:, 0:g0], b_ref.at[:, g0:g1], b_ref.at[:, g1:g1 + 2 * LANES])


def _branch_gates_and_z(hb, w_main, b_main):
    z = jnp.dot(hb, w_main[:, 5 * D_MODEL:6 * D_MODEL], preferred_element_type=F32) \
        + b_main[:, 5 * D_MODEL:6 * D_MODEL]
    return _silu(z)


def _beta_and_logdecay(hb, w_bd, b_bd, alog, dtb):
    bd = jnp.dot(hb, w_bd[...], preferred_element_type=F32) + b_bd[...]
    beta = _sigmoid(bd[:, :128])
    g = -jnp.exp(alog[...]) * _softplus(bd[:, 128:] + dtb[...])
    return beta, g


def _merge_out_ln(x, hb, a_part, yb, w_gates, b_gates, wb_ref, wo_ref, ln_g, ln_b, alpha):
    gate_b = _sigmoid(jnp.dot(hb, w_gates[:, D_MODEL:], preferred_element_type=F32)
                      + b_gates[:, D_MODEL:])
    merged = a_part + gate_b * jnp.dot(yb, wb_ref[...], preferred_element_type=F32)
    mix = jnp.dot(merged.astype(BF16), wo_ref[...], preferred_element_type=F32)
    return _layer_norm(alpha * x + mix, ln_g[...], ln_b[...])


def _mix_prompt_kernel(x_ref, w_ref, b_ref, vg_ref, vb_ref,
                       ws_ref, bst_ref, convw_ref, alog_ref, dtb_ref, normw_ref,
                       wa_ref, wb_ref, wo_ref, ln_g, ln_b,
                       x2_ref, conv_out_ref, ssm_out_ref,
                       s_ref, xc_ref, q_s, k_s, v_s, z_s, g_s, beta_s, yb_s, *, alpha, tt, nseq):
    t = pl.program_id(1)
    nt = pl.num_programs(1)
    w_main, w_gates, w_bd, b_main, b_gates, b_bd = _proj_views(w_ref, b_ref)

    @pl.when(t == 0)
    def _():
        s_ref[...] = jnp.zeros(s_ref.shape, F32)
        xc_ref[...] = jnp.zeros(xc_ref.shape, F32)

    n_rows = nseq * tt
    x = x_ref[...].reshape(n_rows, D_MODEL)
    hb = x.astype(BF16)

    def proj(part, work):
        return _piped_dot(hb, w_main, part * D_MODEL, (part + 1) * D_MODEL, work)

    b_u = b_main[:, 0:D_MODEL]
    b_v, vg, vb = b_main[:, D_MODEL:2 * D_MODEL], vg_ref[...], vb_ref[...]
    pu = proj(0, [])
    job_u = _Slabs(lambda a: _gelu_tanh(a + b_u), pu, slab=SLAB)
    pv = proj(1, job_u.thunks)
    job_v = _Slabs(lambda a: _layer_norm(_gelu_tanh(a + b_v), vg, vb).astype(BF16), pv, slab=SLAB)
    pq = proj(2, job_v.thunks)
    u = job_u.result()
    vn = job_v.result()

    r128 = lax.broadcasted_iota(jnp.int32, (GM_CHUNK, GM_CHUNK), 0)
    c128 = lax.broadcasted_iota(jnp.int32, (GM_CHUNK, GM_CHUNK), 1)
    tril = r128 >= c128
    w_tril = [jnp.where(tril, ws_ref[g], 0.0).astype(BF16) for g in range(GROUPS)]
    rows = []
    for c in range(n_rows // GM_CHUNK):
        cols = []
        for g in range(GROUPS):
            blk = vn[c * GM_CHUNK:(c + 1) * GM_CHUNK, g * GROUP_DIM:(g + 1) * GROUP_DIM]
            cols.append(jnp.dot(w_tril[g], blk, preferred_element_type=F32) + bst_ref[:, g:g + 1])
        rows.append(jnp.concatenate(cols, axis=1))
    mixed = jnp.concatenate(rows, axis=0)

    scale = HEAD_DIM ** -0.5
    heads = [slice(h * HEAD_DIM, (h + 1) * HEAD_DIM) for h in range(HEADS)]

    class _ConvJobs:
        def __init__(self, raw, part, post):
            self.cols = slice(part * D_MODEL, (part + 1) * D_MODEL)
            b_p = b_main[:, (2 + part) * D_MODEL:(3 + part) * D_MODEL]
            self.jobs = [_conv_job(raw[sq * tt:(sq + 1) * tt], b_p, xc_ref[sq, :, self.cols],
                                   convw_ref[:, self.cols], lambda acc: post(_silu(acc)))
                         for sq in range(nseq)]
            self.thunks = [th for job, _ in self.jobs for th in job.thunks]

        def finish(self):
            for sq, (_, state) in enumerate(self.jobs):
                last = state['cur']

                @pl.when(t == nt - 1)
                def _():
                    conv_out_ref[sq, :, self.cols] = last[ROW_TILE - (DN_CONV - 1):, :]

                xc_ref[sq, :, self.cols] = last
            return jnp.concatenate([job.result() for job, _ in self.jobs], axis=0)

    def conv_job(raw, part, post):
        jobs = _ConvJobs(raw, part, post)
        return jobs, jobs.finish

    def l2n_heads(a, mul):
        return jnp.concatenate([_l2n(a[:, sl]) * mul for sl in heads], axis=1)

    job_q, finish_q = conv_job(pq, 0, lambda a: l2n_heads(a, scale))
    pk = proj(3, job_q.thunks)
    q_s[...] = finish_q()
    job_k, finish_k = conv_job(pk, 1, lambda a: l2n_heads(a, 1.0))
    pvv = proj(4, job_k.thunks)
    k_s[...] = finish_k()
    job_vv, finish_vv = conv_job(pvv, 2, lambda a: a)
    pz = proj(5, job_vv.thunks)
    v_s[...] = finish_vv()
    b_z = b_main[:, 5 * D_MODEL:6 * D_MODEL]
    job_z = _Slabs(lambda a: _silu(a + b_z), pz, slab=SLAB)
    job_ya = _Slabs(lambda a, m: (a * m).astype(BF16), u, mixed, slab=SLAB)
    pga = _piped_dot(hb, w_gates, 0, D_MODEL, job_ya.thunks + job_z.thunks)
    z_s[...] = job_z.result()
    pa = _piped_dot(job_ya.result(), wa_ref, 0, D_MODEL, [])
    b_ga = b_gates[:, :D_MODEL]
    job_a = _Slabs(lambda g, p_: _sigmoid(g + b_ga) * p_, pga, pa, slab=SLAB)
    pgb = _piped_dot(hb, w_gates, D_MODEL, 2 * D_MODEL, job_a.thunks)
    a_part = job_a.result()
    beta, g_log = _beta_and_logdecay(hb, w_bd, b_bd, alog_ref, dtb_ref)
    beta_s[...] = beta
    rt = lax.broadcasted_iota(jnp.int32, (n_rows, n_rows), 0)
    ct = lax.broadcasted_iota(jnp.int32, (n_rows, n_rows), 1)
    cum = jnp.where(((rt // DN_CHUNK) == (ct // DN_CHUNK)) & (rt >= ct), 1.0, 0.0)
    g_s[...] = _dot_exact_lhs(cum, g_log)

    masks = _block_masks(SUB, DN_CHUNK)
    norm_w = normw_ref[...]

    chains = [(sq, h) for sq in range(nseq) for h in range(HEADS)]
    sls = [slice(h * HEAD_DIM, (h + 1) * HEAD_DIM) for _, h in chains]
    n_chunks = SUB // DN_CHUNK
    row_chunk = lax.broadcasted_iota(jnp.int32, (SUB, HEAD_DIM), 0) // DN_CHUNK
    zeros = jnp.zeros((DN_CHUNK, HEAD_DIM), F32)
    ids = range(len(chains))
    state = [s_ref[sq, h] for sq, h in chains]
    for j in range(tt // SUB):
        rows = [slice(sq * tt + j * SUB, sq * tt + (j + 1) * SUB) for sq, _ in chains]
        g_sub = [g_s[sq * tt + j * SUB:sq * tt + (j + 1) * SUB, :] for sq in range(nseq)]
        g_t = [g.T for g in g_sub]
        b_sub = [beta_s[sq * tt + j * SUB:sq * tt + (j + 1) * SUB, :] for sq in range(nseq)]
        q = [q_s[rows[i], sls[i]] for i in ids]
        k = [k_s[rows[i], sls[i]] for i in ids]
        vv = [v_s[rows[i], sls[i]] for i in ids]
        g_col = [jnp.broadcast_to(g_sub[sq][:, h:h + 1], (SUB, HEAD_DIM)) for sq, h in chains]
        g_row = [jnp.broadcast_to(g_t[sq][h:h + 1, :], (SUB, SUB)) for sq, h in chains]
        beta_h = [jnp.broadcast_to(b_sub[sq][:, h:h + 1], (SUB, HEAD_DIM)) for sq, h in chains]
        u_h, w_h, qk, qe = _dn_intra(q, k, vv, beta_h, g_col, g_row, masks, 5)
        g_last = [[g_col[i][(c + 1) * DN_CHUNK - 1:(c + 1) * DN_CHUNK, :] for c in range(n_chunks)]
                  for i in ids]
        k_dec_t = []
        for i in ids:
            g_end = g_last[i][n_chunks - 1]
            for c in range(n_chunks - 2, -1, -1):
                g_end = jnp.where(row_chunk == c, g_last[i][c], g_end)
            k_dec_t.append((k[i] * jnp.exp(g_end - g_col[i])).T)
        outs = [[] for _ in ids]
        for c in range(n_chunks):
            rs = slice(c * DN_CHUNK, (c + 1) * DN_CHUNK)
            r = [_dot(jnp.concatenate([w_h[i][rs], qe[i][rs]], axis=0), state[i]) for i in ids]
            v_new = [u_h[i][rs] - r[i][:DN_CHUNK] for i in ids]
            v_pad = [jnp.concatenate([zeros] * c + [v_new[i]] + [zeros] * (n_chunks - 1 - c), axis=0)
                     for i in ids]
            m = [_dot(jnp.concatenate([qk[i][rs], k_dec_t[i]], axis=0), v_pad[i]) for i in ids]
            for i in ids:
                outs[i].append(r[i][DN_CHUNK:] + m[i][:DN_CHUNK])
            state = [state[i] * jnp.exp(g_last[i][c]) + m[i][DN_CHUNK:] for i in ids]
        for i in ids:
            o = jnp.concatenate(outs[i], axis=0)
            yb_s[rows[i], sls[i]] = _gated_rms(o, norm_w, z_s[rows[i], sls[i]]).astype(BF16)
    for i, (sq, h) in enumerate(chains):
        s_ref[sq, h] = state[i]

    @pl.when(t == nt - 1)
    def _():
        ssm_out_ref[...] = s_ref[...]

    pb = jnp.dot(yb_s[...], wb_ref[...], preferred_element_type=F32)
    b_gb = b_gates[:, D_MODEL:]
    merged = _by_rows(lambda a, g, p_: (a + _sigmoid(g + b_gb) * p_).astype(BF16), a_part, pgb, pb, slab=SLAB)
    mix = jnp.dot(merged, wo_ref[...], preferred_element_type=F32)
    ln_gain, ln_bias = ln_g[...], ln_b[...]
    x2 = _by_rows(lambda xx, m: _layer_norm(alpha * xx + m, ln_gain, ln_bias), x, mix, slab=SLAB)
    x2_ref[...] = x2.reshape(nseq, tt, D_MODEL)


def _mix_prompt(x1, p, alpha, tt, nseq):
    b, t, _ = x1.shape
    assert t % tt == 0 and tt % SUB == 0 and b % nseq == 0
    rows = nseq * tt
    consts = [p['w_proj'], p['b_proj'],
              p['gm_v_g'], p['gm_v_b'], p['gm_w_s'], p['gm_b_s_t'], p['conv_w'], p['a_log'],
              p['dt_bias'], p['norm_w'], p['w_a'], p['w_b'], p['w_o'], p['ln2_g'], p['ln2_b']]
    return pl.pallas_call(
        functools.partial(_mix_prompt_kernel, alpha=alpha, tt=tt, nseq=nseq),
        grid=(b // nseq, t // tt),
        in_specs=[pl.BlockSpec((nseq, tt, D_MODEL), lambda i, j: (i, j, 0))]
                 + [_const_spec(c.shape) for c in consts],
        out_specs=[pl.BlockSpec((nseq, tt, D_MODEL), lambda i, j: (i, j, 0)),
                   pl.BlockSpec((nseq, DN_CONV - 1, QKV), lambda i, j: (i, 0, 0)),
                   pl.BlockSpec((nseq, HEADS, HEAD_DIM, HEAD_DIM), lambda i, j: (i, 0, 0, 0))],
        out_shape=[jax.ShapeDtypeStruct((b, t, D_MODEL), F32),
                   jax.ShapeDtypeStruct((b, DN_CONV - 1, QKV), F32),
                   jax.ShapeDtypeStruct((b, HEADS, HEAD_DIM, HEAD_DIM), F32)],
        scratch_shapes=[pltpu.VMEM((nseq, HEADS, HEAD_DIM, HEAD_DIM), F32),
                        pltpu.VMEM((nseq, ROW_TILE, QKV), F32),
                        pltpu.VMEM((rows, D_MODEL), F32),
                        pltpu.VMEM((rows, D_MODEL), F32),
                        pltpu.VMEM((rows, D_MODEL), F32),
                        pltpu.VMEM((rows, D_MODEL), F32),
                        pltpu.VMEM((rows, 128), F32),
                        pltpu.VMEM((rows, 128), F32),
                        pltpu.VMEM((rows, D_MODEL), BF16)],
        compiler_params=pltpu.CompilerParams(dimension_semantics=("arbitrary", "arbitrary"),
                                             vmem_limit_bytes=VMEM_LIMIT),
        name="mix_prompt",
    )(x1, *consts)


def _mix_sample_kernel(x_ref, cs_ref, s_in_ref, w_ref, b_ref,
                       vg_ref, vb_ref, coef_ref, bias_ref, convw_ref, alog_ref, dtb_ref, normw_ref,
                       wa_ref, wb_ref, wo_ref, ln_g, ln_b,
                       x2_ref, vrow_ref, z_out_ref, s_out_ref, *, alpha, nb):
    rows = nb * ROW_TILE
    w_main, w_gates, w_bd, b_main, b_gates, b_bd = _proj_views(w_ref, b_ref)
    x = x_ref[...]
    hb = x.astype(BF16)
    valid = (lax.broadcasted_iota(jnp.int32, (rows, 1), 0) % ROW_TILE) >= (ROW_TILE - DN_CONV)
    validf = jnp.where(valid, 1.0, 0.0).astype(F32)

    u = _gelu_tanh(jnp.dot(hb, w_main[:, 0:D_MODEL], preferred_element_type=F32)
                   + b_main[:, 0:D_MODEL])
    v = _gelu_tanh(jnp.dot(hb, w_main[:, D_MODEL:2 * D_MODEL], preferred_element_type=F32)
                   + b_main[:, D_MODEL:2 * D_MODEL])
    vn = _layer_norm(v, vg_ref[...], vb_ref[...])
    vrow_ref[...] = vn
    vn3 = vn.reshape(nb, ROW_TILE, D_MODEL)
    mixed = vn3 * coef_ref[0][None] + bias_ref[...][None]
    for j in range(1, DN_CONV):
        mixed = mixed + pltpu.roll(vn3, j, 1) * coef_ref[j][None]
    ya = (u * mixed.reshape(rows, D_MODEL)).astype(BF16)
    gate_a = _sigmoid(jnp.dot(hb, w_gates[:, :D_MODEL], preferred_element_type=F32)
                      + b_gates[:, :D_MODEL])
    a_part = gate_a * jnp.dot(ya, wa_ref[...], preferred_element_type=F32)

    qkv = jnp.dot(hb, w_main[:, 2 * D_MODEL:5 * D_MODEL], preferred_element_type=F32) \
        + b_main[:, 2 * D_MODEL:5 * D_MODEL]
    zfull = jnp.where(valid, qkv, 0.0) + cs_ref[...]
    z_out_ref[...] = zfull
    z3 = zfull.reshape(nb, ROW_TILE, QKV)
    acc = z3 * convw_ref[DN_CONV - 1:DN_CONV, :][None]
    for j in range(1, DN_CONV):
        acc = acc + pltpu.roll(z3, j, 1) * convw_ref[DN_CONV - 1 - j:DN_CONV - j, :][None]
    sact = _silu(acc.reshape(rows, QKV)) * validf
    z_act = _branch_gates_and_z(hb, w_main, b_main)
    beta, g_log = _beta_and_logdecay(hb, w_bd, b_bd, alog_ref, dtb_ref)
    beta = beta * validf
    g_log = g_log * validf
    rt = lax.broadcasted_iota(jnp.int32, (rows, rows), 0)
    ct = lax.broadcasted_iota(jnp.int32, (rows, rows), 1)
    cum = jnp.where(((rt // ROW_TILE) == (ct // ROW_TILE)) & (rt >= ct), 1.0, 0.0)
    g_cum = _dot_exact_lhs(cum, g_log)

    pad_rows = SUB - rows
    def pad(a):
        return jnp.concatenate([a, jnp.zeros((pad_rows, a.shape[1]), a.dtype)], axis=0)

    g_pad = pad(g_cum)
    g_t = g_pad.T
    beta_pad = pad(beta)
    masks = _block_masks(SUB, ROW_TILE)
    norm_w = normw_ref[...]
    scale = HEAD_DIM ** -0.5
    row_id = lax.broadcasted_iota(jnp.int32, (SUB, HEAD_DIM), 0)
    heads = range(HEADS)
    seqs = range(nb)
    sls = [slice(h * HEAD_DIM, (h + 1) * HEAD_DIM) for h in heads]
    q = [pad(_l2n(sact[:, sl]) * scale * validf) for sl in sls]
    k = [pad(_l2n(sact[:, D_MODEL + h * HEAD_DIM:D_MODEL + (h + 1) * HEAD_DIM]) * validf) for h in heads]
    vv = [pad(sact[:, 2 * D_MODEL + h * HEAD_DIM:2 * D_MODEL + (h + 1) * HEAD_DIM]) for h in heads]
    g_col = [jnp.broadcast_to(g_pad[:, h:h + 1], (SUB, HEAD_DIM)) for h in heads]
    g_row = [jnp.broadcast_to(g_t[h:h + 1, :], (SUB, SUB)) for h in heads]
    beta_h = [jnp.broadcast_to(beta_pad[:, h:h + 1], (SUB, HEAD_DIM)) for h in heads]
    u_h, w_h, qk, qe = _dn_intra(q, k, vv, beta_h, g_col, g_row, masks, 2)
    tiles = [slice(i * ROW_TILE, (i + 1) * ROW_TILE) for i in seqs]
    r = [[_dot(jnp.concatenate([w_h[h][rs], qe[h][rs]], axis=0), s_in_ref[i, h]) for i, rs in enumerate(tiles)]
         for h in heads]
    v_new = [pad(jnp.concatenate([u_h[h][rs] - r[h][i][:ROW_TILE] for i, rs in enumerate(tiles)], axis=0))
             for h in heads]
    qkv_new = [_dot(qk[h], v_new[h]) for h in heads]
    g_end = [jnp.broadcast_to(g_col[h].reshape(SUB // ROW_TILE, ROW_TILE, HEAD_DIM)[:, ROW_TILE - 1:, :],
                              (SUB // ROW_TILE, ROW_TILE, HEAD_DIM)).reshape(SUB, HEAD_DIM) for h in heads]
    k_dec_t = [(k[h] * jnp.exp(g_end[h] - g_col[h])).T for h in heads]
    seq_of_row = row_id // ROW_TILE
    for i in seqs:
        for h in heads:
            g_last = g_col[h][(i + 1) * ROW_TILE - 1:(i + 1) * ROW_TILE, :]
            s_out_ref[i, h] = (s_in_ref[i, h] * jnp.exp(g_last)
                               + _dot(k_dec_t[h], jnp.where(seq_of_row == i, v_new[h], 0.0)))
    ybs = []
    for h in heads:
        o = jnp.concatenate([r[h][i][ROW_TILE:] for i in seqs], axis=0) + qkv_new[h][:rows]
        ybs.append(_gated_rms(o, norm_w, z_act[:, sls[h]]).astype(BF16))
    yb = jnp.concatenate(ybs, axis=1)
    x2_ref[...] = _merge_out_ln(x, hb, a_part, yb, w_gates, b_gates, wb_ref, wo_ref,
                                ln_g, ln_b, alpha)


def _mix_sample(x1, cs_pad, s_in, p, alpha, nb):
    n = x1.shape[0]
    nseq = n // ROW_TILE
    assert nseq % nb == 0 and nb * ROW_TILE <= SUB
    rows = nb * ROW_TILE
    consts = [p['w_proj'], p['b_proj'],
              p['gm_v_g'], p['gm_v_b'], p['mix_coef'], p['mix_bias'], p['conv_w'], p['a_log'],
              p['dt_bias'], p['norm_w'], p['w_a'], p['w_b'], p['w_o'], p['ln2_g'], p['ln2_b']]
    state_spec = pl.BlockSpec((nb, HEADS, HEAD_DIM, HEAD_DIM), lambda i: (i, 0, 0, 0))
    return pl.pallas_call(
        functools.partial(_mix_sample_kernel, alpha=alpha, nb=nb),
        grid=(nseq // nb,),
        in_specs=[pl.BlockSpec((rows, D_MODEL), lambda i: (i, 0)),
                  pl.BlockSpec((rows, QKV), lambda i: (i, 0)),
                  state_spec] + [_const_spec(c.shape) for c in consts],
        out_specs=[pl.BlockSpec((rows, D_MODEL), lambda i: (i, 0)),
                   pl.BlockSpec((rows, D_MODEL), lambda i: (i, 0)),
                   pl.BlockSpec((rows, QKV), lambda i: (i, 0)),
                   state_spec],
        out_shape=[jax.ShapeDtypeStruct((n, D_MODEL), F32),
                   jax.ShapeDtypeStruct((n, D_MODEL), F32),
                   jax.ShapeDtypeStruct((n, QKV), F32),
                   jax.ShapeDtypeStruct(s_in.shape, F32)],
        compiler_params=pltpu.CompilerParams(dimension_semantics=("arbitrary",),
                                             vmem_limit_bytes=VMEM_LIMIT),
        name="mix_sample",
    )(x1, cs_pad, s_in, *consts)


def _pad_lanes(a, n=128):
    return jnp.pad(a, [(0, 0)] * (a.ndim - 1) + [(0, n - a.shape[-1])])


def _layer_params(l, ffn1_w_up, ffn1_w_down, ln1_g, ln1_b, w_in, b_in, gm_v_g, gm_v_b, gm_w_s,
                  gm_b_s, dn_conv_w, dn_a_log, dn_dt_bias, dn_norm_w, w_branch_a, w_branch_b,
                  w_out, ln2_g, ln2_b, ffn2_w_up, ffn2_w_down, ln3_g, ln3_b):
    row = lambda a: a[l][None, :].astype(F32)
    wi, bi = w_in[l], b_in[l]
    o_beta = MAIN_COLS
    o_dec = o_beta + HEADS
    o_gate = o_dec + HEADS
    ws = gm_w_s[l]
    lsm = DN_CONV
    shift = np.arange(lsm)[:, None]
    pos = np.arange(ROW_TILE)[None, :] - (ROW_TILE - lsm)
    live = (pos >= shift)
    coef = jnp.where(live[:, :, None],
                     jnp.transpose(ws[:, np.clip(pos + 0 * shift, 0, lsm - 1),
                                      np.clip(pos - shift, 0, lsm - 1)], (1, 2, 0)), 0.0)
    bias = jnp.pad(gm_b_s[l][:, :lsm].T, ((ROW_TILE - lsm, 0), (0, 0)))
    return {
        'ffn1': (ffn1_w_up[l].astype(BF16), ffn1_w_down[l].astype(BF16), row(ln1_g), row(ln1_b)),
        'ffn2': (ffn2_w_up[l].astype(BF16), ffn2_w_down[l].astype(BF16), row(ln3_g), row(ln3_b)),
        'w_proj': jnp.concatenate([wi[:, :MAIN_COLS], wi[:, o_gate:], _pad_lanes(wi[:, o_beta:o_dec]),
                                   _pad_lanes(wi[:, o_dec:o_gate])], axis=1).astype(BF16),
        'b_proj': jnp.concatenate([bi[:MAIN_COLS], bi[o_gate:], _pad_lanes(bi[o_beta:o_dec]),
                                   _pad_lanes(bi[o_dec:o_gate])])[None, :],
        'gm_v_g': row(gm_v_g), 'gm_v_b': row(gm_v_b),
        'gm_w_s': ws, 'gm_b_s_t': gm_b_s[l].T,
        'mix_coef': jnp.repeat(coef, GROUP_DIM, axis=-1), 'mix_bias': jnp.repeat(bias, GROUP_DIM, axis=-1),
        'conv_w': dn_conv_w[l],
        'a_log': _pad_lanes(dn_a_log[l][None, :].astype(F32)),
        'dt_bias': _pad_lanes(dn_dt_bias[l][None, :].astype(F32)),
        'norm_w': row(dn_norm_w),
        'w_a': w_branch_a[l].astype(BF16), 'w_b': w_branch_b[l].astype(BF16),
        'w_o': w_out[l].astype(BF16),
        'ln2_g': row(ln2_g), 'ln2_b': row(ln2_b),
    }


def kernel(x_prompt, x_sample, state_conv, state_ssm, ffn1_w_up, ffn1_w_down, ln1_g, ln1_b, w_in, b_in, gm_v_g, gm_v_b, gm_w_s, gm_b_s, dn_conv_w, dn_a_log, dn_dt_bias, dn_norm_w, w_branch_a, w_branch_b, w_out, ln2_g, ln2_b, ffn2_w_up, ffn2_w_down, ln3_g, ln3_b):
    depth = ffn1_w_up.shape[0]
    alpha = (2.0 * depth) ** 0.25
    bp, tp, _ = x_prompt.shape
    bs, ts, _ = x_sample.shape
    assert ts == DN_CONV and ts <= ROW_TILE - (DN_CONV - 1)
    lead = ROW_TILE - ts
    y_p = x_prompt
    y_s = jnp.pad(x_sample, ((0, 0), (lead, 0), (0, 0))).reshape(bs * ROW_TILE, D_MODEL)
    conv_p, ssm_p, conv_s, ssm_s, v_s = [], [], [], [], []
    for l in range(depth):
        p = _layer_params(l, ffn1_w_up, ffn1_w_down, ln1_g, ln1_b, w_in, b_in, gm_v_g, gm_v_b,
                          gm_w_s, gm_b_s, dn_conv_w, dn_a_log, dn_dt_bias, dn_norm_w, w_branch_a,
                          w_branch_b, w_out, ln2_g, ln2_b, ffn2_w_up, ffn2_w_down, ln3_g, ln3_b)
        x1 = _ffn_ln(y_p.reshape(bp * tp, D_MODEL), *p['ffn1'], alpha, 512).reshape(bp, tp, D_MODEL)
        x2, c_p, s_p = _mix_prompt(x1, p, alpha, 128, 2)
        y_p = _ffn_ln(x2.reshape(bp * tp, D_MODEL), *p['ffn2'], alpha, 512).reshape(bp, tp, D_MODEL)
        x1s = _ffn_ln(y_s, *p['ffn1'], alpha, 512)
        cs_pad = jnp.pad(state_conv[l], ((0, 0), (lead - (DN_CONV - 1), ts), (0, 0)))
        x2s, vrows, zfull, s_s = _mix_sample(x1s, cs_pad.reshape(bs * ROW_TILE, QKV),
                                             state_ssm[l], p, alpha, 8)
        y_s = _ffn_ln(x2s, *p['ffn2'], alpha, 512)
        conv_p.append(c_p)
        ssm_p.append(s_p)
        conv_s.append(zfull.reshape(bs, ROW_TILE, QKV)[:, ROW_TILE - (DN_CONV - 1):])
        ssm_s.append(s_s)
        v_s.append(vrows.reshape(bs, ROW_TILE, D_MODEL)[:, lead:])
    y_s_out = y_s.reshape(bs, ROW_TILE, D_MODEL)[:, lead:]
    return (y_p, y_s_out, jnp.stack(conv_p), jnp.stack(ssm_p), jnp.stack(conv_s), jnp.stack(ssm_s),
            jnp.stack(v_s))
```

```python
import functools
import math

import jax
import jax.numpy as jnp
import numpy as np
from jax import lax
from jax.experimental import pallas as pl
from jax.experimental.pallas import tpu as pltpu

F32 = jnp.float32
BF16 = jnp.bfloat16

D_MODEL = 1024
D_FF = 2816
HEADS = 8
HEAD_DIM = 128
GROUPS = 8
GROUP_DIM = 128
GM_CHUNK = 128
DN_CHUNK = 64
DN_CONV = 4
QKV = 3 * D_MODEL
MAIN_COLS = 6 * D_MODEL
LN_EPS = 1e-5
RMS_EPS = 1e-6

MXU_DIM = 256
SUB = 128
ROW_TILE = 8
LANES = 128
SLAB = 16
VMEM_LIMIT = 56 * 1024 * 1024
FFN_ROWS = 512
PROMPT_ROWS, PROMPT_SEQS = 128, 2
SAMPLE_SEQS = 8


def _sigmoid(x):
    return 0.5 * jnp.tanh(0.5 * x) + 0.5


def _silu(x):
    return x * _sigmoid(x)


def _gelu_tanh(x):
    c = math.sqrt(2.0 / math.pi)
    return x * (0.5 * (1.0 + jnp.tanh(c * (x + 0.044715 * (x * x * x)))))


def _softplus(x):
    return jnp.maximum(x, 0.0) + jnp.log(1.0 + jnp.exp(-jnp.abs(x)))


def _layer_norm(y, g, b):
    mu = jnp.mean(y, axis=-1, keepdims=True)
    yc = y - mu
    var = jnp.mean(yc * yc, axis=-1, keepdims=True)
    return yc * lax.rsqrt(var + LN_EPS) * g + b


def _dot(a, b):
    return jnp.dot(a.astype(BF16), b.astype(BF16), preferred_element_type=F32)


def _dot_nt(a, b):
    return lax.dot_general(a.astype(BF16), b.astype(BF16), (((1,), (1,)), ((), ())),
                           preferred_element_type=F32)


def _dot_exact_lhs(m01, x):
    hi = x.astype(BF16)
    r1 = x - hi.astype(F32)
    mid = r1.astype(BF16)
    lo = (r1 - mid.astype(F32)).astype(BF16)
    m = m01.astype(BF16)
    return (jnp.dot(m, hi, preferred_element_type=F32)
            + jnp.dot(m, mid, preferred_element_type=F32)
            + jnp.dot(m, lo, preferred_element_type=F32))


def _block_masks(n, blk):
    row = lax.broadcasted_iota(jnp.int32, (n, n), 0)
    col = lax.broadcasted_iota(jnp.int32, (n, n), 1)
    same = (row // blk) == (col // blk)
    return same & (row >= col), same & (row > col), row == col


def _inv_unit_lower(a, eye, n_iter):
    n = eye.shape[0]
    b = [-x for x in a]
    p = [eye + x for x in b]
    b = [_dot(x, x) for x in b]
    for _ in range(n_iter - 1):
        pb = [_dot(jnp.concatenate([pi, bi], axis=0), bi) for pi, bi in zip(p, b)]
        p = [pi + x[:n] for pi, x in zip(p, pb)]
        b = [x[n:] for x in pb]
    return [pi + _dot(pi, bi) for pi, bi in zip(p, b)]


def _dn_intra(q, k, v, beta, g_col, g_row, masks, n_iter):
    causal, strict, diag = masks
    heads = range(len(q))
    decay = [jnp.where(causal, jnp.exp(jnp.where(causal, g_col[h] - g_row[h], 0.0)), 0.0) for h in heads]
    kb = [k[h] * beta[h] for h in heads]
    kq = [_dot_nt(jnp.concatenate([kb[h], q[h]], axis=0), k[h]) for h in heads]
    a = [jnp.where(strict, kq[h][:SUB] * decay[h], 0.0) for h in heads]
    qk = [kq[h][SUB:] * decay[h] for h in heads]
    eye = jnp.where(diag, 1.0, 0.0).astype(F32)
    t_inv = _inv_unit_lower(a, eye, n_iter)
    e_g = [jnp.exp(g_col[h]) for h in heads]
    uw = [_dot(t_inv[h], jnp.concatenate([v[h] * beta[h], kb[h] * e_g[h]], axis=1)) for h in heads]
    return ([x[:, :HEAD_DIM] for x in uw], [x[:, HEAD_DIM:] for x in uw], qk,
            [q[h] * e_g[h] for h in heads])


class _Slabs:
    def __init__(self, fn, *xs, slab):
        self._outs = []
        self.thunks = [functools.partial(self._run, fn, xs, r, slab)
                       for r in range(0, xs[0].shape[0], slab)]

    def _run(self, fn, xs, r, slab):
        self._outs.append(fn(*[x[r:r + slab] for x in xs]))

    def result(self):
        assert len(self._outs) == len(self.thunks)
        return jnp.concatenate(self._outs, axis=0)


def _by_rows(fn, *xs, slab):
    job = _Slabs(fn, *xs, slab=slab)
    for th in job.thunks:
        th()
    return job.result()


def _piped_dot(lhs, w, lo, hi, work, pieces=4):
    step = (hi - lo) // pieces
    n = len(work)
    outs = []
    for p in range(pieces):
        outs.append(jnp.dot(lhs, w[:, lo + p * step:lo + (p + 1) * step], preferred_element_type=F32))
        for th in work[p * n // pieces:(p + 1) * n // pieces]:
            th()
    return jnp.concatenate(outs, axis=1)


def _conv_job(x, b, tail, w, post):
    c = x.shape[1]
    taps = w.shape[0]
    sub = lax.broadcasted_iota(jnp.int32, (ROW_TILE, c), 0)
    b_rows = jnp.broadcast_to(b, (ROW_TILE, c))
    wj = [jnp.broadcast_to(w[taps - 1 - j:taps - j, :], (ROW_TILE, c)) for j in range(taps)]
    state = {'prev': [pltpu.roll(tail, j, 0) for j in range(1, taps)], 'cur': None}

    def tile(raw):
        cur = raw + b_rows
        rolled = [pltpu.roll(cur, j, 0) for j in range(1, taps)]
        acc = cur * wj[0]
        for j in range(1, taps):
            acc = acc + jnp.where(sub < j, state['prev'][j - 1], rolled[j - 1]) * wj[j]
        state['prev'], state['cur'] = rolled, cur
        return post(acc)

    return _Slabs(tile, x, slab=ROW_TILE), state


def _gated_rms(o, norm_w, z_act):
    return o * lax.rsqrt(jnp.mean(o * o, axis=-1, keepdims=True) + RMS_EPS) * norm_w * z_act


def _l2n(x):
    return x * lax.rsqrt(jnp.sum(x * x, axis=-1, keepdims=True) + RMS_EPS)


def _ffn_chunks():
    n_tiles = D_FF // MXU_DIM
    first = (n_tiles + 1) // 2 * MXU_DIM
    return ((0, first), (first, D_FF))


def _ffn_ln_kernel(x_ref, wu_ref, wd_ref, g_ref, b_ref, o_ref, *, alpha):
    x = x_ref[...]
    xb = x.astype(BF16)
    acc = None
    for lo, hi in _ffn_chunks():
        a = jnp.dot(xb, wu_ref[:, lo:hi], preferred_element_type=F32)
        gt = jnp.dot(xb, wu_ref[:, D_FF + lo:D_FF + hi], preferred_element_type=F32)
        h = (_silu(a) * gt).astype(BF16)
        f = jnp.dot(h, wd_ref[lo:hi, :], preferred_element_type=F32)
        acc = f if acc is None else acc + f
    o_ref[...] = _layer_norm(alpha * x + 0.5 * acc, g_ref[...], b_ref[...])


def _const_spec(shape):
    nd = len(shape)
    return pl.BlockSpec(shape, lambda *_: (0,) * nd, pipeline_mode=pl.Buffered(1))


def _ffn_ln(x2d, wu, wd, g, b, alpha, tm):
    n = x2d.shape[0]
    assert n % tm == 0 and D_FF % MXU_DIM == 0
    return pl.pallas_call(
        functools.partial(_ffn_ln_kernel, alpha=alpha),
        grid=(n // tm,),
        in_specs=[pl.BlockSpec((tm, D_MODEL), lambda i: (i, 0)),
                  _const_spec(wu.shape), _const_spec(wd.shape),
                  _const_spec(g.shape), _const_spec(b.shape)],
        out_specs=pl.BlockSpec((tm, D_MODEL), lambda i: (i, 0)),
        out_shape=jax.ShapeDtypeStruct((n, D_MODEL), F32),
        compiler_params=pltpu.CompilerParams(dimension_semantics=("arbitrary",),
                                             vmem_limit_bytes=VMEM_LIMIT),
        name="ffn_ln",
    )(x2d, wu, wd, g, b)


def _proj_views(w_ref, b_ref):
    g0, g1 = MAIN_COLS, MAIN_COLS + 2 * D_MODEL
    return (w_ref.at[:, 0:g0], w_ref.at[:, g0:g1], w_ref.at[:, g1:g1 + 2 * LANES],
            b_ref.at[:, 0:g0], b_ref.at[:, g0:g1], b_ref.at[:, g1:g1 + 2 * LANES])


def _branch_gates_and_z(hb, w_main, b_main):
    z = jnp.dot(hb, w_main[:, 5 * D_MODEL:6 * D_MODEL], preferred_element_type=F32) \
        + b_main[:, 5 * D_MODEL:6 * D_MODEL]
    return _silu(z)


def _beta_and_logdecay(hb, w_bd, b_bd, alog, dtb):
    bd = jnp.dot(hb, w_bd[...], preferred_element_type=F32) + b_bd[...]
    beta = _sigmoid(bd[:, :128])
    g = -jnp.exp(alog[...]) * _softplus(bd[:, 128:] + dtb[...])
    return beta, g


def _merge_out_ln(x, hb, a_part, yb, w_gates, b_gates, wb_ref, wo_ref, ln_g, ln_b, alpha):
    gate_b = _sigmoid(jnp.dot(hb, w_gates[:, D_MODEL:], preferred_element_type=F32)
                      + b_gates[:, D_MODEL:])
    merged = a_part + gate_b * jnp.dot(yb, wb_ref[...], preferred_element_type=F32)
    mix = jnp.dot(merged.astype(BF16), wo_ref[...], preferred_element_type=F32)
    return _layer_norm(alpha * x + mix, ln_g[...], ln_b[...])


def _mix_prompt_kernel(x_ref, w_ref, b_ref, vg_ref, vb_ref,
                       ws_ref, bst_ref, convw_ref, alog_ref, dtb_ref, normw_ref,
                       wa_ref, wb_ref, wo_ref, ln_g, ln_b,
                       x2_ref, conv_out_ref, ssm_out_ref,
                       s_ref, xc_ref, q_s, k_s, v_s, z_s, g_s, beta_s, yb_s, *, alpha, tt, nseq):
    t = pl.program_id(1)
    nt = pl.num_programs(1)
    w_main, w_gates, w_bd, b_main, b_gates, b_bd = _proj_views(w_ref, b_ref)

    @pl.when(t == 0)
    def _():
        s_ref[...] = jnp.zeros(s_ref.shape, F32)
        xc_ref[...] = jnp.zeros(xc_ref.shape, F32)

    n_rows = nseq * tt
    x = x_ref[...].reshape(n_rows, D_MODEL)
    hb = x.astype(BF16)

    def proj(part, work):
        return _piped_dot(hb, w_main, part * D_MODEL, (part + 1) * D_MODEL, work)

    b_u = b_main[:, 0:D_MODEL]
    b_v, vg, vb = b_main[:, D_MODEL:2 * D_MODEL], vg_ref[...], vb_ref[...]
    pu = proj(0, [])
    job_u = _Slabs(lambda a: _gelu_tanh(a + b_u), pu, slab=SLAB)
    pv = proj(1, job_u.thunks)
    job_v = _Slabs(lambda a: _layer_norm(_gelu_tanh(a + b_v), vg, vb).astype(BF16), pv, slab=SLAB)
    pq = proj(2, job_v.thunks)
    u = job_u.result()
    vn = job_v.result()

    r128 = lax.broadcasted_iota(jnp.int32, (GM_CHUNK, GM_CHUNK), 0)
    c128 = lax.broadcasted_iota(jnp.int32, (GM_CHUNK, GM_CHUNK), 1)
    tril = r128 >= c128
    w_tril = [jnp.where(tril, ws_ref[g], 0.0).astype(BF16) for g in range(GROUPS)]
    rows = []
    for c in range(n_rows // GM_CHUNK):
        cols = []
        for g in range(GROUPS):
            blk = vn[c * GM_CHUNK:(c + 1) * GM_CHUNK, g * GROUP_DIM:(g + 1) * GROUP_DIM]
            cols.append(jnp.dot(w_tril[g], blk, preferred_element_type=F32) + bst_ref[:, g:g + 1])
        rows.append(jnp.concatenate(cols, axis=1))
    mixed = jnp.concatenate(rows, axis=0)

    scale = HEAD_DIM ** -0.5
    heads = [slice(h * HEAD_DIM, (h + 1) * HEAD_DIM) for h in range(HEADS)]

    class _ConvJobs:
        def __init__(self, raw, part, post):
            self.cols = slice(part * D_MODEL, (part + 1) * D_MODEL)
            b_p = b_main[:, (2 + part) * D_MODEL:(3 + part) * D_MODEL]
            self.jobs = [_conv_job(raw[sq * tt:(sq + 1) * tt], b_p, xc_ref[sq, :, self.cols],
                                   convw_ref[:, self.cols], lambda acc: post(_silu(acc)))
                         for sq in range(nseq)]
            self.thunks = [th for job, _ in self.jobs for th in job.thunks]

        def finish(self):
            for sq, (_, state) in enumerate(self.jobs):
                last = state['cur']

                @pl.when(t == nt - 1)
                def _():
                    conv_out_ref[sq, :, self.cols] = last[ROW_TILE - (DN_CONV - 1):, :]

                xc_ref[sq, :, self.cols] = last
            return jnp.concatenate([job.result() for job, _ in self.jobs], axis=0)

    def conv_job(raw, part, post):
        jobs = _ConvJobs(raw, part, post)
        return jobs, jobs.finish

    def l2n_heads(a, mul):
        return jnp.concatenate([_l2n(a[:, sl]) * mul for sl in heads], axis=1)

    job_q, finish_q = conv_job(pq, 0, lambda a: l2n_heads(a, scale))
    pk = proj(3, job_q.thunks)
    q_s[...] = finish_q()
    job_k, finish_k = conv_job(pk, 1, lambda a: l2n_heads(a, 1.0))
    pvv = proj(4, job_k.thunks)
    k_s[...] = finish_k()
    job_vv, finish_vv = conv_job(pvv, 2, lambda a: a)
    pz = proj(5, job_vv.thunks)
    v_s[...] = finish_vv()
    b_z = b_main[:, 5 * D_MODEL:6 * D_MODEL]
    job_z = _Slabs(lambda a: _silu(a + b_z), pz, slab=SLAB)
    job_ya = _Slabs(lambda a, m: (a * m).astype(BF16), u, mixed, slab=SLAB)
    pga = _piped_dot(hb, w_gates, 0, D_MODEL, job_ya.thunks + job_z.thunks)
    z_s[...] = job_z.result()
    pa = _piped_dot(job_ya.result(), wa_ref, 0, D_MODEL, [])
    b_ga = b_gates[:, :D_MODEL]
    job_a = _Slabs(lambda g, p_: _sigmoid(g + b_ga) * p_, pga, pa, slab=SLAB)
    pgb = _piped_dot(hb, w_gates, D_MODEL, 2 * D_MODEL, job_a.thunks)
    a_part = job_a.result()
    beta, g_log = _beta_and_logdecay(hb, w_bd, b_bd, alog_ref, dtb_ref)
    beta_s[...] = beta
    rt = lax.broadcasted_iota(jnp.int32, (n_rows, n_rows), 0)
    ct = lax.broadcasted_iota(jnp.int32, (n_rows, n_rows), 1)
    cum = jnp.where(((rt // DN_CHUNK) == (ct // DN_CHUNK)) & (rt >= ct), 1.0, 0.0)
    g_s[...] = _dot_exact_lhs(cum, g_log)

    masks = _block_masks(SUB, DN_CHUNK)
    norm_w = normw_ref[...]

    chains = [(sq, h) for sq in range(nseq) for h in range(HEADS)]
    sls = [slice(h * HEAD_DIM, (h + 1) * HEAD_DIM) for _, h in chains]
    n_chunks = SUB // DN_CHUNK
    row_chunk = lax.broadcasted_iota(jnp.int32, (SUB, HEAD_DIM), 0) // DN_CHUNK
    zeros = jnp.zeros((DN_CHUNK, HEAD_DIM), F32)
    ids = range(len(chains))
    state = [s_ref[sq, h] for sq, h in chains]
    for j in range(tt // SUB):
        rows = [slice(sq * tt + j * SUB, sq * tt + (j + 1) * SUB) for sq, _ in chains]
        g_sub = [g_s[sq * tt + j * SUB:sq * tt + (j + 1) * SUB, :] for sq in range(nseq)]
        g_t = [g.T for g in g_sub]
        b_sub = [beta_s[sq * tt + j * SUB:sq * tt + (j + 1) * SUB, :] for sq in range(nseq)]
        q = [q_s[rows[i], sls[i]] for i in ids]
        k = [k_s[rows[i], sls[i]] for i in ids]
        vv = [v_s[rows[i], sls[i]] for i in ids]
        g_col = [jnp.broadcast_to(g_sub[sq][:, h:h + 1], (SUB, HEAD_DIM)) for sq, h in chains]
        g_row = [jnp.broadcast_to(g_t[sq][h:h + 1, :], (SUB, SUB)) for sq, h in chains]
        beta_h = [jnp.broadcast_to(b_sub[sq][:, h:h + 1], (SUB, HEAD_DIM)) for sq, h in chains]
        u_h, w_h, qk, qe = _dn_intra(q, k, vv, beta_h, g_col, g_row, masks, 5)
        g_last = [[g_col[i][(c + 1) * DN_CHUNK - 1:(c + 1) * DN_CHUNK, :] for c in range(n_chunks)]
                  for i in ids]
        k_dec_t = []
        for i in ids:
            g_end = g_last[i][n_chunks - 1]
            for c in range(n_chunks - 2, -1, -1):
                g_end = jnp.where(row_chunk == c, g_last[i][c], g_end)
            k_dec_t.append((k[i] * jnp.exp(g_end - g_col[i])).T)
        outs = [[] for _ in ids]
        for c in range(n_chunks):
            rs = slice(c * DN_CHUNK, (c + 1) * DN_CHUNK)
            r = [_dot(jnp.concatenate([w_h[i][rs], qe[i][rs]], axis=0), state[i]) for i in ids]
            v_new = [u_h[i][rs] - r[i][:DN_CHUNK] for i in ids]
            v_pad = [jnp.concatenate([zeros] * c + [v_new[i]] + [zeros] * (n_chunks - 1 - c), axis=0)
                     for i in ids]
            m = [_dot(jnp.concatenate([qk[i][rs], k_dec_t[i]], axis=0), v_pad[i]) for i in ids]
            for i in ids:
                outs[i].append(r[i][DN_CHUNK:] + m[i][:DN_CHUNK])
            state = [state[i] * jnp.exp(g_last[i][c]) + m[i][DN_CHUNK:] for i in ids]
        for i in ids:
            o = jnp.concatenate(outs[i], axis=0)
            yb_s[rows[i], sls[i]] = _gated_rms(o, norm_w, z_s[rows[i], sls[i]]).astype(BF16)
    for i, (sq, h) in enumerate(chains):
        s_ref[sq, h] = state[i]

    @pl.when(t == nt - 1)
    def _():
        ssm_out_ref[...] = s_ref[...]

    pb = jnp.dot(yb_s[...], wb_ref[...], preferred_element_type=F32)
    b_gb = b_gates[:, D_MODEL:]
    merged = _by_rows(lambda a, g, p_: (a + _sigmoid(g + b_gb) * p_).astype(BF16), a_part, pgb, pb, slab=SLAB)
    mix = jnp.dot(merged, wo_ref[...], preferred_element_type=F32)
    ln_gain, ln_bias = ln_g[...], ln_b[...]
    x2 = _by_rows(lambda xx, m: _layer_norm(alpha * xx + m, ln_gain, ln_bias), x, mix, slab=SLAB)
    x2_ref[...] = x2.reshape(nseq, tt, D_MODEL)


def _mix_prompt(x1, p, alpha, tt, nseq):
    b, t, _ = x1.shape
    assert t % tt == 0 and tt % SUB == 0 and b % nseq == 0
    rows = nseq * tt
    consts = [p['w_proj'], p['b_proj'],
              p['gm_v_g'], p['gm_v_b'], p['gm_w_s'], p['gm_b_s_t'], p['conv_w'], p['a_log'],
              p['dt_bias'], p['norm_w'], p['w_a'], p['w_b'], p['w_o'], p['ln2_g'], p['ln2_b']]
    return pl.pallas_call(
        functools.partial(_mix_prompt_kernel, alpha=alpha, tt=tt, nseq=nseq),
        grid=(b // nseq, t // tt),
        in_specs=[pl.BlockSpec((nseq, tt, D_MODEL), lambda i, j: (i, j, 0))]
                 + [_const_spec(c.shape) for c in consts],
        out_specs=[pl.BlockSpec((nseq, tt, D_MODEL), lambda i, j: (i, j, 0)),
                   pl.BlockSpec((nseq, DN_CONV - 1, QKV), lambda i, j: (i, 0, 0)),
                   pl.BlockSpec((nseq, HEADS, HEAD_DIM, HEAD_DIM), lambda i, j: (i, 0, 0, 0))],
        out_shape=[jax.ShapeDtypeStruct((b, t, D_MODEL), F32),
                   jax.ShapeDtypeStruct((b, DN_CONV - 1, QKV), F32),
                   jax.ShapeDtypeStruct((b, HEADS, HEAD_DIM, HEAD_DIM), F32)],
        scratch_shapes=[pltpu.VMEM((nseq, HEADS, HEAD_DIM, HEAD_DIM), F32),
                        pltpu.VMEM((nseq, ROW_TILE, QKV), F32),
                        pltpu.VMEM((rows, D_MODEL), F32),
                        pltpu.VMEM((rows, D_MODEL), F32),
                        pltpu.VMEM((rows, D_MODEL), F32),
                        pltpu.VMEM((rows, D_MODEL), F32),
                        pltpu.VMEM((rows, 128), F32),
                        pltpu.VMEM((rows, 128), F32),
                        pltpu.VMEM((rows, D_MODEL), BF16)],
        compiler_params=pltpu.CompilerParams(dimension_semantics=("arbitrary", "arbitrary"),
                                             vmem_limit_bytes=VMEM_LIMIT),
        name="mix_prompt",
    )(x1, *consts)


def _mix_sample_kernel(x_ref, cs_ref, s_in_ref, w_ref, b_ref,
                       vg_ref, vb_ref, coef_ref, bias_ref, convw_ref, alog_ref, dtb_ref, normw_ref,
                       wa_ref, wb_ref, wo_ref, ln_g, ln_b,
                       x2_ref, vrow_ref, conv_out_ref, s_out_ref, *, alpha, nb, ts):
    rows = nb * ROW_TILE
    w_main, w_gates, w_bd, b_main, b_gates, b_bd = _proj_views(w_ref, b_ref)
    x = x_ref[...]
    hb = x.astype(BF16)
    valid = (lax.broadcasted_iota(jnp.int32, (rows, 1), 0) % ROW_TILE) < ts
    validf = jnp.where(valid, 1.0, 0.0).astype(F32)

    u = _gelu_tanh(jnp.dot(hb, w_main[:, 0:D_MODEL], preferred_element_type=F32)
                   + b_main[:, 0:D_MODEL])
    v = _gelu_tanh(jnp.dot(hb, w_main[:, D_MODEL:2 * D_MODEL], preferred_element_type=F32)
                   + b_main[:, D_MODEL:2 * D_MODEL])
    vn = _layer_norm(v, vg_ref[...], vb_ref[...])
    vn3 = vn.reshape(nb, ROW_TILE, D_MODEL)
    vrow_ref[...] = vn3[:, :ts, :]
    mixed = vn3 * coef_ref[0][None] + bias_ref[...][None]
    for j in range(1, DN_CONV):
        mixed = mixed + pltpu.roll(vn3, j, 1) * coef_ref[j][None]
    ya = (u * mixed.reshape(rows, D_MODEL)).astype(BF16)
    gate_a = _sigmoid(jnp.dot(hb, w_gates[:, :D_MODEL], preferred_element_type=F32)
                      + b_gates[:, :D_MODEL])
    a_part = gate_a * jnp.dot(ya, wa_ref[...], preferred_element_type=F32)

    qkv = jnp.dot(hb, w_main[:, 2 * D_MODEL:5 * D_MODEL], preferred_element_type=F32) \
        + b_main[:, 2 * D_MODEL:5 * D_MODEL]
    zfull = jnp.where(valid, qkv, 0.0) + cs_ref[...]
    z3 = zfull.reshape(nb, ROW_TILE, QKV)
    conv_out_ref[...] = z3[:, ts - (DN_CONV - 1):ts, :]
    acc = z3 * convw_ref[DN_CONV - 1:DN_CONV, :][None]
    for j in range(1, DN_CONV):
        acc = acc + pltpu.roll(z3, j, 1) * convw_ref[DN_CONV - 1 - j:DN_CONV - j, :][None]
    sact = _silu(acc.reshape(rows, QKV)) * validf
    z_act = _branch_gates_and_z(hb, w_main, b_main)
    beta, g_log = _beta_and_logdecay(hb, w_bd, b_bd, alog_ref, dtb_ref)
    beta = beta * validf
    g_log = g_log * validf
    rt = lax.broadcasted_iota(jnp.int32, (rows, rows), 0)
    ct = lax.broadcasted_iota(jnp.int32, (rows, rows), 1)
    cum = jnp.where(((rt // ROW_TILE) == (ct // ROW_TILE)) & (rt >= ct), 1.0, 0.0)
    g_cum = _dot_exact_lhs(cum, g_log)

    pad_rows = SUB - rows
    def pad(a):
        return jnp.concatenate([a, jnp.zeros((pad_rows, a.shape[1]), a.dtype)], axis=0)

    g_pad = pad(g_cum)
    g_t = g_pad.T
    beta_pad = pad(beta)
    masks = _block_masks(SUB, ROW_TILE)
    norm_w = normw_ref[...]
    scale = HEAD_DIM ** -0.5
    row_id = lax.broadcasted_iota(jnp.int32, (SUB, HEAD_DIM), 0)
    heads = range(HEADS)
    seqs = range(nb)
    sls = [slice(h * HEAD_DIM, (h + 1) * HEAD_DIM) for h in heads]
    q = [pad(_l2n(sact[:, sl]) * scale * validf) for sl in sls]
    k = [pad(_l2n(sact[:, D_MODEL + h * HEAD_DIM:D_MODEL + (h + 1) * HEAD_DIM]) * validf) for h in heads]
    vv = [pad(sact[:, 2 * D_MODEL + h * HEAD_DIM:2 * D_MODEL + (h + 1) * HEAD_DIM]) for h in heads]
    g_col = [jnp.broadcast_to(g_pad[:, h:h + 1], (SUB, HEAD_DIM)) for h in heads]
    g_row = [jnp.broadcast_to(g_t[h:h + 1, :], (SUB, SUB)) for h in heads]
    beta_h = [jnp.broadcast_to(beta_pad[:, h:h + 1], (SUB, HEAD_DIM)) for h in heads]
    u_h, w_h, qk, qe = _dn_intra(q, k, vv, beta_h, g_col, g_row, masks, 2)
    tiles = [slice(i * ROW_TILE, (i + 1) * ROW_TILE) for i in seqs]
    r = [[_dot(jnp.concatenate([w_h[h][rs], qe[h][rs]], axis=0), s_in_ref[i, h]) for i, rs in enumerate(tiles)]
         for h in heads]
    v_new = [pad(jnp.concatenate([u_h[h][rs] - r[h][i][:ROW_TILE] for i, rs in enumerate(tiles)], axis=0))
             for h in heads]
    qkv_new = [_dot(qk[h], v_new[h]) for h in heads]
    g_end = [jnp.broadcast_to(g_col[h].reshape(SUB // ROW_TILE, ROW_TILE, HEAD_DIM)[:, ROW_TILE - 1:, :],
                              (SUB // ROW_TILE, ROW_TILE, HEAD_DIM)).reshape(SUB, HEAD_DIM) for h in heads]
    k_dec_t = [(k[h] * jnp.exp(g_end[h] - g_col[h])).T for h in heads]
    seq_of_row = row_id // ROW_TILE
    for i in seqs:
        for h in heads:
            g_last = g_col[h][(i + 1) * ROW_TILE - 1:(i + 1) * ROW_TILE, :]
            s_out_ref[i, h] = (s_in_ref[i, h] * jnp.exp(g_last)
                               + _dot(k_dec_t[h], jnp.where(seq_of_row == i, v_new[h], 0.0)))
    ybs = []
    for h in heads:
        o = jnp.concatenate([r[h][i][ROW_TILE:] for i in seqs], axis=0) + qkv_new[h][:rows]
        ybs.append(_gated_rms(o, norm_w, z_act[:, sls[h]]).astype(BF16))
    yb = jnp.concatenate(ybs, axis=1)
    x2 = _merge_out_ln(x, hb, a_part, yb, w_gates, b_gates, wb_ref, wo_ref, ln_g, ln_b, alpha)
    x2_ref[...] = x2.reshape(nb, ROW_TILE, D_MODEL)[:, :ts, :]


def _mix_sample(x1, cs_pad, s_in, p, alpha, nb, ts):
    n = x1.shape[0]
    nseq = n // ROW_TILE
    assert nseq % nb == 0 and nb * ROW_TILE <= SUB
    rows = nb * ROW_TILE
    consts = [p['w_proj'], p['b_proj'],
              p['gm_v_g'], p['gm_v_b'], p['mix_coef'], p['mix_bias'], p['conv_w'], p['a_log'],
              p['dt_bias'], p['norm_w'], p['w_a'], p['w_b'], p['w_o'], p['ln2_g'], p['ln2_b']]
    state_spec = pl.BlockSpec((nb, HEADS, HEAD_DIM, HEAD_DIM), lambda i: (i, 0, 0, 0))
    token_spec = pl.BlockSpec((nb, ts, D_MODEL), lambda i: (i, 0, 0))
    return pl.pallas_call(
        functools.partial(_mix_sample_kernel, alpha=alpha, nb=nb, ts=ts),
        grid=(nseq // nb,),
        in_specs=[pl.BlockSpec((rows, D_MODEL), lambda i: (i, 0)),
                  pl.BlockSpec((rows, QKV), lambda i: (i, 0)),
                  state_spec] + [_const_spec(c.shape) for c in consts],
        out_specs=[token_spec, token_spec,
                   pl.BlockSpec((nb, DN_CONV - 1, QKV), lambda i: (i, 0, 0)),
                   state_spec],
        out_shape=[jax.ShapeDtypeStruct((nseq, ts, D_MODEL), F32),
                   jax.ShapeDtypeStruct((nseq, ts, D_MODEL), F32),
                   jax.ShapeDtypeStruct((nseq, DN_CONV - 1, QKV), F32),
                   jax.ShapeDtypeStruct(s_in.shape, F32)],
        compiler_params=pltpu.CompilerParams(dimension_semantics=("arbitrary",),
                                             vmem_limit_bytes=VMEM_LIMIT),
        name="mix_sample",
    )(x1, cs_pad, s_in, *consts)


def _pad_lanes(a, n=128):
    return jnp.pad(a, [(0, 0)] * (a.ndim - 1) + [(0, n - a.shape[-1])])


def _layer_params(l, ffn1_w_up, ffn1_w_down, ln1_g, ln1_b, w_in, b_in, gm_v_g, gm_v_b, gm_w_s,
                  gm_b_s, dn_conv_w, dn_a_log, dn_dt_bias, dn_norm_w, w_branch_a, w_branch_b,
                  w_out, ln2_g, ln2_b, ffn2_w_up, ffn2_w_down, ln3_g, ln3_b):
    row = lambda a: a[l][None, :].astype(F32)
    wi, bi = w_in[l], b_in[l]
    o_beta = MAIN_COLS
    o_dec = o_beta + HEADS
    o_gate = o_dec + HEADS
    ws = gm_w_s[l]
    lsm = DN_CONV
    shift = np.arange(lsm)[:, None]
    pos = np.arange(ROW_TILE)[None, :]
    live = (pos >= shift) & (pos < lsm)
    coef = jnp.where(live[:, :, None],
                     jnp.transpose(ws[:, np.clip(pos + 0 * shift, 0, lsm - 1),
                                      np.clip(pos - shift, 0, lsm - 1)], (1, 2, 0)), 0.0)
    bias = jnp.pad(gm_b_s[l][:, :lsm].T, ((0, ROW_TILE - lsm), (0, 0)))
    return {
        'ffn1': (ffn1_w_up[l].astype(BF16), ffn1_w_down[l].astype(BF16), row(ln1_g), row(ln1_b)),
        'ffn2': (ffn2_w_up[l].astype(BF16), ffn2_w_down[l].astype(BF16), row(ln3_g), row(ln3_b)),
        'w_proj': jnp.concatenate([wi[:, :MAIN_COLS], wi[:, o_gate:], _pad_lanes(wi[:, o_beta:o_dec]),
                                   _pad_lanes(wi[:, o_dec:o_gate])], axis=1).astype(BF16),
        'b_proj': jnp.concatenate([bi[:MAIN_COLS], bi[o_gate:], _pad_lanes(bi[o_beta:o_dec]),
                                   _pad_lanes(bi[o_dec:o_gate])])[None, :],
        'gm_v_g': row(gm_v_g), 'gm_v_b': row(gm_v_b),
        'gm_w_s': ws, 'gm_b_s_t': gm_b_s[l].T,
        'mix_coef': jnp.repeat(coef, GROUP_DIM, axis=-1), 'mix_bias': jnp.repeat(bias, GROUP_DIM, axis=-1),
        'conv_w': dn_conv_w[l],
        'a_log': _pad_lanes(dn_a_log[l][None, :].astype(F32)),
        'dt_bias': _pad_lanes(dn_dt_bias[l][None, :].astype(F32)),
        'norm_w': row(dn_norm_w),
        'w_a': w_branch_a[l].astype(BF16), 'w_b': w_branch_b[l].astype(BF16),
        'w_o': w_out[l].astype(BF16),
        'ln2_g': row(ln2_g), 'ln2_b': row(ln2_b),
    }


def kernel(x_prompt, x_sample, state_conv, state_ssm, ffn1_w_up, ffn1_w_down, ln1_g, ln1_b, w_in, b_in, gm_v_g, gm_v_b, gm_w_s, gm_b_s, dn_conv_w, dn_a_log, dn_dt_bias, dn_norm_w, w_branch_a, w_branch_b, w_out, ln2_g, ln2_b, ffn2_w_up, ffn2_w_down, ln3_g, ln3_b):
    depth = ffn1_w_up.shape[0]
    alpha = (2.0 * depth) ** 0.25
    bp, tp, _ = x_prompt.shape
    bs, ts, _ = x_sample.shape
    assert ts == DN_CONV and ts + (DN_CONV - 1) <= ROW_TILE
    y_p, y_s = x_prompt, x_sample
    conv_p, ssm_p, conv_s, ssm_s, v_s = [], [], [], [], []
    for l in range(depth):
        p = _layer_params(l, ffn1_w_up, ffn1_w_down, ln1_g, ln1_b, w_in, b_in, gm_v_g, gm_v_b,
                          gm_w_s, gm_b_s, dn_conv_w, dn_a_log, dn_dt_bias, dn_norm_w, w_branch_a,
                          w_branch_b, w_out, ln2_g, ln2_b, ffn2_w_up, ffn2_w_down, ln3_g, ln3_b)
        x1 = _ffn_ln(y_p.reshape(bp * tp, D_MODEL), *p['ffn1'], alpha, FFN_ROWS).reshape(bp, tp, D_MODEL)
        x2, c_p, s_p = _mix_prompt(x1, p, alpha, PROMPT_ROWS, PROMPT_SEQS)
        y_p = _ffn_ln(x2.reshape(bp * tp, D_MODEL), *p['ffn2'], alpha, FFN_ROWS).reshape(bp, tp, D_MODEL)
        x1s = _ffn_ln(y_s.reshape(bs * ts, D_MODEL), *p['ffn1'], alpha, min(FFN_ROWS, bs * ts))
        x1s = jnp.pad(x1s.reshape(bs, ts, D_MODEL), ((0, 0), (0, ROW_TILE - ts), (0, 0)))
        cs_pad = jnp.pad(state_conv[l], ((0, 0), (ROW_TILE - (DN_CONV - 1), 0), (0, 0)))
        x2s, vrows, c_s, s_s = _mix_sample(x1s.reshape(bs * ROW_TILE, D_MODEL),
                                           cs_pad.reshape(bs * ROW_TILE, QKV),
                                           state_ssm[l], p, alpha, SAMPLE_SEQS, ts)
        y_s = _ffn_ln(x2s.reshape(bs * ts, D_MODEL), *p['ffn2'], alpha,
                      min(FFN_ROWS, bs * ts)).reshape(bs, ts, D_MODEL)
        conv_p.append(c_p)
        ssm_p.append(s_p)
        conv_s.append(c_s)
        ssm_s.append(s_s)
        v_s.append(vrows)
    return (y_p, y_s, jnp.stack(conv_p), jnp.stack(ssm_p), jnp.stack(conv_s), jnp.stack(ssm_s),
            jnp.stack(v_s))
```

```python
import functools
import math

import jax
import jax.numpy as jnp
import numpy as np
from jax import lax
from jax.experimental import pallas as pl
from jax.experimental.pallas import tpu as pltpu

F32 = jnp.float32
BF16 = jnp.bfloat16

D_MODEL = 1024
D_FF = 2816
HEADS = 8
HEAD_DIM = 128
GROUPS = 8
GROUP_DIM = 128
GM_CHUNK = 128
DN_CHUNK = 64
DN_CONV = 4
QKV = 3 * D_MODEL
MAIN_COLS = 6 * D_MODEL
LN_EPS = 1e-5
RMS_EPS = 1e-6

MXU_DIM = 256
SUB = 128
ROW_TILE = 8
LANES = 128
SLAB = 16
PIECE_BUDGET = 200
MXU_PIECE_COST = 256
TICK_BUDGET = 600
VMEM_LIMIT = 56 * 1024 * 1024
FFN_ROWS = 512
PROMPT_ROWS, PROMPT_SEQS = 128, 2
SAMPLE_SEQS = 8


def _sigmoid(x):
    return 0.5 * jnp.tanh(0.5 * x) + 0.5


def _silu(x):
    return x * _sigmoid(x)


def _gelu_tanh(x):
    c = math.sqrt(2.0 / math.pi)
    return x * (0.5 * (1.0 + jnp.tanh(c * (x + 0.044715 * (x * x * x)))))


def _softplus(x):
    return jnp.maximum(x, 0.0) + jnp.log(1.0 + jnp.exp(-jnp.abs(x)))


def _layer_norm(y, g, b):
    mu = jnp.mean(y, axis=-1, keepdims=True)
    yc = y - mu
    var = jnp.mean(yc * yc, axis=-1, keepdims=True)
    return yc * lax.rsqrt(var + LN_EPS) * g + b


def _dot(a, b):
    return jnp.dot(a.astype(BF16), b.astype(BF16), preferred_element_type=F32)


def _dot_nt(a, b):
    return lax.dot_general(a.astype(BF16), b.astype(BF16), (((1,), (1,)), ((), ())),
                           preferred_element_type=F32)


def _dot_exact_lhs(m01, x):
    hi = x.astype(BF16)
    r1 = x - hi.astype(F32)
    mid = r1.astype(BF16)
    lo = (r1 - mid.astype(F32)).astype(BF16)
    m = m01.astype(BF16)
    return (jnp.dot(m, hi, preferred_element_type=F32)
            + jnp.dot(m, mid, preferred_element_type=F32)
            + jnp.dot(m, lo, preferred_element_type=F32))


def _block_masks(n, blk):
    row = lax.broadcasted_iota(jnp.int32, (n, n), 0)
    col = lax.broadcasted_iota(jnp.int32, (n, n), 1)
    same = (row // blk) == (col // blk)
    return same & (row >= col), same & (row > col), row == col


def _no_tick():
    pass


def _inv_unit_lower(a, eye, n_iter, tick=_no_tick):
    n = eye.shape[0]
    b = [-x for x in a]
    p = [eye + x for x in b]
    b = [_dot(x, x) for x in b]
    tick()
    for _ in range(n_iter - 1):
        pb = [_dot(jnp.concatenate([pi, bi], axis=0), bi) for pi, bi in zip(p, b)]
        tick()
        p = [pi + x[:n] for pi, x in zip(p, pb)]
        b = [x[n:] for x in pb]
    return [pi + _dot(pi, bi) for pi, bi in zip(p, b)]


def _dn_intra(q, k, v, beta, g_col, g_row, masks, n_iter, tick=_no_tick):
    causal, strict, diag = masks
    heads = range(len(q))
    decay = [jnp.where(causal, jnp.exp(jnp.where(causal, g_col[h] - g_row[h], 0.0)), 0.0) for h in heads]
    kb = [k[h] * beta[h] for h in heads]
    kq = [_dot_nt(jnp.concatenate([kb[h], q[h]], axis=0), k[h]) for h in heads]
    tick()
    a = [jnp.where(strict, kq[h][:SUB] * decay[h], 0.0) for h in heads]
    qk = [kq[h][SUB:] * decay[h] for h in heads]
    eye = jnp.where(diag, 1.0, 0.0).astype(F32)
    t_inv = _inv_unit_lower(a, eye, n_iter, tick)
    tick()
    e_g = [jnp.exp(g_col[h]) for h in heads]
    uw = [_dot(t_inv[h], jnp.concatenate([v[h] * beta[h], kb[h] * e_g[h]], axis=1)) for h in heads]
    tick()
    return ([x[:, :HEAD_DIM] for x in uw], [x[:, HEAD_DIM:] for x in uw], qk,
            [q[h] * e_g[h] for h in heads])


class _Slabs:
    def __init__(self, fn, *xs, slab):
        self._outs = []
        self.thunks = [functools.partial(self._run, fn, xs, r, slab)
                       for r in range(0, xs[0].shape[0], slab)]

    def _run(self, fn, xs, r, slab):
        self._outs.append(fn(*[x[r:r + slab] for x in xs]))

    def result(self):
        assert len(self._outs) == len(self.thunks)
        return jnp.concatenate(self._outs, axis=0)


def _by_rows(fn, *xs, slab):
    job = _Slabs(fn, *xs, slab=slab)
    for th in job.thunks:
        th()
    return job.result()


class _WorkQueue:
    def __init__(self):
        self._items = []

    def add(self, job, cost):
        self._items += [(cost, th, job) for th in job.thunks]

    def add_front(self, job, cost):
        self._items = [(cost, th, job) for th in job.thunks] + self._items

    def run(self, budget):
        while self._items and budget > 0:
            cost, th, _ = self._items.pop(0)
            th()
            budget -= cost

    def finish(self, job):
        while any(j is job for _, _, j in self._items):
            self._items.pop(0)[1]()


def _piped_dot(lhs, w, lo, hi, queue, pieces=4):
    step = (hi - lo) // pieces
    outs = []
    for p in range(pieces):
        outs.append(jnp.dot(lhs, w[:, lo + p * step:lo + (p + 1) * step], preferred_element_type=F32))
        queue.run(PIECE_BUDGET)
    return jnp.concatenate(outs, axis=1)


def _conv_job(x, b, tail, w, post):
    c = x.shape[1]
    taps = w.shape[0]
    sub = lax.broadcasted_iota(jnp.int32, (ROW_TILE, c), 0)
    b_rows = jnp.broadcast_to(b, (ROW_TILE, c))
    wj = [jnp.broadcast_to(w[taps - 1 - j:taps - j, :], (ROW_TILE, c)) for j in range(taps)]
    state = {'prev': [pltpu.roll(tail, j, 0) for j in range(1, taps)], 'cur': None}

    def tile(raw):
        cur = raw + b_rows
        rolled = [pltpu.roll(cur, j, 0) for j in range(1, taps)]
        acc = cur * wj[0]
        for j in range(1, taps):
            acc = acc + jnp.where(sub < j, state['prev'][j - 1], rolled[j - 1]) * wj[j]
        state['prev'], state['cur'] = rolled, cur
        return post(acc)

    return _Slabs(tile, x, slab=ROW_TILE), state


def _gated_rms(o, norm_w, z_act):
    return o * lax.rsqrt(jnp.mean(o * o, axis=-1, keepdims=True) + RMS_EPS) * norm_w * z_act


def _l2n(x):
    return x * lax.rsqrt(jnp.sum(x * x, axis=-1, keepdims=True) + RMS_EPS)


def _ffn_chunks():
    n_tiles = D_FF // MXU_DIM
    first = (n_tiles + 1) // 2 * MXU_DIM
    return ((0, first), (first, D_FF))


def _ffn_ln_kernel(x_ref, wu_ref, wd_ref, g_ref, b_ref, o_ref, *, alpha):
    x = x_ref[...]
    xb = x.astype(BF16)
    acc = None
    for lo, hi in _ffn_chunks():
        a = jnp.dot(xb, wu_ref[:, lo:hi], preferred_element_type=F32)
        gt = jnp.dot(xb, wu_ref[:, D_FF + lo:D_FF + hi], preferred_element_type=F32)
        h = (_silu(a) * gt).astype(BF16)
        f = jnp.dot(h, wd_ref[lo:hi, :], preferred_element_type=F32)
        acc = f if acc is None else acc + f
    o_ref[...] = _layer_norm(alpha * x + 0.5 * acc, g_ref[...], b_ref[...])


def _const_spec(shape):
    nd = len(shape)
    return pl.BlockSpec(shape, lambda *_: (0,) * nd, pipeline_mode=pl.Buffered(1))


def _ffn_ln(x2d, wu, wd, g, b, alpha, tm):
    n = x2d.shape[0]
    assert n % tm == 0 and D_FF % MXU_DIM == 0
    return pl.pallas_call(
        functools.partial(_ffn_ln_kernel, alpha=alpha),
        grid=(n // tm,),
        in_specs=[pl.BlockSpec((tm, D_MODEL), lambda i: (i, 0)),
                  _const_spec(wu.shape), _const_spec(wd.shape),
                  _const_spec(g.shape), _const_spec(b.shape)],
        out_specs=pl.BlockSpec((tm, D_MODEL), lambda i: (i, 0)),
        out_shape=jax.ShapeDtypeStruct((n, D_MODEL), F32),
        compiler_params=pltpu.CompilerParams(dimension_semantics=("arbitrary",),
                                             vmem_limit_bytes=VMEM_LIMIT),
        name="ffn_ln",
    )(x2d, wu, wd, g, b)


def _proj_views(w_ref, b_ref):
    g0, g1 = MAIN_COLS, MAIN_COLS + 2 * D_MODEL
    return (w_ref.at[:, 0:g0], w_ref.at[:, g0:g1], w_ref.at[:, g1:g1 + 2 * LANES],
            b_ref.at[:, 0:g0], b_ref.at[:, g0:g1], b_ref.at[:, g1:g1 + 2 * LANES])


def _branch_gates_and_z(hb, w_main, b_main):
    z = jnp.dot(hb, w_main[:, 5 * D_MODEL:6 * D_MODEL], preferred_element_type=F32) \
        + b_main[:, 5 * D_MODEL:6 * D_MODEL]
    return _silu(z)


def _beta_and_logdecay(hb, w_bd, b_bd, alog, dtb):
    bd = jnp.dot(hb, w_bd[...], preferred_element_type=F32) + b_bd[...]
    beta = _sigmoid(bd[:, :128])
    g = -jnp.exp(alog[...]) * _softplus(bd[:, 128:] + dtb[...])
    return beta, g


def _merge_out_ln(x, hb, a_part, yb, w_gates, b_gates, wb_ref, wo_ref, ln_g, ln_b, alpha):
    gate_b = _sigmoid(jnp.dot(hb, w_gates[:, D_MODEL:], preferred_element_type=F32)
                      + b_gates[:, D_MODEL:])
    merged = a_part + gate_b * jnp.dot(yb, wb_ref[...], preferred_element_type=F32)
    mix = jnp.dot(merged.astype(BF16), wo_ref[...], preferred_element_type=F32)
    return _layer_norm(alpha * x + mix, ln_g[...], ln_b[...])


def _mix_prompt_kernel(x_ref, w_ref, b_ref, vg_ref, vb_ref,
                       ws_ref, bst_ref, convw_ref, alog_ref, dtb_ref, normw_ref,
                       wa_ref, wb_ref, wo_ref, ln_g, ln_b,
                       x2_ref, conv_out_ref, ssm_out_ref,
                       s_ref, xc_ref, q_s, k_s, v_s, z_s, g_s, beta_s, yb_s, *, alpha, tt, nseq):
    t = pl.program_id(1)
    nt = pl.num_programs(1)
    w_main, w_gates, w_bd, b_main, b_gates, b_bd = _proj_views(w_ref, b_ref)

    @pl.when(t == 0)
    def _():
        s_ref[...] = jnp.zeros(s_ref.shape, F32)
        xc_ref[...] = jnp.zeros(xc_ref.shape, F32)

    n_rows = nseq * tt
    x = x_ref[...].reshape(n_rows, D_MODEL)
    hb = x.astype(BF16)

    queue = _WorkQueue()

    def proj(part):
        return _piped_dot(hb, w_main, part * D_MODEL, (part + 1) * D_MODEL, queue)

    scale = HEAD_DIM ** -0.5
    heads = [slice(h * HEAD_DIM, (h + 1) * HEAD_DIM) for h in range(HEADS)]

    class _ConvJobs:
        def __init__(self, raw, part, post):
            self.cols = slice(part * D_MODEL, (part + 1) * D_MODEL)
            b_p = b_main[:, (2 + part) * D_MODEL:(3 + part) * D_MODEL]
            self.jobs = [_conv_job(raw[sq * tt:(sq + 1) * tt], b_p, xc_ref[sq, :, self.cols],
                                   convw_ref[:, self.cols], lambda acc: post(_silu(acc)))
                         for sq in range(nseq)]
            self.thunks = [th for job, _ in self.jobs for th in job.thunks]

        def result(self):
            for sq, (_, state) in enumerate(self.jobs):
                last = state['cur']

                @pl.when(t == nt - 1)
                def _():
                    conv_out_ref[sq, :, self.cols] = last[ROW_TILE - (DN_CONV - 1):, :]

                xc_ref[sq, :, self.cols] = last
            return jnp.concatenate([job.result() for job, _ in self.jobs], axis=0)

    def l2n_heads(a, mul):
        return jnp.concatenate([_l2n(a[:, sl]) * mul for sl in heads], axis=1)

    b_z = b_main[:, 5 * D_MODEL:6 * D_MODEL]
    pq = proj(2)
    job_q = _ConvJobs(pq, 0, lambda a: l2n_heads(a, scale))
    queue.add(job_q, 45)
    pk = proj(3)
    job_k = _ConvJobs(pk, 1, lambda a: l2n_heads(a, 1.0))
    queue.add(job_k, 45)
    pvv = proj(4)
    job_vv = _ConvJobs(pvv, 2, lambda a: a)
    queue.add(job_vv, 35)
    pz = proj(5)
    job_z = _Slabs(lambda a: _silu(a + b_z), pz, slab=SLAB)
    queue.add(job_z, 20)
    pgb = _piped_dot(hb, w_gates, D_MODEL, 2 * D_MODEL, queue)
    beta, g_log = _beta_and_logdecay(hb, w_bd, b_bd, alog_ref, dtb_ref)
    beta_s[...] = beta
    queue.run(PIECE_BUDGET)
    rt = lax.broadcasted_iota(jnp.int32, (n_rows, n_rows), 0)
    ct = lax.broadcasted_iota(jnp.int32, (n_rows, n_rows), 1)
    cum = jnp.where(((rt // DN_CHUNK) == (ct // DN_CHUNK)) & (rt >= ct), 1.0, 0.0)
    g_s[...] = _dot_exact_lhs(cum, g_log)
    queue.finish(job_z)
    q_s[...] = job_q.result()
    k_s[...] = job_k.result()
    v_s[...] = job_vv.result()
    z_s[...] = job_z.result()

    branch_a = {}

    def dot_steps(lhs, w, lo, hi, pieces=4):
        step = (hi - lo) // pieces
        outs = []
        for p in range(pieces):
            outs.append(jnp.dot(lhs, w[:, lo + p * step:lo + (p + 1) * step], preferred_element_type=F32))
            yield MXU_PIECE_COST
        return jnp.concatenate(outs, axis=1)

    def job_steps(job, cost):
        for th in job.thunks:
            th()
            yield cost
        return job.result()

    def branch_a_steps():
        b_u = b_main[:, 0:D_MODEL]
        b_v, vg, vb = b_main[:, D_MODEL:2 * D_MODEL], vg_ref[...], vb_ref[...]
        b_ga = b_gates[:, :D_MODEL]
        pu = yield from dot_steps(hb, w_main, 0, D_MODEL)
        pv = yield from dot_steps(hb, w_main, D_MODEL, 2 * D_MODEL)
        vn = yield from job_steps(
            _Slabs(lambda a: _layer_norm(_gelu_tanh(a + b_v), vg, vb).astype(BF16), pv, slab=SLAB), 70)
        r128 = lax.broadcasted_iota(jnp.int32, (GM_CHUNK, GM_CHUNK), 0)
        c128 = lax.broadcasted_iota(jnp.int32, (GM_CHUNK, GM_CHUNK), 1)
        tril = r128 >= c128
        w_tril = [jnp.where(tril, ws_ref[g], 0.0).astype(BF16) for g in range(GROUPS)]
        rows = []
        for c in range(n_rows // GM_CHUNK):
            cols = []
            for g in range(GROUPS):
                blk = vn[c * GM_CHUNK:(c + 1) * GM_CHUNK, g * GROUP_DIM:(g + 1) * GROUP_DIM]
                cols.append(jnp.dot(w_tril[g], blk, preferred_element_type=F32) + bst_ref[:, g:g + 1])
            rows.append(jnp.concatenate(cols, axis=1))
            yield MXU_PIECE_COST
        mixed = jnp.concatenate(rows, axis=0)
        u = yield from job_steps(_Slabs(lambda a: _gelu_tanh(a + b_u), pu, slab=SLAB), 50)
        ya = yield from job_steps(_Slabs(lambda a, m: (a * m).astype(BF16), u, mixed, slab=SLAB), 10)
        pa = yield from dot_steps(ya, wa_ref, 0, D_MODEL)
        pga = yield from dot_steps(hb, w_gates, 0, D_MODEL)
        branch_a['a_part'] = yield from job_steps(
            _Slabs(lambda g, p_: _sigmoid(g + b_ga) * p_, pga, pa, slab=SLAB), 25)

    steps = branch_a_steps()

    def tick(budget=TICK_BUDGET):
        while budget > 0:
            cost = next(steps, None)
            if cost is None:
                return
            budget -= cost

    masks = _block_masks(SUB, DN_CHUNK)
    norm_w = normw_ref[...]

    chains = [(sq, h) for sq in range(nseq) for h in range(HEADS)]
    sls = [slice(h * HEAD_DIM, (h + 1) * HEAD_DIM) for _, h in chains]
    n_chunks = SUB // DN_CHUNK
    row_chunk = lax.broadcasted_iota(jnp.int32, (SUB, HEAD_DIM), 0) // DN_CHUNK
    zeros = jnp.zeros((DN_CHUNK, HEAD_DIM), F32)
    ids = range(len(chains))
    state = [s_ref[sq, h] for sq, h in chains]
    for j in range(tt // SUB):
        rows = [slice(sq * tt + j * SUB, sq * tt + (j + 1) * SUB) for sq, _ in chains]
        g_sub = [g_s[sq * tt + j * SUB:sq * tt + (j + 1) * SUB, :] for sq in range(nseq)]
        g_t = [g.T for g in g_sub]
        b_sub = [beta_s[sq * tt + j * SUB:sq * tt + (j + 1) * SUB, :] for sq in range(nseq)]
        q = [q_s[rows[i], sls[i]] for i in ids]
        k = [k_s[rows[i], sls[i]] for i in ids]
        vv = [v_s[rows[i], sls[i]] for i in ids]
        g_col = [jnp.broadcast_to(g_sub[sq][:, h:h + 1], (SUB, HEAD_DIM)) for sq, h in chains]
        g_row = [jnp.broadcast_to(g_t[sq][h:h + 1, :], (SUB, SUB)) for sq, h in chains]
        beta_h = [jnp.broadcast_to(b_sub[sq][:, h:h + 1], (SUB, HEAD_DIM)) for sq, h in chains]
        u_h, w_h, qk, qe = _dn_intra(q, k, vv, beta_h, g_col, g_row, masks, 5, tick)
        g_last = [[g_col[i][(c + 1) * DN_CHUNK - 1:(c + 1) * DN_CHUNK, :] for c in range(n_chunks)]
                  for i in ids]
        k_dec_t = []
        for i in ids:
            g_end = g_last[i][n_chunks - 1]
            for c in range(n_chunks - 2, -1, -1):
                g_end = jnp.where(row_chunk == c, g_last[i][c], g_end)
            k_dec_t.append((k[i] * jnp.exp(g_end - g_col[i])).T)
        outs = [[] for _ in ids]
        for c in range(n_chunks):
            rs = slice(c * DN_CHUNK, (c + 1) * DN_CHUNK)
            r = [_dot(jnp.concatenate([w_h[i][rs], qe[i][rs]], axis=0), state[i]) for i in ids]
            tick()
            v_new = [u_h[i][rs] - r[i][:DN_CHUNK] for i in ids]
            v_pad = [jnp.concatenate([zeros] * c + [v_new[i]] + [zeros] * (n_chunks - 1 - c), axis=0)
                     for i in ids]
            m = [_dot(jnp.concatenate([qk[i][rs], k_dec_t[i]], axis=0), v_pad[i]) for i in ids]
            tick()
            for i in ids:
                outs[i].append(r[i][DN_CHUNK:] + m[i][:DN_CHUNK])
            state = [state[i] * jnp.exp(g_last[i][c]) + m[i][DN_CHUNK:] for i in ids]
        for i in ids:
            o = jnp.concatenate(outs[i], axis=0)
            yb_s[rows[i], sls[i]] = _gated_rms(o, norm_w, z_s[rows[i], sls[i]]).astype(BF16)
    for i, (sq, h) in enumerate(chains):
        s_ref[sq, h] = state[i]
    tick(float('inf'))
    a_part = branch_a['a_part']

    @pl.when(t == nt - 1)
    def _():
        ssm_out_ref[...] = s_ref[...]

    pb = jnp.dot(yb_s[...], wb_ref[...], preferred_element_type=F32)
    b_gb = b_gates[:, D_MODEL:]
    merged = _by_rows(lambda a, g, p_: (a + _sigmoid(g + b_gb) * p_).astype(BF16), a_part, pgb, pb, slab=SLAB)
    mix = jnp.dot(merged, wo_ref[...], preferred_element_type=F32)
    ln_gain, ln_bias = ln_g[...], ln_b[...]
    x2 = _by_rows(lambda xx, m: _layer_norm(alpha * xx + m, ln_gain, ln_bias), x, mix, slab=SLAB)
    x2_ref[...] = x2.reshape(nseq, tt, D_MODEL)


def _mix_prompt(x1, p, alpha, tt, nseq):
    b, t, _ = x1.shape
    assert t % tt == 0 and tt % SUB == 0 and b % nseq == 0
    rows = nseq * tt
    consts = [p['w_proj'], p['b_proj'],
              p['gm_v_g'], p['gm_v_b'], p['gm_w_s'], p['gm_b_s_t'], p['conv_w'], p['a_log'],
              p['dt_bias'], p['norm_w'], p['w_a'], p['w_b'], p['w_o'], p['ln2_g'], p['ln2_b']]
    return pl.pallas_call(
        functools.partial(_mix_prompt_kernel, alpha=alpha, tt=tt, nseq=nseq),
        grid=(b // nseq, t // tt),
        in_specs=[pl.BlockSpec((nseq, tt, D_MODEL), lambda i, j: (i, j, 0))]
                 + [_const_spec(c.shape) for c in consts],
        out_specs=[pl.BlockSpec((nseq, tt, D_MODEL), lambda i, j: (i, j, 0)),
                   pl.BlockSpec((nseq, DN_CONV - 1, QKV), lambda i, j: (i, 0, 0)),
                   pl.BlockSpec((nseq, HEADS, HEAD_DIM, HEAD_DIM), lambda i, j: (i, 0, 0, 0))],
        out_shape=[jax.ShapeDtypeStruct((b, t, D_MODEL), F32),
                   jax.ShapeDtypeStruct((b, DN_CONV - 1, QKV), F32),
                   jax.ShapeDtypeStruct((b, HEADS, HEAD_DIM, HEAD_DIM), F32)],
        scratch_shapes=[pltpu.VMEM((nseq, HEADS, HEAD_DIM, HEAD_DIM), F32),
                        pltpu.VMEM((nseq, ROW_TILE, QKV), F32),
                        pltpu.VMEM((rows, D_MODEL), F32),
                        pltpu.VMEM((rows, D_MODEL), F32),
                        pltpu.VMEM((rows, D_MODEL), F32),
                        pltpu.VMEM((rows, D_MODEL), F32),
                        pltpu.VMEM((rows, 128), F32),
                        pltpu.VMEM((rows, 128), F32),
                        pltpu.VMEM((rows, D_MODEL), BF16)],
        compiler_params=pltpu.CompilerParams(dimension_semantics=("arbitrary", "arbitrary"),
                                             vmem_limit_bytes=VMEM_LIMIT),
        name="mix_prompt",
    )(x1, *consts)


def _mix_sample_kernel(x_ref, cs_ref, s_in_ref, w_ref, b_ref,
                       vg_ref, vb_ref, coef_ref, bias_ref, convw_ref, alog_ref, dtb_ref, normw_ref,
                       wa_ref, wb_ref, wo_ref, ln_g, ln_b,
                       x2_ref, vrow_ref, conv_out_ref, s_out_ref, *, alpha, nb, ts):
    rows = nb * ROW_TILE
    w_main, w_gates, w_bd, b_main, b_gates, b_bd = _proj_views(w_ref, b_ref)
    x = x_ref[...]
    hb = x.astype(BF16)
    valid = (lax.broadcasted_iota(jnp.int32, (rows, 1), 0) % ROW_TILE) < ts
    validf = jnp.where(valid, 1.0, 0.0).astype(F32)

    u = _gelu_tanh(jnp.dot(hb, w_main[:, 0:D_MODEL], preferred_element_type=F32)
                   + b_main[:, 0:D_MODEL])
    v = _gelu_tanh(jnp.dot(hb, w_main[:, D_MODEL:2 * D_MODEL], preferred_element_type=F32)
                   + b_main[:, D_MODEL:2 * D_MODEL])
    vn = _layer_norm(v, vg_ref[...], vb_ref[...])
    vn3 = vn.reshape(nb, ROW_TILE, D_MODEL)
    vrow_ref[...] = vn3[:, :ts, :]
    mixed = vn3 * coef_ref[0][None] + bias_ref[...][None]
    for j in range(1, DN_CONV):
        mixed = mixed + pltpu.roll(vn3, j, 1) * coef_ref[j][None]
    ya = (u * mixed.reshape(rows, D_MODEL)).astype(BF16)
    gate_a = _sigmoid(jnp.dot(hb, w_gates[:, :D_MODEL], preferred_element_type=F32)
                      + b_gates[:, :D_MODEL])
    a_part = gate_a * jnp.dot(ya, wa_ref[...], preferred_element_type=F32)

    qkv = jnp.dot(hb, w_main[:, 2 * D_MODEL:5 * D_MODEL], preferred_element_type=F32) \
        + b_main[:, 2 * D_MODEL:5 * D_MODEL]
    zfull = jnp.where(valid, qkv, 0.0) + cs_ref[...]
    z3 = zfull.reshape(nb, ROW_TILE, QKV)
    conv_out_ref[...] = z3[:, ts - (DN_CONV - 1):ts, :]
    acc = z3 * convw_ref[DN_CONV - 1:DN_CONV, :][None]
    for j in range(1, DN_CONV):
        acc = acc + pltpu.roll(z3, j, 1) * convw_ref[DN_CONV - 1 - j:DN_CONV - j, :][None]
    sact = _silu(acc.reshape(rows, QKV)) * validf
    z_act = _branch_gates_and_z(hb, w_main, b_main)
    beta, g_log = _beta_and_logdecay(hb, w_bd, b_bd, alog_ref, dtb_ref)
    beta = beta * validf
    g_log = g_log * validf
    rt = lax.broadcasted_iota(jnp.int32, (rows, rows), 0)
    ct = lax.broadcasted_iota(jnp.int32, (rows, rows), 1)
    cum = jnp.where(((rt // ROW_TILE) == (ct // ROW_TILE)) & (rt >= ct), 1.0, 0.0)
    g_cum = _dot_exact_lhs(cum, g_log)

    pad_rows = SUB - rows
    def pad(a):
        return jnp.concatenate([a, jnp.zeros((pad_rows, a.shape[1]), a.dtype)], axis=0)

    g_pad = pad(g_cum)
    g_t = g_pad.T
    beta_pad = pad(beta)
    masks = _block_masks(SUB, ROW_TILE)
    norm_w = normw_ref[...]
    scale = HEAD_DIM ** -0.5
    row_id = lax.broadcasted_iota(jnp.int32, (SUB, HEAD_DIM), 0)
    heads = range(HEADS)
    seqs = range(nb)
    sls = [slice(h * HEAD_DIM, (h + 1) * HEAD_DIM) for h in heads]
    q = [pad(_l2n(sact[:, sl]) * scale * validf) for sl in sls]
    k = [pad(_l2n(sact[:, D_MODEL + h * HEAD_DIM:D_MODEL + (h + 1) * HEAD_DIM]) * validf) for h in heads]
    vv = [pad(sact[:, 2 * D_MODEL + h * HEAD_DIM:2 * D_MODEL + (h + 1) * HEAD_DIM]) for h in heads]
    g_col = [jnp.broadcast_to(g_pad[:, h:h + 1], (SUB, HEAD_DIM)) for h in heads]
    g_row = [jnp.broadcast_to(g_t[h:h + 1, :], (SUB, SUB)) for h in heads]
    beta_h = [jnp.broadcast_to(beta_pad[:, h:h + 1], (SUB, HEAD_DIM)) for h in heads]
    u_h, w_h, qk, qe = _dn_intra(q, k, vv, beta_h, g_col, g_row, masks, 2)
    tiles = [slice(i * ROW_TILE, (i + 1) * ROW_TILE) for i in seqs]
    r = [[_dot(jnp.concatenate([w_h[h][rs], qe[h][rs]], axis=0), s_in_ref[i, h]) for i, rs in enumerate(tiles)]
         for h in heads]
    v_new = [pad(jnp.concatenate([u_h[h][rs] - r[h][i][:ROW_TILE] for i, rs in enumerate(tiles)], axis=0))
             for h in heads]
    qkv_new = [_dot(qk[h], v_new[h]) for h in heads]
    g_end = [jnp.broadcast_to(g_col[h].reshape(SUB // ROW_TILE, ROW_TILE, HEAD_DIM)[:, ROW_TILE - 1:, :],
                              (SUB // ROW_TILE, ROW_TILE, HEAD_DIM)).reshape(SUB, HEAD_DIM) for h in heads]
    k_dec_t = [(k[h] * jnp.exp(g_end[h] - g_col[h])).T for h in heads]
    seq_of_row = row_id // ROW_TILE
    for i in seqs:
        for h in heads:
            g_last = g_col[h][(i + 1) * ROW_TILE - 1:(i + 1) * ROW_TILE, :]
            s_out_ref[i, h] = (s_in_ref[i, h] * jnp.exp(g_last)
                               + _dot(k_dec_t[h], jnp.where(seq_of_row == i, v_new[h], 0.0)))
    ybs = []
    for h in heads:
        o = jnp.concatenate([r[h][i][ROW_TILE:] for i in seqs], axis=0) + qkv_new[h][:rows]
        ybs.append(_gated_rms(o, norm_w, z_act[:, sls[h]]).astype(BF16))
    yb = jnp.concatenate(ybs, axis=1)
    x2 = _merge_out_ln(x, hb, a_part, yb, w_gates, b_gates, wb_ref, wo_ref, ln_g, ln_b, alpha)
    x2_ref[...] = x2.reshape(nb, ROW_TILE, D_MODEL)[:, :ts, :]


def _mix_sample(x1, cs_pad, s_in, p, alpha, nb, ts):
    n = x1.shape[0]
    nseq = n // ROW_TILE
    assert nseq % nb == 0 and nb * ROW_TILE <= SUB
    rows = nb * ROW_TILE
    consts = [p['w_proj'], p['b_proj'],
              p['gm_v_g'], p['gm_v_b'], p['mix_coef'], p['mix_bias'], p['conv_w'], p['a_log'],
              p['dt_bias'], p['norm_w'], p['w_a'], p['w_b'], p['w_o'], p['ln2_g'], p['ln2_b']]
    state_spec = pl.BlockSpec((nb, HEADS, HEAD_DIM, HEAD_DIM), lambda i: (i, 0, 0, 0))
    token_spec = pl.BlockSpec((nb, ts, D_MODEL), lambda i: (i, 0, 0))
    return pl.pallas_call(
        functools.partial(_mix_sample_kernel, alpha=alpha, nb=nb, ts=ts),
        grid=(nseq // nb,),
        in_specs=[pl.BlockSpec((rows, D_MODEL), lambda i: (i, 0)),
                  pl.BlockSpec((rows, QKV), lambda i: (i, 0)),
                  state_spec] + [_const_spec(c.shape) for c in consts],
        out_specs=[token_spec, token_spec,
                   pl.BlockSpec((nb, DN_CONV - 1, QKV), lambda i: (i, 0, 0)),
                   state_spec],
        out_shape=[jax.ShapeDtypeStruct((nseq, ts, D_MODEL), F32),
                   jax.ShapeDtypeStruct((nseq, ts, D_MODEL), F32),
                   jax.ShapeDtypeStruct((nseq, DN_CONV - 1, QKV), F32),
                   jax.ShapeDtypeStruct(s_in.shape, F32)],
        compiler_params=pltpu.CompilerParams(dimension_semantics=("arbitrary",),
                                             vmem_limit_bytes=VMEM_LIMIT),
        name="mix_sample",
    )(x1, cs_pad, s_in, *consts)


def _pad_lanes(a, n=128):
    return jnp.pad(a, [(0, 0)] * (a.ndim - 1) + [(0, n - a.shape[-1])])


def _layer_params(l, ffn1_w_up, ffn1_w_down, ln1_g, ln1_b, w_in, b_in, gm_v_g, gm_v_b, gm_w_s,
                  gm_b_s, dn_conv_w, dn_a_log, dn_dt_bias, dn_norm_w, w_branch_a, w_branch_b,
                  w_out, ln2_g, ln2_b, ffn2_w_up, ffn2_w_down, ln3_g, ln3_b):
    row = lambda a: a[l][None, :].astype(F32)
    wi, bi = w_in[l], b_in[l]
    o_beta = MAIN_COLS
    o_dec = o_beta + HEADS
    o_gate = o_dec + HEADS
    ws = gm_w_s[l]
    lsm = DN_CONV
    shift = np.arange(lsm)[:, None]
    pos = np.arange(ROW_TILE)[None, :]
    live = (pos >= shift) & (pos < lsm)
    coef = jnp.where(live[:, :, None],
                     jnp.transpose(ws[:, np.clip(pos + 0 * shift, 0, lsm - 1),
                                      np.clip(pos - shift, 0, lsm - 1)], (1, 2, 0)), 0.0)
    bias = jnp.pad(gm_b_s[l][:, :lsm].T, ((0, ROW_TILE - lsm), (0, 0)))
    return {
        'ffn1': (ffn1_w_up[l].astype(BF16), ffn1_w_down[l].astype(BF16), row(ln1_g), row(ln1_b)),
        'ffn2': (ffn2_w_up[l].astype(BF16), ffn2_w_down[l].astype(BF16), row(ln3_g), row(ln3_b)),
        'w_proj': jnp.concatenate([wi[:, :MAIN_COLS], wi[:, o_gate:], _pad_lanes(wi[:, o_beta:o_dec]),
                                   _pad_lanes(wi[:, o_dec:o_gate])], axis=1).astype(BF16),
        'b_proj': jnp.concatenate([bi[:MAIN_COLS], bi[o_gate:], _pad_lanes(bi[o_beta:o_dec]),
                                   _pad_lanes(bi[o_dec:o_gate])])[None, :],
        'gm_v_g': row(gm_v_g), 'gm_v_b': row(gm_v_b),
        'gm_w_s': ws, 'gm_b_s_t': gm_b_s[l].T,
        'mix_coef': jnp.repeat(coef, GROUP_DIM, axis=-1), 'mix_bias': jnp.repeat(bias, GROUP_DIM, axis=-1),
        'conv_w': dn_conv_w[l],
        'a_log': _pad_lanes(dn_a_log[l][None, :].astype(F32)),
        'dt_bias': _pad_lanes(dn_dt_bias[l][None, :].astype(F32)),
        'norm_w': row(dn_norm_w),
        'w_a': w_branch_a[l].astype(BF16), 'w_b': w_branch_b[l].astype(BF16),
        'w_o': w_out[l].astype(BF16),
        'ln2_g': row(ln2_g), 'ln2_b': row(ln2_b),
    }


def kernel(x_prompt, x_sample, state_conv, state_ssm, ffn1_w_up, ffn1_w_down, ln1_g, ln1_b, w_in, b_in, gm_v_g, gm_v_b, gm_w_s, gm_b_s, dn_conv_w, dn_a_log, dn_dt_bias, dn_norm_w, w_branch_a, w_branch_b, w_out, ln2_g, ln2_b, ffn2_w_up, ffn2_w_down, ln3_g, ln3_b):
    depth = ffn1_w_up.shape[0]
    alpha = (2.0 * depth) ** 0.25
    bp, tp, _ = x_prompt.shape
    bs, ts, _ = x_sample.shape
    assert ts == DN_CONV and ts + (DN_CONV - 1) <= ROW_TILE
    y_p, y_s = x_prompt, x_sample
    conv_p, ssm_p, conv_s, ssm_s, v_s = [], [], [], [], []
    for l in range(depth):
        p = _layer_params(l, ffn1_w_up, ffn1_w_down, ln1_g, ln1_b, w_in, b_in, gm_v_g, gm_v_b,
                          gm_w_s, gm_b_s, dn_conv_w, dn_a_log, dn_dt_bias, dn_norm_w, w_branch_a,
                          w_branch_b, w_out, ln2_g, ln2_b, ffn2_w_up, ffn2_w_down, ln3_g, ln3_b)
        x1 = _ffn_ln(y_p.reshape(bp * tp, D_MODEL), *p['ffn1'], alpha, FFN_ROWS).reshape(bp, tp, D_MODEL)
        x2, c_p, s_p = _mix_prompt(x1, p, alpha, PROMPT_ROWS, PROMPT_SEQS)
        y_p = _ffn_ln(x2.reshape(bp * tp, D_MODEL), *p['ffn2'], alpha, FFN_ROWS).reshape(bp, tp, D_MODEL)
        x1s = _ffn_ln(y_s.reshape(bs * ts, D_MODEL), *p['ffn1'], alpha, min(FFN_ROWS, bs * ts))
        x1s = jnp.pad(x1s.reshape(bs, ts, D_MODEL), ((0, 0), (0, ROW_TILE - ts), (0, 0)))
        cs_pad = jnp.pad(state_conv[l], ((0, 0), (ROW_TILE - (DN_CONV - 1), 0), (0, 0)))
        x2s, vrows, c_s, s_s = _mix_sample(x1s.reshape(bs * ROW_TILE, D_MODEL),
                                           cs_pad.reshape(bs * ROW_TILE, QKV),
                                           state_ssm[l], p, alpha, SAMPLE_SEQS, ts)
        y_s = _ffn_ln(x2s.reshape(bs * ts, D_MODEL), *p['ffn2'], alpha,
                      min(FFN_ROWS, bs * ts)).reshape(bs, ts, D_MODEL)
        conv_p.append(c_p)
        ssm_p.append(s_p)
        conv_s.append(c_s)
        ssm_s.append(s_s)
        v_s.append(vrows)
    return (y_p, y_s, jnp.stack(conv_p), jnp.stack(ssm_p), jnp.stack(conv_s), jnp.stack(ssm_s),
            jnp.stack(v_s))
```

```python
import functools
import math

import jax
import jax.numpy as jnp
import numpy as np
from jax import lax
from jax.experimental import pallas as pl
from jax.experimental.pallas import tpu as pltpu

F32 = jnp.float32
BF16 = jnp.bfloat16

D_MODEL = 1024
D_FF = 2816
HEADS = 8
HEAD_DIM = 128
GROUPS = 8
GROUP_DIM = 128
GM_CHUNK = 128
DN_CHUNK = 64
DN_CONV = 4
QKV = 3 * D_MODEL
MAIN_COLS = 6 * D_MODEL
LN_EPS = 1e-5
RMS_EPS = 1e-6

MXU_DIM = 256
SUB = 128
ROW_TILE = 8
LANES = 128
SLAB = 16
PIECE_BUDGET = 200
MXU_PIECE_COST = 256
TICK_BUDGET = 900
VMEM_LIMIT = 56 * 1024 * 1024
FFN_ROWS = 512
PROMPT_ROWS, PROMPT_SEQS = 128, 2
SAMPLE_SEQS = 8


def _sigmoid(x):
    return 0.5 * jnp.tanh(0.5 * x) + 0.5


def _silu(x):
    return x * _sigmoid(x)


def _gelu_tanh(x):
    c = math.sqrt(2.0 / math.pi)
    return x * (0.5 * (1.0 + jnp.tanh(c * (x + 0.044715 * (x * x * x)))))


def _softplus(x):
    return jnp.maximum(x, 0.0) + jnp.log(1.0 + jnp.exp(-jnp.abs(x)))


def _layer_norm(y, g, b):
    mu = jnp.mean(y, axis=-1, keepdims=True)
    yc = y - mu
    var = jnp.mean(yc * yc, axis=-1, keepdims=True)
    return yc * lax.rsqrt(var + LN_EPS) * g + b


def _dot(a, b):
    return jnp.dot(a.astype(BF16), b.astype(BF16), preferred_element_type=F32)


def _dot_nt(a, b):
    return lax.dot_general(a.astype(BF16), b.astype(BF16), (((1,), (1,)), ((), ())),
                           preferred_element_type=F32)


def _dot_exact_lhs(m01, x):
    hi = x.astype(BF16)
    r1 = x - hi.astype(F32)
    mid = r1.astype(BF16)
    lo = (r1 - mid.astype(F32)).astype(BF16)
    m = m01.astype(BF16)
    return (jnp.dot(m, hi, preferred_element_type=F32)
            + jnp.dot(m, mid, preferred_element_type=F32)
            + jnp.dot(m, lo, preferred_element_type=F32))


def _block_masks(n, blk):
    row = lax.broadcasted_iota(jnp.int32, (n, n), 0)
    col = lax.broadcasted_iota(jnp.int32, (n, n), 1)
    same = (row // blk) == (col // blk)
    return same & (row >= col), same & (row > col), row == col


def _no_tick():
    pass


def _inv_unit_lower(a, eye, n_iter, tick=_no_tick):
    n = eye.shape[0]
    b = [-x for x in a]
    p = [eye + x for x in b]
    b = [_dot(x, x) for x in b]
    tick()
    for _ in range(n_iter - 1):
        pb = [_dot(jnp.concatenate([pi, bi], axis=0), bi) for pi, bi in zip(p, b)]
        tick()
        p = [pi + x[:n] for pi, x in zip(p, pb)]
        b = [x[n:] for x in pb]
    return [pi + _dot(pi, bi) for pi, bi in zip(p, b)]


def _dn_intra(q, k, v, beta, g_col, g_row, masks, n_iter, tick=_no_tick):
    causal, strict, diag = masks
    heads = range(len(q))
    decay = [jnp.where(causal, jnp.exp(jnp.where(causal, g_col[h] - g_row[h], 0.0)), 0.0) for h in heads]
    kb = [k[h] * beta[h] for h in heads]
    kq = [_dot_nt(jnp.concatenate([kb[h], q[h]], axis=0), k[h]) for h in heads]
    tick()
    a = [jnp.where(strict, kq[h][:SUB] * decay[h], 0.0) for h in heads]
    qk = [kq[h][SUB:] * decay[h] for h in heads]
    eye = jnp.where(diag, 1.0, 0.0).astype(F32)
    t_inv = _inv_unit_lower(a, eye, n_iter, tick)
    tick()
    e_g = [jnp.exp(g_col[h]) for h in heads]
    uw = [_dot(t_inv[h], jnp.concatenate([v[h] * beta[h], kb[h] * e_g[h]], axis=1)) for h in heads]
    tick()
    return ([x[:, :HEAD_DIM] for x in uw], [x[:, HEAD_DIM:] for x in uw], qk,
            [q[h] * e_g[h] for h in heads])


class _Slabs:
    def __init__(self, fn, *xs, slab):
        self._outs = []
        self.thunks = [functools.partial(self._run, fn, xs, r, slab)
                       for r in range(0, xs[0].shape[0], slab)]

    def _run(self, fn, xs, r, slab):
        self._outs.append(fn(*[x[r:r + slab] for x in xs]))

    def result(self):
        assert len(self._outs) == len(self.thunks)
        return jnp.concatenate(self._outs, axis=0)


def _by_rows(fn, *xs, slab):
    job = _Slabs(fn, *xs, slab=slab)
    for th in job.thunks:
        th()
    return job.result()


class _WorkQueue:
    def __init__(self):
        self._items = []

    def add(self, job, cost):
        self._items += [(cost, th, job) for th in job.thunks]

    def add_front(self, job, cost):
        self._items = [(cost, th, job) for th in job.thunks] + self._items

    def run(self, budget):
        while self._items and budget > 0:
            cost, th, _ = self._items.pop(0)
            th()
            budget -= cost

    def finish(self, job):
        while any(j is job for _, _, j in self._items):
            self._items.pop(0)[1]()


def _piped_dot(lhs, w, lo, hi, queue, budget=PIECE_BUDGET):
    outs = []
    for c in range(lo, hi, MXU_DIM):
        outs.append(jnp.dot(lhs, w[:, c:c + MXU_DIM], preferred_element_type=F32))
        queue.run(budget)
    return jnp.concatenate(outs, axis=1)


def _conv_job(x, b, tail, w, post):
    c = x.shape[1]
    taps = w.shape[0]
    sub = lax.broadcasted_iota(jnp.int32, (ROW_TILE, c), 0)
    b_rows = jnp.broadcast_to(b, (ROW_TILE, c))
    wj = [jnp.broadcast_to(w[taps - 1 - j:taps - j, :], (ROW_TILE, c)) for j in range(taps)]
    state = {'prev': [pltpu.roll(tail, j, 0) for j in range(1, taps)], 'cur': None}

    def tile(raw):
        cur = raw + b_rows
        rolled = [pltpu.roll(cur, j, 0) for j in range(1, taps)]
        acc = cur * wj[0]
        for j in range(1, taps):
            acc = acc + jnp.where(sub < j, state['prev'][j - 1], rolled[j - 1]) * wj[j]
        state['prev'], state['cur'] = rolled, cur
        return post(acc)

    return _Slabs(tile, x, slab=ROW_TILE), state


def _gated_rms(o, norm_w, z_act):
    return o * lax.rsqrt(jnp.mean(o * o, axis=-1, keepdims=True) + RMS_EPS) * norm_w * z_act


def _l2n(x):
    return x * lax.rsqrt(jnp.sum(x * x, axis=-1, keepdims=True) + RMS_EPS)


def _ffn_chunks():
    n_tiles = D_FF // MXU_DIM
    first = (n_tiles + 1) // 2 * MXU_DIM
    return ((0, first), (first, D_FF))


def _ffn_ln_kernel(x_ref, wu_ref, wd_ref, g_ref, b_ref, o_ref, *, alpha):
    x = x_ref[...]
    xb = x.astype(BF16)
    acc = None
    for lo, hi in _ffn_chunks():
        a = jnp.dot(xb, wu_ref[:, lo:hi], preferred_element_type=F32)
        gt = jnp.dot(xb, wu_ref[:, D_FF + lo:D_FF + hi], preferred_element_type=F32)
        h = (_silu(a) * gt).astype(BF16)
        f = jnp.dot(h, wd_ref[lo:hi, :], preferred_element_type=F32)
        acc = f if acc is None else acc + f
    o_ref[...] = _layer_norm(alpha * x + 0.5 * acc, g_ref[...], b_ref[...])


def _const_spec(shape):
    nd = len(shape)
    return pl.BlockSpec(shape, lambda *_: (0,) * nd, pipeline_mode=pl.Buffered(1))


def _ffn_ln(x2d, wu, wd, g, b, alpha, tm):
    n = x2d.shape[0]
    assert n % tm == 0 and D_FF % MXU_DIM == 0
    return pl.pallas_call(
        functools.partial(_ffn_ln_kernel, alpha=alpha),
        grid=(n // tm,),
        in_specs=[pl.BlockSpec((tm, D_MODEL), lambda i: (i, 0)),
                  _const_spec(wu.shape), _const_spec(wd.shape),
                  _const_spec(g.shape), _const_spec(b.shape)],
        out_specs=pl.BlockSpec((tm, D_MODEL), lambda i: (i, 0)),
        out_shape=jax.ShapeDtypeStruct((n, D_MODEL), F32),
        compiler_params=pltpu.CompilerParams(dimension_semantics=("arbitrary",),
                                             vmem_limit_bytes=VMEM_LIMIT),
        name="ffn_ln",
    )(x2d, wu, wd, g, b)


def _proj_views(w_ref, b_ref):
    g0, g1 = MAIN_COLS, MAIN_COLS + 2 * D_MODEL
    return (w_ref.at[:, 0:g0], w_ref.at[:, g0:g1], w_ref.at[:, g1:g1 + 2 * LANES],
            b_ref.at[:, 0:g0], b_ref.at[:, g0:g1], b_ref.at[:, g1:g1 + 2 * LANES])


def _branch_gates_and_z(hb, w_main, b_main):
    z = jnp.dot(hb, w_main[:, 5 * D_MODEL:6 * D_MODEL], preferred_element_type=F32) \
        + b_main[:, 5 * D_MODEL:6 * D_MODEL]
    return _silu(z)


def _beta_and_logdecay(hb, w_bd, b_bd, alog, dtb):
    bd = jnp.dot(hb, w_bd[...], preferred_element_type=F32) + b_bd[...]
    beta = _sigmoid(bd[:, :128])
    g = -jnp.exp(alog[...]) * _softplus(bd[:, 128:] + dtb[...])
    return beta, g


def _merge_out_ln(x, hb, a_part, yb, w_gates, b_gates, wb_ref, wo_ref, ln_g, ln_b, alpha):
    gate_b = _sigmoid(jnp.dot(hb, w_gates[:, D_MODEL:], preferred_element_type=F32)
                      + b_gates[:, D_MODEL:])
    merged = a_part + gate_b * jnp.dot(yb, wb_ref[...], preferred_element_type=F32)
    mix = jnp.dot(merged.astype(BF16), wo_ref[...], preferred_element_type=F32)
    return _layer_norm(alpha * x + mix, ln_g[...], ln_b[...])


def _mix_prompt_kernel(x_ref, w_ref, b_ref, vg_ref, vb_ref,
                       ws_ref, bst_ref, convw_ref, alog_ref, dtb_ref, normw_ref,
                       wa_ref, wb_ref, wo_ref, ln_g, ln_b,
                       x2_ref, conv_out_ref, ssm_out_ref,
                       s_ref, xc_ref, q_s, k_s, v_s, z_s, g_s, beta_s, yb_s, *, alpha, tt, nseq):
    t = pl.program_id(1)
    nt = pl.num_programs(1)
    w_main, w_gates, w_bd, b_main, b_gates, b_bd = _proj_views(w_ref, b_ref)

    @pl.when(t == 0)
    def _():
        s_ref[...] = jnp.zeros(s_ref.shape, F32)
        xc_ref[...] = jnp.zeros(xc_ref.shape, F32)

    n_rows = nseq * tt
    x = x_ref[...].reshape(n_rows, D_MODEL)
    hb = x.astype(BF16)

    queue = _WorkQueue()

    def proj(part):
        return _piped_dot(hb, w_main, part * D_MODEL, (part + 1) * D_MODEL, queue)

    scale = HEAD_DIM ** -0.5
    heads = [slice(h * HEAD_DIM, (h + 1) * HEAD_DIM) for h in range(HEADS)]

    class _ConvJobs:
        def __init__(self, raw, part, post):
            self.cols = slice(part * D_MODEL, (part + 1) * D_MODEL)
            b_p = b_main[:, (2 + part) * D_MODEL:(3 + part) * D_MODEL]
            self.jobs = [_conv_job(raw[sq * tt:(sq + 1) * tt], b_p, xc_ref[sq, :, self.cols],
                                   convw_ref[:, self.cols], lambda acc: post(_silu(acc)))
                         for sq in range(nseq)]
            self.thunks = [th for job, _ in self.jobs for th in job.thunks]

        def result(self):
            for sq, (_, state) in enumerate(self.jobs):
                last = state['cur']

                @pl.when(t == nt - 1)
                def _():
                    conv_out_ref[sq, :, self.cols] = last[ROW_TILE - (DN_CONV - 1):, :]

                xc_ref[sq, :, self.cols] = last
            return jnp.concatenate([job.result() for job, _ in self.jobs], axis=0)

    def l2n_heads(a, mul):
        return jnp.concatenate([_l2n(a[:, sl]) * mul for sl in heads], axis=1)

    b_z = b_main[:, 5 * D_MODEL:6 * D_MODEL]
    pq = proj(2)
    job_q = _ConvJobs(pq, 0, lambda a: l2n_heads(a, scale))
    queue.add(job_q, 45)
    pk = proj(3)
    job_k = _ConvJobs(pk, 1, lambda a: l2n_heads(a, 1.0))
    queue.add(job_k, 45)
    pvv = proj(4)
    job_vv = _ConvJobs(pvv, 2, lambda a: a)
    queue.add(job_vv, 35)
    pz = proj(5)
    job_z = _Slabs(lambda a: _silu(a + b_z), pz, slab=SLAB)
    queue.add(job_z, 20)
    pgb = _piped_dot(hb, w_gates, D_MODEL, 2 * D_MODEL, queue)
    beta, g_log = _beta_and_logdecay(hb, w_bd, b_bd, alog_ref, dtb_ref)
    beta_s[...] = beta
    queue.run(PIECE_BUDGET)
    rt = lax.broadcasted_iota(jnp.int32, (n_rows, n_rows), 0)
    ct = lax.broadcasted_iota(jnp.int32, (n_rows, n_rows), 1)
    cum = jnp.where(((rt // DN_CHUNK) == (ct // DN_CHUNK)) & (rt >= ct), 1.0, 0.0)
    g_s[...] = _dot_exact_lhs(cum, g_log)
    queue.finish(job_z)
    q_s[...] = job_q.result()
    k_s[...] = job_k.result()
    v_s[...] = job_vv.result()
    z_s[...] = job_z.result()

    branch_a = {}

    def dot_steps(lhs, w, lo, hi, pieces=4):
        step = (hi - lo) // pieces
        outs = []
        for p in range(pieces):
            outs.append(jnp.dot(lhs, w[:, lo + p * step:lo + (p + 1) * step], preferred_element_type=F32))
            yield MXU_PIECE_COST
        return jnp.concatenate(outs, axis=1)

    def job_steps(job, cost):
        for th in job.thunks:
            th()
            yield cost
        return job.result()

    def branch_a_steps():
        b_u = b_main[:, 0:D_MODEL]
        b_v, vg, vb = b_main[:, D_MODEL:2 * D_MODEL], vg_ref[...], vb_ref[...]
        b_ga = b_gates[:, :D_MODEL]
        pu = yield from dot_steps(hb, w_main, 0, D_MODEL)
        pv = yield from dot_steps(hb, w_main, D_MODEL, 2 * D_MODEL)
        vn = yield from job_steps(
            _Slabs(lambda a: _layer_norm(_gelu_tanh(a + b_v), vg, vb).astype(BF16), pv, slab=SLAB), 70)
        r128 = lax.broadcasted_iota(jnp.int32, (GM_CHUNK, GM_CHUNK), 0)
        c128 = lax.broadcasted_iota(jnp.int32, (GM_CHUNK, GM_CHUNK), 1)
        tril = r128 >= c128
        w_tril = [jnp.where(tril, ws_ref[g], 0.0).astype(BF16) for g in range(GROUPS)]
        rows = []
        for c in range(n_rows // GM_CHUNK):
            cols = []
            for g in range(GROUPS):
                blk = vn[c * GM_CHUNK:(c + 1) * GM_CHUNK, g * GROUP_DIM:(g + 1) * GROUP_DIM]
                cols.append(jnp.dot(w_tril[g], blk, preferred_element_type=F32) + bst_ref[:, g:g + 1])
            rows.append(jnp.concatenate(cols, axis=1))
            yield MXU_PIECE_COST
        mixed = jnp.concatenate(rows, axis=0)
        u = yield from job_steps(_Slabs(lambda a: _gelu_tanh(a + b_u), pu, slab=SLAB), 50)
        ya = yield from job_steps(_Slabs(lambda a, m: (a * m).astype(BF16), u, mixed, slab=SLAB), 10)
        pa = yield from dot_steps(ya, wa_ref, 0, D_MODEL)
        pga = yield from dot_steps(hb, w_gates, 0, D_MODEL)
        branch_a['a_part'] = yield from job_steps(
            _Slabs(lambda g, p_: _sigmoid(g + b_ga) * p_, pga, pa, slab=SLAB), 25)

    steps = branch_a_steps()

    def tick(budget=TICK_BUDGET):
        while budget > 0:
            cost = next(steps, None)
            if cost is None:
                return
            budget -= cost

    masks = _block_masks(SUB, DN_CHUNK)
    norm_w = normw_ref[...]

    chains = [(sq, h) for sq in range(nseq) for h in range(HEADS)]
    sls = [slice(h * HEAD_DIM, (h + 1) * HEAD_DIM) for _, h in chains]
    n_chunks = SUB // DN_CHUNK
    row_chunk = lax.broadcasted_iota(jnp.int32, (SUB, HEAD_DIM), 0) // DN_CHUNK
    zeros = jnp.zeros((DN_CHUNK, HEAD_DIM), F32)
    ids = range(len(chains))
    state = [s_ref[sq, h] for sq, h in chains]
    for j in range(tt // SUB):
        rows = [slice(sq * tt + j * SUB, sq * tt + (j + 1) * SUB) for sq, _ in chains]
        g_sub = [g_s[sq * tt + j * SUB:sq * tt + (j + 1) * SUB, :] for sq in range(nseq)]
        g_t = [g.T for g in g_sub]
        b_sub = [beta_s[sq * tt + j * SUB:sq * tt + (j + 1) * SUB, :] for sq in range(nseq)]
        q = [q_s[rows[i], sls[i]] for i in ids]
        k = [k_s[rows[i], sls[i]] for i in ids]
        vv = [v_s[rows[i], sls[i]] for i in ids]
        g_col = [jnp.broadcast_to(g_sub[sq][:, h:h + 1], (SUB, HEAD_DIM)) for sq, h in chains]
        g_row = [jnp.broadcast_to(g_t[sq][h:h + 1, :], (SUB, SUB)) for sq, h in chains]
        beta_h = [jnp.broadcast_to(b_sub[sq][:, h:h + 1], (SUB, HEAD_DIM)) for sq, h in chains]
        u_h, w_h, qk, qe = _dn_intra(q, k, vv, beta_h, g_col, g_row, masks, 5, tick)
        g_last = [[g_col[i][(c + 1) * DN_CHUNK - 1:(c + 1) * DN_CHUNK, :] for c in range(n_chunks)]
                  for i in ids]
        k_dec_t = []
        for i in ids:
            g_end = g_last[i][n_chunks - 1]
            for c in range(n_chunks - 2, -1, -1):
                g_end = jnp.where(row_chunk == c, g_last[i][c], g_end)
            k_dec_t.append((k[i] * jnp.exp(g_end - g_col[i])).T)
        outs = [[] for _ in ids]
        for c in range(n_chunks):
            rs = slice(c * DN_CHUNK, (c + 1) * DN_CHUNK)
            r = [_dot(jnp.concatenate([w_h[i][rs], qe[i][rs]], axis=0), state[i]) for i in ids]
            tick()
            v_new = [u_h[i][rs] - r[i][:DN_CHUNK] for i in ids]
            v_pad = [jnp.concatenate([zeros] * c + [v_new[i]] + [zeros] * (n_chunks - 1 - c), axis=0)
                     for i in ids]
            m = [_dot(jnp.concatenate([qk[i][rs], k_dec_t[i]], axis=0), v_pad[i]) for i in ids]
            tick()
            for i in ids:
                outs[i].append(r[i][DN_CHUNK:] + m[i][:DN_CHUNK])
            state = [state[i] * jnp.exp(g_last[i][c]) + m[i][DN_CHUNK:] for i in ids]
        for i in ids:
            o = jnp.concatenate(outs[i], axis=0)
            yb_s[rows[i], sls[i]] = _gated_rms(o, norm_w, z_s[rows[i], sls[i]]).astype(BF16)
    for i, (sq, h) in enumerate(chains):
        s_ref[sq, h] = state[i]
    tick(float('inf'))
    a_part = branch_a['a_part']

    @pl.when(t == nt - 1)
    def _():
        ssm_out_ref[...] = s_ref[...]

    pb = jnp.dot(yb_s[...], wb_ref[...], preferred_element_type=F32)
    b_gb = b_gates[:, D_MODEL:]
    merged = _by_rows(lambda a, g, p_: (a + _sigmoid(g + b_gb) * p_).astype(BF16), a_part, pgb, pb, slab=SLAB)
    mix = jnp.dot(merged, wo_ref[...], preferred_element_type=F32)
    ln_gain, ln_bias = ln_g[...], ln_b[...]
    x2 = _by_rows(lambda xx, m: _layer_norm(alpha * xx + m, ln_gain, ln_bias), x, mix, slab=SLAB)
    x2_ref[...] = x2.reshape(nseq, tt, D_MODEL)


def _mix_prompt(x1, p, alpha, tt, nseq):
    b, t, _ = x1.shape
    assert t % tt == 0 and tt % SUB == 0 and b % nseq == 0
    rows = nseq * tt
    consts = [p['w_proj'], p['b_proj'],
              p['gm_v_g'], p['gm_v_b'], p['gm_w_s'], p['gm_b_s_t'], p['conv_w'], p['a_log'],
              p['dt_bias'], p['norm_w'], p['w_a'], p['w_b'], p['w_o'], p['ln2_g'], p['ln2_b']]
    return pl.pallas_call(
        functools.partial(_mix_prompt_kernel, alpha=alpha, tt=tt, nseq=nseq),
        grid=(b // nseq, t // tt),
        in_specs=[pl.BlockSpec((nseq, tt, D_MODEL), lambda i, j: (i, j, 0))]
                 + [_const_spec(c.shape) for c in consts],
        out_specs=[pl.BlockSpec((nseq, tt, D_MODEL), lambda i, j: (i, j, 0)),
                   pl.BlockSpec((nseq, DN_CONV - 1, QKV), lambda i, j: (i, 0, 0)),
                   pl.BlockSpec((nseq, HEADS, HEAD_DIM, HEAD_DIM), lambda i, j: (i, 0, 0, 0))],
        out_shape=[jax.ShapeDtypeStruct((b, t, D_MODEL), F32),
                   jax.ShapeDtypeStruct((b, DN_CONV - 1, QKV), F32),
                   jax.ShapeDtypeStruct((b, HEADS, HEAD_DIM, HEAD_DIM), F32)],
        scratch_shapes=[pltpu.VMEM((nseq, HEADS, HEAD_DIM, HEAD_DIM), F32),
                        pltpu.VMEM((nseq, ROW_TILE, QKV), F32),
                        pltpu.VMEM((rows, D_MODEL), F32),
                        pltpu.VMEM((rows, D_MODEL), F32),
                        pltpu.VMEM((rows, D_MODEL), F32),
                        pltpu.VMEM((rows, D_MODEL), F32),
                        pltpu.VMEM((rows, 128), F32),
                        pltpu.VMEM((rows, 128), F32),
                        pltpu.VMEM((rows, D_MODEL), BF16)],
        compiler_params=pltpu.CompilerParams(dimension_semantics=("arbitrary", "arbitrary"),
                                             vmem_limit_bytes=VMEM_LIMIT),
        name="mix_prompt",
    )(x1, *consts)


def _sample_stage_one(x_ref, cs_ref, w_ref, b_ref, vg_ref, vb_ref, coef_ref, bias_ref, convw_ref,
                      alog_ref, dtb_ref, wa_ref, vrow_ref, conv_out_ref,
                      u_s, w_s, qe_s, qk_s, kdt_s, gcol_s, apart_s, zact_s, *, nb, ts):
    rows = nb * ROW_TILE
    w_main, w_gates, w_bd, b_main, b_gates, b_bd = _proj_views(w_ref, b_ref)
    x = x_ref[...]
    hb = x.astype(BF16)
    valid = (lax.broadcasted_iota(jnp.int32, (rows, 1), 0) % ROW_TILE) < ts
    validf = jnp.where(valid, 1.0, 0.0).astype(F32)

    u = _gelu_tanh(jnp.dot(hb, w_main[:, 0:D_MODEL], preferred_element_type=F32)
                   + b_main[:, 0:D_MODEL])
    v = _gelu_tanh(jnp.dot(hb, w_main[:, D_MODEL:2 * D_MODEL], preferred_element_type=F32)
                   + b_main[:, D_MODEL:2 * D_MODEL])
    vn = _layer_norm(v, vg_ref[...], vb_ref[...])
    vn3 = vn.reshape(nb, ROW_TILE, D_MODEL)
    vrow_ref[...] = vn3[:, :ts, :]
    mixed = vn3 * coef_ref[0][None] + bias_ref[...][None]
    for j in range(1, DN_CONV):
        mixed = mixed + pltpu.roll(vn3, j, 1) * coef_ref[j][None]
    ya = (u * mixed.reshape(rows, D_MODEL)).astype(BF16)
    gate_a = _sigmoid(jnp.dot(hb, w_gates[:, :D_MODEL], preferred_element_type=F32)
                      + b_gates[:, :D_MODEL])
    apart_s[...] = gate_a * jnp.dot(ya, wa_ref[...], preferred_element_type=F32)

    qkv = jnp.dot(hb, w_main[:, 2 * D_MODEL:5 * D_MODEL], preferred_element_type=F32) \
        + b_main[:, 2 * D_MODEL:5 * D_MODEL]
    zfull = jnp.where(valid, qkv, 0.0) + cs_ref[...]
    z3 = zfull.reshape(nb, ROW_TILE, QKV)
    conv_out_ref[...] = z3[:, ts - (DN_CONV - 1):ts, :]
    acc = z3 * convw_ref[DN_CONV - 1:DN_CONV, :][None]
    for j in range(1, DN_CONV):
        acc = acc + pltpu.roll(z3, j, 1) * convw_ref[DN_CONV - 1 - j:DN_CONV - j, :][None]
    sact = _silu(acc.reshape(rows, QKV)) * validf
    zact_s[...] = _branch_gates_and_z(hb, w_main, b_main)
    beta, g_log = _beta_and_logdecay(hb, w_bd, b_bd, alog_ref, dtb_ref)
    beta = beta * validf
    g_log = g_log * validf
    rt = lax.broadcasted_iota(jnp.int32, (rows, rows), 0)
    ct = lax.broadcasted_iota(jnp.int32, (rows, rows), 1)
    cum = jnp.where(((rt // ROW_TILE) == (ct // ROW_TILE)) & (rt >= ct), 1.0, 0.0)
    g_cum = _dot_exact_lhs(cum, g_log)

    assert rows == SUB
    g_t = g_cum.T
    masks = _block_masks(SUB, ROW_TILE)
    scale = HEAD_DIM ** -0.5
    heads = range(HEADS)
    sls = [slice(h * HEAD_DIM, (h + 1) * HEAD_DIM) for h in heads]
    q = [_l2n(sact[:, sl]) * scale * validf for sl in sls]
    k = [_l2n(sact[:, D_MODEL + h * HEAD_DIM:D_MODEL + (h + 1) * HEAD_DIM]) * validf for h in heads]
    vv = [sact[:, 2 * D_MODEL + h * HEAD_DIM:2 * D_MODEL + (h + 1) * HEAD_DIM] for h in heads]
    g_col = [jnp.broadcast_to(g_cum[:, h:h + 1], (SUB, HEAD_DIM)) for h in heads]
    g_row = [jnp.broadcast_to(g_t[h:h + 1, :], (SUB, SUB)) for h in heads]
    beta_h = [jnp.broadcast_to(beta[:, h:h + 1], (SUB, HEAD_DIM)) for h in heads]
    u_h, w_h, qk, qe = _dn_intra(q, k, vv, beta_h, g_col, g_row, masks, 2)
    for h in heads:
        g_end = jnp.broadcast_to(g_col[h].reshape(nb, ROW_TILE, HEAD_DIM)[:, ROW_TILE - 1:, :],
                                 (nb, ROW_TILE, HEAD_DIM)).reshape(SUB, HEAD_DIM)
        kdt_s[h] = (k[h] * jnp.exp(g_end - g_col[h])).T
        u_s[h], w_s[h], qe_s[h], qk_s[h], gcol_s[h] = u_h[h], w_h[h], qe[h], qk[h], g_col[h]


def _sample_state_part(part, s_in_ref, normw_ref, s_out_ref, u_s, w_s, qe_s, qk_s, kdt_s, gcol_s,
                       zact_s, yb_s, *, nb):
    span = nb * ROW_TILE
    base = part * span
    heads = range(HEADS)
    tiles = [slice(base + i * ROW_TILE, base + (i + 1) * ROW_TILE) for i in range(nb)]
    zeros = jnp.zeros((SUB - span, HEAD_DIM), F32)
    seq_of_row = lax.broadcasted_iota(jnp.int32, (SUB, HEAD_DIM), 0) // ROW_TILE
    norm_w = normw_ref[...]
    r = [[_dot(jnp.concatenate([w_s[h, rs, :], qe_s[h, rs, :]], axis=0), s_in_ref[i, h])
          for i, rs in enumerate(tiles)] for h in heads]
    v_new = []
    for h in heads:
        mine = jnp.concatenate([u_s[h, rs, :] - r[h][i][:ROW_TILE] for i, rs in enumerate(tiles)], axis=0)
        v_new.append(jnp.concatenate([mine, zeros] if part == 0 else [zeros, mine], axis=0))
    qkv_new = [_dot(qk_s[h, base:base + span, :], v_new[h]) for h in heads]
    for i, rs in enumerate(tiles):
        for h in heads:
            g_last = gcol_s[h, rs.stop - 1:rs.stop, :]
            s_out_ref[i, h] = (s_in_ref[i, h] * jnp.exp(g_last)
                               + _dot(kdt_s[h], jnp.where(seq_of_row == part * nb + i, v_new[h], 0.0)))
    for h in heads:
        sl = slice(h * HEAD_DIM, (h + 1) * HEAD_DIM)
        o = jnp.concatenate([r[h][i][ROW_TILE:] for i in range(nb)], axis=0) + qkv_new[h]
        yb_s[base:base + span, sl] = _gated_rms(o, norm_w, zact_s[base:base + span, sl]).astype(BF16)


def _mix_sample_kernel(x_ref, cs_ref, s_in_ref, w_ref, b_ref,
                       vg_ref, vb_ref, coef_ref, bias_ref, convw_ref, alog_ref, dtb_ref, normw_ref,
                       wa_ref, wb_ref, wo_ref, ln_g, ln_b,
                       x2_ref, vrow_ref, conv_out_ref, s_out_ref,
                       u_s, w_s, qe_s, qk_s, kdt_s, gcol_s, apart_s, zact_s, yb_s, *, alpha, nb, ts):
    half = pl.program_id(1)
    per_head = (u_s, w_s, qe_s, qk_s, kdt_s, gcol_s)
    pl.when(half == 0)(functools.partial(
        _sample_stage_one, x_ref, cs_ref, w_ref, b_ref, vg_ref, vb_ref, coef_ref, bias_ref, convw_ref,
        alog_ref, dtb_ref, wa_ref, vrow_ref, conv_out_ref, *per_head, apart_s, zact_s, nb=2 * nb, ts=ts))
    for part in range(2):
        pl.when(half == part)(functools.partial(
            _sample_state_part, part, s_in_ref, normw_ref, s_out_ref, *per_head, zact_s, yb_s, nb=nb))

    @pl.when(half == 1)
    def _():
        _, w_gates, _, _, b_gates, _ = _proj_views(w_ref, b_ref)
        x = x_ref[...]
        x2 = _merge_out_ln(x, x.astype(BF16), apart_s[...], yb_s[...], w_gates, b_gates, wb_ref, wo_ref,
                           ln_g, ln_b, alpha)
        x2_ref[...] = x2.reshape(2 * nb, ROW_TILE, D_MODEL)[:, :ts, :]


def _mix_sample(x1, cs_pad, s_in, p, alpha, nb, ts):
    n = x1.shape[0]
    nseq = n // ROW_TILE
    assert nseq % (2 * nb) == 0 and 2 * nb * ROW_TILE == SUB
    rows = 2 * nb * ROW_TILE
    consts = [p['w_proj'], p['b_proj'],
              p['gm_v_g'], p['gm_v_b'], p['mix_coef'], p['mix_bias'], p['conv_w'], p['a_log'],
              p['dt_bias'], p['norm_w'], p['w_a'], p['w_b'], p['w_o'], p['ln2_g'], p['ln2_b']]
    state_spec = pl.BlockSpec((nb, HEADS, HEAD_DIM, HEAD_DIM), lambda i, half: (2 * i + half, 0, 0, 0))
    token_spec = pl.BlockSpec((2 * nb, ts, D_MODEL), lambda i, half: (i, 0, 0))
    per_head = pltpu.VMEM((HEADS, SUB, HEAD_DIM), F32)
    return pl.pallas_call(
        functools.partial(_mix_sample_kernel, alpha=alpha, nb=nb, ts=ts),
        grid=(nseq // (2 * nb), 2),
        in_specs=[pl.BlockSpec((rows, D_MODEL), lambda i, half: (i, 0)),
                  pl.BlockSpec((rows, QKV), lambda i, half: (i, 0)),
                  state_spec] + [_const_spec(c.shape) for c in consts],
        out_specs=[token_spec, token_spec,
                   pl.BlockSpec((2 * nb, DN_CONV - 1, QKV), lambda i, half: (i, 0, 0)),
                   state_spec],
        scratch_shapes=[per_head, per_head, per_head, per_head, per_head, per_head,
                        pltpu.VMEM((rows, D_MODEL), F32),
                        pltpu.VMEM((rows, D_MODEL), F32),
                        pltpu.VMEM((rows, D_MODEL), BF16)],
        out_shape=[jax.ShapeDtypeStruct((nseq, ts, D_MODEL), F32),
                   jax.ShapeDtypeStruct((nseq, ts, D_MODEL), F32),
                   jax.ShapeDtypeStruct((nseq, DN_CONV - 1, QKV), F32),
                   jax.ShapeDtypeStruct(s_in.shape, F32)],
        compiler_params=pltpu.CompilerParams(dimension_semantics=("arbitrary", "arbitrary"),
                                             vmem_limit_bytes=VMEM_LIMIT),
        name="mix_sample",
    )(x1, cs_pad, s_in, *consts)


def _pad_lanes(a, n=128):
    return jnp.pad(a, [(0, 0)] * (a.ndim - 1) + [(0, n - a.shape[-1])])


def _layer_params(l, ffn1_w_up, ffn1_w_down, ln1_g, ln1_b, w_in, b_in, gm_v_g, gm_v_b, gm_w_s,
                  gm_b_s, dn_conv_w, dn_a_log, dn_dt_bias, dn_norm_w, w_branch_a, w_branch_b,
                  w_out, ln2_g, ln2_b, ffn2_w_up, ffn2_w_down, ln3_g, ln3_b):
    row = lambda a: a[l][None, :].astype(F32)
    wi, bi = w_in[l], b_in[l]
    o_beta = MAIN_COLS
    o_dec = o_beta + HEADS
    o_gate = o_dec + HEADS
    ws = gm_w_s[l]
    lsm = DN_CONV
    shift = np.arange(lsm)[:, None]
    pos = np.arange(ROW_TILE)[None, :]
    live = (pos >= shift) & (pos < lsm)
    coef = jnp.where(live[:, :, None],
                     jnp.transpose(ws[:, np.clip(pos + 0 * shift, 0, lsm - 1),
                                      np.clip(pos - shift, 0, lsm - 1)], (1, 2, 0)), 0.0)
    bias = jnp.pad(gm_b_s[l][:, :lsm].T, ((0, ROW_TILE - lsm), (0, 0)))
    return {
        'ffn1': (ffn1_w_up[l].astype(BF16), ffn1_w_down[l].astype(BF16), row(ln1_g), row(ln1_b)),
        'ffn2': (ffn2_w_up[l].astype(BF16), ffn2_w_down[l].astype(BF16), row(ln3_g), row(ln3_b)),
        'w_proj': jnp.concatenate([wi[:, :MAIN_COLS], wi[:, o_gate:], _pad_lanes(wi[:, o_beta:o_dec]),
                                   _pad_lanes(wi[:, o_dec:o_gate])], axis=1).astype(BF16),
        'b_proj': jnp.concatenate([bi[:MAIN_COLS], bi[o_gate:], _pad_lanes(bi[o_beta:o_dec]),
                                   _pad_lanes(bi[o_dec:o_gate])])[None, :],
        'gm_v_g': row(gm_v_g), 'gm_v_b': row(gm_v_b),
        'gm_w_s': ws, 'gm_b_s_t': gm_b_s[l].T,
        'mix_coef': jnp.repeat(coef, GROUP_DIM, axis=-1), 'mix_bias': jnp.repeat(bias, GROUP_DIM, axis=-1),
        'conv_w': dn_conv_w[l],
        'a_log': _pad_lanes(dn_a_log[l][None, :].astype(F32)),
        'dt_bias': _pad_lanes(dn_dt_bias[l][None, :].astype(F32)),
        'norm_w': row(dn_norm_w),
        'w_a': w_branch_a[l].astype(BF16), 'w_b': w_branch_b[l].astype(BF16),
        'w_o': w_out[l].astype(BF16),
        'ln2_g': row(ln2_g), 'ln2_b': row(ln2_b),
    }


def kernel(x_prompt, x_sample, state_conv, state_ssm, ffn1_w_up, ffn1_w_down, ln1_g, ln1_b, w_in, b_in, gm_v_g, gm_v_b, gm_w_s, gm_b_s, dn_conv_w, dn_a_log, dn_dt_bias, dn_norm_w, w_branch_a, w_branch_b, w_out, ln2_g, ln2_b, ffn2_w_up, ffn2_w_down, ln3_g, ln3_b):
    depth = ffn1_w_up.shape[0]
    alpha = (2.0 * depth) ** 0.25
    bp, tp, _ = x_prompt.shape
    bs, ts, _ = x_sample.shape
    assert ts == DN_CONV and ts + (DN_CONV - 1) <= ROW_TILE
    y_p, y_s = x_prompt, x_sample
    conv_p, ssm_p, conv_s, ssm_s, v_s = [], [], [], [], []
    for l in range(depth):
        p = _layer_params(l, ffn1_w_up, ffn1_w_down, ln1_g, ln1_b, w_in, b_in, gm_v_g, gm_v_b,
                          gm_w_s, gm_b_s, dn_conv_w, dn_a_log, dn_dt_bias, dn_norm_w, w_branch_a,
                          w_branch_b, w_out, ln2_g, ln2_b, ffn2_w_up, ffn2_w_down, ln3_g, ln3_b)
        x1 = _ffn_ln(y_p.reshape(bp * tp, D_MODEL), *p['ffn1'], alpha, FFN_ROWS).reshape(bp, tp, D_MODEL)
        x2, c_p, s_p = _mix_prompt(x1, p, alpha, PROMPT_ROWS, PROMPT_SEQS)
        y_p = _ffn_ln(x2.reshape(bp * tp, D_MODEL), *p['ffn2'], alpha, FFN_ROWS).reshape(bp, tp, D_MODEL)
        x1s = _ffn_ln(y_s.reshape(bs * ts, D_MODEL), *p['ffn1'], alpha, min(FFN_ROWS, bs * ts))
        x1s = jnp.pad(x1s.reshape(bs, ts, D_MODEL), ((0, 0), (0, ROW_TILE - ts), (0, 0)))
        cs_pad = jnp.pad(state_conv[l], ((0, 0), (ROW_TILE - (DN_CONV - 1), 0), (0, 0)))
        x2s, vrows, c_s, s_s = _mix_sample(x1s.reshape(bs * ROW_TILE, D_MODEL),
                                           cs_pad.reshape(bs * ROW_TILE, QKV),
                                           state_ssm[l], p, alpha, SAMPLE_SEQS, ts)
        y_s = _ffn_ln(x2s.reshape(bs * ts, D_MODEL), *p['ffn2'], alpha,
                      min(FFN_ROWS, bs * ts)).reshape(bs, ts, D_MODEL)
        conv_p.append(c_p)
        ssm_p.append(s_p)
        conv_s.append(c_s)
        ssm_s.append(s_s)
        v_s.append(vrows)
    return (y_p, y_s, jnp.stack(conv_p), jnp.stack(ssm_p), jnp.stack(conv_s), jnp.stack(ssm_s),
            jnp.stack(v_s))
```

```python
import functools
import math

import jax
import jax.numpy as jnp
import numpy as np
from jax import lax
from jax.experimental import pallas as pl
from jax.experimental.pallas import tpu as pltpu

F32 = jnp.float32
BF16 = jnp.bfloat16

D_MODEL = 1024
D_FF = 2816
HEADS = 8
HEAD_DIM = 128
GROUPS = 8
GROUP_DIM = 128
GM_CHUNK = 128
DN_CHUNK = 64
DN_CONV = 4
QKV = 3 * D_MODEL
MAIN_COLS = 6 * D_MODEL
LN_EPS = 1e-5
RMS_EPS = 1e-6

MXU_DIM = 256
SUB = 128
ROW_TILE = 8
LANES = 128
SLAB = 16
PIECE_BUDGET = 200
MXU_PIECE_COST = 256
TICK_BUDGET = 900
VMEM_LIMIT = 56 * 1024 * 1024
FFN_ROWS = 512
PROMPT_ROWS, PROMPT_SEQS = 128, 2
SAMPLE_SEQS = 8


def _sigmoid(x):
    return 0.5 * jnp.tanh(0.5 * x) + 0.5


def _silu(x):
    h = 0.5 * x
    return h + h * jnp.tanh(h)


def _gelu_tanh(x):
    c = math.sqrt(2.0 / math.pi)
    h = 0.5 * x
    return h + h * jnp.tanh(x * (c + (c * 0.044715) * (x * x)))


def _softplus(x):
    return jnp.maximum(x, 0.0) + jnp.log(1.0 + jnp.exp(-jnp.abs(x)))


def _layer_norm(y, g, b):
    mu = jnp.mean(y, axis=-1, keepdims=True)
    yc = y - mu
    var = jnp.mean(yc * yc, axis=-1, keepdims=True)
    return yc * lax.rsqrt(var + LN_EPS) * g + b


def _dot(a, b):
    return jnp.dot(a.astype(BF16), b.astype(BF16), preferred_element_type=F32)


def _dot_nt(a, b):
    return lax.dot_general(a.astype(BF16), b.astype(BF16), (((1,), (1,)), ((), ())),
                           preferred_element_type=F32)


def _dot_exact_lhs(m01, x):
    hi = x.astype(BF16)
    r1 = x - hi.astype(F32)
    mid = r1.astype(BF16)
    lo = (r1 - mid.astype(F32)).astype(BF16)
    m = m01.astype(BF16)
    return (jnp.dot(m, hi, preferred_element_type=F32)
            + jnp.dot(m, mid, preferred_element_type=F32)
            + jnp.dot(m, lo, preferred_element_type=F32))


def _block_masks(n, blk):
    row = lax.broadcasted_iota(jnp.int32, (n, n), 0)
    col = lax.broadcasted_iota(jnp.int32, (n, n), 1)
    same = (row // blk) == (col // blk)
    return same & (row >= col), same & (row > col), row == col


def _no_tick():
    pass


def _inv_unit_lower(a, eye, n_iter, tick=_no_tick):
    n = eye.shape[0]
    b = [-x for x in a]
    p = [eye + x for x in b]
    b = [_dot(x, x) for x in b]
    tick()
    for _ in range(n_iter - 1):
        pb = [_dot(jnp.concatenate([pi, bi], axis=0), bi) for pi, bi in zip(p, b)]
        tick()
        p = [pi + x[:n] for pi, x in zip(p, pb)]
        b = [x[n:] for x in pb]
    return [pi + _dot(pi, bi) for pi, bi in zip(p, b)]


def _dn_intra(q, k, v, beta, g_col, g_row, masks, n_iter, tick=_no_tick):
    causal, strict, diag = masks
    heads = range(len(q))
    decay = [jnp.where(causal, jnp.exp(jnp.where(causal, g_col[h] - g_row[h], 0.0)), 0.0) for h in heads]
    kb = [k[h] * beta[h] for h in heads]
    kq = [_dot_nt(jnp.concatenate([kb[h], q[h]], axis=0), k[h]) for h in heads]
    tick()
    a = [jnp.where(strict, kq[h][:SUB] * decay[h], 0.0) for h in heads]
    qk = [kq[h][SUB:] * decay[h] for h in heads]
    eye = jnp.where(diag, 1.0, 0.0).astype(F32)
    t_inv = _inv_unit_lower(a, eye, n_iter, tick)
    tick()
    e_g = [jnp.exp(g_col[h]) for h in heads]
    uw = [_dot(t_inv[h], jnp.concatenate([v[h] * beta[h], kb[h] * e_g[h]], axis=1)) for h in heads]
    tick()
    return ([x[:, :HEAD_DIM] for x in uw], [x[:, HEAD_DIM:] for x in uw], qk,
            [q[h] * e_g[h] for h in heads])


class _Slabs:
    def __init__(self, fn, *xs, slab):
        self._outs = []
        self.thunks = [functools.partial(self._run, fn, xs, r, slab)
                       for r in range(0, xs[0].shape[0], slab)]

    def _run(self, fn, xs, r, slab):
        self._outs.append(fn(*[x[r:r + slab] for x in xs]))

    def result(self):
        assert len(self._outs) == len(self.thunks)
        return jnp.concatenate(self._outs, axis=0)


def _by_rows(fn, *xs, slab):
    job = _Slabs(fn, *xs, slab=slab)
    for th in job.thunks:
        th()
    return job.result()


class _WorkQueue:
    def __init__(self):
        self._items = []

    def add(self, job, cost):
        self._items += [(cost, th, job) for th in job.thunks]

    def add_front(self, job, cost):
        self._items = [(cost, th, job) for th in job.thunks] + self._items

    def run(self, budget):
        while self._items and budget > 0:
            cost, th, _ = self._items.pop(0)
            th()
            budget -= cost

    def finish(self, job):
        while any(j is job for _, _, j in self._items):
            self._items.pop(0)[1]()


def _piped_dot(lhs, w, lo, hi, queue, budget=PIECE_BUDGET):
    outs = []
    for c in range(lo, hi, MXU_DIM):
        outs.append(jnp.dot(lhs, w[:, c:c + MXU_DIM], preferred_element_type=F32))
        queue.run(budget)
    return jnp.concatenate(outs, axis=1)


def _conv_job(x, b, tail, w, post):
    c = x.shape[1]
    taps = w.shape[0]
    sub = lax.broadcasted_iota(jnp.int32, (ROW_TILE, c), 0)
    b_rows = jnp.broadcast_to(b, (ROW_TILE, c))
    wj = [jnp.broadcast_to(w[taps - 1 - j:taps - j, :], (ROW_TILE, c)) for j in range(taps)]
    state = {'prev': [pltpu.roll(tail, j, 0) for j in range(1, taps)], 'cur': None}

    def tile(raw):
        cur = raw + b_rows
        rolled = [pltpu.roll(cur, j, 0) for j in range(1, taps)]
        acc = cur * wj[0]
        for j in range(1, taps):
            acc = acc + jnp.where(sub < j, state['prev'][j - 1], rolled[j - 1]) * wj[j]
        state['prev'], state['cur'] = rolled, cur
        return post(acc)

    return _Slabs(tile, x, slab=ROW_TILE), state


def _gated_rms(o, norm_w, z_act):
    return o * lax.rsqrt(jnp.mean(o * o, axis=-1, keepdims=True) + RMS_EPS) * norm_w * z_act


def _l2n(x, mul=1.0):
    inv = lax.rsqrt(jnp.sum(x * x, axis=-1, keepdims=True) + RMS_EPS)
    return x * (inv if mul == 1.0 else inv * mul)


def _ffn_chunks():
    n_tiles = D_FF // MXU_DIM
    first = (n_tiles + 1) // 2 * MXU_DIM
    return ((0, first), (first, D_FF))


def _ffn_ln_tile(x_ref, wu_ref, wd_ref, g_ref, b_ref, o_ref, alpha):
    x = x_ref[...]
    xb = x.astype(BF16)
    acc = None
    for lo, hi in _ffn_chunks():
        a = jnp.dot(xb, wu_ref[:, lo:hi], preferred_element_type=F32)
        gt = jnp.dot(xb, wu_ref[:, D_FF + lo:D_FF + hi], preferred_element_type=F32)
        h = (_silu(a) * gt).astype(BF16)
        f = jnp.dot(h, wd_ref[lo:hi, :], preferred_element_type=F32)
        acc = f if acc is None else acc + f
    o_ref[...] = _layer_norm(alpha * x + 0.5 * acc, g_ref[...], b_ref[...])


def _ffn_ln_kernel(xp_ref, xs_ref, wu_ref, wd_ref, g_ref, b_ref, op_ref, os_ref, *, alpha, n_prompt):
    i = pl.program_id(0)
    pl.when(i < n_prompt)(functools.partial(_ffn_ln_tile, xp_ref, wu_ref, wd_ref, g_ref, b_ref, op_ref, alpha))
    pl.when(i >= n_prompt)(functools.partial(_ffn_ln_tile, xs_ref, wu_ref, wd_ref, g_ref, b_ref, os_ref, alpha))


def _const_spec(shape):
    nd = len(shape)
    return pl.BlockSpec(shape, lambda *_: (0,) * nd, pipeline_mode=pl.Buffered(1))


def _ffn_ln(xp, xs, wu, wd, g, b, alpha, tm):
    n_p, n_s = xp.shape[0] // tm, xs.shape[0] // tm
    assert xp.shape[0] % tm == 0 and xs.shape[0] % tm == 0 and D_FF % MXU_DIM == 0
    prompt_spec = pl.BlockSpec((tm, D_MODEL), lambda i: (jnp.minimum(i, n_p - 1), 0))
    sample_spec = pl.BlockSpec((tm, D_MODEL), lambda i: (jnp.maximum(i - n_p, 0), 0))
    return pl.pallas_call(
        functools.partial(_ffn_ln_kernel, alpha=alpha, n_prompt=n_p),
        grid=(n_p + n_s,),
        in_specs=[prompt_spec, sample_spec,
                  _const_spec(wu.shape), _const_spec(wd.shape),
                  _const_spec(g.shape), _const_spec(b.shape)],
        out_specs=[prompt_spec, sample_spec],
        out_shape=[jax.ShapeDtypeStruct(xp.shape, F32), jax.ShapeDtypeStruct(xs.shape, F32)],
        compiler_params=pltpu.CompilerParams(dimension_semantics=("arbitrary",),
                                             vmem_limit_bytes=VMEM_LIMIT),
        name="ffn_ln",
    )(xp, xs, wu, wd, g, b)


def _proj_views(w_ref, b_ref):
    g0, g1 = MAIN_COLS, MAIN_COLS + 2 * D_MODEL
    return (w_ref.at[:, 0:g0], w_ref.at[:, g0:g1], w_ref.at[:, g1:g1 + 2 * LANES],
            b_ref.at[:, 0:g0], b_ref.at[:, g0:g1], b_ref.at[:, g1:g1 + 2 * LANES])


def _branch_gates_and_z(hb, w_main, b_main):
    z = jnp.dot(hb, w_main[:, 5 * D_MODEL:6 * D_MODEL], preferred_element_type=F32) \
        + b_main[:, 5 * D_MODEL:6 * D_MODEL]
    return _silu(z)


def _beta_and_logdecay(hb, w_bd, b_bd, alog, dtb):
    bd = jnp.dot(hb, w_bd[...], preferred_element_type=F32) + b_bd[...]
    beta = _sigmoid(bd[:, :128])
    g = -jnp.exp(alog[...]) * _softplus(bd[:, 128:] + dtb[...])
    return beta, g


def _merge_out_ln(x, hb, a_part, yb, w_gates, b_gates, wb_ref, wo_ref, ln_g, ln_b, alpha):
    gate_b = _sigmoid(jnp.dot(hb, w_gates[:, D_MODEL:], preferred_element_type=F32)
                      + b_gates[:, D_MODEL:])
    merged = a_part + gate_b * jnp.dot(yb, wb_ref[...], preferred_element_type=F32)
    mix = jnp.dot(merged.astype(BF16), wo_ref[...], preferred_element_type=F32)
    return _layer_norm(alpha * x + mix, ln_g[...], ln_b[...])


def _mix_prompt_kernel(x_ref, w_ref, b_ref, vg_ref, vb_ref,
                       ws_ref, bst_ref, convw_ref, alog_ref, dtb_ref, normw_ref,
                       wa_ref, wb_ref, wo_ref, ln_g, ln_b,
                       x2_ref, conv_out_ref, ssm_out_ref,
                       s_ref, xc_ref, q_s, k_s, v_s, z_s, g_s, beta_s, yb_s, *, alpha, tt, nseq):
    t = pl.program_id(1)
    nt = pl.num_programs(1)
    w_main, w_gates, w_bd, b_main, b_gates, b_bd = _proj_views(w_ref, b_ref)

    @pl.when(t == 0)
    def _():
        s_ref[...] = jnp.zeros(s_ref.shape, F32)
        xc_ref[...] = jnp.zeros(xc_ref.shape, F32)

    n_rows = nseq * tt
    x = x_ref[...].reshape(n_rows, D_MODEL)
    hb = x.astype(BF16)

    queue = _WorkQueue()

    def proj(part):
        return _piped_dot(hb, w_main, part * D_MODEL, (part + 1) * D_MODEL, queue)

    scale = HEAD_DIM ** -0.5
    heads = [slice(h * HEAD_DIM, (h + 1) * HEAD_DIM) for h in range(HEADS)]

    class _ConvJobs:
        def __init__(self, raw, part, post):
            self.cols = slice(part * D_MODEL, (part + 1) * D_MODEL)
            b_p = b_main[:, (2 + part) * D_MODEL:(3 + part) * D_MODEL]
            self.jobs = [_conv_job(raw[sq * tt:(sq + 1) * tt], b_p, xc_ref[sq, :, self.cols],
                                   convw_ref[:, self.cols], lambda acc: post(_silu(acc)))
                         for sq in range(nseq)]
            self.thunks = [th for job, _ in self.jobs for th in job.thunks]

        def result(self):
            for sq, (_, state) in enumerate(self.jobs):
                last = state['cur']

                @pl.when(t == nt - 1)
                def _():
                    conv_out_ref[sq, :, self.cols] = last[ROW_TILE - (DN_CONV - 1):, :]

                xc_ref[sq, :, self.cols] = last
            return jnp.concatenate([job.result() for job, _ in self.jobs], axis=0)

    def l2n_heads(a, mul):
        return jnp.concatenate([_l2n(a[:, sl], mul) for sl in heads], axis=1)

    b_z = b_main[:, 5 * D_MODEL:6 * D_MODEL]
    pq = proj(2)
    job_q = _ConvJobs(pq, 0, lambda a: l2n_heads(a, scale))
    queue.add(job_q, 45)
    pk = proj(3)
    job_k = _ConvJobs(pk, 1, lambda a: l2n_heads(a, 1.0))
    queue.add(job_k, 45)
    pvv = proj(4)
    job_vv = _ConvJobs(pvv, 2, lambda a: a)
    queue.add(job_vv, 35)
    pz = proj(5)
    job_z = _Slabs(lambda a: _silu(a + b_z), pz, slab=SLAB)
    queue.add(job_z, 20)
    pgb = _piped_dot(hb, w_gates, D_MODEL, 2 * D_MODEL, queue)
    beta, g_log = _beta_and_logdecay(hb, w_bd, b_bd, alog_ref, dtb_ref)
    beta_s[...] = beta
    queue.run(PIECE_BUDGET)
    rt = lax.broadcasted_iota(jnp.int32, (n_rows, n_rows), 0)
    ct = lax.broadcasted_iota(jnp.int32, (n_rows, n_rows), 1)
    cum = jnp.where(((rt // DN_CHUNK) == (ct // DN_CHUNK)) & (rt >= ct), 1.0, 0.0)
    g_s[...] = _dot_exact_lhs(cum, g_log)
    queue.finish(job_z)
    q_s[...] = job_q.result()
    k_s[...] = job_k.result()
    v_s[...] = job_vv.result()
    z_s[...] = job_z.result()

    branch_a = {}

    def dot_steps(lhs, w, lo, hi, pieces=4):
        step = (hi - lo) // pieces
        outs = []
        for p in range(pieces):
            outs.append(jnp.dot(lhs, w[:, lo + p * step:lo + (p + 1) * step], preferred_element_type=F32))
            yield MXU_PIECE_COST
        return jnp.concatenate(outs, axis=1)

    def job_steps(job, cost):
        for th in job.thunks:
            th()
            yield cost
        return job.result()

    def branch_a_steps():
        b_u = b_main[:, 0:D_MODEL]
        b_v, vg, vb = b_main[:, D_MODEL:2 * D_MODEL], vg_ref[...], vb_ref[...]
        b_ga = b_gates[:, :D_MODEL]
        pu = yield from dot_steps(hb, w_main, 0, D_MODEL)
        pv = yield from dot_steps(hb, w_main, D_MODEL, 2 * D_MODEL)
        vn = yield from job_steps(
            _Slabs(lambda a: _layer_norm(_gelu_tanh(a + b_v), vg, vb).astype(BF16), pv, slab=SLAB), 70)
        r128 = lax.broadcasted_iota(jnp.int32, (GM_CHUNK, GM_CHUNK), 0)
        c128 = lax.broadcasted_iota(jnp.int32, (GM_CHUNK, GM_CHUNK), 1)
        tril = r128 >= c128
        w_tril = [jnp.where(tril, ws_ref[g], 0.0).astype(BF16) for g in range(GROUPS)]
        rows = []
        for c in range(n_rows // GM_CHUNK):
            cols = []
            for g in range(GROUPS):
                blk = vn[c * GM_CHUNK:(c + 1) * GM_CHUNK, g * GROUP_DIM:(g + 1) * GROUP_DIM]
                cols.append(jnp.dot(w_tril[g], blk, preferred_element_type=F32) + bst_ref[:, g:g + 1])
            rows.append(jnp.concatenate(cols, axis=1))
            yield MXU_PIECE_COST
        mixed = jnp.concatenate(rows, axis=0)
        u = yield from job_steps(_Slabs(lambda a: _gelu_tanh(a + b_u), pu, slab=SLAB), 50)
        ya = yield from job_steps(_Slabs(lambda a, m: (a * m).astype(BF16), u, mixed, slab=SLAB), 10)
        pa = yield from dot_steps(ya, wa_ref, 0, D_MODEL)
        pga = yield from dot_steps(hb, w_gates, 0, D_MODEL)
        branch_a['a_part'] = yield from job_steps(
            _Slabs(lambda g, p_: _sigmoid(g + b_ga) * p_, pga, pa, slab=SLAB), 25)

    steps = branch_a_steps()

    def tick(budget=TICK_BUDGET):
        while budget > 0:
            cost = next(steps, None)
            if cost is None:
                return
            budget -= cost

    masks = _block_masks(SUB, DN_CHUNK)
    norm_w = normw_ref[...]

    chains = [(sq, h) for sq in range(nseq) for h in range(HEADS)]
    sls = [slice(h * HEAD_DIM, (h + 1) * HEAD_DIM) for _, h in chains]
    n_chunks = SUB // DN_CHUNK
    row_chunk = lax.broadcasted_iota(jnp.int32, (SUB, HEAD_DIM), 0) // DN_CHUNK
    zeros = jnp.zeros((DN_CHUNK, HEAD_DIM), F32)
    ids = range(len(chains))
    state = [s_ref[sq, h] for sq, h in chains]
    for j in range(tt // SUB):
        rows = [slice(sq * tt + j * SUB, sq * tt + (j + 1) * SUB) for sq, _ in chains]
        g_sub = [g_s[sq * tt + j * SUB:sq * tt + (j + 1) * SUB, :] for sq in range(nseq)]
        g_t = [g.T for g in g_sub]
        b_sub = [beta_s[sq * tt + j * SUB:sq * tt + (j + 1) * SUB, :] for sq in range(nseq)]
        q = [q_s[rows[i], sls[i]] for i in ids]
        k = [k_s[rows[i], sls[i]] for i in ids]
        vv = [v_s[rows[i], sls[i]] for i in ids]
        g_col = [jnp.broadcast_to(g_sub[sq][:, h:h + 1], (SUB, HEAD_DIM)) for sq, h in chains]
        g_row = [jnp.broadcast_to(g_t[sq][h:h + 1, :], (SUB, SUB)) for sq, h in chains]
        beta_h = [jnp.broadcast_to(b_sub[sq][:, h:h + 1], (SUB, HEAD_DIM)) for sq, h in chains]
        u_h, w_h, qk, qe = _dn_intra(q, k, vv, beta_h, g_col, g_row, masks, 5, tick)
        g_last = [[g_col[i][(c + 1) * DN_CHUNK - 1:(c + 1) * DN_CHUNK, :] for c in range(n_chunks)]
                  for i in ids]
        k_dec_t = []
        for i in ids:
            g_end = g_last[i][n_chunks - 1]
            for c in range(n_chunks - 2, -1, -1):
                g_end = jnp.where(row_chunk == c, g_last[i][c], g_end)
            k_dec_t.append((k[i] * jnp.exp(g_end - g_col[i])).T)
        outs = [[] for _ in ids]
        for c in range(n_chunks):
            rs = slice(c * DN_CHUNK, (c + 1) * DN_CHUNK)
            r = [_dot(jnp.concatenate([w_h[i][rs], qe[i][rs]], axis=0), state[i]) for i in ids]
            tick()
            v_new = [u_h[i][rs] - r[i][:DN_CHUNK] for i in ids]
            v_pad = [jnp.concatenate([zeros] * c + [v_new[i]] + [zeros] * (n_chunks - 1 - c), axis=0)
                     for i in ids]
            m = [_dot(jnp.concatenate([qk[i][rs], k_dec_t[i]], axis=0), v_pad[i]) for i in ids]
            tick()
            for i in ids:
                outs[i].append(r[i][DN_CHUNK:] + m[i][:DN_CHUNK])
            state = [state[i] * jnp.exp(g_last[i][c]) + m[i][DN_CHUNK:] for i in ids]
        for i in ids:
            o = jnp.concatenate(outs[i], axis=0)
            yb_s[rows[i], sls[i]] = _gated_rms(o, norm_w, z_s[rows[i], sls[i]]).astype(BF16)
    for i, (sq, h) in enumerate(chains):
        s_ref[sq, h] = state[i]
    tick(float('inf'))
    a_part = branch_a['a_part']

    @pl.when(t == nt - 1)
    def _():
        ssm_out_ref[...] = s_ref[...]

    pb = jnp.dot(yb_s[...], wb_ref[...], preferred_element_type=F32)
    b_gb = b_gates[:, D_MODEL:]
    merged = _by_rows(lambda a, g, p_: (a + _sigmoid(g + b_gb) * p_).astype(BF16), a_part, pgb, pb, slab=SLAB)
    mix = jnp.dot(merged, wo_ref[...], preferred_element_type=F32)
    ln_gain, ln_bias = ln_g[...], ln_b[...]
    x2 = _by_rows(lambda xx, m: _layer_norm(alpha * xx + m, ln_gain, ln_bias), x, mix, slab=SLAB)
    x2_ref[...] = x2.reshape(nseq, tt, D_MODEL)


def _mix_prompt(x1, p, alpha, tt, nseq):
    b, t, _ = x1.shape
    assert t % tt == 0 and tt % SUB == 0 and b % nseq == 0
    rows = nseq * tt
    consts = [p['w_proj'], p['b_proj'],
              p['gm_v_g'], p['gm_v_b'], p['gm_w_s'], p['gm_b_s_t'], p['conv_w'], p['a_log'],
              p['dt_bias'], p['norm_w'], p['w_a'], p['w_b'], p['w_o'], p['ln2_g'], p['ln2_b']]
    return pl.pallas_call(
        functools.partial(_mix_prompt_kernel, alpha=alpha, tt=tt, nseq=nseq),
        grid=(b // nseq, t // tt),
        in_specs=[pl.BlockSpec((nseq, tt, D_MODEL), lambda i, j: (i, j, 0))]
                 + [_const_spec(c.shape) for c in consts],
        out_specs=[pl.BlockSpec((nseq, tt, D_MODEL), lambda i, j: (i, j, 0)),
                   pl.BlockSpec((nseq, DN_CONV - 1, QKV), lambda i, j: (i, 0, 0)),
                   pl.BlockSpec((nseq, HEADS, HEAD_DIM, HEAD_DIM), lambda i, j: (i, 0, 0, 0))],
        out_shape=[jax.ShapeDtypeStruct((b, t, D_MODEL), F32),
                   jax.ShapeDtypeStruct((b, DN_CONV - 1, QKV), F32),
                   jax.ShapeDtypeStruct((b, HEADS, HEAD_DIM, HEAD_DIM), F32)],
        scratch_shapes=[pltpu.VMEM((nseq, HEADS, HEAD_DIM, HEAD_DIM), F32),
                        pltpu.VMEM((nseq, ROW_TILE, QKV), F32),
                        pltpu.VMEM((rows, D_MODEL), F32),
                        pltpu.VMEM((rows, D_MODEL), F32),
                        pltpu.VMEM((rows, D_MODEL), F32),
                        pltpu.VMEM((rows, D_MODEL), F32),
                        pltpu.VMEM((rows, 128), F32),
                        pltpu.VMEM((rows, 128), F32),
                        pltpu.VMEM((rows, D_MODEL), BF16)],
        compiler_params=pltpu.CompilerParams(dimension_semantics=("arbitrary", "arbitrary"),
                                             vmem_limit_bytes=VMEM_LIMIT),
        name="mix_prompt",
    )(x1, *consts)


def _sample_stage_one(x_ref, cs_ref, w_ref, b_ref, vg_ref, vb_ref, coef_ref, bias_ref, convw_ref,
                      alog_ref, dtb_ref, wa_ref, vrow_ref, conv_out_ref,
                      u_s, w_s, qe_s, qk_s, kdt_s, gcol_s, apart_s, zact_s, *, nb, ts):
    rows = nb * ROW_TILE
    w_main, w_gates, w_bd, b_main, b_gates, b_bd = _proj_views(w_ref, b_ref)
    x = x_ref[...]
    hb = x.astype(BF16)
    valid = (lax.broadcasted_iota(jnp.int32, (rows, 1), 0) % ROW_TILE) < ts
    validf = jnp.where(valid, 1.0, 0.0).astype(F32)

    u = _gelu_tanh(jnp.dot(hb, w_main[:, 0:D_MODEL], preferred_element_type=F32)
                   + b_main[:, 0:D_MODEL])
    v = _gelu_tanh(jnp.dot(hb, w_main[:, D_MODEL:2 * D_MODEL], preferred_element_type=F32)
                   + b_main[:, D_MODEL:2 * D_MODEL])
    vn = _layer_norm(v, vg_ref[...], vb_ref[...])
    vn3 = vn.reshape(nb, ROW_TILE, D_MODEL)
    vrow_ref[...] = vn3[:, :ts, :]
    mixed = vn3 * coef_ref[0][None] + bias_ref[...][None]
    for j in range(1, DN_CONV):
        mixed = mixed + pltpu.roll(vn3, j, 1) * coef_ref[j][None]
    ya = (u * mixed.reshape(rows, D_MODEL)).astype(BF16)
    gate_a = _sigmoid(jnp.dot(hb, w_gates[:, :D_MODEL], preferred_element_type=F32)
                      + b_gates[:, :D_MODEL])
    apart_s[...] = gate_a * jnp.dot(ya, wa_ref[...], preferred_element_type=F32)

    qkv = jnp.dot(hb, w_main[:, 2 * D_MODEL:5 * D_MODEL], preferred_element_type=F32) \
        + b_main[:, 2 * D_MODEL:5 * D_MODEL]
    zfull = jnp.where(valid, qkv, 0.0) + cs_ref[...]
    z3 = zfull.reshape(nb, ROW_TILE, QKV)
    conv_out_ref[...] = z3[:, ts - (DN_CONV - 1):ts, :]
    acc = z3 * convw_ref[DN_CONV - 1:DN_CONV, :][None]
    for j in range(1, DN_CONV):
        acc = acc + pltpu.roll(z3, j, 1) * convw_ref[DN_CONV - 1 - j:DN_CONV - j, :][None]
    sact = _silu(acc.reshape(rows, QKV)) * validf
    zact_s[...] = _branch_gates_and_z(hb, w_main, b_main)
    beta, g_log = _beta_and_logdecay(hb, w_bd, b_bd, alog_ref, dtb_ref)
    beta = beta * validf
    g_log = g_log * validf
    rt = lax.broadcasted_iota(jnp.int32, (rows, rows), 0)
    ct = lax.broadcasted_iota(jnp.int32, (rows, rows), 1)
    cum = jnp.where(((rt // ROW_TILE) == (ct // ROW_TILE)) & (rt >= ct), 1.0, 0.0)
    g_cum = _dot_exact_lhs(cum, g_log)

    assert rows == SUB
    g_t = g_cum.T
    masks = _block_masks(SUB, ROW_TILE)
    scale = HEAD_DIM ** -0.5
    heads = range(HEADS)
    sls = [slice(h * HEAD_DIM, (h + 1) * HEAD_DIM) for h in heads]
    q = [_l2n(sact[:, sl]) * scale * validf for sl in sls]
    k = [_l2n(sact[:, D_MODEL + h * HEAD_DIM:D_MODEL + (h + 1) * HEAD_DIM]) * validf for h in heads]
    vv = [sact[:, 2 * D_MODEL + h * HEAD_DIM:2 * D_MODEL + (h + 1) * HEAD_DIM] for h in heads]
    g_col = [jnp.broadcast_to(g_cum[:, h:h + 1], (SUB, HEAD_DIM)) for h in heads]
    g_row = [jnp.broadcast_to(g_t[h:h + 1, :], (SUB, SUB)) for h in heads]
    beta_h = [jnp.broadcast_to(beta[:, h:h + 1], (SUB, HEAD_DIM)) for h in heads]
    u_h, w_h, qk, qe = _dn_intra(q, k, vv, beta_h, g_col, g_row, masks, 2)
    for h in heads:
        g_end = jnp.broadcast_to(g_col[h].reshape(nb, ROW_TILE, HEAD_DIM)[:, ROW_TILE - 1:, :],
                                 (nb, ROW_TILE, HEAD_DIM)).reshape(SUB, HEAD_DIM)
        kdt_s[h] = (k[h] * jnp.exp(g_end - g_col[h])).T
        u_s[h], w_s[h], qe_s[h], qk_s[h], gcol_s[h] = u_h[h], w_h[h], qe[h], qk[h], g_col[h]


def _sample_state_part(part, s_in_ref, normw_ref, s_out_ref, u_s, w_s, qe_s, qk_s, kdt_s, gcol_s,
                       zact_s, yb_s, *, nb):
    span = nb * ROW_TILE
    base = part * span
    heads = range(HEADS)
    tiles = [slice(base + i * ROW_TILE, base + (i + 1) * ROW_TILE) for i in range(nb)]
    zeros = jnp.zeros((SUB - span, HEAD_DIM), F32)
    seq_of_row = lax.broadcasted_iota(jnp.int32, (SUB, HEAD_DIM), 0) // ROW_TILE
    norm_w = normw_ref[...]
    r = [[_dot(jnp.concatenate([w_s[h, rs, :], qe_s[h, rs, :]], axis=0), s_in_ref[i, h])
          for i, rs in enumerate(tiles)] for h in heads]
    v_new = []
    for h in heads:
        mine = jnp.concatenate([u_s[h, rs, :] - r[h][i][:ROW_TILE] for i, rs in enumerate(tiles)], axis=0)
        v_new.append(jnp.concatenate([mine, zeros] if part == 0 else [zeros, mine], axis=0))
    qkv_new = [_dot(qk_s[h, base:base + span, :], v_new[h]) for h in heads]
    for i, rs in enumerate(tiles):
        for h in heads:
            g_last = gcol_s[h, rs.stop - 1:rs.stop, :]
            s_out_ref[i, h] = (s_in_ref[i, h] * jnp.exp(g_last)
                               + _dot(kdt_s[h], jnp.where(seq_of_row == part * nb + i, v_new[h], 0.0)))
    for h in heads:
        sl = slice(h * HEAD_DIM, (h + 1) * HEAD_DIM)
        o = jnp.concatenate([r[h][i][ROW_TILE:] for i in range(nb)], axis=0) + qkv_new[h]
        yb_s[base:base + span, sl] = _gated_rms(o, norm_w, zact_s[base:base + span, sl]).astype(BF16)


def _mix_sample_kernel(x_ref, cs_ref, s_in_ref, w_ref, b_ref,
                       vg_ref, vb_ref, coef_ref, bias_ref, convw_ref, alog_ref, dtb_ref, normw_ref,
                       wa_ref, wb_ref, wo_ref, ln_g, ln_b,
                       x2_ref, vrow_ref, conv_out_ref, s_out_ref,
                       u_s, w_s, qe_s, qk_s, kdt_s, gcol_s, apart_s, zact_s, yb_s, *, alpha, nb, ts):
    half = pl.program_id(1)
    per_head = (u_s, w_s, qe_s, qk_s, kdt_s, gcol_s)
    pl.when(half == 0)(functools.partial(
        _sample_stage_one, x_ref, cs_ref, w_ref, b_ref, vg_ref, vb_ref, coef_ref, bias_ref, convw_ref,
        alog_ref, dtb_ref, wa_ref, vrow_ref, conv_out_ref, *per_head, apart_s, zact_s, nb=2 * nb, ts=ts))
    for part in range(2):
        pl.when(half == part)(functools.partial(
            _sample_state_part, part, s_in_ref, normw_ref, s_out_ref, *per_head, zact_s, yb_s, nb=nb))

    @pl.when(half == 1)
    def _():
        _, w_gates, _, _, b_gates, _ = _proj_views(w_ref, b_ref)
        x = x_ref[...]
        x2 = _merge_out_ln(x, x.astype(BF16), apart_s[...], yb_s[...], w_gates, b_gates, wb_ref, wo_ref,
                           ln_g, ln_b, alpha)
        x2_ref[...] = x2.reshape(2 * nb, ROW_TILE, D_MODEL)[:, :ts, :]


def _mix_sample(x1, cs_pad, s_in, p, alpha, nb, ts):
    n = x1.shape[0]
    nseq = n // ROW_TILE
    assert nseq % (2 * nb) == 0 and 2 * nb * ROW_TILE == SUB
    rows = 2 * nb * ROW_TILE
    consts = [p['w_proj'], p['b_proj'],
              p['gm_v_g'], p['gm_v_b'], p['mix_coef'], p['mix_bias'], p['conv_w'], p['a_log'],
              p['dt_bias'], p['norm_w'], p['w_a'], p['w_b'], p['w_o'], p['ln2_g'], p['ln2_b']]
    state_spec = pl.BlockSpec((nb, HEADS, HEAD_DIM, HEAD_DIM), lambda i, half: (2 * i + half, 0, 0, 0))
    token_spec = pl.BlockSpec((2 * nb, ts, D_MODEL), lambda i, half: (i, 0, 0))
    per_head = pltpu.VMEM((HEADS, SUB, HEAD_DIM), F32)
    return pl.pallas_call(
        functools.partial(_mix_sample_kernel, alpha=alpha, nb=nb, ts=ts),
        grid=(nseq // (2 * nb), 2),
        in_specs=[pl.BlockSpec((rows, D_MODEL), lambda i, half: (i, 0)),
                  pl.BlockSpec((rows, QKV), lambda i, half: (i, 0)),
                  state_spec] + [_const_spec(c.shape) for c in consts],
        out_specs=[token_spec, token_spec,
                   pl.BlockSpec((2 * nb, DN_CONV - 1, QKV), lambda i, half: (i, 0, 0)),
                   state_spec],
        scratch_shapes=[per_head, per_head, per_head, per_head, per_head, per_head,
                        pltpu.VMEM((rows, D_MODEL), F32),
                        pltpu.VMEM((rows, D_MODEL), F32),
                        pltpu.VMEM((rows, D_MODEL), BF16)],
        out_shape=[jax.ShapeDtypeStruct((nseq, ts, D_MODEL), F32),
                   jax.ShapeDtypeStruct((nseq, ts, D_MODEL), F32),
                   jax.ShapeDtypeStruct((nseq, DN_CONV - 1, QKV), F32),
                   jax.ShapeDtypeStruct(s_in.shape, F32)],
        compiler_params=pltpu.CompilerParams(dimension_semantics=("arbitrary", "arbitrary"),
                                             vmem_limit_bytes=VMEM_LIMIT),
        name="mix_sample",
    )(x1, cs_pad, s_in, *consts)


def _pad_lanes(a, n=128):
    return jnp.pad(a, [(0, 0)] * (a.ndim - 1) + [(0, n - a.shape[-1])])


def _layer_params(l, ffn1_w_up, ffn1_w_down, ln1_g, ln1_b, w_in, b_in, gm_v_g, gm_v_b, gm_w_s,
                  gm_b_s, dn_conv_w, dn_a_log, dn_dt_bias, dn_norm_w, w_branch_a, w_branch_b,
                  w_out, ln2_g, ln2_b, ffn2_w_up, ffn2_w_down, ln3_g, ln3_b):
    row = lambda a: a[l][None, :].astype(F32)
    wi, bi = w_in[l], b_in[l]
    o_beta = MAIN_COLS
    o_dec = o_beta + HEADS
    o_gate = o_dec + HEADS
    ws = gm_w_s[l]
    lsm = DN_CONV
    shift = np.arange(lsm)[:, None]
    pos = np.arange(ROW_TILE)[None, :]
    live = (pos >= shift) & (pos < lsm)
    ws_head = ws[:, :lsm, :lsm]
    coef = jnp.where(live[:, :, None],
                     jnp.transpose(ws_head[:, np.clip(pos + 0 * shift, 0, lsm - 1),
                                           np.clip(pos - shift, 0, lsm - 1)], (1, 2, 0)), 0.0)
    bias = jnp.pad(gm_b_s[l][:, :lsm].T, ((0, ROW_TILE - lsm), (0, 0)))
    return {
        'ffn1': (ffn1_w_up[l].astype(BF16), ffn1_w_down[l].astype(BF16), row(ln1_g), row(ln1_b)),
        'ffn2': (ffn2_w_up[l].astype(BF16), ffn2_w_down[l].astype(BF16), row(ln3_g), row(ln3_b)),
        'w_proj': jnp.concatenate([wi[:, :MAIN_COLS], wi[:, o_gate:], _pad_lanes(wi[:, o_beta:o_dec]),
                                   _pad_lanes(wi[:, o_dec:o_gate])], axis=1).astype(BF16),
        'b_proj': jnp.concatenate([bi[:MAIN_COLS], bi[o_gate:], _pad_lanes(bi[o_beta:o_dec]),
                                   _pad_lanes(bi[o_dec:o_gate])])[None, :],
        'gm_v_g': row(gm_v_g), 'gm_v_b': row(gm_v_b),
        'gm_w_s': ws, 'gm_b_s_t': gm_b_s[l].T,
        'mix_coef': jnp.repeat(coef, GROUP_DIM, axis=-1), 'mix_bias': jnp.repeat(bias, GROUP_DIM, axis=-1),
        'conv_w': dn_conv_w[l],
        'a_log': _pad_lanes(dn_a_log[l][None, :].astype(F32)),
        'dt_bias': _pad_lanes(dn_dt_bias[l][None, :].astype(F32)),
        'norm_w': row(dn_norm_w),
        'w_a': w_branch_a[l].astype(BF16), 'w_b': w_branch_b[l].astype(BF16),
        'w_o': w_out[l].astype(BF16),
        'ln2_g': row(ln2_g), 'ln2_b': row(ln2_b),
    }


def kernel(x_prompt, x_sample, state_conv, state_ssm, ffn1_w_up, ffn1_w_down, ln1_g, ln1_b, w_in, b_in, gm_v_g, gm_v_b, gm_w_s, gm_b_s, dn_conv_w, dn_a_log, dn_dt_bias, dn_norm_w, w_branch_a, w_branch_b, w_out, ln2_g, ln2_b, ffn2_w_up, ffn2_w_down, ln3_g, ln3_b):
    depth = ffn1_w_up.shape[0]
    alpha = (2.0 * depth) ** 0.25
    bp, tp, _ = x_prompt.shape
    bs, ts, _ = x_sample.shape
    assert ts == DN_CONV and ts + (DN_CONV - 1) <= ROW_TILE
    y_p, y_s = x_prompt, x_sample
    conv_p, ssm_p, conv_s, ssm_s, v_s = [], [], [], [], []
    for l in range(depth):
        p = _layer_params(l, ffn1_w_up, ffn1_w_down, ln1_g, ln1_b, w_in, b_in, gm_v_g, gm_v_b,
                          gm_w_s, gm_b_s, dn_conv_w, dn_a_log, dn_dt_bias, dn_norm_w, w_branch_a,
                          w_branch_b, w_out, ln2_g, ln2_b, ffn2_w_up, ffn2_w_down, ln3_g, ln3_b)
        x1, x1s = _ffn_ln(y_p.reshape(bp * tp, D_MODEL), y_s.reshape(bs * ts, D_MODEL), *p['ffn1'],
                          alpha, FFN_ROWS)
        x2, c_p, s_p = _mix_prompt(x1.reshape(bp, tp, D_MODEL), p, alpha, PROMPT_ROWS, PROMPT_SEQS)
        x1s = jnp.pad(x1s.reshape(bs, ts, D_MODEL), ((0, 0), (0, ROW_TILE - ts), (0, 0)))
        cs_pad = jnp.pad(state_conv[l], ((0, 0), (ROW_TILE - (DN_CONV - 1), 0), (0, 0)))
        x2s, vrows, c_s, s_s = _mix_sample(x1s.reshape(bs * ROW_TILE, D_MODEL),
                                           cs_pad.reshape(bs * ROW_TILE, QKV),
                                           state_ssm[l], p, alpha, SAMPLE_SEQS, ts)
        y_p, y_s = _ffn_ln(x2.reshape(bp * tp, D_MODEL), x2s.reshape(bs * ts, D_MODEL), *p['ffn2'],
                           alpha, FFN_ROWS)
        y_p, y_s = y_p.reshape(bp, tp, D_MODEL), y_s.reshape(bs, ts, D_MODEL)
        conv_p.append(c_p)
        ssm_p.append(s_p)
        conv_s.append(c_s)
        ssm_s.append(s_s)
        v_s.append(vrows)
    return (y_p, y_s, jnp.stack(conv_p), jnp.stack(ssm_p), jnp.stack(conv_s), jnp.stack(ssm_s),
            jnp.stack(v_s))
```

```python
import functools
import math

import jax
import jax.numpy as jnp
import numpy as np
from jax import lax
from jax.experimental import pallas as pl
from jax.experimental.pallas import tpu as pltpu

F32 = jnp.float32
BF16 = jnp.bfloat16

D_MODEL = 1024
D_FF = 2816
HEADS = 8
HEAD_DIM = 128
GROUPS = 8
GROUP_DIM = 128
GM_CHUNK = 128
DN_CHUNK = 64
DN_CONV = 4
QKV = 3 * D_MODEL
MAIN_COLS = 6 * D_MODEL
LN_EPS = 1e-5
RMS_EPS = 1e-6

MXU_DIM = 256
SUB = 128
ROW_TILE = 8
LANES = 128
SLAB = 16
PIECE_BUDGET = 200
MXU_PIECE_COST = 256
TICK_BUDGET = 900
VMEM_LIMIT = 56 * 1024 * 1024
FFN_ROWS = 512
PROMPT_ROWS, PROMPT_SEQS = 128, 2
SAMPLE_SEQS = 8


def _sigmoid(x):
    return 0.5 * jnp.tanh(0.5 * x) + 0.5


def _silu(x):
    h = 0.5 * x
    return h + h * jnp.tanh(h)


def _gelu_tanh(x):
    c = math.sqrt(2.0 / math.pi)
    h = 0.5 * x
    return h + h * jnp.tanh(x * (c + (c * 0.044715) * (x * x)))


def _softplus(x):
    return jnp.maximum(x, 0.0) + jnp.log(1.0 + jnp.exp(-jnp.abs(x)))


def _layer_norm(y, g, b):
    mu = jnp.mean(y, axis=-1, keepdims=True)
    yc = y - mu
    var = jnp.mean(yc * yc, axis=-1, keepdims=True)
    return yc * lax.rsqrt(var + LN_EPS) * g + b


def _dot(a, b):
    return jnp.dot(a.astype(BF16), b.astype(BF16), preferred_element_type=F32)


def _dot_nt(a, b):
    return lax.dot_general(a.astype(BF16), b.astype(BF16), (((1,), (1,)), ((), ())),
                           preferred_element_type=F32)


def _dot_exact_lhs(m01, x):
    hi = x.astype(BF16)
    r1 = x - hi.astype(F32)
    mid = r1.astype(BF16)
    lo = (r1 - mid.astype(F32)).astype(BF16)
    m = m01.astype(BF16)
    return (jnp.dot(m, hi, preferred_element_type=F32)
            + jnp.dot(m, mid, preferred_element_type=F32)
            + jnp.dot(m, lo, preferred_element_type=F32))


def _block_masks(n, blk):
    row = lax.broadcasted_iota(jnp.int32, (n, n), 0)
    col = lax.broadcasted_iota(jnp.int32, (n, n), 1)
    same = (row // blk) == (col // blk)
    return same & (row >= col), same & (row > col), row == col


def _no_tick():
    pass


def _inv_unit_lower(a, eye, n_iter, tick=_no_tick):
    n = eye.shape[0]
    b = [-x for x in a]
    p = [eye + x for x in b]
    b = [_dot(x, x) for x in b]
    tick()
    for _ in range(n_iter - 1):
        pb = [_dot(jnp.concatenate([pi, bi], axis=0), bi) for pi, bi in zip(p, b)]
        tick()
        p = [pi + x[:n] for pi, x in zip(p, pb)]
        b = [x[n:] for x in pb]
    return [pi + _dot(pi, bi) for pi, bi in zip(p, b)]


def _dn_intra(q, k, v, beta, g_col, g_row, masks, n_iter, tick=_no_tick):
    causal, strict, diag = masks
    heads = range(len(q))
    decay = [jnp.where(causal, jnp.exp(jnp.where(causal, g_col[h] - g_row[h], 0.0)), 0.0) for h in heads]
    kb = [k[h] * beta[h] for h in heads]
    kq = [_dot_nt(jnp.concatenate([kb[h], q[h]], axis=0), k[h]) for h in heads]
    tick()
    a = [jnp.where(strict, kq[h][:SUB] * decay[h], 0.0) for h in heads]
    qk = [kq[h][SUB:] * decay[h] for h in heads]
    eye = jnp.where(diag, 1.0, 0.0).astype(F32)
    t_inv = _inv_unit_lower(a, eye, n_iter, tick)
    tick()
    e_g = [jnp.exp(g_col[h]) for h in heads]
    uw = [_dot(t_inv[h], jnp.concatenate([v[h] * beta[h], kb[h] * e_g[h]], axis=1)) for h in heads]
    tick()
    return ([x[:, :HEAD_DIM] for x in uw], [x[:, HEAD_DIM:] for x in uw], qk,
            [q[h] * e_g[h] for h in heads])


class _Slabs:
    def __init__(self, fn, *xs, slab):
        self._outs = []
        self.thunks = [functools.partial(self._run, fn, xs, r, slab)
                       for r in range(0, xs[0].shape[0], slab)]

    def _run(self, fn, xs, r, slab):
        self._outs.append(fn(*[x[r:r + slab] for x in xs]))

    def result(self):
        assert len(self._outs) == len(self.thunks)
        return jnp.concatenate(self._outs, axis=0)


def _by_rows(fn, *xs, slab):
    job = _Slabs(fn, *xs, slab=slab)
    for th in job.thunks:
        th()
    return job.result()


class _WorkQueue:
    def __init__(self):
        self._items = []

    def add(self, job, cost):
        self._items += [(cost, th, job) for th in job.thunks]

    def add_front(self, job, cost):
        self._items = [(cost, th, job) for th in job.thunks] + self._items

    def run(self, budget):
        while self._items and budget > 0:
            cost, th, _ = self._items.pop(0)
            th()
            budget -= cost

    def finish(self, job):
        while any(j is job for _, _, j in self._items):
            self._items.pop(0)[1]()


def _piped_dot(lhs, w, lo, hi, queue, budget=PIECE_BUDGET):
    outs = []
    for c in range(lo, hi, MXU_DIM):
        outs.append(jnp.dot(lhs, w[:, c:c + MXU_DIM], preferred_element_type=F32))
        queue.run(budget)
    return jnp.concatenate(outs, axis=1)


def _conv_job(x, b, tail, w, post):
    c = x.shape[1]
    taps = w.shape[0]
    sub = lax.broadcasted_iota(jnp.int32, (ROW_TILE, c), 0)
    b_rows = jnp.broadcast_to(b, (ROW_TILE, c))
    wj = [jnp.broadcast_to(w[taps - 1 - j:taps - j, :], (ROW_TILE, c)) for j in range(taps)]
    state = {'prev': [pltpu.roll(tail, j, 0) for j in range(1, taps)], 'cur': None}

    def tile(raw):
        cur = raw + b_rows
        rolled = [pltpu.roll(cur, 1, 0)]
        for _ in range(2, taps):
            rolled.append(pltpu.roll(rolled[-1], 1, 0))
        acc = cur * wj[0]
        for j in range(1, taps):
            acc = acc + jnp.where(sub < j, state['prev'][j - 1], rolled[j - 1]) * wj[j]
        state['prev'], state['cur'] = rolled, cur
        return post(acc)

    return _Slabs(tile, x, slab=ROW_TILE), state


def _gated_rms(o, norm_w, z_act):
    return o * lax.rsqrt(jnp.mean(o * o, axis=-1, keepdims=True) + RMS_EPS) * norm_w * z_act


def _l2n(x, mul=1.0):
    inv = lax.rsqrt(jnp.sum(x * x, axis=-1, keepdims=True) + RMS_EPS)
    return x * (inv if mul == 1.0 else inv * mul)


def _ffn_chunks():
    n_tiles = D_FF // MXU_DIM
    first = (n_tiles + 1) // 2 * MXU_DIM
    return ((0, first), (first, D_FF))


def _ffn_ln_tile(x_ref, wu_ref, wd_ref, g_ref, b_ref, o_ref, alpha):
    x = x_ref[...]
    xb = x.astype(BF16)
    acc = None
    for lo, hi in _ffn_chunks():
        a = jnp.dot(xb, wu_ref[:, lo:hi], preferred_element_type=F32)
        gt = jnp.dot(xb, wu_ref[:, D_FF + lo:D_FF + hi], preferred_element_type=F32)
        h = (_silu(a) * gt).astype(BF16)
        f = jnp.dot(h, wd_ref[lo:hi, :], preferred_element_type=F32)
        acc = f if acc is None else acc + f
    o_ref[...] = _layer_norm(alpha * x + 0.5 * acc, g_ref[...], b_ref[...])


def _ffn_ln_kernel(xp_ref, xs_ref, wu_ref, wd_ref, g_ref, b_ref, op_ref, os_ref, *, alpha, n_prompt):
    i = pl.program_id(0)
    pl.when(i < n_prompt)(functools.partial(_ffn_ln_tile, xp_ref, wu_ref, wd_ref, g_ref, b_ref, op_ref, alpha))
    pl.when(i >= n_prompt)(functools.partial(_ffn_ln_tile, xs_ref, wu_ref, wd_ref, g_ref, b_ref, os_ref, alpha))


def _const_spec(shape):
    nd = len(shape)
    return pl.BlockSpec(shape, lambda *_: (0,) * nd, pipeline_mode=pl.Buffered(1))


def _ffn_ln(xp, xs, wu, wd, g, b, alpha, tm):
    n_p, n_s = xp.shape[0] // tm, xs.shape[0] // tm
    assert xp.shape[0] % tm == 0 and xs.shape[0] % tm == 0 and D_FF % MXU_DIM == 0
    prompt_spec = pl.BlockSpec((tm, D_MODEL), lambda i: (jnp.minimum(i, n_p - 1), 0))
    sample_spec = pl.BlockSpec((tm, D_MODEL), lambda i: (jnp.maximum(i - n_p, 0), 0))
    return pl.pallas_call(
        functools.partial(_ffn_ln_kernel, alpha=alpha, n_prompt=n_p),
        grid=(n_p + n_s,),
        in_specs=[prompt_spec, sample_spec,
                  _const_spec(wu.shape), _const_spec(wd.shape),
                  _const_spec(g.shape), _const_spec(b.shape)],
        out_specs=[prompt_spec, sample_spec],
        out_shape=[jax.ShapeDtypeStruct(xp.shape, F32), jax.ShapeDtypeStruct(xs.shape, F32)],
        compiler_params=pltpu.CompilerParams(dimension_semantics=("arbitrary",),
                                             vmem_limit_bytes=VMEM_LIMIT),
        name="ffn_ln",
    )(xp, xs, wu, wd, g, b)


def _proj_views(w_ref, b_ref):
    g0, g1 = MAIN_COLS, MAIN_COLS + 2 * D_MODEL
    return (w_ref.at[:, 0:g0], w_ref.at[:, g0:g1], w_ref.at[:, g1:g1 + 2 * LANES],
            b_ref.at[:, 0:g0], b_ref.at[:, g0:g1], b_ref.at[:, g1:g1 + 2 * LANES])


def _branch_gates_and_z(hb, w_main, b_main):
    z = jnp.dot(hb, w_main[:, 5 * D_MODEL:6 * D_MODEL], preferred_element_type=F32) \
        + b_main[:, 5 * D_MODEL:6 * D_MODEL]
    return _silu(z)


def _beta_and_logdecay(hb, w_bd, b_bd, alog, dtb):
    bd = jnp.dot(hb, w_bd[...], preferred_element_type=F32) + b_bd[...]
    beta = _sigmoid(bd[:, :128])
    g = -jnp.exp(alog[...]) * _softplus(bd[:, 128:] + dtb[...])
    return beta, g


def _merge_out_ln(x, hb, a_part, yb, w_gates, b_gates, wb_ref, wo_ref, ln_g, ln_b, alpha):
    gate_b = _sigmoid(jnp.dot(hb, w_gates[:, D_MODEL:], preferred_element_type=F32)
                      + b_gates[:, D_MODEL:])
    merged = a_part + gate_b * jnp.dot(yb, wb_ref[...], preferred_element_type=F32)
    mix = jnp.dot(merged.astype(BF16), wo_ref[...], preferred_element_type=F32)
    return _layer_norm(alpha * x + mix, ln_g[...], ln_b[...])


def _mix_prompt_kernel(x_ref, w_ref, b_ref, vg_ref, vb_ref,
                       ws_ref, bst_ref, convw_ref, alog_ref, dtb_ref, normw_ref,
                       wa_ref, wb_ref, wo_ref, ln_g, ln_b,
                       x2_ref, conv_out_ref, ssm_out_ref,
                       s_ref, xc_ref, q_s, k_s, v_s, z_s, g_s, beta_s, yb_s, *, alpha, tt, nseq):
    t = pl.program_id(1)
    nt = pl.num_programs(1)
    w_main, w_gates, w_bd, b_main, b_gates, b_bd = _proj_views(w_ref, b_ref)

    @pl.when(t == 0)
    def _():
        s_ref[...] = jnp.zeros(s_ref.shape, F32)
        xc_ref[...] = jnp.zeros(xc_ref.shape, F32)

    n_rows = nseq * tt
    x = x_ref[...].reshape(n_rows, D_MODEL)
    hb = x.astype(BF16)

    queue = _WorkQueue()

    def proj(part):
        return _piped_dot(hb, w_main, part * D_MODEL, (part + 1) * D_MODEL, queue)

    scale = HEAD_DIM ** -0.5
    heads = [slice(h * HEAD_DIM, (h + 1) * HEAD_DIM) for h in range(HEADS)]

    class _ConvJobs:
        def __init__(self, raw, part, post):
            self.cols = slice(part * D_MODEL, (part + 1) * D_MODEL)
            b_p = b_main[:, (2 + part) * D_MODEL:(3 + part) * D_MODEL]
            self.jobs = [_conv_job(raw[sq * tt:(sq + 1) * tt], b_p, xc_ref[sq, :, self.cols],
                                   convw_ref[:, self.cols], lambda acc: post(_silu(acc)))
                         for sq in range(nseq)]
            self.thunks = [th for job, _ in self.jobs for th in job.thunks]

        def result(self):
            for sq, (_, state) in enumerate(self.jobs):
                last = state['cur']

                @pl.when(t == nt - 1)
                def _():
                    conv_out_ref[sq, :, self.cols] = last[ROW_TILE - (DN_CONV - 1):, :]

                xc_ref[sq, :, self.cols] = last
            return jnp.concatenate([job.result() for job, _ in self.jobs], axis=0)

    def l2n_heads(a, mul):
        return jnp.concatenate([_l2n(a[:, sl], mul) for sl in heads], axis=1)

    b_z = b_main[:, 5 * D_MODEL:6 * D_MODEL]
    pq = proj(2)
    job_q = _ConvJobs(pq, 0, lambda a: l2n_heads(a, scale))
    queue.add(job_q, 45)
    pk = proj(3)
    job_k = _ConvJobs(pk, 1, lambda a: l2n_heads(a, 1.0))
    queue.add(job_k, 45)
    pvv = proj(4)
    job_vv = _ConvJobs(pvv, 2, lambda a: a)
    queue.add(job_vv, 35)
    pz = proj(5)
    job_z = _Slabs(lambda a: _silu(a + b_z), pz, slab=SLAB)
    queue.add(job_z, 20)
    pgb = _piped_dot(hb, w_gates, D_MODEL, 2 * D_MODEL, queue)
    beta, g_log = _beta_and_logdecay(hb, w_bd, b_bd, alog_ref, dtb_ref)
    beta_s[...] = beta
    queue.run(PIECE_BUDGET)
    rt = lax.broadcasted_iota(jnp.int32, (n_rows, n_rows), 0)
    ct = lax.broadcasted_iota(jnp.int32, (n_rows, n_rows), 1)
    cum = jnp.where(((rt // DN_CHUNK) == (ct // DN_CHUNK)) & (rt >= ct), 1.0, 0.0)
    g_s[...] = _dot_exact_lhs(cum, g_log)
    queue.finish(job_z)
    q_s[...] = job_q.result()
    k_s[...] = job_k.result()
    v_s[...] = job_vv.result()
    z_s[...] = job_z.result()

    branch_a = {}

    def dot_steps(lhs, w, lo, hi, pieces=4):
        step = (hi - lo) // pieces
        outs = []
        for p in range(pieces):
            outs.append(jnp.dot(lhs, w[:, lo + p * step:lo + (p + 1) * step], preferred_element_type=F32))
            yield MXU_PIECE_COST
        return jnp.concatenate(outs, axis=1)

    def job_steps(job, cost):
        for th in job.thunks:
            th()
            yield cost
        return job.result()

    def branch_a_steps():
        b_u = b_main[:, 0:D_MODEL]
        b_v, vg, vb = b_main[:, D_MODEL:2 * D_MODEL], vg_ref[...], vb_ref[...]
        b_ga = b_gates[:, :D_MODEL]
        pu = yield from dot_steps(hb, w_main, 0, D_MODEL)
        pv = yield from dot_steps(hb, w_main, D_MODEL, 2 * D_MODEL)
        vn = yield from job_steps(
            _Slabs(lambda a: _layer_norm(_gelu_tanh(a + b_v), vg, vb).astype(BF16), pv, slab=SLAB), 70)
        r128 = lax.broadcasted_iota(jnp.int32, (GM_CHUNK, GM_CHUNK), 0)
        c128 = lax.broadcasted_iota(jnp.int32, (GM_CHUNK, GM_CHUNK), 1)
        tril = r128 >= c128
        w_tril = [jnp.where(tril, ws_ref[g], 0.0).astype(BF16) for g in range(GROUPS)]
        rows = []
        for c in range(n_rows // GM_CHUNK):
            cols = []
            for g in range(GROUPS):
                blk = vn[c * GM_CHUNK:(c + 1) * GM_CHUNK, g * GROUP_DIM:(g + 1) * GROUP_DIM]
                cols.append(jnp.dot(w_tril[g], blk, preferred_element_type=F32) + bst_ref[:, g:g + 1])
            rows.append(jnp.concatenate(cols, axis=1))
            yield MXU_PIECE_COST
        mixed = jnp.concatenate(rows, axis=0)
        u = yield from job_steps(_Slabs(lambda a: _gelu_tanh(a + b_u), pu, slab=SLAB), 50)
        ya = yield from job_steps(_Slabs(lambda a, m: (a * m).astype(BF16), u, mixed, slab=SLAB), 10)
        pa = yield from dot_steps(ya, wa_ref, 0, D_MODEL)
        pga = yield from dot_steps(hb, w_gates, 0, D_MODEL)
        branch_a['a_part'] = yield from job_steps(
            _Slabs(lambda g, p_: _sigmoid(g + b_ga) * p_, pga, pa, slab=SLAB), 25)

    steps = branch_a_steps()

    def tick(budget=TICK_BUDGET):
        while budget > 0:
            cost = next(steps, None)
            if cost is None:
                return
            budget -= cost

    masks = _block_masks(SUB, DN_CHUNK)
    norm_w = normw_ref[...]

    chains = [(sq, h) for sq in range(nseq) for h in range(HEADS)]
    sls = [slice(h * HEAD_DIM, (h + 1) * HEAD_DIM) for _, h in chains]
    n_chunks = SUB // DN_CHUNK
    row_chunk = lax.broadcasted_iota(jnp.int32, (SUB, HEAD_DIM), 0) // DN_CHUNK
    zeros = jnp.zeros((DN_CHUNK, HEAD_DIM), F32)
    ids = range(len(chains))
    state = [s_ref[sq, h] for sq, h in chains]
    for j in range(tt // SUB):
        rows = [slice(sq * tt + j * SUB, sq * tt + (j + 1) * SUB) for sq, _ in chains]
        g_sub = [g_s[sq * tt + j * SUB:sq * tt + (j + 1) * SUB, :] for sq in range(nseq)]
        g_t = [g.T for g in g_sub]
        b_sub = [beta_s[sq * tt + j * SUB:sq * tt + (j + 1) * SUB, :] for sq in range(nseq)]
        q = [q_s[rows[i], sls[i]] for i in ids]
        k = [k_s[rows[i], sls[i]] for i in ids]
        vv = [v_s[rows[i], sls[i]] for i in ids]
        g_col = [jnp.broadcast_to(g_sub[sq][:, h:h + 1], (SUB, HEAD_DIM)) for sq, h in chains]
        g_row = [jnp.broadcast_to(g_t[sq][h:h + 1, :], (SUB, SUB)) for sq, h in chains]
        beta_h = [jnp.broadcast_to(b_sub[sq][:, h:h + 1], (SUB, HEAD_DIM)) for sq, h in chains]
        u_h, w_h, qk, qe = _dn_intra(q, k, vv, beta_h, g_col, g_row, masks, 5, tick)
        g_last = [[g_col[i][(c + 1) * DN_CHUNK - 1:(c + 1) * DN_CHUNK, :] for c in range(n_chunks)]
                  for i in ids]
        k_dec_t = []
        for i in ids:
            g_end = g_last[i][n_chunks - 1]
            for c in range(n_chunks - 2, -1, -1):
                g_end = jnp.where(row_chunk == c, g_last[i][c], g_end)
            k_dec_t.append((k[i] * jnp.exp(g_end - g_col[i])).T)
        outs = [[] for _ in ids]
        for c in range(n_chunks):
            rs = slice(c * DN_CHUNK, (c + 1) * DN_CHUNK)
            r = [_dot(jnp.concatenate([w_h[i][rs], qe[i][rs]], axis=0), state[i]) for i in ids]
            tick()
            v_new = [u_h[i][rs] - r[i][:DN_CHUNK] for i in ids]
            v_pad = [jnp.concatenate([zeros] * c + [v_new[i]] + [zeros] * (n_chunks - 1 - c), axis=0)
                     for i in ids]
            m = [_dot(jnp.concatenate([qk[i][rs], k_dec_t[i]], axis=0), v_pad[i]) for i in ids]
            tick()
            for i in ids:
                outs[i].append(r[i][DN_CHUNK:] + m[i][:DN_CHUNK])
            state = [state[i] * jnp.exp(g_last[i][c]) + m[i][DN_CHUNK:] for i in ids]
        for i in ids:
            o = jnp.concatenate(outs[i], axis=0)
            yb_s[rows[i], sls[i]] = _gated_rms(o, norm_w, z_s[rows[i], sls[i]]).astype(BF16)
    for i, (sq, h) in enumerate(chains):
        s_ref[sq, h] = state[i]
    tick(float('inf'))
    a_part = branch_a['a_part']

    @pl.when(t == nt - 1)
    def _():
        ssm_out_ref[...] = s_ref[...]

    pb = jnp.dot(yb_s[...], wb_ref[...], preferred_element_type=F32)
    b_gb = b_gates[:, D_MODEL:]
    merged = _by_rows(lambda a, g, p_: (a + _sigmoid(g + b_gb) * p_).astype(BF16), a_part, pgb, pb, slab=SLAB)
    mix = jnp.dot(merged, wo_ref[...], preferred_element_type=F32)
    ln_gain, ln_bias = ln_g[...], ln_b[...]
    x2 = _by_rows(lambda xx, m: _layer_norm(alpha * xx + m, ln_gain, ln_bias), x, mix, slab=SLAB)
    x2_ref[...] = x2.reshape(nseq, tt, D_MODEL)


def _mix_prompt(x1, p, alpha, tt, nseq):
    b, t, _ = x1.shape
    assert t % tt == 0 and tt % SUB == 0 and b % nseq == 0
    rows = nseq * tt
    consts = [p['w_proj'], p['b_proj'],
              p['gm_v_g'], p['gm_v_b'], p['gm_w_s'], p['gm_b_s_t'], p['conv_w'], p['a_log'],
              p['dt_bias'], p['norm_w'], p['w_a'], p['w_b'], p['w_o'], p['ln2_g'], p['ln2_b']]
    return pl.pallas_call(
        functools.partial(_mix_prompt_kernel, alpha=alpha, tt=tt, nseq=nseq),
        grid=(b // nseq, t // tt),
        in_specs=[pl.BlockSpec((nseq, tt, D_MODEL), lambda i, j: (i, j, 0))]
                 + [_const_spec(c.shape) for c in consts],
        out_specs=[pl.BlockSpec((nseq, tt, D_MODEL), lambda i, j: (i, j, 0)),
                   pl.BlockSpec((nseq, DN_CONV - 1, QKV), lambda i, j: (i, 0, 0)),
                   pl.BlockSpec((nseq, HEADS, HEAD_DIM, HEAD_DIM), lambda i, j: (i, 0, 0, 0))],
        out_shape=[jax.ShapeDtypeStruct((b, t, D_MODEL), F32),
                   jax.ShapeDtypeStruct((b, DN_CONV - 1, QKV), F32),
                   jax.ShapeDtypeStruct((b, HEADS, HEAD_DIM, HEAD_DIM), F32)],
        scratch_shapes=[pltpu.VMEM((nseq, HEADS, HEAD_DIM, HEAD_DIM), F32),
                        pltpu.VMEM((nseq, ROW_TILE, QKV), F32),
                        pltpu.VMEM((rows, D_MODEL), F32),
                        pltpu.VMEM((rows, D_MODEL), F32),
                        pltpu.VMEM((rows, D_MODEL), F32),
                        pltpu.VMEM((rows, D_MODEL), F32),
                        pltpu.VMEM((rows, 128), F32),
                        pltpu.VMEM((rows, 128), F32),
                        pltpu.VMEM((rows, D_MODEL), BF16)],
        compiler_params=pltpu.CompilerParams(dimension_semantics=("arbitrary", "arbitrary"),
                                             vmem_limit_bytes=VMEM_LIMIT),
        name="mix_prompt",
    )(x1, *consts)


def _sample_stage_one(x_ref, cs_ref, w_ref, b_ref, vg_ref, vb_ref, coef_ref, bias_ref, convw_ref,
                      alog_ref, dtb_ref, wa_ref, vrow_ref, conv_out_ref,
                      u_s, w_s, qe_s, qk_s, kdt_s, gcol_s, apart_s, zact_s, *, nb, ts):
    rows = nb * ROW_TILE
    w_main, w_gates, w_bd, b_main, b_gates, b_bd = _proj_views(w_ref, b_ref)
    x = x_ref[...]
    hb = x.astype(BF16)
    valid = (lax.broadcasted_iota(jnp.int32, (rows, 1), 0) % ROW_TILE) < ts
    validf = jnp.where(valid, 1.0, 0.0).astype(F32)

    u = _gelu_tanh(jnp.dot(hb, w_main[:, 0:D_MODEL], preferred_element_type=F32)
                   + b_main[:, 0:D_MODEL])
    v = _gelu_tanh(jnp.dot(hb, w_main[:, D_MODEL:2 * D_MODEL], preferred_element_type=F32)
                   + b_main[:, D_MODEL:2 * D_MODEL])
    vn = _layer_norm(v, vg_ref[...], vb_ref[...])
    vn3 = vn.reshape(nb, ROW_TILE, D_MODEL)
    vrow_ref[...] = vn3[:, :ts, :]
    mixed = vn3 * coef_ref[0][None] + bias_ref[...][None]
    for j in range(1, DN_CONV):
        mixed = mixed + pltpu.roll(vn3, j, 1) * coef_ref[j][None]
    ya = (u * mixed.reshape(rows, D_MODEL)).astype(BF16)
    gate_a = _sigmoid(jnp.dot(hb, w_gates[:, :D_MODEL], preferred_element_type=F32)
                      + b_gates[:, :D_MODEL])
    apart_s[...] = gate_a * jnp.dot(ya, wa_ref[...], preferred_element_type=F32)

    qkv = jnp.dot(hb, w_main[:, 2 * D_MODEL:5 * D_MODEL], preferred_element_type=F32) \
        + b_main[:, 2 * D_MODEL:5 * D_MODEL]
    zfull = jnp.where(valid, qkv, 0.0) + cs_ref[...]
    z3 = zfull.reshape(nb, ROW_TILE, QKV)
    conv_out_ref[...] = z3[:, ts - (DN_CONV - 1):ts, :]
    acc = z3 * convw_ref[DN_CONV - 1:DN_CONV, :][None]
    for j in range(1, DN_CONV):
        acc = acc + pltpu.roll(z3, j, 1) * convw_ref[DN_CONV - 1 - j:DN_CONV - j, :][None]
    sact = _silu(acc.reshape(rows, QKV)) * validf
    zact_s[...] = _branch_gates_and_z(hb, w_main, b_main)
    beta, g_log = _beta_and_logdecay(hb, w_bd, b_bd, alog_ref, dtb_ref)
    beta = beta * validf
    g_log = g_log * validf
    rt = lax.broadcasted_iota(jnp.int32, (rows, rows), 0)
    ct = lax.broadcasted_iota(jnp.int32, (rows, rows), 1)
    cum = jnp.where(((rt // ROW_TILE) == (ct // ROW_TILE)) & (rt >= ct), 1.0, 0.0)
    g_cum = _dot_exact_lhs(cum, g_log)

    assert rows == SUB
    g_t = g_cum.T
    masks = _block_masks(SUB, ROW_TILE)
    scale = HEAD_DIM ** -0.5
    heads = range(HEADS)
    sls = [slice(h * HEAD_DIM, (h + 1) * HEAD_DIM) for h in heads]
    q = [_l2n(sact[:, sl]) * scale * validf for sl in sls]
    k = [_l2n(sact[:, D_MODEL + h * HEAD_DIM:D_MODEL + (h + 1) * HEAD_DIM]) * validf for h in heads]
    vv = [sact[:, 2 * D_MODEL + h * HEAD_DIM:2 * D_MODEL + (h + 1) * HEAD_DIM] for h in heads]
    g_col = [jnp.broadcast_to(g_cum[:, h:h + 1], (SUB, HEAD_DIM)) for h in heads]
    g_row = [jnp.broadcast_to(g_t[h:h + 1, :], (SUB, SUB)) for h in heads]
    beta_h = [jnp.broadcast_to(beta[:, h:h + 1], (SUB, HEAD_DIM)) for h in heads]
    u_h, w_h, qk, qe = _dn_intra(q, k, vv, beta_h, g_col, g_row, masks, 2)
    for h in heads:
        g_end = jnp.broadcast_to(g_col[h].reshape(nb, ROW_TILE, HEAD_DIM)[:, ROW_TILE - 1:, :],
                                 (nb, ROW_TILE, HEAD_DIM)).reshape(SUB, HEAD_DIM)
        kdt_s[h] = (k[h] * jnp.exp(g_end - g_col[h])).T
        u_s[h], w_s[h], qe_s[h], qk_s[h], gcol_s[h] = u_h[h], w_h[h], qe[h], qk[h], g_col[h]


def _sample_state_part(part, s_in_ref, normw_ref, s_out_ref, u_s, w_s, qe_s, qk_s, kdt_s, gcol_s,
                       zact_s, yb_s, *, nb):
    span = nb * ROW_TILE
    base = part * span
    heads = range(HEADS)
    tiles = [slice(base + i * ROW_TILE, base + (i + 1) * ROW_TILE) for i in range(nb)]
    zeros = jnp.zeros((SUB - span, HEAD_DIM), F32)
    seq_of_row = lax.broadcasted_iota(jnp.int32, (SUB, HEAD_DIM), 0) // ROW_TILE
    norm_w = normw_ref[...]
    r = [[_dot(jnp.concatenate([w_s[h, rs, :], qe_s[h, rs, :]], axis=0), s_in_ref[i, h])
          for i, rs in enumerate(tiles)] for h in heads]
    v_new = []
    for h in heads:
        mine = jnp.concatenate([u_s[h, rs, :] - r[h][i][:ROW_TILE] for i, rs in enumerate(tiles)], axis=0)
        v_new.append(jnp.concatenate([mine, zeros] if part == 0 else [zeros, mine], axis=0))
    qkv_new = [_dot(qk_s[h, base:base + span, :], v_new[h]) for h in heads]
    for i, rs in enumerate(tiles):
        for h in heads:
            g_last = gcol_s[h, rs.stop - 1:rs.stop, :]
            s_out_ref[i, h] = (s_in_ref[i, h] * jnp.exp(g_last)
                               + _dot(kdt_s[h], jnp.where(seq_of_row == part * nb + i, v_new[h], 0.0)))
    for h in heads:
        sl = slice(h * HEAD_DIM, (h + 1) * HEAD_DIM)
        o = jnp.concatenate([r[h][i][ROW_TILE:] for i in range(nb)], axis=0) + qkv_new[h]
        yb_s[base:base + span, sl] = _gated_rms(o, norm_w, zact_s[base:base + span, sl]).astype(BF16)


def _mix_sample_kernel(x_ref, cs_ref, s_in_ref, w_ref, b_ref,
                       vg_ref, vb_ref, coef_ref, bias_ref, convw_ref, alog_ref, dtb_ref, normw_ref,
                       wa_ref, wb_ref, wo_ref, ln_g, ln_b,
                       x2_ref, vrow_ref, conv_out_ref, s_out_ref,
                       u_s, w_s, qe_s, qk_s, kdt_s, gcol_s, apart_s, zact_s, yb_s, *, alpha, nb, ts):
    half = pl.program_id(1)
    per_head = (u_s, w_s, qe_s, qk_s, kdt_s, gcol_s)
    pl.when(half == 0)(functools.partial(
        _sample_stage_one, x_ref, cs_ref, w_ref, b_ref, vg_ref, vb_ref, coef_ref, bias_ref, convw_ref,
        alog_ref, dtb_ref, wa_ref, vrow_ref, conv_out_ref, *per_head, apart_s, zact_s, nb=2 * nb, ts=ts))
    for part in range(2):
        pl.when(half == part)(functools.partial(
            _sample_state_part, part, s_in_ref, normw_ref, s_out_ref, *per_head, zact_s, yb_s, nb=nb))

    @pl.when(half == 1)
    def _():
        _, w_gates, _, _, b_gates, _ = _proj_views(w_ref, b_ref)
        x = x_ref[...]
        x2 = _merge_out_ln(x, x.astype(BF16), apart_s[...], yb_s[...], w_gates, b_gates, wb_ref, wo_ref,
                           ln_g, ln_b, alpha)
        x2_ref[...] = x2.reshape(2 * nb, ROW_TILE, D_MODEL)[:, :ts, :]


def _mix_sample(x1, cs_pad, s_in, p, alpha, nb, ts):
    n = x1.shape[0]
    nseq = n // ROW_TILE
    assert nseq % (2 * nb) == 0 and 2 * nb * ROW_TILE == SUB
    rows = 2 * nb * ROW_TILE
    consts = [p['w_proj'], p['b_proj'],
              p['gm_v_g'], p['gm_v_b'], p['mix_coef'], p['mix_bias'], p['conv_w'], p['a_log'],
              p['dt_bias'], p['norm_w'], p['w_a'], p['w_b'], p['w_o'], p['ln2_g'], p['ln2_b']]
    state_spec = pl.BlockSpec((nb, HEADS, HEAD_DIM, HEAD_DIM), lambda i, half: (2 * i + half, 0, 0, 0))
    token_spec = pl.BlockSpec((2 * nb, ts, D_MODEL), lambda i, half: (i, 0, 0))
    per_head = pltpu.VMEM((HEADS, SUB, HEAD_DIM), F32)
    return pl.pallas_call(
        functools.partial(_mix_sample_kernel, alpha=alpha, nb=nb, ts=ts),
        grid=(nseq // (2 * nb), 2),
        in_specs=[pl.BlockSpec((rows, D_MODEL), lambda i, half: (i, 0)),
                  pl.BlockSpec((rows, QKV), lambda i, half: (i, 0)),
                  state_spec] + [_const_spec(c.shape) for c in consts],
        out_specs=[token_spec, token_spec,
                   pl.BlockSpec((2 * nb, DN_CONV - 1, QKV), lambda i, half: (i, 0, 0)),
                   state_spec],
        scratch_shapes=[per_head, per_head, per_head, per_head, per_head, per_head,
                        pltpu.VMEM((rows, D_MODEL), F32),
                        pltpu.VMEM((rows, D_MODEL), F32),
                        pltpu.VMEM((rows, D_MODEL), BF16)],
        out_shape=[jax.ShapeDtypeStruct((nseq, ts, D_MODEL), F32),
                   jax.ShapeDtypeStruct((nseq, ts, D_MODEL), F32),
                   jax.ShapeDtypeStruct((nseq, DN_CONV - 1, QKV), F32),
                   jax.ShapeDtypeStruct(s_in.shape, F32)],
        compiler_params=pltpu.CompilerParams(dimension_semantics=("arbitrary", "arbitrary"),
                                             vmem_limit_bytes=VMEM_LIMIT),
        name="mix_sample",
    )(x1, cs_pad, s_in, *consts)


def _pad_lanes(a, n=128):
    return jnp.pad(a, [(0, 0)] * (a.ndim - 1) + [(0, n - a.shape[-1])])


def _layer_params(l, ffn1_w_up, ffn1_w_down, ln1_g, ln1_b, w_in, b_in, gm_v_g, gm_v_b, gm_w_s,
                  gm_b_s, dn_conv_w, dn_a_log, dn_dt_bias, dn_norm_w, w_branch_a, w_branch_b,
                  w_out, ln2_g, ln2_b, ffn2_w_up, ffn2_w_down, ln3_g, ln3_b):
    row = lambda a: a[l][None, :].astype(F32)
    wi, bi = w_in[l], b_in[l]
    o_beta = MAIN_COLS
    o_dec = o_beta + HEADS
    o_gate = o_dec + HEADS
    ws = gm_w_s[l]
    lsm = DN_CONV
    shift = np.arange(lsm)[:, None]
    pos = np.arange(ROW_TILE)[None, :]
    live = (pos >= shift) & (pos < lsm)
    ws_head = ws[:, :lsm, :lsm]
    coef = jnp.where(live[:, :, None],
                     jnp.transpose(ws_head[:, np.clip(pos + 0 * shift, 0, lsm - 1),
                                           np.clip(pos - shift, 0, lsm - 1)], (1, 2, 0)), 0.0)
    bias = jnp.pad(gm_b_s[l][:, :lsm].T, ((0, ROW_TILE - lsm), (0, 0)))
    return {
        'ffn1': (ffn1_w_up[l].astype(BF16), ffn1_w_down[l].astype(BF16), row(ln1_g), row(ln1_b)),
        'ffn2': (ffn2_w_up[l].astype(BF16), ffn2_w_down[l].astype(BF16), row(ln3_g), row(ln3_b)),
        'w_proj': jnp.concatenate([wi[:, :MAIN_COLS], wi[:, o_gate:], _pad_lanes(wi[:, o_beta:o_dec]),
                                   _pad_lanes(wi[:, o_dec:o_gate])], axis=1).astype(BF16),
        'b_proj': jnp.concatenate([bi[:MAIN_COLS], bi[o_gate:], _pad_lanes(bi[o_beta:o_dec]),
                                   _pad_lanes(bi[o_dec:o_gate])])[None, :],
        'gm_v_g': row(gm_v_g), 'gm_v_b': row(gm_v_b),
        'gm_w_s': ws, 'gm_b_s_t': gm_b_s[l].T,
        'mix_coef': jnp.repeat(coef, GROUP_DIM, axis=-1), 'mix_bias': jnp.repeat(bias, GROUP_DIM, axis=-1),
        'conv_w': dn_conv_w[l],
        'a_log': _pad_lanes(dn_a_log[l][None, :].astype(F32)),
        'dt_bias': _pad_lanes(dn_dt_bias[l][None, :].astype(F32)),
        'norm_w': row(dn_norm_w),
        'w_a': w_branch_a[l].astype(BF16), 'w_b': w_branch_b[l].astype(BF16),
        'w_o': w_out[l].astype(BF16),
        'ln2_g': row(ln2_g), 'ln2_b': row(ln2_b),
    }


def kernel(x_prompt, x_sample, state_conv, state_ssm, ffn1_w_up, ffn1_w_down, ln1_g, ln1_b, w_in, b_in, gm_v_g, gm_v_b, gm_w_s, gm_b_s, dn_conv_w, dn_a_log, dn_dt_bias, dn_norm_w, w_branch_a, w_branch_b, w_out, ln2_g, ln2_b, ffn2_w_up, ffn2_w_down, ln3_g, ln3_b):
    depth = ffn1_w_up.shape[0]
    alpha = (2.0 * depth) ** 0.25
    bp, tp, _ = x_prompt.shape
    bs, ts, _ = x_sample.shape
    assert ts == DN_CONV and ts + (DN_CONV - 1) <= ROW_TILE
    y_p, y_s = x_prompt, x_sample
    conv_p, ssm_p, conv_s, ssm_s, v_s = [], [], [], [], []
    for l in range(depth):
        p = _layer_params(l, ffn1_w_up, ffn1_w_down, ln1_g, ln1_b, w_in, b_in, gm_v_g, gm_v_b,
                          gm_w_s, gm_b_s, dn_conv_w, dn_a_log, dn_dt_bias, dn_norm_w, w_branch_a,
                          w_branch_b, w_out, ln2_g, ln2_b, ffn2_w_up, ffn2_w_down, ln3_g, ln3_b)
        x1, x1s = _ffn_ln(y_p.reshape(bp * tp, D_MODEL), y_s.reshape(bs * ts, D_MODEL), *p['ffn1'],
                          alpha, FFN_ROWS)
        x2, c_p, s_p = _mix_prompt(x1.reshape(bp, tp, D_MODEL), p, alpha, PROMPT_ROWS, PROMPT_SEQS)
        x1s = jnp.pad(x1s.reshape(bs, ts, D_MODEL), ((0, 0), (0, ROW_TILE - ts), (0, 0)))
        cs_pad = jnp.pad(state_conv[l], ((0, 0), (ROW_TILE - (DN_CONV - 1), 0), (0, 0)))
        x2s, vrows, c_s, s_s = _mix_sample(x1s.reshape(bs * ROW_TILE, D_MODEL),
                                           cs_pad.reshape(bs * ROW_TILE, QKV),
                                           state_ssm[l], p, alpha, SAMPLE_SEQS, ts)
        y_p, y_s = _ffn_ln(x2.reshape(bp * tp, D_MODEL), x2s.reshape(bs * ts, D_MODEL), *p['ffn2'],
                           alpha, FFN_ROWS)
        y_p, y_s = y_p.reshape(bp, tp, D_MODEL), y_s.reshape(bs, ts, D_MODEL)
        conv_p.append(c_p)
        ssm_p.append(s_p)
        conv_s.append(c_s)
        ssm_s.append(s_s)
        v_s.append(vrows)
    return (y_p, y_s, jnp.stack(conv_p), jnp.stack(ssm_p), jnp.stack(conv_s), jnp.stack(ssm_s),
            jnp.stack(v_s))
```

```python
import functools
import math

import jax
import jax.numpy as jnp
import numpy as np
from jax import lax
from jax.experimental import pallas as pl
from jax.experimental.pallas import tpu as pltpu

F32 = jnp.float32
BF16 = jnp.bfloat16

D_MODEL = 1024
D_FF = 2816
HEADS = 8
HEAD_DIM = 128
GROUPS = 8
GROUP_DIM = 128
GM_CHUNK = 128
DN_CHUNK = 64
DN_CONV = 4
QKV = 3 * D_MODEL
MAIN_COLS = 6 * D_MODEL
LN_EPS = 1e-5
RMS_EPS = 1e-6

MXU_DIM = 256
SUB = 128
ROW_TILE = 8
LANES = 128
SLAB = 16
PIECE_BUDGET = 200
MXU_PIECE_COST = 256
TICK_BUDGET = 900
VMEM_LIMIT = 56 * 1024 * 1024
FFN_ROWS = 512
PROMPT_ROWS, PROMPT_SEQS = 128, 2
SAMPLE_SEQS = 8


def _sigmoid(x):
    return 0.5 * jnp.tanh(0.5 * x) + 0.5


def _silu(x):
    h = 0.5 * x
    return h + h * jnp.tanh(h)


def _gelu_tanh(x):
    c = math.sqrt(2.0 / math.pi)
    h = 0.5 * x
    return h + h * jnp.tanh(x * (c + (c * 0.044715) * (x * x)))


def _softplus(x):
    return jnp.maximum(x, 0.0) + jnp.log(1.0 + jnp.exp(-jnp.abs(x)))


def _layer_norm(y, g, b):
    mu = jnp.mean(y, axis=-1, keepdims=True)
    yc = y - mu
    var = jnp.mean(yc * yc, axis=-1, keepdims=True)
    return yc * lax.rsqrt(var + LN_EPS) * g + b


def _dot(a, b):
    return jnp.dot(a.astype(BF16), b.astype(BF16), preferred_element_type=F32)


def _dot_nt(a, b):
    return lax.dot_general(a.astype(BF16), b.astype(BF16), (((1,), (1,)), ((), ())),
                           preferred_element_type=F32)


def _dot_exact_lhs(m01, x):
    hi = x.astype(BF16)
    r1 = x - hi.astype(F32)
    mid = r1.astype(BF16)
    lo = (r1 - mid.astype(F32)).astype(BF16)
    m = m01.astype(BF16)
    return (jnp.dot(m, hi, preferred_element_type=F32)
            + jnp.dot(m, mid, preferred_element_type=F32)
            + jnp.dot(m, lo, preferred_element_type=F32))


def _block_masks(n, blk):
    row = lax.broadcasted_iota(jnp.int32, (n, n), 0)
    col = lax.broadcasted_iota(jnp.int32, (n, n), 1)
    same = (row // blk) == (col // blk)
    return same & (row >= col), same & (row > col), row == col


def _no_tick():
    pass


def _inv_unit_lower(a, eye, n_iter, tick=_no_tick):
    n = eye.shape[0]
    b = [-x for x in a]
    p = [eye + x for x in b]
    b = [_dot(x, x) for x in b]
    tick()
    for _ in range(n_iter - 1):
        pb = [_dot(jnp.concatenate([pi, bi], axis=0), bi) for pi, bi in zip(p, b)]
        tick()
        p = [pi + x[:n] for pi, x in zip(p, pb)]
        b = [x[n:] for x in pb]
    return [pi + _dot(pi, bi) for pi, bi in zip(p, b)]


def _dn_intra(q, k, v, beta, g_col, g_row, masks, n_iter, tick=_no_tick):
    causal, strict, diag = masks
    heads = range(len(q))
    decay = [jnp.where(causal, jnp.exp(jnp.where(causal, g_col[h] - g_row[h], 0.0)), 0.0) for h in heads]
    kb = [k[h] * beta[h] for h in heads]
    kq = [_dot_nt(jnp.concatenate([kb[h], q[h]], axis=0), k[h]) for h in heads]
    tick()
    a = [jnp.where(strict, kq[h][:SUB] * decay[h], 0.0) for h in heads]
    qk = [kq[h][SUB:] * decay[h] for h in heads]
    eye = jnp.where(diag, 1.0, 0.0).astype(F32)
    t_inv = _inv_unit_lower(a, eye, n_iter, tick)
    tick()
    e_g = [jnp.exp(g_col[h]) for h in heads]
    uw = [_dot(t_inv[h], jnp.concatenate([v[h] * beta[h], kb[h] * e_g[h]], axis=1)) for h in heads]
    tick()
    return ([x[:, :HEAD_DIM] for x in uw], [x[:, HEAD_DIM:] for x in uw], qk,
            [q[h] * e_g[h] for h in heads])


class _Slabs:
    def __init__(self, fn, *xs, slab):
        self._outs = []
        self.thunks = [functools.partial(self._run, fn, xs, r, slab)
                       for r in range(0, xs[0].shape[0], slab)]

    def _run(self, fn, xs, r, slab):
        self._outs.append(fn(*[x[r:r + slab] for x in xs]))

    def result(self):
        assert len(self._outs) == len(self.thunks)
        return jnp.concatenate(self._outs, axis=0)


def _by_rows(fn, *xs, slab):
    job = _Slabs(fn, *xs, slab=slab)
    for th in job.thunks:
        th()
    return job.result()


class _WorkQueue:
    def __init__(self):
        self._items = []

    def add(self, job, cost):
        self._items += [(cost, th, job) for th in job.thunks]

    def run(self, budget):
        while self._items and budget > 0:
            cost, th, _ = self._items.pop(0)
            th()
            budget -= cost

    def finish(self, job):
        while any(j is job for _, _, j in self._items):
            self._items.pop(0)[1]()


def _piped_dot(lhs, w, lo, hi, queue, budget=PIECE_BUDGET):
    outs = []
    for c in range(lo, hi, MXU_DIM):
        outs.append(jnp.dot(lhs, w[:, c:c + MXU_DIM], preferred_element_type=F32))
        queue.run(budget)
    return jnp.concatenate(outs, axis=1)


def _conv_job(x, b, tail, w, post):
    c = x.shape[1]
    taps = w.shape[0]
    sub = lax.broadcasted_iota(jnp.int32, (ROW_TILE, c), 0)
    b_rows = jnp.broadcast_to(b, (ROW_TILE, c))
    wj = [jnp.broadcast_to(w[taps - 1 - j:taps - j, :], (ROW_TILE, c)) for j in range(taps)]
    state = {'prev': [pltpu.roll(tail, j, 0) for j in range(1, taps)], 'cur': None}

    def tile(raw):
        cur = raw + b_rows
        rolled = [pltpu.roll(cur, 1, 0)]
        for _ in range(2, taps):
            rolled.append(pltpu.roll(rolled[-1], 1, 0))
        acc = cur * wj[0]
        for j in range(1, taps):
            acc = acc + jnp.where(sub < j, state['prev'][j - 1], rolled[j - 1]) * wj[j]
        state['prev'], state['cur'] = rolled, cur
        return post(acc)

    return _Slabs(tile, x, slab=ROW_TILE), state


def _gated_rms(o, norm_w, z_act):
    return o * lax.rsqrt(jnp.mean(o * o, axis=-1, keepdims=True) + RMS_EPS) * norm_w * z_act


def _l2n(x, mul=1.0):
    inv = lax.rsqrt(jnp.sum(x * x, axis=-1, keepdims=True) + RMS_EPS)
    return x * (inv if mul == 1.0 else inv * mul)


def _ffn_chunks():
    n_tiles = D_FF // MXU_DIM
    first = (n_tiles + 1) // 2 * MXU_DIM
    return ((0, first), (first, D_FF))


def _ffn_ln_tile(x_ref, wu_ref, wd_ref, g_ref, b_ref, o_ref, alpha):
    x = x_ref[...]
    xb = x.astype(BF16)
    acc = None
    for lo, hi in _ffn_chunks():
        a = jnp.dot(xb, wu_ref[:, lo:hi], preferred_element_type=F32)
        gt = jnp.dot(xb, wu_ref[:, D_FF + lo:D_FF + hi], preferred_element_type=F32)
        h = (_silu(a) * gt).astype(BF16)
        f = jnp.dot(h, wd_ref[lo:hi, :], preferred_element_type=F32)
        acc = f if acc is None else acc + f
    o_ref[...] = _layer_norm(alpha * x + 0.5 * acc, g_ref[...], b_ref[...])


def _ffn_ln_kernel(xp_ref, xs_ref, wu_ref, wd_ref, g_ref, b_ref, op_ref, os_ref, *, alpha, n_prompt):
    i = pl.program_id(0)
    pl.when(i < n_prompt)(functools.partial(_ffn_ln_tile, xp_ref, wu_ref, wd_ref, g_ref, b_ref, op_ref, alpha))
    pl.when(i >= n_prompt)(functools.partial(_ffn_ln_tile, xs_ref, wu_ref, wd_ref, g_ref, b_ref, os_ref, alpha))


def _const_spec(shape):
    nd = len(shape)
    return pl.BlockSpec(shape, lambda *_: (0,) * nd, pipeline_mode=pl.Buffered(1))


def _ffn_ln(xp, xs, wu, wd, g, b, alpha, tm):
    n_p, n_s = xp.shape[0] // tm, xs.shape[0] // tm
    assert xp.shape[0] % tm == 0 and xs.shape[0] % tm == 0 and D_FF % MXU_DIM == 0
    prompt_spec = pl.BlockSpec((tm, D_MODEL), lambda i: (jnp.minimum(i, n_p - 1), 0))
    sample_spec = pl.BlockSpec((tm, D_MODEL), lambda i: (jnp.maximum(i - n_p, 0), 0))
    return pl.pallas_call(
        functools.partial(_ffn_ln_kernel, alpha=alpha, n_prompt=n_p),
        grid=(n_p + n_s,),
        in_specs=[prompt_spec, sample_spec,
                  _const_spec(wu.shape), _const_spec(wd.shape),
                  _const_spec(g.shape), _const_spec(b.shape)],
        out_specs=[prompt_spec, sample_spec],
        out_shape=[jax.ShapeDtypeStruct(xp.shape, F32), jax.ShapeDtypeStruct(xs.shape, F32)],
        compiler_params=pltpu.CompilerParams(dimension_semantics=("arbitrary",),
                                             vmem_limit_bytes=VMEM_LIMIT),
        name="ffn_ln",
    )(xp, xs, wu, wd, g, b)


def _proj_views(w_ref, b_ref):
    g0, g1 = MAIN_COLS, MAIN_COLS + 2 * D_MODEL
    return (w_ref.at[:, 0:g0], w_ref.at[:, g0:g1], w_ref.at[:, g1:g1 + 2 * LANES],
            b_ref.at[:, 0:g0], b_ref.at[:, g0:g1], b_ref.at[:, g1:g1 + 2 * LANES])


def _branch_gates_and_z(hb, w_main, b_main):
    z = jnp.dot(hb, w_main[:, 5 * D_MODEL:6 * D_MODEL], preferred_element_type=F32) \
        + b_main[:, 5 * D_MODEL:6 * D_MODEL]
    return _silu(z)


def _beta_and_logdecay(hb, w_bd, b_bd, alog, dtb):
    bd = jnp.dot(hb, w_bd[...], preferred_element_type=F32) + b_bd[...]
    beta = _sigmoid(bd[:, :LANES])
    g = -jnp.exp(alog[...]) * _softplus(bd[:, LANES:] + dtb[...])
    return beta, g


def _merge_out_ln(x, hb, a_part, yb, w_gates, b_gates, wb_ref, wo_ref, ln_g, ln_b, alpha):
    gate_b = _sigmoid(jnp.dot(hb, w_gates[:, D_MODEL:], preferred_element_type=F32)
                      + b_gates[:, D_MODEL:])
    merged = a_part + gate_b * jnp.dot(yb, wb_ref[...], preferred_element_type=F32)
    mix = jnp.dot(merged.astype(BF16), wo_ref[...], preferred_element_type=F32)
    return _layer_norm(alpha * x + mix, ln_g[...], ln_b[...])


def _mix_prompt_kernel(x_ref, w_ref, b_ref, vg_ref, vb_ref,
                       ws_ref, bst_ref, convw_ref, alog_ref, dtb_ref, normw_ref,
                       wa_ref, wb_ref, wo_ref, ln_g, ln_b,
                       x2_ref, conv_out_ref, ssm_out_ref,
                       s_ref, xc_ref, q_s, k_s, v_s, z_s, g_s, beta_s, yb_s, *, alpha, tt, nseq):
    t = pl.program_id(1)
    nt = pl.num_programs(1)
    w_main, w_gates, w_bd, b_main, b_gates, b_bd = _proj_views(w_ref, b_ref)

    @pl.when(t == 0)
    def _():
        s_ref[...] = jnp.zeros(s_ref.shape, F32)
        xc_ref[...] = jnp.zeros(xc_ref.shape, F32)

    n_rows = nseq * tt
    x = x_ref[...].reshape(n_rows, D_MODEL)
    hb = x.astype(BF16)

    queue = _WorkQueue()

    def proj(part):
        return _piped_dot(hb, w_main, part * D_MODEL, (part + 1) * D_MODEL, queue)

    scale = HEAD_DIM ** -0.5
    heads = [slice(h * HEAD_DIM, (h + 1) * HEAD_DIM) for h in range(HEADS)]

    class _ConvJobs:
        def __init__(self, raw, part, post):
            self.cols = slice(part * D_MODEL, (part + 1) * D_MODEL)
            b_p = b_main[:, (2 + part) * D_MODEL:(3 + part) * D_MODEL]
            self.jobs = [_conv_job(raw[sq * tt:(sq + 1) * tt], b_p, xc_ref[sq, :, self.cols],
                                   convw_ref[:, self.cols], lambda acc: post(_silu(acc)))
                         for sq in range(nseq)]
            self.thunks = [th for job, _ in self.jobs for th in job.thunks]

        def result(self):
            for sq, (_, state) in enumerate(self.jobs):
                last = state['cur']

                @pl.when(t == nt - 1)
                def _():
                    conv_out_ref[sq, :, self.cols] = last[ROW_TILE - (DN_CONV - 1):, :]

                xc_ref[sq, :, self.cols] = last
            return jnp.concatenate([job.result() for job, _ in self.jobs], axis=0)

    def l2n_heads(a, mul):
        return jnp.concatenate([_l2n(a[:, sl], mul) for sl in heads], axis=1)

    b_z = b_main[:, 5 * D_MODEL:6 * D_MODEL]
    pq = proj(2)
    job_q = _ConvJobs(pq, 0, lambda a: l2n_heads(a, scale))
    queue.add(job_q, 45)
    pk = proj(3)
    job_k = _ConvJobs(pk, 1, lambda a: l2n_heads(a, 1.0))
    queue.add(job_k, 45)
    pvv = proj(4)
    job_vv = _ConvJobs(pvv, 2, lambda a: a)
    queue.add(job_vv, 35)
    pz = proj(5)
    job_z = _Slabs(lambda a: _silu(a + b_z), pz, slab=SLAB)
    queue.add(job_z, 20)
    pgb = _piped_dot(hb, w_gates, D_MODEL, 2 * D_MODEL, queue)
    beta, g_log = _beta_and_logdecay(hb, w_bd, b_bd, alog_ref, dtb_ref)
    beta_s[...] = beta
    queue.run(PIECE_BUDGET)
    rt = lax.broadcasted_iota(jnp.int32, (n_rows, n_rows), 0)
    ct = lax.broadcasted_iota(jnp.int32, (n_rows, n_rows), 1)
    cum = jnp.where(((rt // DN_CHUNK) == (ct // DN_CHUNK)) & (rt >= ct), 1.0, 0.0)
    g_s[...] = _dot_exact_lhs(cum, g_log)
    queue.finish(job_z)
    q_s[...] = job_q.result()
    k_s[...] = job_k.result()
    v_s[...] = job_vv.result()
    z_s[...] = job_z.result()

    branch_a = {}

    def dot_steps(lhs, w, lo, hi, pieces=4):
        step = (hi - lo) // pieces
        outs = []
        for p in range(pieces):
            outs.append(jnp.dot(lhs, w[:, lo + p * step:lo + (p + 1) * step], preferred_element_type=F32))
            yield MXU_PIECE_COST
        return jnp.concatenate(outs, axis=1)

    def job_steps(job, cost):
        for th in job.thunks:
            th()
            yield cost
        return job.result()

    def branch_a_steps():
        b_u = b_main[:, 0:D_MODEL]
        b_v, vg, vb = b_main[:, D_MODEL:2 * D_MODEL], vg_ref[...], vb_ref[...]
        b_ga = b_gates[:, :D_MODEL]
        pu = yield from dot_steps(hb, w_main, 0, D_MODEL)
        pv = yield from dot_steps(hb, w_main, D_MODEL, 2 * D_MODEL)
        vn = yield from job_steps(
            _Slabs(lambda a: _layer_norm(_gelu_tanh(a + b_v), vg, vb).astype(BF16), pv, slab=SLAB), 70)
        r128 = lax.broadcasted_iota(jnp.int32, (GM_CHUNK, GM_CHUNK), 0)
        c128 = lax.broadcasted_iota(jnp.int32, (GM_CHUNK, GM_CHUNK), 1)
        tril = r128 >= c128
        w_tril = [jnp.where(tril, ws_ref[g], 0.0).astype(BF16) for g in range(GROUPS)]
        rows = []
        for c in range(n_rows // GM_CHUNK):
            cols = []
            for g in range(GROUPS):
                blk = vn[c * GM_CHUNK:(c + 1) * GM_CHUNK, g * GROUP_DIM:(g + 1) * GROUP_DIM]
                cols.append(jnp.dot(w_tril[g], blk, preferred_element_type=F32) + bst_ref[:, g:g + 1])
            rows.append(jnp.concatenate(cols, axis=1))
            yield MXU_PIECE_COST
        mixed = jnp.concatenate(rows, axis=0)
        u = yield from job_steps(_Slabs(lambda a: _gelu_tanh(a + b_u), pu, slab=SLAB), 50)
        ya = yield from job_steps(_Slabs(lambda a, m: (a * m).astype(BF16), u, mixed, slab=SLAB), 10)
        pa = yield from dot_steps(ya, wa_ref, 0, D_MODEL)
        pga = yield from dot_steps(hb, w_gates, 0, D_MODEL)
        branch_a['a_part'] = yield from job_steps(
            _Slabs(lambda g, p_: _sigmoid(g + b_ga) * p_, pga, pa, slab=SLAB), 25)

    steps = branch_a_steps()

    def tick(budget=TICK_BUDGET):
        while budget > 0:
            cost = next(steps, None)
            if cost is None:
                return
            budget -= cost

    masks = _block_masks(SUB, DN_CHUNK)
    norm_w = normw_ref[...]

    chains = [(sq, h) for sq in range(nseq) for h in range(HEADS)]
    sls = [slice(h * HEAD_DIM, (h + 1) * HEAD_DIM) for _, h in chains]
    n_chunks = SUB // DN_CHUNK
    row_chunk = lax.broadcasted_iota(jnp.int32, (SUB, HEAD_DIM), 0) // DN_CHUNK
    zeros = jnp.zeros((DN_CHUNK, HEAD_DIM), F32)
    ids = range(len(chains))
    state = [s_ref[sq, h] for sq, h in chains]
    for j in range(tt // SUB):
        rows = [slice(sq * tt + j * SUB, sq * tt + (j + 1) * SUB) for sq, _ in chains]
        g_sub = [g_s[sq * tt + j * SUB:sq * tt + (j + 1) * SUB, :] for sq in range(nseq)]
        g_t = [g.T for g in g_sub]
        b_sub = [beta_s[sq * tt + j * SUB:sq * tt + (j + 1) * SUB, :] for sq in range(nseq)]
        q = [q_s[rows[i], sls[i]] for i in ids]
        k = [k_s[rows[i], sls[i]] for i in ids]
        vv = [v_s[rows[i], sls[i]] for i in ids]
        g_col = [jnp.broadcast_to(g_sub[sq][:, h:h + 1], (SUB, HEAD_DIM)) for sq, h in chains]
        g_row = [jnp.broadcast_to(g_t[sq][h:h + 1, :], (SUB, SUB)) for sq, h in chains]
        beta_h = [jnp.broadcast_to(b_sub[sq][:, h:h + 1], (SUB, HEAD_DIM)) for sq, h in chains]
        u_h, w_h, qk, qe = _dn_intra(q, k, vv, beta_h, g_col, g_row, masks, 5, tick)
        g_last = [[g_col[i][(c + 1) * DN_CHUNK - 1:(c + 1) * DN_CHUNK, :] for c in range(n_chunks)]
                  for i in ids]
        k_dec_t = []
        for i in ids:
            g_end = g_last[i][n_chunks - 1]
            for c in range(n_chunks - 2, -1, -1):
                g_end = jnp.where(row_chunk == c, g_last[i][c], g_end)
            k_dec_t.append((k[i] * jnp.exp(g_end - g_col[i])).T)
        outs = [[] for _ in ids]
        for c in range(n_chunks):
            rs = slice(c * DN_CHUNK, (c + 1) * DN_CHUNK)
            r = [_dot(jnp.concatenate([w_h[i][rs], qe[i][rs]], axis=0), state[i]) for i in ids]
            tick()
            v_new = [u_h[i][rs] - r[i][:DN_CHUNK] for i in ids]
            v_pad = [jnp.concatenate([zeros] * c + [v_new[i]] + [zeros] * (n_chunks - 1 - c), axis=0)
                     for i in ids]
            m = [_dot(jnp.concatenate([qk[i][rs], k_dec_t[i]], axis=0), v_pad[i]) for i in ids]
            tick()
            for i in ids:
                outs[i].append(r[i][DN_CHUNK:] + m[i][:DN_CHUNK])
            state = [state[i] * jnp.exp(g_last[i][c]) + m[i][DN_CHUNK:] for i in ids]
        for i in ids:
            o = jnp.concatenate(outs[i], axis=0)
            yb_s[rows[i], sls[i]] = _gated_rms(o, norm_w, z_s[rows[i], sls[i]]).astype(BF16)
    for i, (sq, h) in enumerate(chains):
        s_ref[sq, h] = state[i]
    tick(float('inf'))
    a_part = branch_a['a_part']

    @pl.when(t == nt - 1)
    def _():
        ssm_out_ref[...] = s_ref[...]

    pb = jnp.dot(yb_s[...], wb_ref[...], preferred_element_type=F32)
    b_gb = b_gates[:, D_MODEL:]
    merged = _by_rows(lambda a, g, p_: (a + _sigmoid(g + b_gb) * p_).astype(BF16), a_part, pgb, pb, slab=SLAB)
    mix = jnp.dot(merged, wo_ref[...], preferred_element_type=F32)
    ln_gain, ln_bias = ln_g[...], ln_b[...]
    x2 = _by_rows(lambda xx, m: _layer_norm(alpha * xx + m, ln_gain, ln_bias), x, mix, slab=SLAB)
    x2_ref[...] = x2.reshape(nseq, tt, D_MODEL)


def _mix_prompt(x1, p, alpha, tt, nseq):
    b, t, _ = x1.shape
    assert t % tt == 0 and tt % SUB == 0 and b % nseq == 0
    rows = nseq * tt
    consts = [p['w_proj'], p['b_proj'],
              p['gm_v_g'], p['gm_v_b'], p['gm_w_s'], p['gm_b_s_t'], p['conv_w'], p['a_log'],
              p['dt_bias'], p['norm_w'], p['w_a'], p['w_b'], p['w_o'], p['ln2_g'], p['ln2_b']]
    return pl.pallas_call(
        functools.partial(_mix_prompt_kernel, alpha=alpha, tt=tt, nseq=nseq),
        grid=(b // nseq, t // tt),
        in_specs=[pl.BlockSpec((nseq, tt, D_MODEL), lambda i, j: (i, j, 0))]
                 + [_const_spec(c.shape) for c in consts],
        out_specs=[pl.BlockSpec((nseq, tt, D_MODEL), lambda i, j: (i, j, 0)),
                   pl.BlockSpec((nseq, DN_CONV - 1, QKV), lambda i, j: (i, 0, 0)),
                   pl.BlockSpec((nseq, HEADS, HEAD_DIM, HEAD_DIM), lambda i, j: (i, 0, 0, 0))],
        out_shape=[jax.ShapeDtypeStruct((b, t, D_MODEL), F32),
                   jax.ShapeDtypeStruct((b, DN_CONV - 1, QKV), F32),
                   jax.ShapeDtypeStruct((b, HEADS, HEAD_DIM, HEAD_DIM), F32)],
        scratch_shapes=[pltpu.VMEM((nseq, HEADS, HEAD_DIM, HEAD_DIM), F32),
                        pltpu.VMEM((nseq, ROW_TILE, QKV), F32),
                        pltpu.VMEM((rows, D_MODEL), F32),
                        pltpu.VMEM((rows, D_MODEL), F32),
                        pltpu.VMEM((rows, D_MODEL), F32),
                        pltpu.VMEM((rows, D_MODEL), F32),
                        pltpu.VMEM((rows, LANES), F32),
                        pltpu.VMEM((rows, LANES), F32),
                        pltpu.VMEM((rows, D_MODEL), BF16)],
        compiler_params=pltpu.CompilerParams(dimension_semantics=("arbitrary", "arbitrary"),
                                             vmem_limit_bytes=VMEM_LIMIT),
        name="mix_prompt",
    )(x1, *consts)


def _sample_stage_one(x_ref, cs_ref, w_ref, b_ref, vg_ref, vb_ref, coef_ref, bias_ref, convw_ref,
                      alog_ref, dtb_ref, wa_ref, vrow_ref, conv_out_ref,
                      u_s, w_s, qe_s, qk_s, kdt_s, gcol_s, apart_s, zact_s, *, nb, ts):
    rows = nb * ROW_TILE
    w_main, w_gates, w_bd, b_main, b_gates, b_bd = _proj_views(w_ref, b_ref)
    x = x_ref[...]
    hb = x.astype(BF16)
    valid = (lax.broadcasted_iota(jnp.int32, (rows, 1), 0) % ROW_TILE) < ts
    validf = jnp.where(valid, 1.0, 0.0).astype(F32)

    u = _gelu_tanh(jnp.dot(hb, w_main[:, 0:D_MODEL], preferred_element_type=F32)
                   + b_main[:, 0:D_MODEL])
    v = _gelu_tanh(jnp.dot(hb, w_main[:, D_MODEL:2 * D_MODEL], preferred_element_type=F32)
                   + b_main[:, D_MODEL:2 * D_MODEL])
    vn = _layer_norm(v, vg_ref[...], vb_ref[...])
    vn3 = vn.reshape(nb, ROW_TILE, D_MODEL)
    vrow_ref[...] = vn3[:, :ts, :]
    mixed = vn3 * coef_ref[0][None] + bias_ref[...][None]
    for j in range(1, DN_CONV):
        mixed = mixed + pltpu.roll(vn3, j, 1) * coef_ref[j][None]
    ya = (u * mixed.reshape(rows, D_MODEL)).astype(BF16)
    gate_a = _sigmoid(jnp.dot(hb, w_gates[:, :D_MODEL], preferred_element_type=F32)
                      + b_gates[:, :D_MODEL])
    apart_s[...] = gate_a * jnp.dot(ya, wa_ref[...], preferred_element_type=F32)

    qkv = jnp.dot(hb, w_main[:, 2 * D_MODEL:5 * D_MODEL], preferred_element_type=F32) \
        + b_main[:, 2 * D_MODEL:5 * D_MODEL]
    zfull = jnp.where(valid, qkv, 0.0) + cs_ref[...]
    z3 = zfull.reshape(nb, ROW_TILE, QKV)
    conv_out_ref[...] = z3[:, ts - (DN_CONV - 1):ts, :]
    acc = z3 * convw_ref[DN_CONV - 1:DN_CONV, :][None]
    for j in range(1, DN_CONV):
        acc = acc + pltpu.roll(z3, j, 1) * convw_ref[DN_CONV - 1 - j:DN_CONV - j, :][None]
    sact = _silu(acc.reshape(rows, QKV)) * validf
    zact_s[...] = _branch_gates_and_z(hb, w_main, b_main)
    beta, g_log = _beta_and_logdecay(hb, w_bd, b_bd, alog_ref, dtb_ref)
    beta = beta * validf
    g_log = g_log * validf
    rt = lax.broadcasted_iota(jnp.int32, (rows, rows), 0)
    ct = lax.broadcasted_iota(jnp.int32, (rows, rows), 1)
    cum = jnp.where(((rt // ROW_TILE) == (ct // ROW_TILE)) & (rt >= ct), 1.0, 0.0)
    g_cum = _dot_exact_lhs(cum, g_log)

    assert rows == SUB
    g_t = g_cum.T
    masks = _block_masks(SUB, ROW_TILE)
    scale = HEAD_DIM ** -0.5
    heads = range(HEADS)
    sls = [slice(h * HEAD_DIM, (h + 1) * HEAD_DIM) for h in heads]
    q = [_l2n(sact[:, sl]) * scale * validf for sl in sls]
    k = [_l2n(sact[:, D_MODEL + h * HEAD_DIM:D_MODEL + (h + 1) * HEAD_DIM]) * validf for h in heads]
    vv = [sact[:, 2 * D_MODEL + h * HEAD_DIM:2 * D_MODEL + (h + 1) * HEAD_DIM] for h in heads]
    g_col = [jnp.broadcast_to(g_cum[:, h:h + 1], (SUB, HEAD_DIM)) for h in heads]
    g_row = [jnp.broadcast_to(g_t[h:h + 1, :], (SUB, SUB)) for h in heads]
    beta_h = [jnp.broadcast_to(beta[:, h:h + 1], (SUB, HEAD_DIM)) for h in heads]
    u_h, w_h, qk, qe = _dn_intra(q, k, vv, beta_h, g_col, g_row, masks, 2)
    for h in heads:
        g_end = jnp.broadcast_to(g_col[h].reshape(nb, ROW_TILE, HEAD_DIM)[:, ROW_TILE - 1:, :],
                                 (nb, ROW_TILE, HEAD_DIM)).reshape(SUB, HEAD_DIM)
        kdt_s[h] = (k[h] * jnp.exp(g_end - g_col[h])).T
        u_s[h], w_s[h], qe_s[h], qk_s[h], gcol_s[h] = u_h[h], w_h[h], qe[h], qk[h], g_col[h]


def _sample_state_part(part, s_in_ref, normw_ref, s_out_ref, u_s, w_s, qe_s, qk_s, kdt_s, gcol_s,
                       zact_s, yb_s, *, nb):
    span = nb * ROW_TILE
    base = part * span
    heads = range(HEADS)
    tiles = [slice(base + i * ROW_TILE, base + (i + 1) * ROW_TILE) for i in range(nb)]
    zeros = jnp.zeros((SUB - span, HEAD_DIM), F32)
    seq_of_row = lax.broadcasted_iota(jnp.int32, (SUB, HEAD_DIM), 0) // ROW_TILE
    norm_w = normw_ref[...]
    r = [[_dot(jnp.concatenate([w_s[h, rs, :], qe_s[h, rs, :]], axis=0), s_in_ref[i, h])
          for i, rs in enumerate(tiles)] for h in heads]
    v_new = []
    for h in heads:
        mine = jnp.concatenate([u_s[h, rs, :] - r[h][i][:ROW_TILE] for i, rs in enumerate(tiles)], axis=0)
        v_new.append(jnp.concatenate([mine, zeros] if part == 0 else [zeros, mine], axis=0))
    qkv_new = [_dot(qk_s[h, base:base + span, :], v_new[h]) for h in heads]
    for i, rs in enumerate(tiles):
        for h in heads:
            g_last = gcol_s[h, rs.stop - 1:rs.stop, :]
            s_out_ref[i, h] = (s_in_ref[i, h] * jnp.exp(g_last)
                               + _dot(kdt_s[h], jnp.where(seq_of_row == part * nb + i, v_new[h], 0.0)))
    for h in heads:
        sl = slice(h * HEAD_DIM, (h + 1) * HEAD_DIM)
        o = jnp.concatenate([r[h][i][ROW_TILE:] for i in range(nb)], axis=0) + qkv_new[h]
        yb_s[base:base + span, sl] = _gated_rms(o, norm_w, zact_s[base:base + span, sl]).astype(BF16)


def _mix_sample_kernel(x_ref, cs_ref, s_in_ref, w_ref, b_ref,
                       vg_ref, vb_ref, coef_ref, bias_ref, convw_ref, alog_ref, dtb_ref, normw_ref,
                       wa_ref, wb_ref, wo_ref, ln_g, ln_b,
                       x2_ref, vrow_ref, conv_out_ref, s_out_ref,
                       u_s, w_s, qe_s, qk_s, kdt_s, gcol_s, apart_s, zact_s, yb_s, *, alpha, nb, ts):
    half = pl.program_id(1)
    per_head = (u_s, w_s, qe_s, qk_s, kdt_s, gcol_s)
    pl.when(half == 0)(functools.partial(
        _sample_stage_one, x_ref, cs_ref, w_ref, b_ref, vg_ref, vb_ref, coef_ref, bias_ref, convw_ref,
        alog_ref, dtb_ref, wa_ref, vrow_ref, conv_out_ref, *per_head, apart_s, zact_s, nb=2 * nb, ts=ts))
    for part in range(2):
        pl.when(half == part)(functools.partial(
            _sample_state_part, part, s_in_ref, normw_ref, s_out_ref, *per_head, zact_s, yb_s, nb=nb))

    @pl.when(half == 1)
    def _():
        _, w_gates, _, _, b_gates, _ = _proj_views(w_ref, b_ref)
        x = x_ref[...]
        x2 = _merge_out_ln(x, x.astype(BF16), apart_s[...], yb_s[...], w_gates, b_gates, wb_ref, wo_ref,
                           ln_g, ln_b, alpha)
        x2_ref[...] = x2.reshape(2 * nb, ROW_TILE, D_MODEL)[:, :ts, :]


def _mix_sample(x1, cs_pad, s_in, p, alpha, nb, ts):
    n = x1.shape[0]
    nseq = n // ROW_TILE
    assert nseq % (2 * nb) == 0 and 2 * nb * ROW_TILE == SUB
    rows = 2 * nb * ROW_TILE
    consts = [p['w_proj'], p['b_proj'],
              p['gm_v_g'], p['gm_v_b'], p['mix_coef'], p['mix_bias'], p['conv_w'], p['a_log'],
              p['dt_bias'], p['norm_w'], p['w_a'], p['w_b'], p['w_o'], p['ln2_g'], p['ln2_b']]
    state_spec = pl.BlockSpec((nb, HEADS, HEAD_DIM, HEAD_DIM), lambda i, half: (2 * i + half, 0, 0, 0))
    token_spec = pl.BlockSpec((2 * nb, ts, D_MODEL), lambda i, half: (i, 0, 0))
    per_head = pltpu.VMEM((HEADS, SUB, HEAD_DIM), F32)
    return pl.pallas_call(
        functools.partial(_mix_sample_kernel, alpha=alpha, nb=nb, ts=ts),
        grid=(nseq // (2 * nb), 2),
        in_specs=[pl.BlockSpec((rows, D_MODEL), lambda i, half: (i, 0)),
                  pl.BlockSpec((rows, QKV), lambda i, half: (i, 0)),
                  state_spec] + [_const_spec(c.shape) for c in consts],
        out_specs=[token_spec, token_spec,
                   pl.BlockSpec((2 * nb, DN_CONV - 1, QKV), lambda i, half: (i, 0, 0)),
                   state_spec],
        scratch_shapes=[per_head, per_head, per_head, per_head, per_head, per_head,
                        pltpu.VMEM((rows, D_MODEL), F32),
                        pltpu.VMEM((rows, D_MODEL), F32),
                        pltpu.VMEM((rows, D_MODEL), BF16)],
        out_shape=[jax.ShapeDtypeStruct((nseq, ts, D_MODEL), F32),
                   jax.ShapeDtypeStruct((nseq, ts, D_MODEL), F32),
                   jax.ShapeDtypeStruct((nseq, DN_CONV - 1, QKV), F32),
                   jax.ShapeDtypeStruct(s_in.shape, F32)],
        compiler_params=pltpu.CompilerParams(dimension_semantics=("arbitrary", "arbitrary"),
                                             vmem_limit_bytes=VMEM_LIMIT),
        name="mix_sample",
    )(x1, cs_pad, s_in, *consts)


def _pad_lanes(a, n=LANES):
    return jnp.pad(a, [(0, 0)] * (a.ndim - 1) + [(0, n - a.shape[-1])])


def _layer_params(l, ffn1_w_up, ffn1_w_down, ln1_g, ln1_b, w_in, b_in, gm_v_g, gm_v_b, gm_w_s,
                  gm_b_s, dn_conv_w, dn_a_log, dn_dt_bias, dn_norm_w, w_branch_a, w_branch_b,
                  w_out, ln2_g, ln2_b, ffn2_w_up, ffn2_w_down, ln3_g, ln3_b):
    row = lambda a: a[l][None, :].astype(F32)
    wi, bi = w_in[l], b_in[l]
    o_beta = MAIN_COLS
    o_dec = o_beta + HEADS
    o_gate = o_dec + HEADS
    ws = gm_w_s[l]
    lsm = DN_CONV
    shift = np.arange(lsm)[:, None]
    pos = np.arange(ROW_TILE)[None, :]
    live = (pos >= shift) & (pos < lsm)
    ws_head = ws[:, :lsm, :lsm]
    coef = jnp.where(live[:, :, None],
                     jnp.transpose(ws_head[:, np.clip(pos + 0 * shift, 0, lsm - 1),
                                           np.clip(pos - shift, 0, lsm - 1)], (1, 2, 0)), 0.0)
    bias = jnp.pad(gm_b_s[l][:, :lsm].T, ((0, ROW_TILE - lsm), (0, 0)))
    return {
        'ffn1': (ffn1_w_up[l].astype(BF16), ffn1_w_down[l].astype(BF16), row(ln1_g), row(ln1_b)),
        'ffn2': (ffn2_w_up[l].astype(BF16), ffn2_w_down[l].astype(BF16), row(ln3_g), row(ln3_b)),
        'w_proj': jnp.concatenate([wi[:, :MAIN_COLS], wi[:, o_gate:], _pad_lanes(wi[:, o_beta:o_dec]),
                                   _pad_lanes(wi[:, o_dec:o_gate])], axis=1).astype(BF16),
        'b_proj': jnp.concatenate([bi[:MAIN_COLS], bi[o_gate:], _pad_lanes(bi[o_beta:o_dec]),
                                   _pad_lanes(bi[o_dec:o_gate])])[None, :],
        'gm_v_g': row(gm_v_g), 'gm_v_b': row(gm_v_b),
        'gm_w_s': ws, 'gm_b_s_t': gm_b_s[l].T,
        'mix_coef': jnp.repeat(coef, GROUP_DIM, axis=-1), 'mix_bias': jnp.repeat(bias, GROUP_DIM, axis=-1),
        'conv_w': dn_conv_w[l],
        'a_log': _pad_lanes(dn_a_log[l][None, :].astype(F32)),
        'dt_bias': _pad_lanes(dn_dt_bias[l][None, :].astype(F32)),
        'norm_w': row(dn_norm_w),
        'w_a': w_branch_a[l].astype(BF16), 'w_b': w_branch_b[l].astype(BF16),
        'w_o': w_out[l].astype(BF16),
        'ln2_g': row(ln2_g), 'ln2_b': row(ln2_b),
    }


def kernel(x_prompt, x_sample, state_conv, state_ssm, ffn1_w_up, ffn1_w_down, ln1_g, ln1_b, w_in, b_in, gm_v_g, gm_v_b, gm_w_s, gm_b_s, dn_conv_w, dn_a_log, dn_dt_bias, dn_norm_w, w_branch_a, w_branch_b, w_out, ln2_g, ln2_b, ffn2_w_up, ffn2_w_down, ln3_g, ln3_b):
    depth = ffn1_w_up.shape[0]
    alpha = (2.0 * depth) ** 0.25
    bp, tp, _ = x_prompt.shape
    bs, ts, _ = x_sample.shape
    assert ts == DN_CONV and ts + (DN_CONV - 1) <= ROW_TILE
    y_p, y_s = x_prompt, x_sample
    conv_p, ssm_p, conv_s, ssm_s, v_s = [], [], [], [], []
    for l in range(depth):
        p = _layer_params(l, ffn1_w_up, ffn1_w_down, ln1_g, ln1_b, w_in, b_in, gm_v_g, gm_v_b,
                          gm_w_s, gm_b_s, dn_conv_w, dn_a_log, dn_dt_bias, dn_norm_w, w_branch_a,
                          w_branch_b, w_out, ln2_g, ln2_b, ffn2_w_up, ffn2_w_down, ln3_g, ln3_b)
        x1, x1s = _ffn_ln(y_p.reshape(bp * tp, D_MODEL), y_s.reshape(bs * ts, D_MODEL), *p['ffn1'],
                          alpha, FFN_ROWS)
        x2, c_p, s_p = _mix_prompt(x1.reshape(bp, tp, D_MODEL), p, alpha, PROMPT_ROWS, PROMPT_SEQS)
        x1s = jnp.pad(x1s.reshape(bs, ts, D_MODEL), ((0, 0), (0, ROW_TILE - ts), (0, 0)))
        cs_pad = jnp.pad(state_conv[l], ((0, 0), (ROW_TILE - (DN_CONV - 1), 0), (0, 0)))
        x2s, vrows, c_s, s_s = _mix_sample(x1s.reshape(bs * ROW_TILE, D_MODEL),
                                           cs_pad.reshape(bs * ROW_TILE, QKV),
                                           state_ssm[l], p, alpha, SAMPLE_SEQS, ts)
        y_p, y_s = _ffn_ln(x2.reshape(bp * tp, D_MODEL), x2s.reshape(bs * ts, D_MODEL), *p['ffn2'],
                           alpha, FFN_ROWS)
        y_p, y_s = y_p.reshape(bp, tp, D_MODEL), y_s.reshape(bs, ts, D_MODEL)
        conv_p.append(c_p)
        ssm_p.append(s_p)
        conv_s.append(c_s)
        ssm_s.append(s_s)
        v_s.append(vrows)
    return (y_p, y_s, jnp.stack(conv_p), jnp.stack(ssm_p), jnp.stack(conv_s), jnp.stack(ssm_s),
            jnp.stack(v_s))
```

```python
import functools
import math

import jax
import jax.numpy as jnp
import numpy as np
from jax import lax
from jax.experimental import pallas as pl
from jax.experimental.pallas import tpu as pltpu

F32 = jnp.float32
BF16 = jnp.bfloat16

D_MODEL = 1024
D_FF = 2816
HEADS = 8
HEAD_DIM = 128
GROUPS = 8
GROUP_DIM = 128
GM_CHUNK = 128
DN_CHUNK = 64
DN_CONV = 4
QKV = 3 * D_MODEL
MAIN_COLS = 6 * D_MODEL
LN_EPS = 1e-5
RMS_EPS = 1e-6

MXU_DIM = 256
SUB = 128
ROW_TILE = 8
LANES = 128
SLAB = 16
PIECE_BUDGET = 200
MXU_PIECE_COST = 256
TICK_BUDGET = 900
VMEM_LIMIT = 56 * 1024 * 1024
FFN_ROWS = 512
PROMPT_ROWS, PROMPT_SEQS = 128, 2
SAMPLE_SEQS = 8


def _sigmoid(x):
    return 0.5 * jnp.tanh(0.5 * x) + 0.5


def _silu(x):
    h = 0.5 * x
    return h + h * jnp.tanh(h)


def _gelu_tanh(x):
    c = math.sqrt(2.0 / math.pi)
    h = 0.5 * x
    return h + h * jnp.tanh(x * (c + (c * 0.044715) * (x * x)))


def _softplus(x):
    return jnp.maximum(x, 0.0) + jnp.log(1.0 + jnp.exp(-jnp.abs(x)))


def _layer_norm(y, g, b):
    mu = jnp.mean(y, axis=-1, keepdims=True)
    yc = y - mu
    var = jnp.mean(yc * yc, axis=-1, keepdims=True)
    return yc * lax.rsqrt(var + LN_EPS) * g + b


def _dot(a, b):
    return jnp.dot(a.astype(BF16), b.astype(BF16), preferred_element_type=F32)


def _dot_nt(a, b):
    return lax.dot_general(a.astype(BF16), b.astype(BF16), (((1,), (1,)), ((), ())),
                           preferred_element_type=F32)


def _dot_exact_lhs(m01, x):
    hi = x.astype(BF16)
    r1 = x - hi.astype(F32)
    mid = r1.astype(BF16)
    lo = (r1 - mid.astype(F32)).astype(BF16)
    m = m01.astype(BF16)
    return (jnp.dot(m, hi, preferred_element_type=F32)
            + jnp.dot(m, mid, preferred_element_type=F32)
            + jnp.dot(m, lo, preferred_element_type=F32))


def _block_masks(n, blk):
    row = lax.broadcasted_iota(jnp.int32, (n, n), 0)
    col = lax.broadcasted_iota(jnp.int32, (n, n), 1)
    same = (row // blk) == (col // blk)
    return same & (row >= col), same & (row > col), row == col


def _no_tick():
    pass


def _inv_unit_lower(a, eye, n_iter, tick=_no_tick):
    n = eye.shape[0]
    b = [-x for x in a]
    p = [eye + x for x in b]
    b = [_dot(x, x) for x in b]
    tick()
    for _ in range(n_iter - 1):
        pb = [_dot(jnp.concatenate([pi, bi], axis=0), bi) for pi, bi in zip(p, b)]
        tick()
        p = [pi + x[:n] for pi, x in zip(p, pb)]
        b = [x[n:] for x in pb]
    return [pi + _dot(pi, bi) for pi, bi in zip(p, b)]


def _dn_intra(q, k, v, beta, g_col, g_row, masks, n_iter, tick=_no_tick):
    causal, strict, diag = masks
    heads = range(len(q))
    decay = [jnp.where(causal, jnp.exp(jnp.where(causal, g_col[h] - g_row[h], 0.0)), 0.0) for h in heads]
    kb = [k[h] * beta[h] for h in heads]
    kq = [_dot_nt(jnp.concatenate([kb[h], q[h]], axis=0), k[h]) for h in heads]
    tick()
    a = [jnp.where(strict, kq[h][:SUB] * decay[h], 0.0) for h in heads]
    qk = [kq[h][SUB:] * decay[h] for h in heads]
    eye = jnp.where(diag, 1.0, 0.0).astype(F32)
    t_inv = _inv_unit_lower(a, eye, n_iter, tick)
    tick()
    e_g = [jnp.exp(g_col[h]) for h in heads]
    uw = [_dot(t_inv[h], jnp.concatenate([v[h] * beta[h], kb[h] * e_g[h]], axis=1)) for h in heads]
    tick()
    return ([x[:, :HEAD_DIM] for x in uw], [x[:, HEAD_DIM:] for x in uw], qk,
            [q[h] * e_g[h] for h in heads])


class _Slabs:
    def __init__(self, fn, *xs, slab):
        self._outs = []
        self.thunks = [functools.partial(self._run, fn, xs, r, slab)
                       for r in range(0, xs[0].shape[0], slab)]

    def _run(self, fn, xs, r, slab):
        self._outs.append(fn(*[x[r:r + slab] for x in xs]))

    def result(self):
        assert len(self._outs) == len(self.thunks)
        return jnp.concatenate(self._outs, axis=0)


def _by_rows(fn, *xs, slab):
    job = _Slabs(fn, *xs, slab=slab)
    for th in job.thunks:
        th()
    return job.result()


class _WorkQueue:
    def __init__(self):
        self._items = []

    def add(self, job, cost):
        self._items += [(cost, th, job) for th in job.thunks]

    def run(self, budget):
        while self._items and budget > 0:
            cost, th, _ = self._items.pop(0)
            th()
            budget -= cost

    def finish(self, job):
        while any(j is job for _, _, j in self._items):
            self._items.pop(0)[1]()


def _piped_dot(lhs, w, lo, hi, queue, budget=PIECE_BUDGET):
    outs = []
    for c in range(lo, hi, MXU_DIM):
        outs.append(jnp.dot(lhs, w[:, c:c + MXU_DIM], preferred_element_type=F32))
        queue.run(budget)
    return jnp.concatenate(outs, axis=1)


def _conv_job(x, b, tail, w, post):
    c = x.shape[1]
    taps = w.shape[0]
    sub = lax.broadcasted_iota(jnp.int32, (ROW_TILE, c), 0)
    b_rows = jnp.broadcast_to(b, (ROW_TILE, c))
    wj = [jnp.broadcast_to(w[taps - 1 - j:taps - j, :], (ROW_TILE, c)) for j in range(taps)]
    state = {'prev': [pltpu.roll(tail, j, 0) for j in range(1, taps)], 'cur': None}

    def tile(raw):
        cur = raw + b_rows
        rolled = [pltpu.roll(cur, 1, 0)]
        for _ in range(2, taps):
            rolled.append(pltpu.roll(rolled[-1], 1, 0))
        acc = cur * wj[0]
        for j in range(1, taps):
            acc = acc + jnp.where(sub < j, state['prev'][j - 1], rolled[j - 1]) * wj[j]
        state['prev'], state['cur'] = rolled, cur
        return post(acc)

    return _Slabs(tile, x, slab=ROW_TILE), state


def _gated_rms(o, norm_w, z_act):
    return o * lax.rsqrt(jnp.mean(o * o, axis=-1, keepdims=True) + RMS_EPS) * norm_w * z_act


def _l2n(x, mul=1.0):
    inv = lax.rsqrt(jnp.sum(x * x, axis=-1, keepdims=True) + RMS_EPS)
    return x * (inv if mul == 1.0 else inv * mul)


def _ffn_chunks():
    n_tiles = D_FF // MXU_DIM
    first = (n_tiles + 1) // 2 * MXU_DIM
    return ((0, first), (first, D_FF))


def _ffn_ln_tile(x_ref, wu_ref, wd_ref, g_ref, b_ref, o_ref, alpha):
    x = x_ref[...]
    xb = x.astype(BF16)
    acc = None
    for lo, hi in _ffn_chunks():
        a = jnp.dot(xb, wu_ref[:, lo:hi], preferred_element_type=F32)
        gt = jnp.dot(xb, wu_ref[:, D_FF + lo:D_FF + hi], preferred_element_type=F32)
        h = (_silu(a) * gt).astype(BF16)
        f = jnp.dot(h, wd_ref[lo:hi, :], preferred_element_type=F32)
        acc = f if acc is None else acc + f
    o_ref[...] = _layer_norm(alpha * x + 0.5 * acc, g_ref[...], b_ref[...])


def _ffn_ln_kernel(xp_ref, xs_ref, wu_ref, wd_ref, g_ref, b_ref, op_ref, os_ref, *, alpha, n_prompt):
    i = pl.program_id(0)
    pl.when(i < n_prompt)(functools.partial(_ffn_ln_tile, xp_ref, wu_ref, wd_ref, g_ref, b_ref, op_ref, alpha))
    pl.when(i >= n_prompt)(functools.partial(_ffn_ln_tile, xs_ref, wu_ref, wd_ref, g_ref, b_ref, os_ref, alpha))


def _ffn_ln_kernel_piped(xp_ref, xs_ref, wu_ref, wd_ref, g_ref, b_ref, op_ref, os_ref, pre_ref, a_ref,
                         *, alpha, n_prompt, n_tiles):
    i = pl.program_id(0)
    gain, bias = g_ref[...], b_ref[...]
    chunks = _ffn_chunks()
    lo1, hi1 = chunks[0]

    def norm_prev(o_ref):
        for r in range(0, pre_ref.shape[0], SLAB):
            o_ref[r:r + SLAB, :] = _layer_norm(pre_ref[r:r + SLAB, :], gain, bias)

    def x_tile():
        return jnp.where(i < n_prompt, xp_ref[...], xs_ref[...])

    def first_region(o_ref):
        norm_prev(o_ref)
        a_ref[...] = jnp.dot(x_tile().astype(BF16), wu_ref[:, lo1:hi1], preferred_element_type=F32)

    @pl.when(i == 0)
    def _():
        pre_ref[...] = jnp.zeros(pre_ref.shape, F32)

    pl.when((i < n_tiles) & (i <= n_prompt))(functools.partial(first_region, op_ref))
    pl.when((i < n_tiles) & (i > n_prompt))(functools.partial(first_region, os_ref))

    @pl.when(i + 1 <= n_tiles)
    def _():
        x = x_tile()
        xb = x.astype(BF16)
        acc = None
        for c, (lo, hi) in enumerate(chunks):
            a = a_ref[...] if c == 0 else jnp.dot(xb, wu_ref[:, lo:hi], preferred_element_type=F32)
            gt = jnp.dot(xb, wu_ref[:, D_FF + lo:D_FF + hi], preferred_element_type=F32)
            h = (_silu(a) * gt).astype(BF16)
            f = jnp.dot(h, wd_ref[lo:hi, :], preferred_element_type=F32)
            acc = f if acc is None else acc + f
        pre_ref[...] = alpha * x + 0.5 * acc

    pl.when(i == n_tiles)(functools.partial(norm_prev, os_ref))


def _const_spec(shape):
    nd = len(shape)
    return pl.BlockSpec(shape, lambda *_: (0,) * nd, pipeline_mode=pl.Buffered(1))


def _ffn_ln(xp, xs, wu, wd, g, b, alpha, tm):
    n_p, n_s = xp.shape[0] // tm, xs.shape[0] // tm
    assert xp.shape[0] % tm == 0 and xs.shape[0] % tm == 0 and D_FF % MXU_DIM == 0 and n_p >= 1 and n_s >= 1
    def prompt_spec(lag):
        return pl.BlockSpec((tm, D_MODEL), lambda i: (jnp.clip(i - lag, 0, n_p - 1), 0))

    def sample_spec(lag):
        return pl.BlockSpec((tm, D_MODEL), lambda i: (jnp.clip(i - lag - n_p, 0, n_s - 1), 0))

    first_chunk = _ffn_chunks()[0]
    return pl.pallas_call(
        functools.partial(_ffn_ln_kernel_piped, alpha=alpha, n_prompt=n_p, n_tiles=n_p + n_s),
        grid=(n_p + n_s + 1,),
        in_specs=[prompt_spec(0), sample_spec(0),
                  _const_spec(wu.shape), _const_spec(wd.shape),
                  _const_spec(g.shape), _const_spec(b.shape)],
        out_specs=[prompt_spec(1), sample_spec(1)],
        out_shape=[jax.ShapeDtypeStruct(xp.shape, F32), jax.ShapeDtypeStruct(xs.shape, F32)],
        scratch_shapes=[pltpu.VMEM((tm, D_MODEL), F32),
                        pltpu.VMEM((tm, first_chunk[1] - first_chunk[0]), F32)],
        compiler_params=pltpu.CompilerParams(dimension_semantics=("arbitrary",),
                                             vmem_limit_bytes=VMEM_LIMIT),
        name="ffn_ln",
    )(xp, xs, wu, wd, g, b)


def _proj_views(w_ref, b_ref):
    g0, g1 = MAIN_COLS, MAIN_COLS + 2 * D_MODEL
    return (w_ref.at[:, 0:g0], w_ref.at[:, g0:g1], w_ref.at[:, g1:g1 + 2 * LANES],
            b_ref.at[:, 0:g0], b_ref.at[:, g0:g1], b_ref.at[:, g1:g1 + 2 * LANES])


def _branch_gates_and_z(hb, w_main, b_main):
    z = jnp.dot(hb, w_main[:, 5 * D_MODEL:6 * D_MODEL], preferred_element_type=F32) \
        + b_main[:, 5 * D_MODEL:6 * D_MODEL]
    return _silu(z)


def _beta_and_logdecay(hb, w_bd, b_bd, alog, dtb):
    bd = jnp.dot(hb, w_bd[...], preferred_element_type=F32) + b_bd[...]
    beta = _sigmoid(bd[:, :LANES])
    g = -jnp.exp(alog[...]) * _softplus(bd[:, LANES:] + dtb[...])
    return beta, g


def _merge_out_ln(x, hb, a_part, yb, w_gates, b_gates, wb_ref, wo_ref, ln_g, ln_b, alpha):
    gate_b = _sigmoid(jnp.dot(hb, w_gates[:, D_MODEL:], preferred_element_type=F32)
                      + b_gates[:, D_MODEL:])
    merged = a_part + gate_b * jnp.dot(yb, wb_ref[...], preferred_element_type=F32)
    mix = jnp.dot(merged.astype(BF16), wo_ref[...], preferred_element_type=F32)
    return _layer_norm(alpha * x + mix, ln_g[...], ln_b[...])


def _mix_prompt_kernel(x_ref, w_ref, b_ref, vg_ref, vb_ref,
                       ws_ref, bst_ref, convw_ref, alog_ref, dtb_ref, normw_ref,
                       wa_ref, wb_ref, wo_ref, ln_g, ln_b,
                       x2_ref, conv_out_ref, ssm_out_ref,
                       s_ref, xc_ref, q_s, k_s, v_s, z_s, g_s, beta_s, yb_s, *, alpha, tt, nseq):
    t = pl.program_id(1)
    nt = pl.num_programs(1)
    w_main, w_gates, w_bd, b_main, b_gates, b_bd = _proj_views(w_ref, b_ref)

    @pl.when(t == 0)
    def _():
        s_ref[...] = jnp.zeros(s_ref.shape, F32)
        xc_ref[...] = jnp.zeros(xc_ref.shape, F32)

    n_rows = nseq * tt
    x = x_ref[...].reshape(n_rows, D_MODEL)
    hb = x.astype(BF16)

    queue = _WorkQueue()

    def proj(part):
        return _piped_dot(hb, w_main, part * D_MODEL, (part + 1) * D_MODEL, queue)

    scale = HEAD_DIM ** -0.5
    heads = [slice(h * HEAD_DIM, (h + 1) * HEAD_DIM) for h in range(HEADS)]

    class _ConvJobs:
        def __init__(self, raw, part, post):
            self.cols = slice(part * D_MODEL, (part + 1) * D_MODEL)
            b_p = b_main[:, (2 + part) * D_MODEL:(3 + part) * D_MODEL]
            self.jobs = [_conv_job(raw[sq * tt:(sq + 1) * tt], b_p, xc_ref[sq, :, self.cols],
                                   convw_ref[:, self.cols], lambda acc: post(_silu(acc)))
                         for sq in range(nseq)]
            self.thunks = [th for job, _ in self.jobs for th in job.thunks]

        def result(self):
            for sq, (_, state) in enumerate(self.jobs):
                last = state['cur']

                @pl.when(t == nt - 1)
                def _():
                    conv_out_ref[sq, :, self.cols] = last[ROW_TILE - (DN_CONV - 1):, :]

                xc_ref[sq, :, self.cols] = last
            return jnp.concatenate([job.result() for job, _ in self.jobs], axis=0)

    def l2n_heads(a, mul):
        return jnp.concatenate([_l2n(a[:, sl], mul) for sl in heads], axis=1)

    b_z = b_main[:, 5 * D_MODEL:6 * D_MODEL]
    pq = proj(2)
    job_q = _ConvJobs(pq, 0, lambda a: l2n_heads(a, scale))
    queue.add(job_q, 45)
    pk = proj(3)
    job_k = _ConvJobs(pk, 1, lambda a: l2n_heads(a, 1.0))
    queue.add(job_k, 45)
    pvv = proj(4)
    job_vv = _ConvJobs(pvv, 2, lambda a: a)
    queue.add(job_vv, 35)
    pz = proj(5)
    job_z = _Slabs(lambda a: _silu(a + b_z), pz, slab=SLAB)
    queue.add(job_z, 20)
    pgb = _piped_dot(hb, w_gates, D_MODEL, 2 * D_MODEL, queue)
    beta, g_log = _beta_and_logdecay(hb, w_bd, b_bd, alog_ref, dtb_ref)
    beta_s[...] = beta
    queue.run(PIECE_BUDGET)
    rt = lax.broadcasted_iota(jnp.int32, (n_rows, n_rows), 0)
    ct = lax.broadcasted_iota(jnp.int32, (n_rows, n_rows), 1)
    cum = jnp.where(((rt // DN_CHUNK) == (ct // DN_CHUNK)) & (rt >= ct), 1.0, 0.0)
    g_s[...] = _dot_exact_lhs(cum, g_log)
    queue.finish(job_z)
    q_s[...] = job_q.result()
    k_s[...] = job_k.result()
    v_s[...] = job_vv.result()
    z_s[...] = job_z.result()

    branch_a = {}

    def dot_steps(lhs, w, lo, hi, pieces=4):
        step = (hi - lo) // pieces
        outs = []
        for p in range(pieces):
            outs.append(jnp.dot(lhs, w[:, lo + p * step:lo + (p + 1) * step], preferred_element_type=F32))
            yield MXU_PIECE_COST
        return jnp.concatenate(outs, axis=1)

    def job_steps(job, cost):
        for th in job.thunks:
            th()
            yield cost
        return job.result()

    def branch_a_steps():
        b_u = b_main[:, 0:D_MODEL]
        b_v, vg, vb = b_main[:, D_MODEL:2 * D_MODEL], vg_ref[...], vb_ref[...]
        b_ga = b_gates[:, :D_MODEL]
        pu = yield from dot_steps(hb, w_main, 0, D_MODEL)
        pv = yield from dot_steps(hb, w_main, D_MODEL, 2 * D_MODEL)
        vn = yield from job_steps(
            _Slabs(lambda a: _layer_norm(_gelu_tanh(a + b_v), vg, vb).astype(BF16), pv, slab=SLAB), 70)
        r128 = lax.broadcasted_iota(jnp.int32, (GM_CHUNK, GM_CHUNK), 0)
        c128 = lax.broadcasted_iota(jnp.int32, (GM_CHUNK, GM_CHUNK), 1)
        tril = r128 >= c128
        w_tril = [jnp.where(tril, ws_ref[g], 0.0).astype(BF16) for g in range(GROUPS)]
        rows = []
        for c in range(n_rows // GM_CHUNK):
            cols = []
            for g in range(GROUPS):
                blk = vn[c * GM_CHUNK:(c + 1) * GM_CHUNK, g * GROUP_DIM:(g + 1) * GROUP_DIM]
                cols.append(jnp.dot(w_tril[g], blk, preferred_element_type=F32) + bst_ref[:, g:g + 1])
            rows.append(jnp.concatenate(cols, axis=1))
            yield MXU_PIECE_COST
        mixed = jnp.concatenate(rows, axis=0)
        u = yield from job_steps(_Slabs(lambda a: _gelu_tanh(a + b_u), pu, slab=SLAB), 50)
        ya = yield from job_steps(_Slabs(lambda a, m: (a * m).astype(BF16), u, mixed, slab=SLAB), 10)
        pa = yield from dot_steps(ya, wa_ref, 0, D_MODEL)
        pga = yield from dot_steps(hb, w_gates, 0, D_MODEL)
        branch_a['a_part'] = yield from job_steps(
            _Slabs(lambda g, p_: _sigmoid(g + b_ga) * p_, pga, pa, slab=SLAB), 25)

    steps = branch_a_steps()

    def tick(budget=TICK_BUDGET):
        while budget > 0:
            cost = next(steps, None)
            if cost is None:
                return
            budget -= cost

    masks = _block_masks(SUB, DN_CHUNK)
    norm_w = normw_ref[...]

    chains = [(sq, h) for sq in range(nseq) for h in range(HEADS)]
    sls = [slice(h * HEAD_DIM, (h + 1) * HEAD_DIM) for _, h in chains]
    n_chunks = SUB // DN_CHUNK
    row_chunk = lax.broadcasted_iota(jnp.int32, (SUB, HEAD_DIM), 0) // DN_CHUNK
    zeros = jnp.zeros((DN_CHUNK, HEAD_DIM), F32)
    ids = range(len(chains))
    state = [s_ref[sq, h] for sq, h in chains]
    for j in range(tt // SUB):
        rows = [slice(sq * tt + j * SUB, sq * tt + (j + 1) * SUB) for sq, _ in chains]
        g_sub = [g_s[sq * tt + j * SUB:sq * tt + (j + 1) * SUB, :] for sq in range(nseq)]
        g_t = [g.T for g in g_sub]
        b_sub = [beta_s[sq * tt + j * SUB:sq * tt + (j + 1) * SUB, :] for sq in range(nseq)]
        q = [q_s[rows[i], sls[i]] for i in ids]
        k = [k_s[rows[i], sls[i]] for i in ids]
        vv = [v_s[rows[i], sls[i]] for i in ids]
        g_col = [jnp.broadcast_to(g_sub[sq][:, h:h + 1], (SUB, HEAD_DIM)) for sq, h in chains]
        g_row = [jnp.broadcast_to(g_t[sq][h:h + 1, :], (SUB, SUB)) for sq, h in chains]
        beta_h = [jnp.broadcast_to(b_sub[sq][:, h:h + 1], (SUB, HEAD_DIM)) for sq, h in chains]
        u_h, w_h, qk, qe = _dn_intra(q, k, vv, beta_h, g_col, g_row, masks, 5, tick)
        g_last = [[g_col[i][(c + 1) * DN_CHUNK - 1:(c + 1) * DN_CHUNK, :] for c in range(n_chunks)]
                  for i in ids]
        k_dec_t = []
        for i in ids:
            g_end = g_last[i][n_chunks - 1]
            for c in range(n_chunks - 2, -1, -1):
                g_end = jnp.where(row_chunk == c, g_last[i][c], g_end)
            k_dec_t.append((k[i] * jnp.exp(g_end - g_col[i])).T)
        outs = [[] for _ in ids]
        for c in range(n_chunks):
            rs = slice(c * DN_CHUNK, (c + 1) * DN_CHUNK)
            r = [_dot(jnp.concatenate([w_h[i][rs], qe[i][rs]], axis=0), state[i]) for i in ids]
            tick()
            v_new = [u_h[i][rs] - r[i][:DN_CHUNK] for i in ids]
            v_pad = [jnp.concatenate([zeros] * c + [v_new[i]] + [zeros] * (n_chunks - 1 - c), axis=0)
                     for i in ids]
            m = [_dot(jnp.concatenate([qk[i][rs], k_dec_t[i]], axis=0), v_pad[i]) for i in ids]
            tick()
            for i in ids:
                outs[i].append(r[i][DN_CHUNK:] + m[i][:DN_CHUNK])
            state = [state[i] * jnp.exp(g_last[i][c]) + m[i][DN_CHUNK:] for i in ids]
        for i in ids:
            o = jnp.concatenate(outs[i], axis=0)
            yb_s[rows[i], sls[i]] = _gated_rms(o, norm_w, z_s[rows[i], sls[i]]).astype(BF16)
    for i, (sq, h) in enumerate(chains):
        s_ref[sq, h] = state[i]
    tick(float('inf'))
    a_part = branch_a['a_part']

    @pl.when(t == nt - 1)
    def _():
        ssm_out_ref[...] = s_ref[...]

    pb = jnp.dot(yb_s[...], wb_ref[...], preferred_element_type=F32)
    b_gb = b_gates[:, D_MODEL:]
    merged = _by_rows(lambda a, g, p_: (a + _sigmoid(g + b_gb) * p_).astype(BF16), a_part, pgb, pb, slab=SLAB)
    mix = jnp.dot(merged, wo_ref[...], preferred_element_type=F32)
    ln_gain, ln_bias = ln_g[...], ln_b[...]
    x2 = _by_rows(lambda xx, m: _layer_norm(alpha * xx + m, ln_gain, ln_bias), x, mix, slab=SLAB)
    x2_ref[...] = x2.reshape(nseq, tt, D_MODEL)


def _mix_prompt(x1, p, alpha, tt, nseq):
    b, t, _ = x1.shape
    assert t % tt == 0 and tt % SUB == 0 and b % nseq == 0
    rows = nseq * tt
    consts = [p['w_proj'], p['b_proj'],
              p['gm_v_g'], p['gm_v_b'], p['gm_w_s'], p['gm_b_s_t'], p['conv_w'], p['a_log'],
              p['dt_bias'], p['norm_w'], p['w_a'], p['w_b'], p['w_o'], p['ln2_g'], p['ln2_b']]
    return pl.pallas_call(
        functools.partial(_mix_prompt_kernel, alpha=alpha, tt=tt, nseq=nseq),
        grid=(b // nseq, t // tt),
        in_specs=[pl.BlockSpec((nseq, tt, D_MODEL), lambda i, j: (i, j, 0))]
                 + [_const_spec(c.shape) for c in consts],
        out_specs=[pl.BlockSpec((nseq, tt, D_MODEL), lambda i, j: (i, j, 0)),
                   pl.BlockSpec((nseq, DN_CONV - 1, QKV), lambda i, j: (i, 0, 0)),
                   pl.BlockSpec((nseq, HEADS, HEAD_DIM, HEAD_DIM), lambda i, j: (i, 0, 0, 0))],
        out_shape=[jax.ShapeDtypeStruct((b, t, D_MODEL), F32),
                   jax.ShapeDtypeStruct((b, DN_CONV - 1, QKV), F32),
                   jax.ShapeDtypeStruct((b, HEADS, HEAD_DIM, HEAD_DIM), F32)],
        scratch_shapes=[pltpu.VMEM((nseq, HEADS, HEAD_DIM, HEAD_DIM), F32),
                        pltpu.VMEM((nseq, ROW_TILE, QKV), F32),
                        pltpu.VMEM((rows, D_MODEL), F32),
                        pltpu.VMEM((rows, D_MODEL), F32),
                        pltpu.VMEM((rows, D_MODEL), F32),
                        pltpu.VMEM((rows, D_MODEL), F32),
                        pltpu.VMEM((rows, LANES), F32),
                        pltpu.VMEM((rows, LANES), F32),
                        pltpu.VMEM((rows, D_MODEL), BF16)],
        compiler_params=pltpu.CompilerParams(dimension_semantics=("arbitrary", "arbitrary"),
                                             vmem_limit_bytes=VMEM_LIMIT),
        name="mix_prompt",
    )(x1, *consts)


def _sample_stage_one(x_ref, cs_ref, w_ref, b_ref, vg_ref, vb_ref, coef_ref, bias_ref, convw_ref,
                      alog_ref, dtb_ref, wa_ref, vrow_ref, conv_out_ref,
                      u_s, w_s, qe_s, qk_s, kdt_s, gcol_s, apart_s, zact_s, *, nb, ts):
    rows = nb * ROW_TILE
    w_main, w_gates, w_bd, b_main, b_gates, b_bd = _proj_views(w_ref, b_ref)
    x = x_ref[...]
    hb = x.astype(BF16)
    valid = (lax.broadcasted_iota(jnp.int32, (rows, 1), 0) % ROW_TILE) < ts
    validf = jnp.where(valid, 1.0, 0.0).astype(F32)

    u = _gelu_tanh(jnp.dot(hb, w_main[:, 0:D_MODEL], preferred_element_type=F32)
                   + b_main[:, 0:D_MODEL])
    v = _gelu_tanh(jnp.dot(hb, w_main[:, D_MODEL:2 * D_MODEL], preferred_element_type=F32)
                   + b_main[:, D_MODEL:2 * D_MODEL])
    vn = _layer_norm(v, vg_ref[...], vb_ref[...])
    vn3 = vn.reshape(nb, ROW_TILE, D_MODEL)
    vrow_ref[...] = vn3[:, :ts, :]
    mixed = vn3 * coef_ref[0][None] + bias_ref[...][None]
    for j in range(1, DN_CONV):
        mixed = mixed + pltpu.roll(vn3, j, 1) * coef_ref[j][None]
    ya = (u * mixed.reshape(rows, D_MODEL)).astype(BF16)
    gate_a = _sigmoid(jnp.dot(hb, w_gates[:, :D_MODEL], preferred_element_type=F32)
                      + b_gates[:, :D_MODEL])
    apart_s[...] = gate_a * jnp.dot(ya, wa_ref[...], preferred_element_type=F32)

    qkv = jnp.dot(hb, w_main[:, 2 * D_MODEL:5 * D_MODEL], preferred_element_type=F32) \
        + b_main[:, 2 * D_MODEL:5 * D_MODEL]
    zfull = jnp.where(valid, qkv, 0.0) + cs_ref[...]
    z3 = zfull.reshape(nb, ROW_TILE, QKV)
    conv_out_ref[...] = z3[:, ts - (DN_CONV - 1):ts, :]
    acc = z3 * convw_ref[DN_CONV - 1:DN_CONV, :][None]
    for j in range(1, DN_CONV):
        acc = acc + pltpu.roll(z3, j, 1) * convw_ref[DN_CONV - 1 - j:DN_CONV - j, :][None]
    sact = _silu(acc.reshape(rows, QKV)) * validf
    zact_s[...] = _branch_gates_and_z(hb, w_main, b_main)
    beta, g_log = _beta_and_logdecay(hb, w_bd, b_bd, alog_ref, dtb_ref)
    beta = beta * validf
    g_log = g_log * validf
    rt = lax.broadcasted_iota(jnp.int32, (rows, rows), 0)
    ct = lax.broadcasted_iota(jnp.int32, (rows, rows), 1)
    cum = jnp.where(((rt // ROW_TILE) == (ct // ROW_TILE)) & (rt >= ct), 1.0, 0.0)
    g_cum = _dot_exact_lhs(cum, g_log)

    assert rows == SUB
    g_t = g_cum.T
    masks = _block_masks(SUB, ROW_TILE)
    scale = HEAD_DIM ** -0.5
    heads = range(HEADS)
    sls = [slice(h * HEAD_DIM, (h + 1) * HEAD_DIM) for h in heads]
    q = [_l2n(sact[:, sl]) * scale * validf for sl in sls]
    k = [_l2n(sact[:, D_MODEL + h * HEAD_DIM:D_MODEL + (h + 1) * HEAD_DIM]) * validf for h in heads]
    vv = [sact[:, 2 * D_MODEL + h * HEAD_DIM:2 * D_MODEL + (h + 1) * HEAD_DIM] for h in heads]
    g_col = [jnp.broadcast_to(g_cum[:, h:h + 1], (SUB, HEAD_DIM)) for h in heads]
    g_row = [jnp.broadcast_to(g_t[h:h + 1, :], (SUB, SUB)) for h in heads]
    beta_h = [jnp.broadcast_to(beta[:, h:h + 1], (SUB, HEAD_DIM)) for h in heads]
    u_h, w_h, qk, qe = _dn_intra(q, k, vv, beta_h, g_col, g_row, masks, 2)
    for h in heads:
        g_end = jnp.broadcast_to(g_col[h].reshape(nb, ROW_TILE, HEAD_DIM)[:, ROW_TILE - 1:, :],
                                 (nb, ROW_TILE, HEAD_DIM)).reshape(SUB, HEAD_DIM)
        kdt_s[h] = (k[h] * jnp.exp(g_end - g_col[h])).T
        u_s[h], w_s[h], qe_s[h], qk_s[h], gcol_s[h] = u_h[h], w_h[h], qe[h], qk[h], g_col[h]


def _sample_state_part(part, s_in_ref, normw_ref, s_out_ref, u_s, w_s, qe_s, qk_s, kdt_s, gcol_s,
                       zact_s, yb_s, *, nb):
    span = nb * ROW_TILE
    base = part * span
    heads = range(HEADS)
    tiles = [slice(base + i * ROW_TILE, base + (i + 1) * ROW_TILE) for i in range(nb)]
    zeros = jnp.zeros((SUB - span, HEAD_DIM), F32)
    seq_of_row = lax.broadcasted_iota(jnp.int32, (SUB, HEAD_DIM), 0) // ROW_TILE
    norm_w = normw_ref[...]
    r = [[_dot(jnp.concatenate([w_s[h, rs, :], qe_s[h, rs, :]], axis=0), s_in_ref[i, h])
          for i, rs in enumerate(tiles)] for h in heads]
    v_new = []
    for h in heads:
        mine = jnp.concatenate([u_s[h, rs, :] - r[h][i][:ROW_TILE] for i, rs in enumerate(tiles)], axis=0)
        v_new.append(jnp.concatenate([mine, zeros] if part == 0 else [zeros, mine], axis=0))
    qkv_new = [_dot(qk_s[h, base:base + span, :], v_new[h]) for h in heads]
    for i, rs in enumerate(tiles):
        for h in heads:
            g_last = gcol_s[h, rs.stop - 1:rs.stop, :]
            s_out_ref[i, h] = (s_in_ref[i, h] * jnp.exp(g_last)
                               + _dot(kdt_s[h], jnp.where(seq_of_row == part * nb + i, v_new[h], 0.0)))
    for h in heads:
        sl = slice(h * HEAD_DIM, (h + 1) * HEAD_DIM)
        o = jnp.concatenate([r[h][i][ROW_TILE:] for i in range(nb)], axis=0) + qkv_new[h]
        yb_s[base:base + span, sl] = _gated_rms(o, norm_w, zact_s[base:base + span, sl]).astype(BF16)


def _mix_sample_kernel(x_ref, cs_ref, s_in_ref, w_ref, b_ref,
                       vg_ref, vb_ref, coef_ref, bias_ref, convw_ref, alog_ref, dtb_ref, normw_ref,
                       wa_ref, wb_ref, wo_ref, ln_g, ln_b,
                       x2_ref, vrow_ref, conv_out_ref, s_out_ref,
                       u_s, w_s, qe_s, qk_s, kdt_s, gcol_s, apart_s, zact_s, yb_s, *, alpha, nb, ts):
    half = pl.program_id(1)
    per_head = (u_s, w_s, qe_s, qk_s, kdt_s, gcol_s)
    pl.when(half == 0)(functools.partial(
        _sample_stage_one, x_ref, cs_ref, w_ref, b_ref, vg_ref, vb_ref, coef_ref, bias_ref, convw_ref,
        alog_ref, dtb_ref, wa_ref, vrow_ref, conv_out_ref, *per_head, apart_s, zact_s, nb=2 * nb, ts=ts))
    for part in range(2):
        pl.when(half == part)(functools.partial(
            _sample_state_part, part, s_in_ref, normw_ref, s_out_ref, *per_head, zact_s, yb_s, nb=nb))

    @pl.when(half == 1)
    def _():
        _, w_gates, _, _, b_gates, _ = _proj_views(w_ref, b_ref)
        x = x_ref[...]
        x2 = _merge_out_ln(x, x.astype(BF16), apart_s[...], yb_s[...], w_gates, b_gates, wb_ref, wo_ref,
                           ln_g, ln_b, alpha)
        x2_ref[...] = x2.reshape(2 * nb, ROW_TILE, D_MODEL)[:, :ts, :]


def _mix_sample(x1, cs_pad, s_in, p, alpha, nb, ts):
    n = x1.shape[0]
    nseq = n // ROW_TILE
    assert nseq % (2 * nb) == 0 and 2 * nb * ROW_TILE == SUB
    rows = 2 * nb * ROW_TILE
    consts = [p['w_proj'], p['b_proj'],
              p['gm_v_g'], p['gm_v_b'], p['mix_coef'], p['mix_bias'], p['conv_w'], p['a_log'],
              p['dt_bias'], p['norm_w'], p['w_a'], p['w_b'], p['w_o'], p['ln2_g'], p['ln2_b']]
    state_spec = pl.BlockSpec((nb, HEADS, HEAD_DIM, HEAD_DIM), lambda i, half: (2 * i + half, 0, 0, 0))
    token_spec = pl.BlockSpec((2 * nb, ts, D_MODEL), lambda i, half: (i, 0, 0))
    per_head = pltpu.VMEM((HEADS, SUB, HEAD_DIM), F32)
    return pl.pallas_call(
        functools.partial(_mix_sample_kernel, alpha=alpha, nb=nb, ts=ts),
        grid=(nseq // (2 * nb), 2),
        in_specs=[pl.BlockSpec((rows, D_MODEL), lambda i, half: (i, 0)),
                  pl.BlockSpec((rows, QKV), lambda i, half: (i, 0)),
                  state_spec] + [_const_spec(c.shape) for c in consts],
        out_specs=[token_spec, token_spec,
                   pl.BlockSpec((2 * nb, DN_CONV - 1, QKV), lambda i, half: (i, 0, 0)),
                   state_spec],
        scratch_shapes=[per_head, per_head, per_head, per_head, per_head, per_head,
                        pltpu.VMEM((rows, D_MODEL), F32),
                        pltpu.VMEM((rows, D_MODEL), F32),
                        pltpu.VMEM((rows, D_MODEL), BF16)],
        out_shape=[jax.ShapeDtypeStruct((nseq, ts, D_MODEL), F32),
                   jax.ShapeDtypeStruct((nseq, ts, D_MODEL), F32),
                   jax.ShapeDtypeStruct((nseq, DN_CONV - 1, QKV), F32),
                   jax.ShapeDtypeStruct(s_in.shape, F32)],
        compiler_params=pltpu.CompilerParams(dimension_semantics=("arbitrary", "arbitrary"),
                                             vmem_limit_bytes=VMEM_LIMIT),
        name="mix_sample",
    )(x1, cs_pad, s_in, *consts)


def _pad_lanes(a, n=LANES):
    return jnp.pad(a, [(0, 0)] * (a.ndim - 1) + [(0, n - a.shape[-1])])


def _layer_params(l, ffn1_w_up, ffn1_w_down, ln1_g, ln1_b, w_in, b_in, gm_v_g, gm_v_b, gm_w_s,
                  gm_b_s, dn_conv_w, dn_a_log, dn_dt_bias, dn_norm_w, w_branch_a, w_branch_b,
                  w_out, ln2_g, ln2_b, ffn2_w_up, ffn2_w_down, ln3_g, ln3_b):
    row = lambda a: a[l][None, :].astype(F32)
    wi, bi = w_in[l], b_in[l]
    o_beta = MAIN_COLS
    o_dec = o_beta + HEADS
    o_gate = o_dec + HEADS
    ws = gm_w_s[l]
    lsm = DN_CONV
    shift = np.arange(lsm)[:, None]
    pos = np.arange(ROW_TILE)[None, :]
    live = (pos >= shift) & (pos < lsm)
    ws_head = ws[:, :lsm, :lsm]
    coef = jnp.where(live[:, :, None],
                     jnp.transpose(ws_head[:, np.clip(pos + 0 * shift, 0, lsm - 1),
                                           np.clip(pos - shift, 0, lsm - 1)], (1, 2, 0)), 0.0)
    bias = jnp.pad(gm_b_s[l][:, :lsm].T, ((0, ROW_TILE - lsm), (0, 0)))
    return {
        'ffn1': (ffn1_w_up[l].astype(BF16), ffn1_w_down[l].astype(BF16), row(ln1_g), row(ln1_b)),
        'ffn2': (ffn2_w_up[l].astype(BF16), ffn2_w_down[l].astype(BF16), row(ln3_g), row(ln3_b)),
        'w_proj': jnp.concatenate([wi[:, :MAIN_COLS], wi[:, o_gate:], _pad_lanes(wi[:, o_beta:o_dec]),
                                   _pad_lanes(wi[:, o_dec:o_gate])], axis=1).astype(BF16),
        'b_proj': jnp.concatenate([bi[:MAIN_COLS], bi[o_gate:], _pad_lanes(bi[o_beta:o_dec]),
                                   _pad_lanes(bi[o_dec:o_gate])])[None, :],
        'gm_v_g': row(gm_v_g), 'gm_v_b': row(gm_v_b),
        'gm_w_s': ws, 'gm_b_s_t': gm_b_s[l].T,
        'mix_coef': jnp.repeat(coef, GROUP_DIM, axis=-1), 'mix_bias': jnp.repeat(bias, GROUP_DIM, axis=-1),
        'conv_w': dn_conv_w[l],
        'a_log': _pad_lanes(dn_a_log[l][None, :].astype(F32)),
        'dt_bias': _pad_lanes(dn_dt_bias[l][None, :].astype(F32)),
        'norm_w': row(dn_norm_w),
        'w_a': w_branch_a[l].astype(BF16), 'w_b': w_branch_b[l].astype(BF16),
        'w_o': w_out[l].astype(BF16),
        'ln2_g': row(ln2_g), 'ln2_b': row(ln2_b),
    }


def kernel(x_prompt, x_sample, state_conv, state_ssm, ffn1_w_up, ffn1_w_down, ln1_g, ln1_b, w_in, b_in, gm_v_g, gm_v_b, gm_w_s, gm_b_s, dn_conv_w, dn_a_log, dn_dt_bias, dn_norm_w, w_branch_a, w_branch_b, w_out, ln2_g, ln2_b, ffn2_w_up, ffn2_w_down, ln3_g, ln3_b):
    depth = ffn1_w_up.shape[0]
    alpha = (2.0 * depth) ** 0.25
    bp, tp, _ = x_prompt.shape
    bs, ts, _ = x_sample.shape
    assert ts == DN_CONV and ts + (DN_CONV - 1) <= ROW_TILE
    y_p, y_s = x_prompt, x_sample
    conv_p, ssm_p, conv_s, ssm_s, v_s = [], [], [], [], []
    for l in range(depth):
        p = _layer_params(l, ffn1_w_up, ffn1_w_down, ln1_g, ln1_b, w_in, b_in, gm_v_g, gm_v_b,
                          gm_w_s, gm_b_s, dn_conv_w, dn_a_log, dn_dt_bias, dn_norm_w, w_branch_a,
                          w_branch_b, w_out, ln2_g, ln2_b, ffn2_w_up, ffn2_w_down, ln3_g, ln3_b)
        x1, x1s = _ffn_ln(y_p.reshape(bp * tp, D_MODEL), y_s.reshape(bs * ts, D_MODEL), *p['ffn1'],
                          alpha, FFN_ROWS)
        x2, c_p, s_p = _mix_prompt(x1.reshape(bp, tp, D_MODEL), p, alpha, PROMPT_ROWS, PROMPT_SEQS)
        x1s = jnp.pad(x1s.reshape(bs, ts, D_MODEL), ((0, 0), (0, ROW_TILE - ts), (0, 0)))
        cs_pad = jnp.pad(state_conv[l], ((0, 0), (ROW_TILE - (DN_CONV - 1), 0), (0, 0)))
        x2s, vrows, c_s, s_s = _mix_sample(x1s.reshape(bs * ROW_TILE, D_MODEL),
                                           cs_pad.reshape(bs * ROW_TILE, QKV),
                                           state_ssm[l], p, alpha, SAMPLE_SEQS, ts)
        y_p, y_s = _ffn_ln(x2.reshape(bp * tp, D_MODEL), x2s.reshape(bs * ts, D_MODEL), *p['ffn2'],
                           alpha, FFN_ROWS)
        y_p, y_s = y_p.reshape(bp, tp, D_MODEL), y_s.reshape(bs, ts, D_MODEL)
        conv_p.append(c_p)
        ssm_p.append(s_p)
        conv_s.append(c_s)
        ssm_s.append(s_s)
        v_s.append(vrows)
    return (y_p, y_s, jnp.stack(conv_p), jnp.stack(ssm_p), jnp.stack(conv_s), jnp.stack(ssm_s),
            jnp.stack(v_s))
```

```python
import functools
import math

import jax
import jax.numpy as jnp
import numpy as np
from jax import lax
from jax.experimental import pallas as pl
from jax.experimental.pallas import tpu as pltpu

F32 = jnp.float32
BF16 = jnp.bfloat16

D_MODEL = 1024
D_FF = 2816
HEADS = 8
HEAD_DIM = 128
GROUPS = 8
GROUP_DIM = 128
GM_CHUNK = 128
DN_CHUNK = 64
DN_CONV = 4
QKV = 3 * D_MODEL
MAIN_COLS = 6 * D_MODEL
LN_EPS = 1e-5
RMS_EPS = 1e-6

MXU_DIM = 256
SUB = 128
ROW_TILE = 8
LANES = 128
SLAB = 16
PIECE_BUDGET = 200
MXU_PIECE_COST = 256
TICK_BUDGET = 1200
VMEM_LIMIT = 56 * 1024 * 1024
FFN_ROWS = 512
PROMPT_ROWS, PROMPT_SEQS = 128, 2
SAMPLE_SEQS = 8


def _sigmoid(x):
    return 0.5 * jnp.tanh(0.5 * x) + 0.5


def _silu(x):
    h = 0.5 * x
    return h + h * jnp.tanh(h)


def _gelu_tanh(x):
    c = math.sqrt(2.0 / math.pi)
    h = 0.5 * x
    return h + h * jnp.tanh(x * (c + (c * 0.044715) * (x * x)))


def _softplus(x):
    return jnp.maximum(x, 0.0) + jnp.log(1.0 + jnp.exp(-jnp.abs(x)))


def _layer_norm(y, g, b):
    mu = jnp.mean(y, axis=-1, keepdims=True)
    yc = y - mu
    var = jnp.mean(yc * yc, axis=-1, keepdims=True)
    return yc * lax.rsqrt(var + LN_EPS) * g + b


def _dot(a, b):
    return jnp.dot(a.astype(BF16), b.astype(BF16), preferred_element_type=F32)


def _dot_nt(a, b):
    return lax.dot_general(a.astype(BF16), b.astype(BF16), (((1,), (1,)), ((), ())),
                           preferred_element_type=F32)


def _dot_exact_lhs(m01, x):
    hi = x.astype(BF16)
    r1 = x - hi.astype(F32)
    mid = r1.astype(BF16)
    lo = (r1 - mid.astype(F32)).astype(BF16)
    m = m01.astype(BF16)
    return (jnp.dot(m, hi, preferred_element_type=F32)
            + jnp.dot(m, mid, preferred_element_type=F32)
            + jnp.dot(m, lo, preferred_element_type=F32))


def _block_masks(n, blk):
    row = lax.broadcasted_iota(jnp.int32, (n, n), 0)
    col = lax.broadcasted_iota(jnp.int32, (n, n), 1)
    same = (row // blk) == (col // blk)
    return same & (row >= col), same & (row > col), row == col


def _no_tick():
    pass


def _inv_unit_lower(a, eye, n_iter, tick=_no_tick):
    n = eye.shape[0]
    b = [-x for x in a]
    p = [eye + x for x in b]
    b = [_dot(x, x) for x in b]
    tick()
    for _ in range(n_iter - 1):
        pb = [_dot(jnp.concatenate([pi, bi], axis=0), bi) for pi, bi in zip(p, b)]
        tick()
        p = [pi + x[:n] for pi, x in zip(p, pb)]
        b = [x[n:] for x in pb]
    return [pi + _dot(pi, bi) for pi, bi in zip(p, b)]


def _dn_intra(q, k, v, beta, g_col, g_row, masks, n_iter, tick=_no_tick):
    causal, strict, diag = masks
    heads = range(len(q))
    decay = [jnp.where(causal, jnp.exp(jnp.where(causal, g_col[h] - g_row[h], 0.0)), 0.0) for h in heads]
    kb = [k[h] * beta[h] for h in heads]
    kq = [_dot_nt(jnp.concatenate([kb[h], q[h]], axis=0), k[h]) for h in heads]
    tick()
    a = [jnp.where(strict, kq[h][:SUB] * decay[h], 0.0) for h in heads]
    qk = [kq[h][SUB:] * decay[h] for h in heads]
    eye = jnp.where(diag, 1.0, 0.0).astype(F32)
    t_inv = _inv_unit_lower(a, eye, n_iter, tick)
    tick()
    e_g = [jnp.exp(g_col[h]) for h in heads]
    uw = [_dot(t_inv[h], jnp.concatenate([v[h] * beta[h], kb[h] * e_g[h]], axis=1)) for h in heads]
    tick()
    return ([x[:, :HEAD_DIM] for x in uw], [x[:, HEAD_DIM:] for x in uw], qk,
            [q[h] * e_g[h] for h in heads])


class _Slabs:
    def __init__(self, fn, *xs, slab):
        self._outs = []
        self.thunks = [functools.partial(self._run, fn, xs, r, slab)
                       for r in range(0, xs[0].shape[0], slab)]

    def _run(self, fn, xs, r, slab):
        self._outs.append(fn(*[x[r:r + slab] for x in xs]))

    def result(self):
        assert len(self._outs) == len(self.thunks)
        return jnp.concatenate(self._outs, axis=0)


def _by_rows(fn, *xs, slab):
    job = _Slabs(fn, *xs, slab=slab)
    for th in job.thunks:
        th()
    return job.result()


class _WorkQueue:
    def __init__(self):
        self._items = []

    def add(self, job, cost):
        self._items += [(cost, th, job) for th in job.thunks]

    def run(self, budget):
        while self._items and budget > 0:
            cost, th, _ = self._items.pop(0)
            th()
            budget -= cost

    def finish(self, job):
        while any(j is job for _, _, j in self._items):
            self._items.pop(0)[1]()


def _piped_dot(lhs, w, lo, hi, queue, budget=PIECE_BUDGET):
    outs = []
    for c in range(lo, hi, MXU_DIM):
        outs.append(jnp.dot(lhs, w[:, c:c + MXU_DIM], preferred_element_type=F32))
        queue.run(budget)
    return jnp.concatenate(outs, axis=1)


def _conv_job(x, b, tail, w, post):
    c = x.shape[1]
    taps = w.shape[0]
    sub = lax.broadcasted_iota(jnp.int32, (ROW_TILE, c), 0)
    b_rows = jnp.broadcast_to(b, (ROW_TILE, c))
    wj = [jnp.broadcast_to(w[taps - 1 - j:taps - j, :], (ROW_TILE, c)) for j in range(taps)]
    state = {'prev': [pltpu.roll(tail, j, 0) for j in range(1, taps)], 'cur': None}

    def tile(raw):
        cur = raw + b_rows
        rolled = [pltpu.roll(cur, 1, 0)]
        for _ in range(2, taps):
            rolled.append(pltpu.roll(rolled[-1], 1, 0))
        acc = cur * wj[0]
        for j in range(1, taps):
            acc = acc + jnp.where(sub < j, state['prev'][j - 1], rolled[j - 1]) * wj[j]
        state['prev'], state['cur'] = rolled, cur
        return post(acc)

    return _Slabs(tile, x, slab=ROW_TILE), state


def _gated_rms(o, norm_w, z_act):
    return o * lax.rsqrt(jnp.mean(o * o, axis=-1, keepdims=True) + RMS_EPS) * norm_w * z_act


def _l2n(x, mul=1.0):
    inv = lax.rsqrt(jnp.sum(x * x, axis=-1, keepdims=True) + RMS_EPS)
    return x * (inv if mul == 1.0 else inv * mul)


def _ffn_chunks():
    n_tiles = D_FF // MXU_DIM
    first = (n_tiles + 1) // 2 * MXU_DIM
    return ((0, first), (first, D_FF))


def _ffn_ln_tile(x_ref, wu_ref, wd_ref, g_ref, b_ref, o_ref, alpha):
    x = x_ref[...]
    xb = x.astype(BF16)
    acc = None
    for lo, hi in _ffn_chunks():
        a = jnp.dot(xb, wu_ref[:, lo:hi], preferred_element_type=F32)
        gt = jnp.dot(xb, wu_ref[:, D_FF + lo:D_FF + hi], preferred_element_type=F32)
        h = (_silu(a) * gt).astype(BF16)
        f = jnp.dot(h, wd_ref[lo:hi, :], preferred_element_type=F32)
        acc = f if acc is None else acc + f
    o_ref[...] = _layer_norm(alpha * x + 0.5 * acc, g_ref[...], b_ref[...])


def _ffn_ln_kernel(xp_ref, xs_ref, wu_ref, wd_ref, g_ref, b_ref, op_ref, os_ref, *, alpha, n_prompt):
    i = pl.program_id(0)
    pl.when(i < n_prompt)(functools.partial(_ffn_ln_tile, xp_ref, wu_ref, wd_ref, g_ref, b_ref, op_ref, alpha))
    pl.when(i >= n_prompt)(functools.partial(_ffn_ln_tile, xs_ref, wu_ref, wd_ref, g_ref, b_ref, os_ref, alpha))


def _const_spec(shape):
    nd = len(shape)
    return pl.BlockSpec(shape, lambda *_: (0,) * nd, pipeline_mode=pl.Buffered(1))


def _ffn_ln(xp, xs, wu, wd, g, b, alpha, tm):
    n_p, n_s = xp.shape[0] // tm, xs.shape[0] // tm
    assert xp.shape[0] % tm == 0 and xs.shape[0] % tm == 0 and D_FF % MXU_DIM == 0
    prompt_spec = pl.BlockSpec((tm, D_MODEL), lambda i: (jnp.minimum(i, n_p - 1), 0))
    sample_spec = pl.BlockSpec((tm, D_MODEL), lambda i: (jnp.maximum(i - n_p, 0), 0))
    return pl.pallas_call(
        functools.partial(_ffn_ln_kernel, alpha=alpha, n_prompt=n_p),
        grid=(n_p + n_s,),
        in_specs=[prompt_spec, sample_spec,
                  _const_spec(wu.shape), _const_spec(wd.shape),
                  _const_spec(g.shape), _const_spec(b.shape)],
        out_specs=[prompt_spec, sample_spec],
        out_shape=[jax.ShapeDtypeStruct(xp.shape, F32), jax.ShapeDtypeStruct(xs.shape, F32)],
        compiler_params=pltpu.CompilerParams(dimension_semantics=("arbitrary",),
                                             vmem_limit_bytes=VMEM_LIMIT),
        name="ffn_ln",
    )(xp, xs, wu, wd, g, b)


def _proj_views(w_ref, b_ref):
    g0, g1 = MAIN_COLS, MAIN_COLS + 2 * D_MODEL
    return (w_ref.at[:, 0:g0], w_ref.at[:, g0:g1], w_ref.at[:, g1:g1 + 2 * LANES],
            b_ref.at[:, 0:g0], b_ref.at[:, g0:g1], b_ref.at[:, g1:g1 + 2 * LANES])


def _branch_gates_and_z(hb, w_main, b_main):
    z = jnp.dot(hb, w_main[:, 5 * D_MODEL:6 * D_MODEL], preferred_element_type=F32) \
        + b_main[:, 5 * D_MODEL:6 * D_MODEL]
    return _silu(z)


def _beta_and_logdecay(hb, w_bd, b_bd, alog, dtb):
    bd = jnp.dot(hb, w_bd[...], preferred_element_type=F32) + b_bd[...]
    beta = _sigmoid(bd[:, :LANES])
    g = -jnp.exp(alog[...]) * _softplus(bd[:, LANES:] + dtb[...])
    return beta, g


def _merge_out_ln(x, hb, a_part, yb, w_gates, b_gates, wb_ref, wo_ref, ln_g, ln_b, alpha):
    gate_b = _sigmoid(jnp.dot(hb, w_gates[:, D_MODEL:], preferred_element_type=F32)
                      + b_gates[:, D_MODEL:])
    merged = a_part + gate_b * jnp.dot(yb, wb_ref[...], preferred_element_type=F32)
    mix = jnp.dot(merged.astype(BF16), wo_ref[...], preferred_element_type=F32)
    return _layer_norm(alpha * x + mix, ln_g[...], ln_b[...])


def _mix_prompt_kernel(x_ref, w_ref, b_ref, vg_ref, vb_ref,
                       ws_ref, bst_ref, convw_ref, alog_ref, dtb_ref, normw_ref,
                       wa_ref, wb_ref, wo_ref, ln_g, ln_b,
                       x2_ref, conv_out_ref, ssm_out_ref,
                       s_ref, xc_ref, q_s, k_s, v_s, z_s, g_s, beta_s, yb_s, *, alpha, tt, nseq):
    t = pl.program_id(1)
    nt = pl.num_programs(1)
    w_main, w_gates, w_bd, b_main, b_gates, b_bd = _proj_views(w_ref, b_ref)

    @pl.when(t == 0)
    def _():
        s_ref[...] = jnp.zeros(s_ref.shape, F32)
        xc_ref[...] = jnp.zeros(xc_ref.shape, F32)

    n_rows = nseq * tt
    x = x_ref[...].reshape(n_rows, D_MODEL)
    hb = x.astype(BF16)

    queue = _WorkQueue()

    def proj(part):
        return _piped_dot(hb, w_main, part * D_MODEL, (part + 1) * D_MODEL, queue)

    scale = HEAD_DIM ** -0.5
    heads = [slice(h * HEAD_DIM, (h + 1) * HEAD_DIM) for h in range(HEADS)]

    class _ConvJobs:
        def __init__(self, raw, part, post):
            self.cols = slice(part * D_MODEL, (part + 1) * D_MODEL)
            b_p = b_main[:, (2 + part) * D_MODEL:(3 + part) * D_MODEL]
            self.jobs = [_conv_job(raw[sq * tt:(sq + 1) * tt], b_p, xc_ref[sq, :, self.cols],
                                   convw_ref[:, self.cols], lambda acc: post(_silu(acc)))
                         for sq in range(nseq)]
            self.thunks = [th for job, _ in self.jobs for th in job.thunks]

        def result(self):
            for sq, (_, state) in enumerate(self.jobs):
                last = state['cur']

                @pl.when(t == nt - 1)
                def _():
                    conv_out_ref[sq, :, self.cols] = last[ROW_TILE - (DN_CONV - 1):, :]

                xc_ref[sq, :, self.cols] = last
            return jnp.concatenate([job.result() for job, _ in self.jobs], axis=0)

    def l2n_heads(a, mul):
        return jnp.concatenate([_l2n(a[:, sl], mul) for sl in heads], axis=1)

    b_z = b_main[:, 5 * D_MODEL:6 * D_MODEL]
    beta, g_log = _beta_and_logdecay(hb, w_bd, b_bd, alog_ref, dtb_ref)
    beta_s[...] = beta
    rt = lax.broadcasted_iota(jnp.int32, (n_rows, n_rows), 0)
    ct = lax.broadcasted_iota(jnp.int32, (n_rows, n_rows), 1)
    cum = jnp.where(((rt // DN_CHUNK) == (ct // DN_CHUNK)) & (rt >= ct), 1.0, 0.0)
    g_s[...] = _dot_exact_lhs(cum, g_log)
    pq = proj(2)
    job_q = _ConvJobs(pq, 0, lambda a: l2n_heads(a, scale))
    queue.add(job_q, 45)
    pk = proj(3)
    job_k = _ConvJobs(pk, 1, lambda a: l2n_heads(a, 1.0))
    queue.add(job_k, 45)
    pvv = proj(4)
    job_vv = _ConvJobs(pvv, 2, lambda a: a)
    queue.add(job_vv, 35)
    pz = proj(5)
    job_z = _Slabs(lambda a: _silu(a + b_z), pz, slab=SLAB)
    queue.add(job_z, 20)
    pgb = _piped_dot(hb, w_gates, D_MODEL, 2 * D_MODEL, queue)
    queue.finish(job_z)
    q_s[...] = job_q.result()
    k_s[...] = job_k.result()
    v_s[...] = job_vv.result()
    z_s[...] = job_z.result()

    branch_a = {}

    def dot_steps(lhs, w, lo, hi, pieces=4):
        step = (hi - lo) // pieces
        outs = []
        for p in range(pieces):
            outs.append(jnp.dot(lhs, w[:, lo + p * step:lo + (p + 1) * step], preferred_element_type=F32))
            yield MXU_PIECE_COST
        return jnp.concatenate(outs, axis=1)

    def job_steps(job, cost):
        for th in job.thunks:
            th()
            yield cost
        return job.result()

    def branch_a_steps():
        b_u = b_main[:, 0:D_MODEL]
        b_v, vg, vb = b_main[:, D_MODEL:2 * D_MODEL], vg_ref[...], vb_ref[...]
        b_ga = b_gates[:, :D_MODEL]
        pu = yield from dot_steps(hb, w_main, 0, D_MODEL)
        pv = yield from dot_steps(hb, w_main, D_MODEL, 2 * D_MODEL)
        vn = yield from job_steps(
            _Slabs(lambda a: _layer_norm(_gelu_tanh(a + b_v), vg, vb).astype(BF16), pv, slab=SLAB), 70)
        r128 = lax.broadcasted_iota(jnp.int32, (GM_CHUNK, GM_CHUNK), 0)
        c128 = lax.broadcasted_iota(jnp.int32, (GM_CHUNK, GM_CHUNK), 1)
        tril = r128 >= c128
        w_tril = [jnp.where(tril, ws_ref[g], 0.0).astype(BF16) for g in range(GROUPS)]
        rows = []
        for c in range(n_rows // GM_CHUNK):
            cols = []
            for g in range(GROUPS):
                blk = vn[c * GM_CHUNK:(c + 1) * GM_CHUNK, g * GROUP_DIM:(g + 1) * GROUP_DIM]
                cols.append(jnp.dot(w_tril[g], blk, preferred_element_type=F32) + bst_ref[:, g:g + 1])
            rows.append(jnp.concatenate(cols, axis=1))
            yield MXU_PIECE_COST
        mixed = jnp.concatenate(rows, axis=0)
        u = yield from job_steps(_Slabs(lambda a: _gelu_tanh(a + b_u), pu, slab=SLAB), 50)
        ya = yield from job_steps(_Slabs(lambda a, m: (a * m).astype(BF16), u, mixed, slab=SLAB), 10)
        pa = yield from dot_steps(ya, wa_ref, 0, D_MODEL)
        pga = yield from dot_steps(hb, w_gates, 0, D_MODEL)
        branch_a['a_part'] = yield from job_steps(
            _Slabs(lambda g, p_: _sigmoid(g + b_ga) * p_, pga, pa, slab=SLAB), 25)

    steps = branch_a_steps()

    def tick(budget=TICK_BUDGET):
        while budget > 0:
            cost = next(steps, None)
            if cost is None:
                return
            budget -= cost

    masks = _block_masks(SUB, DN_CHUNK)
    norm_w = normw_ref[...]

    chains = [(sq, h) for sq in range(nseq) for h in range(HEADS)]
    sls = [slice(h * HEAD_DIM, (h + 1) * HEAD_DIM) for _, h in chains]
    n_chunks = SUB // DN_CHUNK
    row_chunk = lax.broadcasted_iota(jnp.int32, (SUB, HEAD_DIM), 0) // DN_CHUNK
    zeros = jnp.zeros((DN_CHUNK, HEAD_DIM), F32)
    ids = range(len(chains))
    state = [s_ref[sq, h] for sq, h in chains]
    for j in range(tt // SUB):
        rows = [slice(sq * tt + j * SUB, sq * tt + (j + 1) * SUB) for sq, _ in chains]
        g_sub = [g_s[sq * tt + j * SUB:sq * tt + (j + 1) * SUB, :] for sq in range(nseq)]
        g_t = [g.T for g in g_sub]
        b_sub = [beta_s[sq * tt + j * SUB:sq * tt + (j + 1) * SUB, :] for sq in range(nseq)]
        q = [q_s[rows[i], sls[i]] for i in ids]
        k = [k_s[rows[i], sls[i]] for i in ids]
        vv = [v_s[rows[i], sls[i]] for i in ids]
        g_col = [jnp.broadcast_to(g_sub[sq][:, h:h + 1], (SUB, HEAD_DIM)) for sq, h in chains]
        g_row = [jnp.broadcast_to(g_t[sq][h:h + 1, :], (SUB, SUB)) for sq, h in chains]
        beta_h = [jnp.broadcast_to(b_sub[sq][:, h:h + 1], (SUB, HEAD_DIM)) for sq, h in chains]
        u_h, w_h, qk, qe = _dn_intra(q, k, vv, beta_h, g_col, g_row, masks, 5, tick)
        g_last = [[g_col[i][(c + 1) * DN_CHUNK - 1:(c + 1) * DN_CHUNK, :] for c in range(n_chunks)]
                  for i in ids]
        k_dec_t = []
        for i in ids:
            g_end = g_last[i][n_chunks - 1]
            for c in range(n_chunks - 2, -1, -1):
                g_end = jnp.where(row_chunk == c, g_last[i][c], g_end)
            k_dec_t.append((k[i] * jnp.exp(g_end - g_col[i])).T)
        outs = [[] for _ in ids]
        for c in range(n_chunks):
            rs = slice(c * DN_CHUNK, (c + 1) * DN_CHUNK)
            r = [_dot(jnp.concatenate([w_h[i][rs], qe[i][rs]], axis=0), state[i]) for i in ids]
            tick()
            v_new = [u_h[i][rs] - r[i][:DN_CHUNK] for i in ids]
            v_pad = [jnp.concatenate([zeros] * c + [v_new[i]] + [zeros] * (n_chunks - 1 - c), axis=0)
                     for i in ids]
            m = [_dot(jnp.concatenate([qk[i][rs], k_dec_t[i]], axis=0), v_pad[i]) for i in ids]
            tick()
            for i in ids:
                outs[i].append(r[i][DN_CHUNK:] + m[i][:DN_CHUNK])
            state = [state[i] * jnp.exp(g_last[i][c]) + m[i][DN_CHUNK:] for i in ids]
        for i in ids:
            o = jnp.concatenate(outs[i], axis=0)
            yb_s[rows[i], sls[i]] = _gated_rms(o, norm_w, z_s[rows[i], sls[i]]).astype(BF16)
    for i, (sq, h) in enumerate(chains):
        s_ref[sq, h] = state[i]
    tick(float('inf'))
    a_part = branch_a['a_part']

    @pl.when(t == nt - 1)
    def _():
        ssm_out_ref[...] = s_ref[...]

    pb = jnp.dot(yb_s[...], wb_ref[...], preferred_element_type=F32)
    b_gb = b_gates[:, D_MODEL:]
    merged = _by_rows(lambda a, g, p_: (a + _sigmoid(g + b_gb) * p_).astype(BF16), a_part, pgb, pb, slab=SLAB)
    mix = jnp.dot(merged, wo_ref[...], preferred_element_type=F32)
    ln_gain, ln_bias = ln_g[...], ln_b[...]
    x2 = _by_rows(lambda xx, m: _layer_norm(alpha * xx + m, ln_gain, ln_bias), x, mix, slab=SLAB)
    x2_ref[...] = x2.reshape(nseq, tt, D_MODEL)


def _mix_prompt(x1, p, alpha, tt, nseq):
    b, t, _ = x1.shape
    assert t % tt == 0 and tt % SUB == 0 and b % nseq == 0
    rows = nseq * tt
    consts = [p['w_proj'], p['b_proj'],
              p['gm_v_g'], p['gm_v_b'], p['gm_w_s'], p['gm_b_s_t'], p['conv_w'], p['a_log'],
              p['dt_bias'], p['norm_w'], p['w_a'], p['w_b'], p['w_o'], p['ln2_g'], p['ln2_b']]
    return pl.pallas_call(
        functools.partial(_mix_prompt_kernel, alpha=alpha, tt=tt, nseq=nseq),
        grid=(b // nseq, t // tt),
        in_specs=[pl.BlockSpec((nseq, tt, D_MODEL), lambda i, j: (i, j, 0))]
                 + [_const_spec(c.shape) for c in consts],
        out_specs=[pl.BlockSpec((nseq, tt, D_MODEL), lambda i, j: (i, j, 0)),
                   pl.BlockSpec((nseq, DN_CONV - 1, QKV), lambda i, j: (i, 0, 0)),
                   pl.BlockSpec((nseq, HEADS, HEAD_DIM, HEAD_DIM), lambda i, j: (i, 0, 0, 0))],
        out_shape=[jax.ShapeDtypeStruct((b, t, D_MODEL), F32),
                   jax.ShapeDtypeStruct((b, DN_CONV - 1, QKV), F32),
                   jax.ShapeDtypeStruct((b, HEADS, HEAD_DIM, HEAD_DIM), F32)],
        scratch_shapes=[pltpu.VMEM((nseq, HEADS, HEAD_DIM, HEAD_DIM), F32),
                        pltpu.VMEM((nseq, ROW_TILE, QKV), F32),
                        pltpu.VMEM((rows, D_MODEL), F32),
                        pltpu.VMEM((rows, D_MODEL), F32),
                        pltpu.VMEM((rows, D_MODEL), F32),
                        pltpu.VMEM((rows, D_MODEL), F32),
                        pltpu.VMEM((rows, LANES), F32),
                        pltpu.VMEM((rows, LANES), F32),
                        pltpu.VMEM((rows, D_MODEL), BF16)],
        compiler_params=pltpu.CompilerParams(dimension_semantics=("arbitrary", "arbitrary"),
                                             vmem_limit_bytes=VMEM_LIMIT),
        name="mix_prompt",
    )(x1, *consts)


def _sample_stage_one(x_ref, cs_ref, w_ref, b_ref, vg_ref, vb_ref, coef_ref, bias_ref, convw_ref,
                      alog_ref, dtb_ref, wa_ref, vrow_ref, conv_out_ref,
                      u_s, w_s, qe_s, qk_s, kdt_s, gcol_s, apart_s, zact_s, *, nb, ts):
    rows = nb * ROW_TILE
    w_main, w_gates, w_bd, b_main, b_gates, b_bd = _proj_views(w_ref, b_ref)
    x = x_ref[...]
    hb = x.astype(BF16)
    valid = (lax.broadcasted_iota(jnp.int32, (rows, 1), 0) % ROW_TILE) < ts
    validf = jnp.where(valid, 1.0, 0.0).astype(F32)

    u = _gelu_tanh(jnp.dot(hb, w_main[:, 0:D_MODEL], preferred_element_type=F32)
                   + b_main[:, 0:D_MODEL])
    v = _gelu_tanh(jnp.dot(hb, w_main[:, D_MODEL:2 * D_MODEL], preferred_element_type=F32)
                   + b_main[:, D_MODEL:2 * D_MODEL])
    vn = _layer_norm(v, vg_ref[...], vb_ref[...])
    vn3 = vn.reshape(nb, ROW_TILE, D_MODEL)
    vrow_ref[...] = vn3[:, :ts, :]
    mixed = vn3 * coef_ref[0][None] + bias_ref[...][None]
    for j in range(1, DN_CONV):
        mixed = mixed + pltpu.roll(vn3, j, 1) * coef_ref[j][None]
    ya = (u * mixed.reshape(rows, D_MODEL)).astype(BF16)
    gate_a = _sigmoid(jnp.dot(hb, w_gates[:, :D_MODEL], preferred_element_type=F32)
                      + b_gates[:, :D_MODEL])
    apart_s[...] = gate_a * jnp.dot(ya, wa_ref[...], preferred_element_type=F32)

    qkv = jnp.dot(hb, w_main[:, 2 * D_MODEL:5 * D_MODEL], preferred_element_type=F32) \
        + b_main[:, 2 * D_MODEL:5 * D_MODEL]
    zfull = jnp.where(valid, qkv, 0.0) + cs_ref[...]
    z3 = zfull.reshape(nb, ROW_TILE, QKV)
    conv_out_ref[...] = z3[:, ts - (DN_CONV - 1):ts, :]
    acc = z3 * convw_ref[DN_CONV - 1:DN_CONV, :][None]
    for j in range(1, DN_CONV):
        acc = acc + pltpu.roll(z3, j, 1) * convw_ref[DN_CONV - 1 - j:DN_CONV - j, :][None]
    sact = _silu(acc.reshape(rows, QKV)) * validf
    zact_s[...] = _branch_gates_and_z(hb, w_main, b_main)
    beta, g_log = _beta_and_logdecay(hb, w_bd, b_bd, alog_ref, dtb_ref)
    beta = beta * validf
    g_log = g_log * validf
    rt = lax.broadcasted_iota(jnp.int32, (rows, rows), 0)
    ct = lax.broadcasted_iota(jnp.int32, (rows, rows), 1)
    cum = jnp.where(((rt // ROW_TILE) == (ct // ROW_TILE)) & (rt >= ct), 1.0, 0.0)
    g_cum = _dot_exact_lhs(cum, g_log)

    assert rows == SUB
    g_t = g_cum.T
    masks = _block_masks(SUB, ROW_TILE)
    scale = HEAD_DIM ** -0.5
    heads = range(HEADS)
    sls = [slice(h * HEAD_DIM, (h + 1) * HEAD_DIM) for h in heads]
    q = [_l2n(sact[:, sl]) * scale * validf for sl in sls]
    k = [_l2n(sact[:, D_MODEL + h * HEAD_DIM:D_MODEL + (h + 1) * HEAD_DIM]) * validf for h in heads]
    vv = [sact[:, 2 * D_MODEL + h * HEAD_DIM:2 * D_MODEL + (h + 1) * HEAD_DIM] for h in heads]
    g_col = [jnp.broadcast_to(g_cum[:, h:h + 1], (SUB, HEAD_DIM)) for h in heads]
    g_row = [jnp.broadcast_to(g_t[h:h + 1, :], (SUB, SUB)) for h in heads]
    beta_h = [jnp.broadcast_to(beta[:, h:h + 1], (SUB, HEAD_DIM)) for h in heads]
    u_h, w_h, qk, qe = _dn_intra(q, k, vv, beta_h, g_col, g_row, masks, 2)
    for h in heads:
        g_end = jnp.broadcast_to(g_col[h].reshape(nb, ROW_TILE, HEAD_DIM)[:, ROW_TILE - 1:, :],
                                 (nb, ROW_TILE, HEAD_DIM)).reshape(SUB, HEAD_DIM)
        kdt_s[h] = (k[h] * jnp.exp(g_end - g_col[h])).T
        u_s[h], w_s[h], qe_s[h], qk_s[h], gcol_s[h] = u_h[h], w_h[h], qe[h], qk[h], g_col[h]


def _sample_state_part(part, s_in_ref, normw_ref, s_out_ref, u_s, w_s, qe_s, qk_s, kdt_s, gcol_s,
                       zact_s, yb_s, *, nb):
    span = nb * ROW_TILE
    base = part * span
    heads = range(HEADS)
    tiles = [slice(base + i * ROW_TILE, base + (i + 1) * ROW_TILE) for i in range(nb)]
    zeros = jnp.zeros((SUB - span, HEAD_DIM), F32)
    seq_of_row = lax.broadcasted_iota(jnp.int32, (SUB, HEAD_DIM), 0) // ROW_TILE
    norm_w = normw_ref[...]
    r = [[_dot(jnp.concatenate([w_s[h, rs, :], qe_s[h, rs, :]], axis=0), s_in_ref[i, h])
          for i, rs in enumerate(tiles)] for h in heads]
    v_new = []
    for h in heads:
        mine = jnp.concatenate([u_s[h, rs, :] - r[h][i][:ROW_TILE] for i, rs in enumerate(tiles)], axis=0)
        v_new.append(jnp.concatenate([mine, zeros] if part == 0 else [zeros, mine], axis=0))
    qkv_new = [_dot(qk_s[h, base:base + span, :], v_new[h]) for h in heads]
    for i, rs in enumerate(tiles):
        for h in heads:
            g_last = gcol_s[h, rs.stop - 1:rs.stop, :]
            s_out_ref[i, h] = (s_in_ref[i, h] * jnp.exp(g_last)
                               + _dot(kdt_s[h], jnp.where(seq_of_row == part * nb + i, v_new[h], 0.0)))
    for h in heads:
        sl = slice(h * HEAD_DIM, (h + 1) * HEAD_DIM)
        o = jnp.concatenate([r[h][i][ROW_TILE:] for i in range(nb)], axis=0) + qkv_new[h]
        yb_s[base:base + span, sl] = _gated_rms(o, norm_w, zact_s[base:base + span, sl]).astype(BF16)


def _mix_sample_kernel(x_ref, cs_ref, s_in_ref, w_ref, b_ref,
                       vg_ref, vb_ref, coef_ref, bias_ref, convw_ref, alog_ref, dtb_ref, normw_ref,
                       wa_ref, wb_ref, wo_ref, ln_g, ln_b,
                       x2_ref, vrow_ref, conv_out_ref, s_out_ref,
                       u_s, w_s, qe_s, qk_s, kdt_s, gcol_s, apart_s, zact_s, yb_s, *, alpha, nb, ts):
    half = pl.program_id(1)
    per_head = (u_s, w_s, qe_s, qk_s, kdt_s, gcol_s)
    pl.when(half == 0)(functools.partial(
        _sample_stage_one, x_ref, cs_ref, w_ref, b_ref, vg_ref, vb_ref, coef_ref, bias_ref, convw_ref,
        alog_ref, dtb_ref, wa_ref, vrow_ref, conv_out_ref, *per_head, apart_s, zact_s, nb=2 * nb, ts=ts))
    for part in range(2):
        pl.when(half == part)(functools.partial(
            _sample_state_part, part, s_in_ref, normw_ref, s_out_ref, *per_head, zact_s, yb_s, nb=nb))

    @pl.when(half == 1)
    def _():
        _, w_gates, _, _, b_gates, _ = _proj_views(w_ref, b_ref)
        x = x_ref[...]
        x2 = _merge_out_ln(x, x.astype(BF16), apart_s[...], yb_s[...], w_gates, b_gates, wb_ref, wo_ref,
                           ln_g, ln_b, alpha)
        x2_ref[...] = x2.reshape(2 * nb, ROW_TILE, D_MODEL)[:, :ts, :]


def _mix_sample(x1, cs_pad, s_in, p, alpha, nb, ts):
    n = x1.shape[0]
    nseq = n // ROW_TILE
    assert nseq % (2 * nb) == 0 and 2 * nb * ROW_TILE == SUB
    rows = 2 * nb * ROW_TILE
    consts = [p['w_proj'], p['b_proj'],
              p['gm_v_g'], p['gm_v_b'], p['mix_coef'], p['mix_bias'], p['conv_w'], p['a_log'],
              p['dt_bias'], p['norm_w'], p['w_a'], p['w_b'], p['w_o'], p['ln2_g'], p['ln2_b']]
    state_spec = pl.BlockSpec((nb, HEADS, HEAD_DIM, HEAD_DIM), lambda i, half: (2 * i + half, 0, 0, 0))
    token_spec = pl.BlockSpec((2 * nb, ts, D_MODEL), lambda i, half: (i, 0, 0))
    per_head = pltpu.VMEM((HEADS, SUB, HEAD_DIM), F32)
    return pl.pallas_call(
        functools.partial(_mix_sample_kernel, alpha=alpha, nb=nb, ts=ts),
        grid=(nseq // (2 * nb), 2),
        in_specs=[pl.BlockSpec((rows, D_MODEL), lambda i, half: (i, 0)),
                  pl.BlockSpec((rows, QKV), lambda i, half: (i, 0)),
                  state_spec] + [_const_spec(c.shape) for c in consts],
        out_specs=[token_spec, token_spec,
                   pl.BlockSpec((2 * nb, DN_CONV - 1, QKV), lambda i, half: (i, 0, 0)),
                   state_spec],
        scratch_shapes=[per_head, per_head, per_head, per_head, per_head, per_head,
                        pltpu.VMEM((rows, D_MODEL), F32),
                        pltpu.VMEM((rows, D_MODEL), F32),
                        pltpu.VMEM((rows, D_MODEL), BF16)],
        out_shape=[jax.ShapeDtypeStruct((nseq, ts, D_MODEL), F32),
                   jax.ShapeDtypeStruct((nseq, ts, D_MODEL), F32),
                   jax.ShapeDtypeStruct((nseq, DN_CONV - 1, QKV), F32),
                   jax.ShapeDtypeStruct(s_in.shape, F32)],
        compiler_params=pltpu.CompilerParams(dimension_semantics=("arbitrary", "arbitrary"),
                                             vmem_limit_bytes=VMEM_LIMIT),
        name="mix_sample",
    )(x1, cs_pad, s_in, *consts)


def _pad_lanes(a, n=LANES):
    return jnp.pad(a, [(0, 0)] * (a.ndim - 1) + [(0, n - a.shape[-1])])


def _layer_params(l, ffn1_w_up, ffn1_w_down, ln1_g, ln1_b, w_in, b_in, gm_v_g, gm_v_b, gm_w_s,
                  gm_b_s, dn_conv_w, dn_a_log, dn_dt_bias, dn_norm_w, w_branch_a, w_branch_b,
                  w_out, ln2_g, ln2_b, ffn2_w_up, ffn2_w_down, ln3_g, ln3_b):
    row = lambda a: a[l][None, :].astype(F32)
    wi, bi = w_in[l], b_in[l]
    o_beta = MAIN_COLS
    o_dec = o_beta + HEADS
    o_gate = o_dec + HEADS
    ws = gm_w_s[l]
    lsm = DN_CONV
    shift = np.arange(lsm)[:, None]
    pos = np.arange(ROW_TILE)[None, :]
    live = (pos >= shift) & (pos < lsm)
    ws_head = ws[:, :lsm, :lsm]
    coef = jnp.where(live[:, :, None],
                     jnp.transpose(ws_head[:, np.clip(pos + 0 * shift, 0, lsm - 1),
                                           np.clip(pos - shift, 0, lsm - 1)], (1, 2, 0)), 0.0)
    bias = jnp.pad(gm_b_s[l][:, :lsm].T, ((0, ROW_TILE - lsm), (0, 0)))
    return {
        'ffn1': (ffn1_w_up[l].astype(BF16), ffn1_w_down[l].astype(BF16), row(ln1_g), row(ln1_b)),
        'ffn2': (ffn2_w_up[l].astype(BF16), ffn2_w_down[l].astype(BF16), row(ln3_g), row(ln3_b)),
        'w_proj': jnp.concatenate([wi[:, :MAIN_COLS], wi[:, o_gate:], _pad_lanes(wi[:, o_beta:o_dec]),
                                   _pad_lanes(wi[:, o_dec:o_gate])], axis=1).astype(BF16),
        'b_proj': jnp.concatenate([bi[:MAIN_COLS], bi[o_gate:], _pad_lanes(bi[o_beta:o_dec]),
                                   _pad_lanes(bi[o_dec:o_gate])])[None, :],
        'gm_v_g': row(gm_v_g), 'gm_v_b': row(gm_v_b),
        'gm_w_s': ws, 'gm_b_s_t': gm_b_s[l].T,
        'mix_coef': jnp.repeat(coef, GROUP_DIM, axis=-1), 'mix_bias': jnp.repeat(bias, GROUP_DIM, axis=-1),
        'conv_w': dn_conv_w[l],
        'a_log': _pad_lanes(dn_a_log[l][None, :].astype(F32)),
        'dt_bias': _pad_lanes(dn_dt_bias[l][None, :].astype(F32)),
        'norm_w': row(dn_norm_w),
        'w_a': w_branch_a[l].astype(BF16), 'w_b': w_branch_b[l].astype(BF16),
        'w_o': w_out[l].astype(BF16),
        'ln2_g': row(ln2_g), 'ln2_b': row(ln2_b),
    }


def kernel(x_prompt, x_sample, state_conv, state_ssm, ffn1_w_up, ffn1_w_down, ln1_g, ln1_b, w_in, b_in, gm_v_g, gm_v_b, gm_w_s, gm_b_s, dn_conv_w, dn_a_log, dn_dt_bias, dn_norm_w, w_branch_a, w_branch_b, w_out, ln2_g, ln2_b, ffn2_w_up, ffn2_w_down, ln3_g, ln3_b):
    depth = ffn1_w_up.shape[0]
    alpha = (2.0 * depth) ** 0.25
    bp, tp, _ = x_prompt.shape
    bs, ts, _ = x_sample.shape
    assert ts == DN_CONV and ts + (DN_CONV - 1) <= ROW_TILE
    y_p, y_s = x_prompt, x_sample
    conv_p, ssm_p, conv_s, ssm_s, v_s = [], [], [], [], []
    for l in range(depth):
        p = _layer_params(l, ffn1_w_up, ffn1_w_down, ln1_g, ln1_b, w_in, b_in, gm_v_g, gm_v_b,
                          gm_w_s, gm_b_s, dn_conv_w, dn_a_log, dn_dt_bias, dn_norm_w, w_branch_a,
                          w_branch_b, w_out, ln2_g, ln2_b, ffn2_w_up, ffn2_w_down, ln3_g, ln3_b)
        x1, x1s = _ffn_ln(y_p.reshape(bp * tp, D_MODEL), y_s.reshape(bs * ts, D_MODEL), *p['ffn1'],
                          alpha, FFN_ROWS)
        x2, c_p, s_p = _mix_prompt(x1.reshape(bp, tp, D_MODEL), p, alpha, PROMPT_ROWS, PROMPT_SEQS)
        x1s = jnp.pad(x1s.reshape(bs, ts, D_MODEL), ((0, 0), (0, ROW_TILE - ts), (0, 0)))
        cs_pad = jnp.pad(state_conv[l], ((0, 0), (ROW_TILE - (DN_CONV - 1), 0), (0, 0)))
        x2s, vrows, c_s, s_s = _mix_sample(x1s.reshape(bs * ROW_TILE, D_MODEL),
                                           cs_pad.reshape(bs * ROW_TILE, QKV),
                                           state_ssm[l], p, alpha, SAMPLE_SEQS, ts)
        y_p, y_s = _ffn_ln(x2.reshape(bp * tp, D_MODEL), x2s.reshape(bs * ts, D_MODEL), *p['ffn2'],
                           alpha, FFN_ROWS)
        y_p, y_s = y_p.reshape(bp, tp, D_MODEL), y_s.reshape(bs, ts, D_MODEL)
        conv_p.append(c_p)
        ssm_p.append(s_p)
        conv_s.append(c_s)
        ssm_s.append(s_s)
        v_s.append(vrows)
    return (y_p, y_s, jnp.stack(conv_p), jnp.stack(ssm_p), jnp.stack(conv_s), jnp.stack(ssm_s),
            jnp.stack(v_s))
```

```python
import functools
import math

import jax
import jax.numpy as jnp
import numpy as np
from jax import lax
from jax.experimental import pallas as pl
from jax.experimental.pallas import tpu as pltpu

F32 = jnp.float32
BF16 = jnp.bfloat16

D_MODEL = 1024
D_FF = 2816
HEADS = 8
HEAD_DIM = 128
GROUPS = 8
GROUP_DIM = 128
GM_CHUNK = 128
DN_CHUNK = 64
DN_CONV = 4
QKV = 3 * D_MODEL
MAIN_COLS = 6 * D_MODEL
LN_EPS = 1e-5
RMS_EPS = 1e-6

MXU_DIM = 256
SUB = 128
ROW_TILE = 8
LANES = 128
SLAB = 16
PIECE_BUDGET = 200
MXU_PIECE_COST = 256
TICK_BUDGET = 1200
VMEM_LIMIT = 56 * 1024 * 1024
FFN_ROWS = 512
PROMPT_ROWS, PROMPT_SEQS = 128, 2
SAMPLE_SEQS = 8


def _sigmoid(x):
    return 0.5 * jnp.tanh(0.5 * x) + 0.5


def _silu(x):
    h = 0.5 * x
    return h + h * jnp.tanh(h)


def _gelu_tanh(x):
    c = math.sqrt(2.0 / math.pi)
    h = 0.5 * x
    return h + h * jnp.tanh(x * (c + (c * 0.044715) * (x * x)))


def _softplus(x):
    return jnp.maximum(x, 0.0) + jnp.log(1.0 + jnp.exp(-jnp.abs(x)))


def _layer_norm(y, g, b):
    mu = jnp.mean(y, axis=-1, keepdims=True)
    yc = y - mu
    var = jnp.mean(yc * yc, axis=-1, keepdims=True)
    return yc * lax.rsqrt(var + LN_EPS) * g + b


def _dot(a, b):
    return jnp.dot(a.astype(BF16), b.astype(BF16), preferred_element_type=F32)


def _dot_nt(a, b):
    return lax.dot_general(a.astype(BF16), b.astype(BF16), (((1,), (1,)), ((), ())),
                           preferred_element_type=F32)


def _dot_exact_lhs(m01, x):
    hi = x.astype(BF16)
    r1 = x - hi.astype(F32)
    mid = r1.astype(BF16)
    lo = (r1 - mid.astype(F32)).astype(BF16)
    m = m01.astype(BF16)
    return (jnp.dot(m, hi, preferred_element_type=F32)
            + jnp.dot(m, mid, preferred_element_type=F32)
            + jnp.dot(m, lo, preferred_element_type=F32))


def _block_masks(n, blk):
    row = lax.broadcasted_iota(jnp.int32, (n, n), 0)
    col = lax.broadcasted_iota(jnp.int32, (n, n), 1)
    same = (row // blk) == (col // blk)
    return same & (row >= col), same & (row > col), row == col


def _no_tick():
    pass


def _inv_unit_lower(a, eye, n_iter, tick=_no_tick):
    n = eye.shape[0]
    b = [-x for x in a]
    p = [eye + x for x in b]
    b = [_dot(x, x) for x in b]
    tick()
    for _ in range(n_iter - 1):
        pb = [_dot(jnp.concatenate([pi, bi], axis=0), bi) for pi, bi in zip(p, b)]
        tick()
        p = [pi + x[:n] for pi, x in zip(p, pb)]
        b = [x[n:] for x in pb]
    return [pi + _dot(pi, bi) for pi, bi in zip(p, b)]


def _dn_intra(q, k, v, beta, g_col, g_row, masks, n_iter, tick=_no_tick):
    causal, strict, diag = masks
    heads = range(len(q))
    decay = [jnp.where(causal, jnp.exp(jnp.where(causal, g_col[h] - g_row[h], 0.0)), 0.0) for h in heads]
    kb = [k[h] * beta[h] for h in heads]
    kq = [_dot_nt(jnp.concatenate([kb[h], q[h]], axis=0), k[h]) for h in heads]
    tick()
    a = [jnp.where(strict, kq[h][:SUB] * decay[h], 0.0) for h in heads]
    qk = [kq[h][SUB:] * decay[h] for h in heads]
    eye = jnp.where(diag, 1.0, 0.0).astype(F32)
    t_inv = _inv_unit_lower(a, eye, n_iter, tick)
    tick()
    e_g = [jnp.exp(g_col[h]) for h in heads]
    uw = [_dot(t_inv[h], jnp.concatenate([v[h] * beta[h], kb[h] * e_g[h]], axis=1)) for h in heads]
    tick()
    return ([x[:, :HEAD_DIM] for x in uw], [x[:, HEAD_DIM:] for x in uw], qk,
            [q[h] * e_g[h] for h in heads])


class _Slabs:
    def __init__(self, fn, *xs, slab):
        self._outs = []
        self.thunks = [functools.partial(self._run, fn, xs, r, slab)
                       for r in range(0, xs[0].shape[0], slab)]

    def _run(self, fn, xs, r, slab):
        self._outs.append(fn(*[x[r:r + slab] for x in xs]))

    def result(self):
        assert len(self._outs) == len(self.thunks)
        return jnp.concatenate(self._outs, axis=0)


def _by_rows(fn, *xs, slab):
    job = _Slabs(fn, *xs, slab=slab)
    for th in job.thunks:
        th()
    return job.result()


class _WorkQueue:
    def __init__(self):
        self._items = []

    def add(self, job, cost):
        self._items += [(cost, th, job) for th in job.thunks]

    def run(self, budget):
        while self._items and budget > 0:
            cost, th, _ = self._items.pop(0)
            th()
            budget -= cost

    def finish(self, job):
        while any(j is job for _, _, j in self._items):
            self._items.pop(0)[1]()


def _piped_dot(lhs, w, lo, hi, queue, budget=PIECE_BUDGET):
    outs = []
    for c in range(lo, hi, MXU_DIM):
        outs.append(jnp.dot(lhs, w[:, c:c + MXU_DIM], preferred_element_type=F32))
        queue.run(budget)
    return jnp.concatenate(outs, axis=1)


def _conv_job(x, b, tail, w, post):
    c = x.shape[1]
    taps = w.shape[0]
    sub = lax.broadcasted_iota(jnp.int32, (ROW_TILE, c), 0)
    b_rows = jnp.broadcast_to(b, (ROW_TILE, c))
    wj = [jnp.broadcast_to(w[taps - 1 - j:taps - j, :], (ROW_TILE, c)) for j in range(taps)]
    state = {'prev': [pltpu.roll(tail, j, 0) for j in range(1, taps)], 'cur': None}

    def tile(raw):
        cur = raw + b_rows
        rolled = [pltpu.roll(cur, 1, 0)]
        for _ in range(2, taps):
            rolled.append(pltpu.roll(rolled[-1], 1, 0))
        acc = cur * wj[0]
        for j in range(1, taps):
            acc = acc + jnp.where(sub < j, state['prev'][j - 1], rolled[j - 1]) * wj[j]
        state['prev'], state['cur'] = rolled, cur
        return post(acc)

    return _Slabs(tile, x, slab=ROW_TILE), state


def _gated_rms(o, norm_w, z_act):
    return o * lax.rsqrt(jnp.mean(o * o, axis=-1, keepdims=True) + RMS_EPS) * norm_w * z_act


def _l2n(x, mul=1.0):
    inv = lax.rsqrt(jnp.sum(x * x, axis=-1, keepdims=True) + RMS_EPS)
    return x * (inv if mul == 1.0 else inv * mul)


def _ffn_chunks():
    n_tiles = D_FF // MXU_DIM
    first = (n_tiles + 1) // 2 * MXU_DIM
    return ((0, first), (first, D_FF))


def _ffn_ln_tile(x_ref, wu_ref, wd_ref, g_ref, b_ref, o_ref, alpha):
    x = x_ref[...]
    xb = x.astype(BF16)
    acc = None
    for lo, hi in _ffn_chunks():
        a = jnp.dot(xb, wu_ref[:, lo:hi], preferred_element_type=F32)
        gt = jnp.dot(xb, wu_ref[:, D_FF + lo:D_FF + hi], preferred_element_type=F32)
        h = (_silu(a) * gt).astype(BF16)
        f = jnp.dot(h, wd_ref[lo:hi, :], preferred_element_type=F32)
        acc = f if acc is None else acc + f
    o_ref[...] = _layer_norm(alpha * x + 0.5 * acc, g_ref[...], b_ref[...])


def _ffn_ln_kernel(xp_ref, xs_ref, wu_ref, wd_ref, g_ref, b_ref, op_ref, os_ref, *, alpha, n_prompt):
    i = pl.program_id(0)
    pl.when(i < n_prompt)(functools.partial(_ffn_ln_tile, xp_ref, wu_ref, wd_ref, g_ref, b_ref, op_ref, alpha))
    pl.when(i >= n_prompt)(functools.partial(_ffn_ln_tile, xs_ref, wu_ref, wd_ref, g_ref, b_ref, os_ref, alpha))


def _const_spec(shape):
    nd = len(shape)
    return pl.BlockSpec(shape, lambda *_: (0,) * nd, pipeline_mode=pl.Buffered(1))


def _ffn_ln(xp, xs, wu, wd, g, b, alpha, tm):
    n_p, n_s = xp.shape[0] // tm, xs.shape[0] // tm
    assert xp.shape[0] % tm == 0 and xs.shape[0] % tm == 0 and D_FF % MXU_DIM == 0
    prompt_spec = pl.BlockSpec((tm, D_MODEL), lambda i: (jnp.minimum(i, n_p - 1), 0))
    sample_spec = pl.BlockSpec((tm, D_MODEL), lambda i: (jnp.maximum(i - n_p, 0), 0))
    return pl.pallas_call(
        functools.partial(_ffn_ln_kernel, alpha=alpha, n_prompt=n_p),
        grid=(n_p + n_s,),
        in_specs=[prompt_spec, sample_spec,
                  _const_spec(wu.shape), _const_spec(wd.shape),
                  _const_spec(g.shape), _const_spec(b.shape)],
        out_specs=[prompt_spec, sample_spec],
        out_shape=[jax.ShapeDtypeStruct(xp.shape, F32), jax.ShapeDtypeStruct(xs.shape, F32)],
        compiler_params=pltpu.CompilerParams(dimension_semantics=("arbitrary",),
                                             vmem_limit_bytes=VMEM_LIMIT),
        name="ffn_ln",
    )(xp, xs, wu, wd, g, b)


def _proj_views(w_ref, b_ref):
    g0, g1 = MAIN_COLS, MAIN_COLS + 2 * D_MODEL
    return (w_ref.at[:, 0:g0], w_ref.at[:, g0:g1], w_ref.at[:, g1:g1 + 2 * LANES],
            b_ref.at[:, 0:g0], b_ref.at[:, g0:g1], b_ref.at[:, g1:g1 + 2 * LANES])


def _branch_gates_and_z(hb, w_main, b_main):
    z = jnp.dot(hb, w_main[:, 5 * D_MODEL:6 * D_MODEL], preferred_element_type=F32) \
        + b_main[:, 5 * D_MODEL:6 * D_MODEL]
    return _silu(z)


def _beta_and_logdecay(hb, w_bd, b_bd, alog, dtb):
    bd = jnp.dot(hb, w_bd[...], preferred_element_type=F32) + b_bd[...]
    beta = _sigmoid(bd[:, :LANES])
    g = -jnp.exp(alog[...]) * _softplus(bd[:, LANES:] + dtb[...])
    return beta, g


def _merge_out_ln(x, hb, a_part, yb, w_gates, b_gates, wb_ref, wo_ref, ln_g, ln_b, alpha):
    gate_b = _sigmoid(jnp.dot(hb, w_gates[:, D_MODEL:], preferred_element_type=F32)
                      + b_gates[:, D_MODEL:])
    merged = a_part + gate_b * jnp.dot(yb, wb_ref[...], preferred_element_type=F32)
    mix = jnp.dot(merged.astype(BF16), wo_ref[...], preferred_element_type=F32)
    return _layer_norm(alpha * x + mix, ln_g[...], ln_b[...])


def _mix_prompt_kernel(x_ref, w_ref, b_ref, vg_ref, vb_ref,
                       ws_ref, bst_ref, convw_ref, alog_ref, dtb_ref, normw_ref,
                       wa_ref, wb_ref, wo_ref, ln_g, ln_b,
                       x2_ref, conv_out_ref, ssm_out_ref,
                       s_ref, xc_ref, q_s, k_s, v_s, z_s, g_s, beta_s, yb_s, *, alpha, tt, nseq):
    t = pl.program_id(1)
    nt = pl.num_programs(1)
    w_main, w_gates, w_bd, b_main, b_gates, b_bd = _proj_views(w_ref, b_ref)

    @pl.when(t == 0)
    def _():
        s_ref[...] = jnp.zeros(s_ref.shape, F32)
        xc_ref[...] = jnp.zeros(xc_ref.shape, F32)

    n_rows = nseq * tt
    x = x_ref[...].reshape(n_rows, D_MODEL)
    hb = x.astype(BF16)

    queue = _WorkQueue()

    def proj(part):
        return _piped_dot(hb, w_main, part * D_MODEL, (part + 1) * D_MODEL, queue)

    scale = HEAD_DIM ** -0.5
    heads = [slice(h * HEAD_DIM, (h + 1) * HEAD_DIM) for h in range(HEADS)]

    class _ConvJobs:
        def __init__(self, raw, part, post):
            self.cols = slice(part * D_MODEL, (part + 1) * D_MODEL)
            b_p = b_main[:, (2 + part) * D_MODEL:(3 + part) * D_MODEL]
            self.jobs = [_conv_job(raw[sq * tt:(sq + 1) * tt], b_p, xc_ref[sq, :, self.cols],
                                   convw_ref[:, self.cols], lambda acc: post(_silu(acc)))
                         for sq in range(nseq)]
            self.thunks = [th for job, _ in self.jobs for th in job.thunks]

        def result(self):
            for sq, (_, state) in enumerate(self.jobs):
                last = state['cur']

                @pl.when(t == nt - 1)
                def _():
                    conv_out_ref[sq, :, self.cols] = last[ROW_TILE - (DN_CONV - 1):, :]

                xc_ref[sq, :, self.cols] = last
            return jnp.concatenate([job.result() for job, _ in self.jobs], axis=0)

    def l2n_heads(a, mul):
        return jnp.concatenate([_l2n(a[:, sl], mul) for sl in heads], axis=1)

    b_z = b_main[:, 5 * D_MODEL:6 * D_MODEL]
    beta, g_log = _beta_and_logdecay(hb, w_bd, b_bd, alog_ref, dtb_ref)
    beta_s[...] = beta
    rt = lax.broadcasted_iota(jnp.int32, (n_rows, n_rows), 0)
    ct = lax.broadcasted_iota(jnp.int32, (n_rows, n_rows), 1)
    cum = jnp.where(((rt // DN_CHUNK) == (ct // DN_CHUNK)) & (rt >= ct), 1.0, 0.0)
    g_s[...] = _dot_exact_lhs(cum, g_log)
    pq = proj(2)
    job_q = _ConvJobs(pq, 0, lambda a: l2n_heads(a, scale))
    queue.add(job_q, 45)
    pk = proj(3)
    job_k = _ConvJobs(pk, 1, lambda a: l2n_heads(a, 1.0))
    queue.add(job_k, 45)
    pvv = proj(4)
    job_vv = _ConvJobs(pvv, 2, lambda a: a)
    queue.add(job_vv, 35)
    pz = proj(5)
    job_z = _Slabs(lambda a: _silu(a + b_z), pz, slab=SLAB)
    queue.add(job_z, 20)
    pgb = _piped_dot(hb, w_gates, D_MODEL, 2 * D_MODEL, queue)
    queue.finish(job_z)
    q_s[...] = job_q.result()
    k_s[...] = job_k.result()
    v_s[...] = job_vv.result()
    z_s[...] = job_z.result()

    branch_a = {}

    def dot_steps(lhs, w, lo, hi, pieces=4):
        step = (hi - lo) // pieces
        outs = []
        for p in range(pieces):
            outs.append(jnp.dot(lhs, w[:, lo + p * step:lo + (p + 1) * step], preferred_element_type=F32))
            yield MXU_PIECE_COST
        return jnp.concatenate(outs, axis=1)

    def job_steps(job, cost):
        for th in job.thunks:
            th()
            yield cost
        return job.result()

    def branch_a_steps():
        b_u = b_main[:, 0:D_MODEL]
        b_v, vg, vb = b_main[:, D_MODEL:2 * D_MODEL], vg_ref[...], vb_ref[...]
        b_ga = b_gates[:, :D_MODEL]
        pu = yield from dot_steps(hb, w_main, 0, D_MODEL)
        pv = yield from dot_steps(hb, w_main, D_MODEL, 2 * D_MODEL)
        vn = yield from job_steps(
            _Slabs(lambda a: _layer_norm(_gelu_tanh(a + b_v), vg, vb).astype(BF16), pv, slab=SLAB), 70)
        r128 = lax.broadcasted_iota(jnp.int32, (GM_CHUNK, GM_CHUNK), 0)
        c128 = lax.broadcasted_iota(jnp.int32, (GM_CHUNK, GM_CHUNK), 1)
        tril = r128 >= c128
        w_tril = [jnp.where(tril, ws_ref[g], 0.0).astype(BF16) for g in range(GROUPS)]
        rows = []
        for c in range(n_rows // GM_CHUNK):
            cols = []
            for g in range(GROUPS):
                blk = vn[c * GM_CHUNK:(c + 1) * GM_CHUNK, g * GROUP_DIM:(g + 1) * GROUP_DIM]
                cols.append(jnp.dot(w_tril[g], blk, preferred_element_type=F32) + bst_ref[:, g:g + 1])
            rows.append(jnp.concatenate(cols, axis=1))
            yield MXU_PIECE_COST
        mixed = jnp.concatenate(rows, axis=0)
        u = yield from job_steps(_Slabs(lambda a: _gelu_tanh(a + b_u), pu, slab=SLAB), 50)
        pga = yield from dot_steps(hb, w_gates, 0, D_MODEL)
        ya = yield from job_steps(_Slabs(lambda a, m: (a * m).astype(BF16), u, mixed, slab=SLAB), 10)
        pa = yield from dot_steps(ya, wa_ref, 0, D_MODEL)
        branch_a['a_part'] = yield from job_steps(
            _Slabs(lambda g, p_: _sigmoid(g + b_ga) * p_, pga, pa, slab=SLAB), 25)

    steps = branch_a_steps()

    def tick(budget=TICK_BUDGET):
        while budget > 0:
            cost = next(steps, None)
            if cost is None:
                return
            budget -= cost

    masks = _block_masks(SUB, DN_CHUNK)
    norm_w = normw_ref[...]

    chains = [(sq, h) for sq in range(nseq) for h in range(HEADS)]
    sls = [slice(h * HEAD_DIM, (h + 1) * HEAD_DIM) for _, h in chains]
    n_chunks = SUB // DN_CHUNK
    row_chunk = lax.broadcasted_iota(jnp.int32, (SUB, HEAD_DIM), 0) // DN_CHUNK
    zeros = jnp.zeros((DN_CHUNK, HEAD_DIM), F32)
    ids = range(len(chains))
    state = [s_ref[sq, h] for sq, h in chains]
    for j in range(tt // SUB):
        rows = [slice(sq * tt + j * SUB, sq * tt + (j + 1) * SUB) for sq, _ in chains]
        g_sub = [g_s[sq * tt + j * SUB:sq * tt + (j + 1) * SUB, :] for sq in range(nseq)]
        g_t = [g.T for g in g_sub]
        b_sub = [beta_s[sq * tt + j * SUB:sq * tt + (j + 1) * SUB, :] for sq in range(nseq)]
        q = [q_s[rows[i], sls[i]] for i in ids]
        k = [k_s[rows[i], sls[i]] for i in ids]
        vv = [v_s[rows[i], sls[i]] for i in ids]
        g_col = [jnp.broadcast_to(g_sub[sq][:, h:h + 1], (SUB, HEAD_DIM)) for sq, h in chains]
        g_row = [jnp.broadcast_to(g_t[sq][h:h + 1, :], (SUB, SUB)) for sq, h in chains]
        beta_h = [jnp.broadcast_to(b_sub[sq][:, h:h + 1], (SUB, HEAD_DIM)) for sq, h in chains]
        u_h, w_h, qk, qe = _dn_intra(q, k, vv, beta_h, g_col, g_row, masks, 5, tick)
        g_last = [[g_col[i][(c + 1) * DN_CHUNK - 1:(c + 1) * DN_CHUNK, :] for c in range(n_chunks)]
                  for i in ids]
        k_dec_t = []
        for i in ids:
            g_end = g_last[i][n_chunks - 1]
            for c in range(n_chunks - 2, -1, -1):
                g_end = jnp.where(row_chunk == c, g_last[i][c], g_end)
            k_dec_t.append((k[i] * jnp.exp(g_end - g_col[i])).T)
        outs = [[] for _ in ids]
        for c in range(n_chunks):
            rs = slice(c * DN_CHUNK, (c + 1) * DN_CHUNK)
            r = [_dot(jnp.concatenate([w_h[i][rs], qe[i][rs]], axis=0), state[i]) for i in ids]
            tick()
            v_new = [u_h[i][rs] - r[i][:DN_CHUNK] for i in ids]
            v_pad = [jnp.concatenate([zeros] * c + [v_new[i]] + [zeros] * (n_chunks - 1 - c), axis=0)
                     for i in ids]
            m = [_dot(jnp.concatenate([qk[i][rs], k_dec_t[i]], axis=0), v_pad[i]) for i in ids]
            tick()
            for i in ids:
                outs[i].append(r[i][DN_CHUNK:] + m[i][:DN_CHUNK])
            state = [state[i] * jnp.exp(g_last[i][c]) + m[i][DN_CHUNK:] for i in ids]
        for i in ids:
            o = jnp.concatenate(outs[i], axis=0)
            yb_s[rows[i], sls[i]] = _gated_rms(o, norm_w, z_s[rows[i], sls[i]]).astype(BF16)
    for i, (sq, h) in enumerate(chains):
        s_ref[sq, h] = state[i]
    tick(float('inf'))
    a_part = branch_a['a_part']

    @pl.when(t == nt - 1)
    def _():
        ssm_out_ref[...] = s_ref[...]

    pb = jnp.dot(yb_s[...], wb_ref[...], preferred_element_type=F32)
    b_gb = b_gates[:, D_MODEL:]
    merged = _by_rows(lambda a, g, p_: (a + _sigmoid(g + b_gb) * p_).astype(BF16), a_part, pgb, pb, slab=SLAB)
    mix = jnp.dot(merged, wo_ref[...], preferred_element_type=F32)
    ln_gain, ln_bias = ln_g[...], ln_b[...]
    x2 = _by_rows(lambda xx, m: _layer_norm(alpha * xx + m, ln_gain, ln_bias), x, mix, slab=SLAB)
    x2_ref[...] = x2.reshape(nseq, tt, D_MODEL)


def _mix_prompt(x1, p, alpha, tt, nseq):
    b, t, _ = x1.shape
    assert t % tt == 0 and tt % SUB == 0 and b % nseq == 0
    rows = nseq * tt
    consts = [p['w_proj'], p['b_proj'],
              p['gm_v_g'], p['gm_v_b'], p['gm_w_s'], p['gm_b_s_t'], p['conv_w'], p['a_log'],
              p['dt_bias'], p['norm_w'], p['w_a'], p['w_b'], p['w_o'], p['ln2_g'], p['ln2_b']]
    return pl.pallas_call(
        functools.partial(_mix_prompt_kernel, alpha=alpha, tt=tt, nseq=nseq),
        grid=(b // nseq, t // tt),
        in_specs=[pl.BlockSpec((nseq, tt, D_MODEL), lambda i, j: (i, j, 0))]
                 + [_const_spec(c.shape) for c in consts],
        out_specs=[pl.BlockSpec((nseq, tt, D_MODEL), lambda i, j: (i, j, 0)),
                   pl.BlockSpec((nseq, DN_CONV - 1, QKV), lambda i, j: (i, 0, 0)),
                   pl.BlockSpec((nseq, HEADS, HEAD_DIM, HEAD_DIM), lambda i, j: (i, 0, 0, 0))],
        out_shape=[jax.ShapeDtypeStruct((b, t, D_MODEL), F32),
                   jax.ShapeDtypeStruct((b, DN_CONV - 1, QKV), F32),
                   jax.ShapeDtypeStruct((b, HEADS, HEAD_DIM, HEAD_DIM), F32)],
        scratch_shapes=[pltpu.VMEM((nseq, HEADS, HEAD_DIM, HEAD_DIM), F32),
                        pltpu.VMEM((nseq, ROW_TILE, QKV), F32),
                        pltpu.VMEM((rows, D_MODEL), F32),
                        pltpu.VMEM((rows, D_MODEL), F32),
                        pltpu.VMEM((rows, D_MODEL), F32),
                        pltpu.VMEM((rows, D_MODEL), F32),
                        pltpu.VMEM((rows, LANES), F32),
                        pltpu.VMEM((rows, LANES), F32),
                        pltpu.VMEM((rows, D_MODEL), BF16)],
        compiler_params=pltpu.CompilerParams(dimension_semantics=("arbitrary", "arbitrary"),
                                             vmem_limit_bytes=VMEM_LIMIT),
        name="mix_prompt",
    )(x1, *consts)


def _sample_stage_one(x_ref, cs_ref, w_ref, b_ref, vg_ref, vb_ref, coef_ref, bias_ref, convw_ref,
                      alog_ref, dtb_ref, wa_ref, vrow_ref, conv_out_ref,
                      u_s, w_s, qe_s, qk_s, kdt_s, gcol_s, apart_s, zact_s, *, nb, ts):
    rows = nb * ROW_TILE
    w_main, w_gates, w_bd, b_main, b_gates, b_bd = _proj_views(w_ref, b_ref)
    x = x_ref[...]
    hb = x.astype(BF16)
    valid = (lax.broadcasted_iota(jnp.int32, (rows, 1), 0) % ROW_TILE) < ts
    validf = jnp.where(valid, 1.0, 0.0).astype(F32)

    u = _gelu_tanh(jnp.dot(hb, w_main[:, 0:D_MODEL], preferred_element_type=F32)
                   + b_main[:, 0:D_MODEL])
    v = _gelu_tanh(jnp.dot(hb, w_main[:, D_MODEL:2 * D_MODEL], preferred_element_type=F32)
                   + b_main[:, D_MODEL:2 * D_MODEL])
    vn = _layer_norm(v, vg_ref[...], vb_ref[...])
    vn3 = vn.reshape(nb, ROW_TILE, D_MODEL)
    vrow_ref[...] = vn3[:, :ts, :]
    mixed = vn3 * coef_ref[0][None] + bias_ref[...][None]
    for j in range(1, DN_CONV):
        mixed = mixed + pltpu.roll(vn3, j, 1) * coef_ref[j][None]
    ya = (u * mixed.reshape(rows, D_MODEL)).astype(BF16)
    gate_a = _sigmoid(jnp.dot(hb, w_gates[:, :D_MODEL], preferred_element_type=F32)
                      + b_gates[:, :D_MODEL])
    apart_s[...] = gate_a * jnp.dot(ya, wa_ref[...], preferred_element_type=F32)

    qkv = jnp.dot(hb, w_main[:, 2 * D_MODEL:5 * D_MODEL], preferred_element_type=F32) \
        + b_main[:, 2 * D_MODEL:5 * D_MODEL]
    zfull = jnp.where(valid, qkv, 0.0) + cs_ref[...]
    z3 = zfull.reshape(nb, ROW_TILE, QKV)
    conv_out_ref[...] = z3[:, ts - (DN_CONV - 1):ts, :]
    acc = z3 * convw_ref[DN_CONV - 1:DN_CONV, :][None]
    for j in range(1, DN_CONV):
        acc = acc + pltpu.roll(z3, j, 1) * convw_ref[DN_CONV - 1 - j:DN_CONV - j, :][None]
    sact = _silu(acc.reshape(rows, QKV)) * validf
    zact_s[...] = _branch_gates_and_z(hb, w_main, b_main)
    beta, g_log = _beta_and_logdecay(hb, w_bd, b_bd, alog_ref, dtb_ref)
    beta = beta * validf
    g_log = g_log * validf
    rt = lax.broadcasted_iota(jnp.int32, (rows, rows), 0)
    ct = lax.broadcasted_iota(jnp.int32, (rows, rows), 1)
    cum = jnp.where(((rt // ROW_TILE) == (ct // ROW_TILE)) & (rt >= ct), 1.0, 0.0)
    g_cum = _dot_exact_lhs(cum, g_log)

    assert rows == SUB
    g_t = g_cum.T
    masks = _block_masks(SUB, ROW_TILE)
    scale = HEAD_DIM ** -0.5
    heads = range(HEADS)
    sls = [slice(h * HEAD_DIM, (h + 1) * HEAD_DIM) for h in heads]
    q = [_l2n(sact[:, sl]) * scale * validf for sl in sls]
    k = [_l2n(sact[:, D_MODEL + h * HEAD_DIM:D_MODEL + (h + 1) * HEAD_DIM]) * validf for h in heads]
    vv = [sact[:, 2 * D_MODEL + h * HEAD_DIM:2 * D_MODEL + (h + 1) * HEAD_DIM] for h in heads]
    g_col = [jnp.broadcast_to(g_cum[:, h:h + 1], (SUB, HEAD_DIM)) for h in heads]
    g_row = [jnp.broadcast_to(g_t[h:h + 1, :], (SUB, SUB)) for h in heads]
    beta_h = [jnp.broadcast_to(beta[:, h:h + 1], (SUB, HEAD_DIM)) for h in heads]
    u_h, w_h, qk, qe = _dn_intra(q, k, vv, beta_h, g_col, g_row, masks, 2)
    for h in heads:
        g_end = jnp.broadcast_to(g_col[h].reshape(nb, ROW_TILE, HEAD_DIM)[:, ROW_TILE - 1:, :],
                                 (nb, ROW_TILE, HEAD_DIM)).reshape(SUB, HEAD_DIM)
        kdt_s[h] = (k[h] * jnp.exp(g_end - g_col[h])).T
        u_s[h], w_s[h], qe_s[h], qk_s[h], gcol_s[h] = u_h[h], w_h[h], qe[h], qk[h], g_col[h]


def _sample_state_part(part, s_in_ref, normw_ref, s_out_ref, u_s, w_s, qe_s, qk_s, kdt_s, gcol_s,
                       zact_s, yb_s, *, nb):
    span = nb * ROW_TILE
    base = part * span
    heads = range(HEADS)
    tiles = [slice(base + i * ROW_TILE, base + (i + 1) * ROW_TILE) for i in range(nb)]
    zeros = jnp.zeros((SUB - span, HEAD_DIM), F32)
    seq_of_row = lax.broadcasted_iota(jnp.int32, (SUB, HEAD_DIM), 0) // ROW_TILE
    norm_w = normw_ref[...]
    r = [[_dot(jnp.concatenate([w_s[h, rs, :], qe_s[h, rs, :]], axis=0), s_in_ref[i, h])
          for i, rs in enumerate(tiles)] for h in heads]
    v_new = []
    for h in heads:
        mine = jnp.concatenate([u_s[h, rs, :] - r[h][i][:ROW_TILE] for i, rs in enumerate(tiles)], axis=0)
        v_new.append(jnp.concatenate([mine, zeros] if part == 0 else [zeros, mine], axis=0))
    qkv_new = [_dot(qk_s[h, base:base + span, :], v_new[h]) for h in heads]
    for i, rs in enumerate(tiles):
        for h in heads:
            g_last = gcol_s[h, rs.stop - 1:rs.stop, :]
            s_out_ref[i, h] = (s_in_ref[i, h] * jnp.exp(g_last)
                               + _dot(kdt_s[h], jnp.where(seq_of_row == part * nb + i, v_new[h], 0.0)))
    for h in heads:
        sl = slice(h * HEAD_DIM, (h + 1) * HEAD_DIM)
        o = jnp.concatenate([r[h][i][ROW_TILE:] for i in range(nb)], axis=0) + qkv_new[h]
        yb_s[base:base + span, sl] = _gated_rms(o, norm_w, zact_s[base:base + span, sl]).astype(BF16)


def _mix_sample_kernel(x_ref, cs_ref, s_in_ref, w_ref, b_ref,
                       vg_ref, vb_ref, coef_ref, bias_ref, convw_ref, alog_ref, dtb_ref, normw_ref,
                       wa_ref, wb_ref, wo_ref, ln_g, ln_b,
                       x2_ref, vrow_ref, conv_out_ref, s_out_ref,
                       u_s, w_s, qe_s, qk_s, kdt_s, gcol_s, apart_s, zact_s, yb_s, *, alpha, nb, ts):
    half = pl.program_id(1)
    per_head = (u_s, w_s, qe_s, qk_s, kdt_s, gcol_s)
    pl.when(half == 0)(functools.partial(
        _sample_stage_one, x_ref, cs_ref, w_ref, b_ref, vg_ref, vb_ref, coef_ref, bias_ref, convw_ref,
        alog_ref, dtb_ref, wa_ref, vrow_ref, conv_out_ref, *per_head, apart_s, zact_s, nb=2 * nb, ts=ts))
    for part in range(2):
        pl.when(half == part)(functools.partial(
            _sample_state_part, part, s_in_ref, normw_ref, s_out_ref, *per_head, zact_s, yb_s, nb=nb))

    @pl.when(half == 1)
    def _():
        _, w_gates, _, _, b_gates, _ = _proj_views(w_ref, b_ref)
        x = x_ref[...]
        x2 = _merge_out_ln(x, x.astype(BF16), apart_s[...], yb_s[...], w_gates, b_gates, wb_ref, wo_ref,
                           ln_g, ln_b, alpha)
        x2_ref[...] = x2.reshape(2 * nb, ROW_TILE, D_MODEL)[:, :ts, :]


def _mix_sample(x1, cs_pad, s_in, p, alpha, nb, ts):
    n = x1.shape[0]
    nseq = n // ROW_TILE
    assert nseq % (2 * nb) == 0 and 2 * nb * ROW_TILE == SUB
    rows = 2 * nb * ROW_TILE
    consts = [p['w_proj'], p['b_proj'],
              p['gm_v_g'], p['gm_v_b'], p['mix_coef'], p['mix_bias'], p['conv_w'], p['a_log'],
              p['dt_bias'], p['norm_w'], p['w_a'], p['w_b'], p['w_o'], p['ln2_g'], p['ln2_b']]
    state_spec = pl.BlockSpec((nb, HEADS, HEAD_DIM, HEAD_DIM), lambda i, half: (2 * i + half, 0, 0, 0))
    token_spec = pl.BlockSpec((2 * nb, ts, D_MODEL), lambda i, half: (i, 0, 0))
    per_head = pltpu.VMEM((HEADS, SUB, HEAD_DIM), F32)
    return pl.pallas_call(
        functools.partial(_mix_sample_kernel, alpha=alpha, nb=nb, ts=ts),
        grid=(nseq // (2 * nb), 2),
        in_specs=[pl.BlockSpec((rows, D_MODEL), lambda i, half: (i, 0)),
                  pl.BlockSpec((rows, QKV), lambda i, half: (i, 0)),
                  state_spec] + [_const_spec(c.shape) for c in consts],
        out_specs=[token_spec, token_spec,
                   pl.BlockSpec((2 * nb, DN_CONV - 1, QKV), lambda i, half: (i, 0, 0)),
                   state_spec],
        scratch_shapes=[per_head, per_head, per_head, per_head, per_head, per_head,
                        pltpu.VMEM((rows, D_MODEL), F32),
                        pltpu.VMEM((rows, D_MODEL), F32),
                        pltpu.VMEM((rows, D_MODEL), BF16)],
        out_shape=[jax.ShapeDtypeStruct((nseq, ts, D_MODEL), F32),
                   jax.ShapeDtypeStruct((nseq, ts, D_MODEL), F32),
                   jax.ShapeDtypeStruct((nseq, DN_CONV - 1, QKV), F32),
                   jax.ShapeDtypeStruct(s_in.shape, F32)],
        compiler_params=pltpu.CompilerParams(dimension_semantics=("arbitrary", "arbitrary"),
                                             vmem_limit_bytes=VMEM_LIMIT),
        name="mix_sample",
    )(x1, cs_pad, s_in, *consts)


def _pad_lanes(a, n=LANES):
    return jnp.pad(a, [(0, 0)] * (a.ndim - 1) + [(0, n - a.shape[-1])])


def _layer_params(l, ffn1_w_up, ffn1_w_down, ln1_g, ln1_b, w_in, b_in, gm_v_g, gm_v_b, gm_w_s,
                  gm_b_s, dn_conv_w, dn_a_log, dn_dt_bias, dn_norm_w, w_branch_a, w_branch_b,
                  w_out, ln2_g, ln2_b, ffn2_w_up, ffn2_w_down, ln3_g, ln3_b):
    row = lambda a: a[l][None, :].astype(F32)
    wi, bi = w_in[l], b_in[l]
    o_beta = MAIN_COLS
    o_dec = o_beta + HEADS
    o_gate = o_dec + HEADS
    ws = gm_w_s[l]
    lsm = DN_CONV
    shift = np.arange(lsm)[:, None]
    pos = np.arange(ROW_TILE)[None, :]
    live = (pos >= shift) & (pos < lsm)
    ws_head = ws[:, :lsm, :lsm]
    coef = jnp.where(live[:, :, None],
                     jnp.transpose(ws_head[:, np.clip(pos + 0 * shift, 0, lsm - 1),
                                           np.clip(pos - shift, 0, lsm - 1)], (1, 2, 0)), 0.0)
    bias = jnp.pad(gm_b_s[l][:, :lsm].T, ((0, ROW_TILE - lsm), (0, 0)))
    return {
        'ffn1': (ffn1_w_up[l].astype(BF16), ffn1_w_down[l].astype(BF16), row(ln1_g), row(ln1_b)),
        'ffn2': (ffn2_w_up[l].astype(BF16), ffn2_w_down[l].astype(BF16), row(ln3_g), row(ln3_b)),
        'w_proj': jnp.concatenate([wi[:, :MAIN_COLS], wi[:, o_gate:], _pad_lanes(wi[:, o_beta:o_dec]),
                                   _pad_lanes(wi[:, o_dec:o_gate])], axis=1).astype(BF16),
        'b_proj': jnp.concatenate([bi[:MAIN_COLS], bi[o_gate:], _pad_lanes(bi[o_beta:o_dec]),
                                   _pad_lanes(bi[o_dec:o_gate])])[None, :],
        'gm_v_g': row(gm_v_g), 'gm_v_b': row(gm_v_b),
        'gm_w_s': ws, 'gm_b_s_t': gm_b_s[l].T,
        'mix_coef': jnp.repeat(coef, GROUP_DIM, axis=-1), 'mix_bias': jnp.repeat(bias, GROUP_DIM, axis=-1),
        'conv_w': dn_conv_w[l],
        'a_log': _pad_lanes(dn_a_log[l][None, :].astype(F32)),
        'dt_bias': _pad_lanes(dn_dt_bias[l][None, :].astype(F32)),
        'norm_w': row(dn_norm_w),
        'w_a': w_branch_a[l].astype(BF16), 'w_b': w_branch_b[l].astype(BF16),
        'w_o': w_out[l].astype(BF16),
        'ln2_g': row(ln2_g), 'ln2_b': row(ln2_b),
    }


def kernel(x_prompt, x_sample, state_conv, state_ssm, ffn1_w_up, ffn1_w_down, ln1_g, ln1_b, w_in, b_in, gm_v_g, gm_v_b, gm_w_s, gm_b_s, dn_conv_w, dn_a_log, dn_dt_bias, dn_norm_w, w_branch_a, w_branch_b, w_out, ln2_g, ln2_b, ffn2_w_up, ffn2_w_down, ln3_g, ln3_b):
    depth = ffn1_w_up.shape[0]
    alpha = (2.0 * depth) ** 0.25
    bp, tp, _ = x_prompt.shape
    bs, ts, _ = x_sample.shape
    assert ts == DN_CONV and ts + (DN_CONV - 1) <= ROW_TILE
    y_p, y_s = x_prompt, x_sample
    conv_p, ssm_p, conv_s, ssm_s, v_s = [], [], [], [], []
    for l in range(depth):
        p = _layer_params(l, ffn1_w_up, ffn1_w_down, ln1_g, ln1_b, w_in, b_in, gm_v_g, gm_v_b,
                          gm_w_s, gm_b_s, dn_conv_w, dn_a_log, dn_dt_bias, dn_norm_w, w_branch_a,
                          w_branch_b, w_out, ln2_g, ln2_b, ffn2_w_up, ffn2_w_down, ln3_g, ln3_b)
        x1, x1s = _ffn_ln(y_p.reshape(bp * tp, D_MODEL), y_s.reshape(bs * ts, D_MODEL), *p['ffn1'],
                          alpha, FFN_ROWS)
        x2, c_p, s_p = _mix_prompt(x1.reshape(bp, tp, D_MODEL), p, alpha, PROMPT_ROWS, PROMPT_SEQS)
        x1s = jnp.pad(x1s.reshape(bs, ts, D_MODEL), ((0, 0), (0, ROW_TILE - ts), (0, 0)))
        cs_pad = jnp.pad(state_conv[l], ((0, 0), (ROW_TILE - (DN_CONV - 1), 0), (0, 0)))
        x2s, vrows, c_s, s_s = _mix_sample(x1s.reshape(bs * ROW_TILE, D_MODEL),
                                           cs_pad.reshape(bs * ROW_TILE, QKV),
                                           state_ssm[l], p, alpha, SAMPLE_SEQS, ts)
        y_p, y_s = _ffn_ln(x2.reshape(bp * tp, D_MODEL), x2s.reshape(bs * ts, D_MODEL), *p['ffn2'],
                           alpha, FFN_ROWS)
        y_p, y_s = y_p.reshape(bp, tp, D_MODEL), y_s.reshape(bs, ts, D_MODEL)
        conv_p.append(c_p)
        ssm_p.append(s_p)
        conv_s.append(c_s)
        ssm_s.append(s_s)
        v_s.append(vrows)
    return (y_p, y_s, jnp.stack(conv_p), jnp.stack(ssm_p), jnp.stack(conv_s), jnp.stack(ssm_s),
            jnp.stack(v_s))
```

```python
import functools
import math

import jax
import jax.numpy as jnp
import numpy as np
from jax import lax
from jax.experimental import pallas as pl
from jax.experimental.pallas import tpu as pltpu

F32 = jnp.float32
BF16 = jnp.bfloat16

D_MODEL = 1024
D_FF = 2816
HEADS = 8
HEAD_DIM = 128
GROUPS = 8
GROUP_DIM = 128
GM_CHUNK = 128
DN_CHUNK = 64
DN_CONV = 4
QKV = 3 * D_MODEL
MAIN_COLS = 6 * D_MODEL
LN_EPS = 1e-5
RMS_EPS = 1e-6

MXU_DIM = 256
SUB = 128
ROW_TILE = 8
LANES = 128
SLAB = 16
PIECE_BUDGET = 200
MXU_PIECE_COST = 256
TICK_BUDGET = 1200
VMEM_LIMIT = 56 * 1024 * 1024
FFN_ROWS = 512
PROMPT_ROWS, PROMPT_SEQS = 128, 2
SAMPLE_SEQS = 8


def _sigmoid(x):
    return 0.5 * jnp.tanh(0.5 * x) + 0.5


def _silu(x):
    h = 0.5 * x
    return h + h * jnp.tanh(h)


def _gelu_tanh(x):
    c = math.sqrt(2.0 / math.pi)
    h = 0.5 * x
    return h + h * jnp.tanh(x * (c + (c * 0.044715) * (x * x)))


def _softplus(x):
    return jnp.maximum(x, 0.0) + jnp.log(1.0 + jnp.exp(-jnp.abs(x)))


def _layer_norm(y, g, b):
    mu = jnp.mean(y, axis=-1, keepdims=True)
    yc = y - mu
    var = jnp.mean(yc * yc, axis=-1, keepdims=True)
    return yc * lax.rsqrt(var + LN_EPS) * g + b


def _dot(a, b):
    return jnp.dot(a.astype(BF16), b.astype(BF16), preferred_element_type=F32)


def _dot_nt(a, b):
    return lax.dot_general(a.astype(BF16), b.astype(BF16), (((1,), (1,)), ((), ())),
                           preferred_element_type=F32)


def _dot_exact_lhs(m01, x):
    hi = x.astype(BF16)
    r1 = x - hi.astype(F32)
    mid = r1.astype(BF16)
    lo = (r1 - mid.astype(F32)).astype(BF16)
    m = m01.astype(BF16)
    return (jnp.dot(m, hi, preferred_element_type=F32)
            + jnp.dot(m, mid, preferred_element_type=F32)
            + jnp.dot(m, lo, preferred_element_type=F32))


def _block_masks(n, blk):
    row = lax.broadcasted_iota(jnp.int32, (n, n), 0)
    col = lax.broadcasted_iota(jnp.int32, (n, n), 1)
    same = (row // blk) == (col // blk)
    return same & (row >= col), same & (row > col), row == col


def _no_tick():
    pass


def _inv_unit_lower(a, eye, n_iter, tick=_no_tick):
    n = eye.shape[0]
    b = [-x for x in a]
    p = [eye + x for x in b]
    b = [_dot(x, x) for x in b]
    tick()
    for _ in range(n_iter - 1):
        pb = [_dot(jnp.concatenate([pi, bi], axis=0), bi) for pi, bi in zip(p, b)]
        tick()
        p = [pi + x[:n] for pi, x in zip(p, pb)]
        b = [x[n:] for x in pb]
    return [pi + _dot(pi, bi) for pi, bi in zip(p, b)]


def _dn_intra(q, k, v, beta, g_col, g_row, masks, n_iter, tick=_no_tick):
    causal, strict, diag = masks
    heads = range(len(q))
    decay = [jnp.where(causal, jnp.exp(jnp.where(causal, g_col[h] - g_row[h], 0.0)), 0.0) for h in heads]
    kb = [k[h] * beta[h] for h in heads]
    kq = [_dot_nt(jnp.concatenate([kb[h], q[h]], axis=0), k[h]) for h in heads]
    tick()
    a = [jnp.where(strict, kq[h][:SUB] * decay[h], 0.0) for h in heads]
    qk = [kq[h][SUB:] * decay[h] for h in heads]
    eye = jnp.where(diag, 1.0, 0.0).astype(F32)
    t_inv = _inv_unit_lower(a, eye, n_iter, tick)
    tick()
    e_g = [jnp.exp(g_col[h]) for h in heads]
    uw = [_dot(t_inv[h], jnp.concatenate([v[h] * beta[h], kb[h] * e_g[h]], axis=1)) for h in heads]
    tick()
    return ([x[:, :HEAD_DIM] for x in uw], [x[:, HEAD_DIM:] for x in uw], qk,
            [q[h] * e_g[h] for h in heads])


class _Slabs:
    def __init__(self, fn, *xs, slab):
        self._outs = []
        self.thunks = [functools.partial(self._run, fn, xs, r, slab)
                       for r in range(0, xs[0].shape[0], slab)]

    def _run(self, fn, xs, r, slab):
        self._outs.append(fn(*[x[r:r + slab] for x in xs]))

    def result(self):
        assert len(self._outs) == len(self.thunks)
        return jnp.concatenate(self._outs, axis=0)


def _by_rows(fn, *xs, slab):
    job = _Slabs(fn, *xs, slab=slab)
    for th in job.thunks:
        th()
    return job.result()


class _WorkQueue:
    def __init__(self):
        self._items = []

    def add(self, job, cost):
        self._items += [(cost, th, job) for th in job.thunks]

    def run(self, budget):
        while self._items and budget > 0:
            cost, th, _ = self._items.pop(0)
            th()
            budget -= cost

    def finish(self, job):
        while any(j is job for _, _, j in self._items):
            self._items.pop(0)[1]()


def _piped_dot(lhs, w, lo, hi, queue, budget=PIECE_BUDGET):
    outs = []
    for c in range(lo, hi, MXU_DIM):
        outs.append(jnp.dot(lhs, w[:, c:c + MXU_DIM], preferred_element_type=F32))
        queue.run(budget)
    return jnp.concatenate(outs, axis=1)


def _conv_job(x, b, tail, w, post):
    c = x.shape[1]
    taps = w.shape[0]
    sub = lax.broadcasted_iota(jnp.int32, (ROW_TILE, c), 0)
    b_rows = jnp.broadcast_to(b, (ROW_TILE, c))
    wj = [jnp.broadcast_to(w[taps - 1 - j:taps - j, :], (ROW_TILE, c)) for j in range(taps)]
    state = {'prev': [pltpu.roll(tail, j, 0) for j in range(1, taps)], 'cur': None}

    def tile(raw):
        cur = raw + b_rows
        rolled = [pltpu.roll(cur, 1, 0)]
        for _ in range(2, taps):
            rolled.append(pltpu.roll(rolled[-1], 1, 0))
        acc = cur * wj[0]
        for j in range(1, taps):
            acc = acc + jnp.where(sub < j, state['prev'][j - 1], rolled[j - 1]) * wj[j]
        state['prev'], state['cur'] = rolled, cur
        return post(acc)

    return _Slabs(tile, x, slab=ROW_TILE), state


def _gated_rms(o, norm_w, z_act):
    return o * lax.rsqrt(jnp.mean(o * o, axis=-1, keepdims=True) + RMS_EPS) * norm_w * z_act


def _l2n(x, mul=1.0):
    inv = lax.rsqrt(jnp.sum(x * x, axis=-1, keepdims=True) + RMS_EPS)
    return x * (inv if mul == 1.0 else inv * mul)


def _ffn_chunks():
    n_tiles = D_FF // MXU_DIM
    first = (n_tiles + 1) // 2 * MXU_DIM
    return ((0, first), (first, D_FF))


def _ffn_ln_tile(x_ref, wu_ref, wd_ref, g_ref, b_ref, o_ref, alpha):
    x = x_ref[...]
    xb = x.astype(BF16)
    acc = None
    for lo, hi in _ffn_chunks():
        a = jnp.dot(xb, wu_ref[:, lo:hi], preferred_element_type=F32)
        gt = jnp.dot(xb, wu_ref[:, D_FF + lo:D_FF + hi], preferred_element_type=F32)
        h = (_silu(a) * gt).astype(BF16)
        f = jnp.dot(h, wd_ref[lo:hi, :], preferred_element_type=F32)
        acc = f if acc is None else acc + f
    o_ref[...] = _layer_norm(alpha * x + 0.5 * acc, g_ref[...], b_ref[...])


def _ffn_ln_kernel(xp_ref, xs_ref, wu_ref, wd_ref, g_ref, b_ref, op_ref, os_ref, *, alpha, n_prompt):
    i = pl.program_id(0)
    pl.when(i < n_prompt)(functools.partial(_ffn_ln_tile, xp_ref, wu_ref, wd_ref, g_ref, b_ref, op_ref, alpha))
    pl.when(i >= n_prompt)(functools.partial(_ffn_ln_tile, xs_ref, wu_ref, wd_ref, g_ref, b_ref, os_ref, alpha))


def _const_spec(shape):
    nd = len(shape)
    return pl.BlockSpec(shape, lambda *_: (0,) * nd, pipeline_mode=pl.Buffered(1))


def _ffn_ln(xp, xs, wu, wd, g, b, alpha, tm):
    n_p, n_s = xp.shape[0] // tm, xs.shape[0] // tm
    assert xp.shape[0] % tm == 0 and xs.shape[0] % tm == 0 and D_FF % MXU_DIM == 0
    prompt_spec = pl.BlockSpec((tm, D_MODEL), lambda i: (jnp.minimum(i, n_p - 1), 0))
    sample_spec = pl.BlockSpec((tm, D_MODEL), lambda i: (jnp.maximum(i - n_p, 0), 0))
    return pl.pallas_call(
        functools.partial(_ffn_ln_kernel, alpha=alpha, n_prompt=n_p),
        grid=(n_p + n_s,),
        in_specs=[prompt_spec, sample_spec,
                  _const_spec(wu.shape), _const_spec(wd.shape),
                  _const_spec(g.shape), _const_spec(b.shape)],
        out_specs=[prompt_spec, sample_spec],
        out_shape=[jax.ShapeDtypeStruct(xp.shape, F32), jax.ShapeDtypeStruct(xs.shape, F32)],
        compiler_params=pltpu.CompilerParams(dimension_semantics=("arbitrary",),
                                             vmem_limit_bytes=VMEM_LIMIT),
        name="ffn_ln",
    )(xp, xs, wu, wd, g, b)


def _proj_views(w_ref, b_ref):
    g0, g1 = MAIN_COLS, MAIN_COLS + 2 * D_MODEL
    return (w_ref.at[:, 0:g0], w_ref.at[:, g0:g1], w_ref.at[:, g1:g1 + 2 * LANES],
            b_ref.at[:, 0:g0], b_ref.at[:, g0:g1], b_ref.at[:, g1:g1 + 2 * LANES])


def _branch_gates_and_z(hb, w_main, b_main):
    z = jnp.dot(hb, w_main[:, 5 * D_MODEL:6 * D_MODEL], preferred_element_type=F32) \
        + b_main[:, 5 * D_MODEL:6 * D_MODEL]
    return _silu(z)


def _beta_and_logdecay(hb, w_bd, b_bd, alog, dtb):
    bd = jnp.dot(hb, w_bd[...], preferred_element_type=F32) + b_bd[...]
    beta = _sigmoid(bd[:, :LANES])
    g = -jnp.exp(alog[...]) * _softplus(bd[:, LANES:] + dtb[...])
    return beta, g


def _merge_out_ln(x, hb, a_part, yb, w_gates, b_gates, wb_ref, wo_ref, ln_g, ln_b, alpha):
    gate_b = _sigmoid(jnp.dot(hb, w_gates[:, D_MODEL:], preferred_element_type=F32)
                      + b_gates[:, D_MODEL:])
    merged = a_part + gate_b * jnp.dot(yb, wb_ref[...], preferred_element_type=F32)
    mix = jnp.dot(merged.astype(BF16), wo_ref[...], preferred_element_type=F32)
    return _layer_norm(alpha * x + mix, ln_g[...], ln_b[...])


def _mix_prompt_kernel(x_ref, w_ref, b_ref, vg_ref, vb_ref,
                       ws_ref, bst_ref, convw_ref, alog_ref, dtb_ref, normw_ref,
                       wa_ref, wb_ref, wo_ref, ln_g, ln_b,
                       x2_ref, conv_out_ref, ssm_out_ref,
                       s_ref, xc_ref, q_s, k_s, v_s, z_s, g_s, beta_s, yb_s, *, alpha, tt, nseq):
    t = pl.program_id(1)
    nt = pl.num_programs(1)
    w_main, w_gates, w_bd, b_main, b_gates, b_bd = _proj_views(w_ref, b_ref)

    @pl.when(t == 0)
    def _():
        s_ref[...] = jnp.zeros(s_ref.shape, F32)
        xc_ref[...] = jnp.zeros(xc_ref.shape, F32)

    n_rows = nseq * tt
    x = x_ref[...].reshape(n_rows, D_MODEL)
    hb = x.astype(BF16)

    queue = _WorkQueue()

    def proj(part):
        return _piped_dot(hb, w_main, part * D_MODEL, (part + 1) * D_MODEL, queue)

    scale = HEAD_DIM ** -0.5
    heads = [slice(h * HEAD_DIM, (h + 1) * HEAD_DIM) for h in range(HEADS)]

    class _ConvJobs:
        def __init__(self, raw, part, post):
            self.cols = slice(part * D_MODEL, (part + 1) * D_MODEL)
            b_p = b_main[:, (2 + part) * D_MODEL:(3 + part) * D_MODEL]
            self.jobs = [_conv_job(raw[sq * tt:(sq + 1) * tt], b_p, xc_ref[sq, :, self.cols],
                                   convw_ref[:, self.cols], lambda acc: post(_silu(acc)))
                         for sq in range(nseq)]
            self.thunks = [th for job, _ in self.jobs for th in job.thunks]

        def result(self):
            for sq, (_, state) in enumerate(self.jobs):
                last = state['cur']

                @pl.when(t == nt - 1)
                def _():
                    conv_out_ref[sq, :, self.cols] = last[ROW_TILE - (DN_CONV - 1):, :]

                xc_ref[sq, :, self.cols] = last
            return jnp.concatenate([job.result() for job, _ in self.jobs], axis=0)

    def l2n_heads(a, mul):
        return jnp.concatenate([_l2n(a[:, sl], mul) for sl in heads], axis=1)

    b_z = b_main[:, 5 * D_MODEL:6 * D_MODEL]
    beta, g_log = _beta_and_logdecay(hb, w_bd, b_bd, alog_ref, dtb_ref)
    beta_s[...] = beta
    rt = lax.broadcasted_iota(jnp.int32, (n_rows, n_rows), 0)
    ct = lax.broadcasted_iota(jnp.int32, (n_rows, n_rows), 1)
    cum = jnp.where(((rt // DN_CHUNK) == (ct // DN_CHUNK)) & (rt >= ct), 1.0, 0.0)
    g_s[...] = _dot_exact_lhs(cum, g_log)
    pq = proj(2)
    job_q = _ConvJobs(pq, 0, lambda a: l2n_heads(a, scale))
    queue.add(job_q, 45)
    pk = proj(3)
    job_k = _ConvJobs(pk, 1, lambda a: l2n_heads(a, 1.0))
    queue.add(job_k, 45)
    pvv = proj(4)
    job_vv = _ConvJobs(pvv, 2, lambda a: a)
    queue.add(job_vv, 35)
    pz = proj(5)
    job_z = _Slabs(lambda a: _silu(a + b_z), pz, slab=SLAB)
    queue.add(job_z, 20)
    queue.finish(job_z)
    q_s[...] = job_q.result()
    k_s[...] = job_k.result()
    v_s[...] = job_vv.result()
    z_s[...] = job_z.result()

    branch_a = {}

    def dot_steps(lhs, w, lo, hi, pieces=4):
        step = (hi - lo) // pieces
        outs = []
        for p in range(pieces):
            outs.append(jnp.dot(lhs, w[:, lo + p * step:lo + (p + 1) * step], preferred_element_type=F32))
            yield MXU_PIECE_COST
        return jnp.concatenate(outs, axis=1)

    def job_steps(job, cost):
        for th in job.thunks:
            th()
            yield cost
        return job.result()

    def branch_a_steps():
        b_u = b_main[:, 0:D_MODEL]
        b_v, vg, vb = b_main[:, D_MODEL:2 * D_MODEL], vg_ref[...], vb_ref[...]
        b_ga = b_gates[:, :D_MODEL]
        pu = yield from dot_steps(hb, w_main, 0, D_MODEL)
        pv = yield from dot_steps(hb, w_main, D_MODEL, 2 * D_MODEL)
        vn = yield from job_steps(
            _Slabs(lambda a: _layer_norm(_gelu_tanh(a + b_v), vg, vb).astype(BF16), pv, slab=SLAB), 70)
        r128 = lax.broadcasted_iota(jnp.int32, (GM_CHUNK, GM_CHUNK), 0)
        c128 = lax.broadcasted_iota(jnp.int32, (GM_CHUNK, GM_CHUNK), 1)
        tril = r128 >= c128
        w_tril = [jnp.where(tril, ws_ref[g], 0.0).astype(BF16) for g in range(GROUPS)]
        rows = []
        for c in range(n_rows // GM_CHUNK):
            cols = []
            for g in range(GROUPS):
                blk = vn[c * GM_CHUNK:(c + 1) * GM_CHUNK, g * GROUP_DIM:(g + 1) * GROUP_DIM]
                cols.append(jnp.dot(w_tril[g], blk, preferred_element_type=F32) + bst_ref[:, g:g + 1])
            rows.append(jnp.concatenate(cols, axis=1))
            yield MXU_PIECE_COST
        mixed = jnp.concatenate(rows, axis=0)
        u = yield from job_steps(_Slabs(lambda a: _gelu_tanh(a + b_u), pu, slab=SLAB), 50)
        pga = yield from dot_steps(hb, w_gates, 0, D_MODEL)
        ya = yield from job_steps(_Slabs(lambda a, m: (a * m).astype(BF16), u, mixed, slab=SLAB), 10)
        pa = yield from dot_steps(ya, wa_ref, 0, D_MODEL)
        branch_a['a_part'] = yield from job_steps(
            _Slabs(lambda g, p_: _sigmoid(g + b_ga) * p_, pga, pa, slab=SLAB), 25)
        branch_a['pgb'] = yield from dot_steps(hb, w_gates, D_MODEL, 2 * D_MODEL)

    steps = branch_a_steps()

    def tick(budget=TICK_BUDGET):
        while budget > 0:
            cost = next(steps, None)
            if cost is None:
                return
            budget -= cost

    masks = _block_masks(SUB, DN_CHUNK)
    norm_w = normw_ref[...]

    chains = [(sq, h) for sq in range(nseq) for h in range(HEADS)]
    sls = [slice(h * HEAD_DIM, (h + 1) * HEAD_DIM) for _, h in chains]
    n_chunks = SUB // DN_CHUNK
    row_chunk = lax.broadcasted_iota(jnp.int32, (SUB, HEAD_DIM), 0) // DN_CHUNK
    zeros = jnp.zeros((DN_CHUNK, HEAD_DIM), F32)
    ids = range(len(chains))
    state = [s_ref[sq, h] for sq, h in chains]
    for j in range(tt // SUB):
        rows = [slice(sq * tt + j * SUB, sq * tt + (j + 1) * SUB) for sq, _ in chains]
        g_sub = [g_s[sq * tt + j * SUB:sq * tt + (j + 1) * SUB, :] for sq in range(nseq)]
        g_t = [g.T for g in g_sub]
        b_sub = [beta_s[sq * tt + j * SUB:sq * tt + (j + 1) * SUB, :] for sq in range(nseq)]
        q = [q_s[rows[i], sls[i]] for i in ids]
        k = [k_s[rows[i], sls[i]] for i in ids]
        vv = [v_s[rows[i], sls[i]] for i in ids]
        g_col = [jnp.broadcast_to(g_sub[sq][:, h:h + 1], (SUB, HEAD_DIM)) for sq, h in chains]
        g_row = [jnp.broadcast_to(g_t[sq][h:h + 1, :], (SUB, SUB)) for sq, h in chains]
        beta_h = [jnp.broadcast_to(b_sub[sq][:, h:h + 1], (SUB, HEAD_DIM)) for sq, h in chains]
        u_h, w_h, qk, qe = _dn_intra(q, k, vv, beta_h, g_col, g_row, masks, 5, tick)
        g_last = [[g_col[i][(c + 1) * DN_CHUNK - 1:(c + 1) * DN_CHUNK, :] for c in range(n_chunks)]
                  for i in ids]
        k_dec_t = []
        for i in ids:
            g_end = g_last[i][n_chunks - 1]
            for c in range(n_chunks - 2, -1, -1):
                g_end = jnp.where(row_chunk == c, g_last[i][c], g_end)
            k_dec_t.append((k[i] * jnp.exp(g_end - g_col[i])).T)
        outs = [[] for _ in ids]
        for c in range(n_chunks):
            rs = slice(c * DN_CHUNK, (c + 1) * DN_CHUNK)
            r = [_dot(jnp.concatenate([w_h[i][rs], qe[i][rs]], axis=0), state[i]) for i in ids]
            tick()
            v_new = [u_h[i][rs] - r[i][:DN_CHUNK] for i in ids]
            v_pad = [jnp.concatenate([zeros] * c + [v_new[i]] + [zeros] * (n_chunks - 1 - c), axis=0)
                     for i in ids]
            m = [_dot(jnp.concatenate([qk[i][rs], k_dec_t[i]], axis=0), v_pad[i]) for i in ids]
            tick()
            for i in ids:
                outs[i].append(r[i][DN_CHUNK:] + m[i][:DN_CHUNK])
            state = [state[i] * jnp.exp(g_last[i][c]) + m[i][DN_CHUNK:] for i in ids]
        for i in ids:
            o = jnp.concatenate(outs[i], axis=0)
            yb_s[rows[i], sls[i]] = _gated_rms(o, norm_w, z_s[rows[i], sls[i]]).astype(BF16)
    for i, (sq, h) in enumerate(chains):
        s_ref[sq, h] = state[i]
    tick(float('inf'))
    a_part, pgb = branch_a['a_part'], branch_a['pgb']

    @pl.when(t == nt - 1)
    def _():
        ssm_out_ref[...] = s_ref[...]

    pb = jnp.dot(yb_s[...], wb_ref[...], preferred_element_type=F32)
    b_gb = b_gates[:, D_MODEL:]
    merged = _by_rows(lambda a, g, p_: (a + _sigmoid(g + b_gb) * p_).astype(BF16), a_part, pgb, pb, slab=SLAB)
    mix = jnp.dot(merged, wo_ref[...], preferred_element_type=F32)
    ln_gain, ln_bias = ln_g[...], ln_b[...]
    x2 = _by_rows(lambda xx, m: _layer_norm(alpha * xx + m, ln_gain, ln_bias), x, mix, slab=SLAB)
    x2_ref[...] = x2.reshape(nseq, tt, D_MODEL)


def _mix_prompt(x1, p, alpha, tt, nseq):
    b, t, _ = x1.shape
    assert t % tt == 0 and tt % SUB == 0 and b % nseq == 0
    rows = nseq * tt
    consts = [p['w_proj'], p['b_proj'],
              p['gm_v_g'], p['gm_v_b'], p['gm_w_s'], p['gm_b_s_t'], p['conv_w'], p['a_log'],
              p['dt_bias'], p['norm_w'], p['w_a'], p['w_b'], p['w_o'], p['ln2_g'], p['ln2_b']]
    return pl.pallas_call(
        functools.partial(_mix_prompt_kernel, alpha=alpha, tt=tt, nseq=nseq),
        grid=(b // nseq, t // tt),
        in_specs=[pl.BlockSpec((nseq, tt, D_MODEL), lambda i, j: (i, j, 0))]
                 + [_const_spec(c.shape) for c in consts],
        out_specs=[pl.BlockSpec((nseq, tt, D_MODEL), lambda i, j: (i, j, 0)),
                   pl.BlockSpec((nseq, DN_CONV - 1, QKV), lambda i, j: (i, 0, 0)),
                   pl.BlockSpec((nseq, HEADS, HEAD_DIM, HEAD_DIM), lambda i, j: (i, 0, 0, 0))],
        out_shape=[jax.ShapeDtypeStruct((b, t, D_MODEL), F32),
                   jax.ShapeDtypeStruct((b, DN_CONV - 1, QKV), F32),
                   jax.ShapeDtypeStruct((b, HEADS, HEAD_DIM, HEAD_DIM), F32)],
        scratch_shapes=[pltpu.VMEM((nseq, HEADS, HEAD_DIM, HEAD_DIM), F32),
                        pltpu.VMEM((nseq, ROW_TILE, QKV), F32),
                        pltpu.VMEM((rows, D_MODEL), F32),
                        pltpu.VMEM((rows, D_MODEL), F32),
                        pltpu.VMEM((rows, D_MODEL), F32),
                        pltpu.VMEM((rows, D_MODEL), F32),
                        pltpu.VMEM((rows, LANES), F32),
                        pltpu.VMEM((rows, LANES), F32),
                        pltpu.VMEM((rows, D_MODEL), BF16)],
        compiler_params=pltpu.CompilerParams(dimension_semantics=("arbitrary", "arbitrary"),
                                             vmem_limit_bytes=VMEM_LIMIT),
        name="mix_prompt",
    )(x1, *consts)


def _sample_stage_one(x_ref, cs_ref, w_ref, b_ref, vg_ref, vb_ref, coef_ref, bias_ref, convw_ref,
                      alog_ref, dtb_ref, wa_ref, vrow_ref, conv_out_ref,
                      u_s, w_s, qe_s, qk_s, kdt_s, gcol_s, apart_s, zact_s, *, nb, ts):
    rows = nb * ROW_TILE
    w_main, w_gates, w_bd, b_main, b_gates, b_bd = _proj_views(w_ref, b_ref)
    x = x_ref[...]
    hb = x.astype(BF16)
    valid = (lax.broadcasted_iota(jnp.int32, (rows, 1), 0) % ROW_TILE) < ts
    validf = jnp.where(valid, 1.0, 0.0).astype(F32)

    u = _gelu_tanh(jnp.dot(hb, w_main[:, 0:D_MODEL], preferred_element_type=F32)
                   + b_main[:, 0:D_MODEL])
    v = _gelu_tanh(jnp.dot(hb, w_main[:, D_MODEL:2 * D_MODEL], preferred_element_type=F32)
                   + b_main[:, D_MODEL:2 * D_MODEL])
    vn = _layer_norm(v, vg_ref[...], vb_ref[...])
    vn3 = vn.reshape(nb, ROW_TILE, D_MODEL)
    vrow_ref[...] = vn3[:, :ts, :]
    mixed = vn3 * coef_ref[0][None] + bias_ref[...][None]
    for j in range(1, DN_CONV):
        mixed = mixed + pltpu.roll(vn3, j, 1) * coef_ref[j][None]
    ya = (u * mixed.reshape(rows, D_MODEL)).astype(BF16)
    gate_a = _sigmoid(jnp.dot(hb, w_gates[:, :D_MODEL], preferred_element_type=F32)
                      + b_gates[:, :D_MODEL])
    apart_s[...] = gate_a * jnp.dot(ya, wa_ref[...], preferred_element_type=F32)

    qkv = jnp.dot(hb, w_main[:, 2 * D_MODEL:5 * D_MODEL], preferred_element_type=F32) \
        + b_main[:, 2 * D_MODEL:5 * D_MODEL]
    zfull = jnp.where(valid, qkv, 0.0) + cs_ref[...]
    z3 = zfull.reshape(nb, ROW_TILE, QKV)
    conv_out_ref[...] = z3[:, ts - (DN_CONV - 1):ts, :]
    acc = z3 * convw_ref[DN_CONV - 1:DN_CONV, :][None]
    for j in range(1, DN_CONV):
        acc = acc + pltpu.roll(z3, j, 1) * convw_ref[DN_CONV - 1 - j:DN_CONV - j, :][None]
    sact = _silu(acc.reshape(rows, QKV)) * validf
    zact_s[...] = _branch_gates_and_z(hb, w_main, b_main)
    beta, g_log = _beta_and_logdecay(hb, w_bd, b_bd, alog_ref, dtb_ref)
    beta = beta * validf
    g_log = g_log * validf
    rt = lax.broadcasted_iota(jnp.int32, (rows, rows), 0)
    ct = lax.broadcasted_iota(jnp.int32, (rows, rows), 1)
    cum = jnp.where(((rt // ROW_TILE) == (ct // ROW_TILE)) & (rt >= ct), 1.0, 0.0)
    g_cum = _dot_exact_lhs(cum, g_log)

    assert rows == SUB
    g_t = g_cum.T
    masks = _block_masks(SUB, ROW_TILE)
    scale = HEAD_DIM ** -0.5
    heads = range(HEADS)
    sls = [slice(h * HEAD_DIM, (h + 1) * HEAD_DIM) for h in heads]
    q = [_l2n(sact[:, sl]) * scale * validf for sl in sls]
    k = [_l2n(sact[:, D_MODEL + h * HEAD_DIM:D_MODEL + (h + 1) * HEAD_DIM]) * validf for h in heads]
    vv = [sact[:, 2 * D_MODEL + h * HEAD_DIM:2 * D_MODEL + (h + 1) * HEAD_DIM] for h in heads]
    g_col = [jnp.broadcast_to(g_cum[:, h:h + 1], (SUB, HEAD_DIM)) for h in heads]
    g_row = [jnp.broadcast_to(g_t[h:h + 1, :], (SUB, SUB)) for h in heads]
    beta_h = [jnp.broadcast_to(beta[:, h:h + 1], (SUB, HEAD_DIM)) for h in heads]
    u_h, w_h, qk, qe = _dn_intra(q, k, vv, beta_h, g_col, g_row, masks, 2)
    for h in heads:
        g_end = jnp.broadcast_to(g_col[h].reshape(nb, ROW_TILE, HEAD_DIM)[:, ROW_TILE - 1:, :],
                                 (nb, ROW_TILE, HEAD_DIM)).reshape(SUB, HEAD_DIM)
        kdt_s[h] = (k[h] * jnp.exp(g_end - g_col[h])).T
        u_s[h], w_s[h], qe_s[h], qk_s[h], gcol_s[h] = u_h[h], w_h[h], qe[h], qk[h], g_col[h]


def _sample_state_part(part, s_in_ref, normw_ref, s_out_ref, u_s, w_s, qe_s, qk_s, kdt_s, gcol_s,
                       zact_s, yb_s, *, nb):
    span = nb * ROW_TILE
    base = part * span
    heads = range(HEADS)
    tiles = [slice(base + i * ROW_TILE, base + (i + 1) * ROW_TILE) for i in range(nb)]
    zeros = jnp.zeros((SUB - span, HEAD_DIM), F32)
    seq_of_row = lax.broadcasted_iota(jnp.int32, (SUB, HEAD_DIM), 0) // ROW_TILE
    norm_w = normw_ref[...]
    r = [[_dot(jnp.concatenate([w_s[h, rs, :], qe_s[h, rs, :]], axis=0), s_in_ref[i, h])
          for i, rs in enumerate(tiles)] for h in heads]
    v_new = []
    for h in heads:
        mine = jnp.concatenate([u_s[h, rs, :] - r[h][i][:ROW_TILE] for i, rs in enumerate(tiles)], axis=0)
        v_new.append(jnp.concatenate([mine, zeros] if part == 0 else [zeros, mine], axis=0))
    qkv_new = [_dot(qk_s[h, base:base + span, :], v_new[h]) for h in heads]
    for i, rs in enumerate(tiles):
        for h in heads:
            g_last = gcol_s[h, rs.stop - 1:rs.stop, :]
            s_out_ref[i, h] = (s_in_ref[i, h] * jnp.exp(g_last)
                               + _dot(kdt_s[h], jnp.where(seq_of_row == part * nb + i, v_new[h], 0.0)))
    for h in heads:
        sl = slice(h * HEAD_DIM, (h + 1) * HEAD_DIM)
        o = jnp.concatenate([r[h][i][ROW_TILE:] for i in range(nb)], axis=0) + qkv_new[h]
        yb_s[base:base + span, sl] = _gated_rms(o, norm_w, zact_s[base:base + span, sl]).astype(BF16)


def _mix_sample_kernel(x_ref, cs_ref, s_in_ref, w_ref, b_ref,
                       vg_ref, vb_ref, coef_ref, bias_ref, convw_ref, alog_ref, dtb_ref, normw_ref,
                       wa_ref, wb_ref, wo_ref, ln_g, ln_b,
                       x2_ref, vrow_ref, conv_out_ref, s_out_ref,
                       u_s, w_s, qe_s, qk_s, kdt_s, gcol_s, apart_s, zact_s, yb_s, *, alpha, nb, ts):
    half = pl.program_id(1)
    per_head = (u_s, w_s, qe_s, qk_s, kdt_s, gcol_s)
    pl.when(half == 0)(functools.partial(
        _sample_stage_one, x_ref, cs_ref, w_ref, b_ref, vg_ref, vb_ref, coef_ref, bias_ref, convw_ref,
        alog_ref, dtb_ref, wa_ref, vrow_ref, conv_out_ref, *per_head, apart_s, zact_s, nb=2 * nb, ts=ts))
    for part in range(2):
        pl.when(half == part)(functools.partial(
            _sample_state_part, part, s_in_ref, normw_ref, s_out_ref, *per_head, zact_s, yb_s, nb=nb))

    @pl.when(half == 1)
    def _():
        _, w_gates, _, _, b_gates, _ = _proj_views(w_ref, b_ref)
        x = x_ref[...]
        x2 = _merge_out_ln(x, x.astype(BF16), apart_s[...], yb_s[...], w_gates, b_gates, wb_ref, wo_ref,
                           ln_g, ln_b, alpha)
        x2_ref[...] = x2.reshape(2 * nb, ROW_TILE, D_MODEL)[:, :ts, :]


def _mix_sample(x1, cs_pad, s_in, p, alpha, nb, ts):
    n = x1.shape[0]
    nseq = n // ROW_TILE
    assert nseq % (2 * nb) == 0 and 2 * nb * ROW_TILE == SUB
    rows = 2 * nb * ROW_TILE
    consts = [p['w_proj'], p['b_proj'],
              p['gm_v_g'], p['gm_v_b'], p['mix_coef'], p['mix_bias'], p['conv_w'], p['a_log'],
              p['dt_bias'], p['norm_w'], p['w_a'], p['w_b'], p['w_o'], p['ln2_g'], p['ln2_b']]
    state_spec = pl.BlockSpec((nb, HEADS, HEAD_DIM, HEAD_DIM), lambda i, half: (2 * i + half, 0, 0, 0))
    token_spec = pl.BlockSpec((2 * nb, ts, D_MODEL), lambda i, half: (i, 0, 0))
    per_head = pltpu.VMEM((HEADS, SUB, HEAD_DIM), F32)
    return pl.pallas_call(
        functools.partial(_mix_sample_kernel, alpha=alpha, nb=nb, ts=ts),
        grid=(nseq // (2 * nb), 2),
        in_specs=[pl.BlockSpec((rows, D_MODEL), lambda i, half: (i, 0)),
                  pl.BlockSpec((rows, QKV), lambda i, half: (i, 0)),
                  state_spec] + [_const_spec(c.shape) for c in consts],
        out_specs=[token_spec, token_spec,
                   pl.BlockSpec((2 * nb, DN_CONV - 1, QKV), lambda i, half: (i, 0, 0)),
                   state_spec],
        scratch_shapes=[per_head, per_head, per_head, per_head, per_head, per_head,
                        pltpu.VMEM((rows, D_MODEL), F32),
                        pltpu.VMEM((rows, D_MODEL), F32),
                        pltpu.VMEM((rows, D_MODEL), BF16)],
        out_shape=[jax.ShapeDtypeStruct((nseq, ts, D_MODEL), F32),
                   jax.ShapeDtypeStruct((nseq, ts, D_MODEL), F32),
                   jax.ShapeDtypeStruct((nseq, DN_CONV - 1, QKV), F32),
                   jax.ShapeDtypeStruct(s_in.shape, F32)],
        compiler_params=pltpu.CompilerParams(dimension_semantics=("arbitrary", "arbitrary"),
                                             vmem_limit_bytes=VMEM_LIMIT),
        name="mix_sample",
    )(x1, cs_pad, s_in, *consts)


def _pad_lanes(a, n=LANES):
    return jnp.pad(a, [(0, 0)] * (a.ndim - 1) + [(0, n - a.shape[-1])])


def _layer_params(l, ffn1_w_up, ffn1_w_down, ln1_g, ln1_b, w_in, b_in, gm_v_g, gm_v_b, gm_w_s,
                  gm_b_s, dn_conv_w, dn_a_log, dn_dt_bias, dn_norm_w, w_branch_a, w_branch_b,
                  w_out, ln2_g, ln2_b, ffn2_w_up, ffn2_w_down, ln3_g, ln3_b):
    row = lambda a: a[l][None, :].astype(F32)
    wi, bi = w_in[l], b_in[l]
    o_beta = MAIN_COLS
    o_dec = o_beta + HEADS
    o_gate = o_dec + HEADS
    ws = gm_w_s[l]
    lsm = DN_CONV
    shift = np.arange(lsm)[:, None]
    pos = np.arange(ROW_TILE)[None, :]
    live = (pos >= shift) & (pos < lsm)
    ws_head = ws[:, :lsm, :lsm]
    coef = jnp.where(live[:, :, None],
                     jnp.transpose(ws_head[:, np.clip(pos + 0 * shift, 0, lsm - 1),
                                           np.clip(pos - shift, 0, lsm - 1)], (1, 2, 0)), 0.0)
    bias = jnp.pad(gm_b_s[l][:, :lsm].T, ((0, ROW_TILE - lsm), (0, 0)))
    return {
        'ffn1': (ffn1_w_up[l].astype(BF16), ffn1_w_down[l].astype(BF16), row(ln1_g), row(ln1_b)),
        'ffn2': (ffn2_w_up[l].astype(BF16), ffn2_w_down[l].astype(BF16), row(ln3_g), row(ln3_b)),
        'w_proj': jnp.concatenate([wi[:, :MAIN_COLS], wi[:, o_gate:], _pad_lanes(wi[:, o_beta:o_dec]),
                                   _pad_lanes(wi[:, o_dec:o_gate])], axis=1).astype(BF16),
        'b_proj': jnp.concatenate([bi[:MAIN_COLS], bi[o_gate:], _pad_lanes(bi[o_beta:o_dec]),
                                   _pad_lanes(bi[o_dec:o_gate])])[None, :],
        'gm_v_g': row(gm_v_g), 'gm_v_b': row(gm_v_b),
        'gm_w_s': ws, 'gm_b_s_t': gm_b_s[l].T,
        'mix_coef': jnp.repeat(coef, GROUP_DIM, axis=-1), 'mix_bias': jnp.repeat(bias, GROUP_DIM, axis=-1),
        'conv_w': dn_conv_w[l],
        'a_log': _pad_lanes(dn_a_log[l][None, :].astype(F32)),
        'dt_bias': _pad_lanes(dn_dt_bias[l][None, :].astype(F32)),
        'norm_w': row(dn_norm_w),
        'w_a': w_branch_a[l].astype(BF16), 'w_b': w_branch_b[l].astype(BF16),
        'w_o': w_out[l].astype(BF16),
        'ln2_g': row(ln2_g), 'ln2_b': row(ln2_b),
    }


def kernel(x_prompt, x_sample, state_conv, state_ssm, ffn1_w_up, ffn1_w_down, ln1_g, ln1_b, w_in, b_in, gm_v_g, gm_v_b, gm_w_s, gm_b_s, dn_conv_w, dn_a_log, dn_dt_bias, dn_norm_w, w_branch_a, w_branch_b, w_out, ln2_g, ln2_b, ffn2_w_up, ffn2_w_down, ln3_g, ln3_b):
    depth = ffn1_w_up.shape[0]
    alpha = (2.0 * depth) ** 0.25
    bp, tp, _ = x_prompt.shape
    bs, ts, _ = x_sample.shape
    assert ts == DN_CONV and ts + (DN_CONV - 1) <= ROW_TILE
    y_p, y_s = x_prompt, x_sample
    conv_p, ssm_p, conv_s, ssm_s, v_s = [], [], [], [], []
    for l in range(depth):
        p = _layer_params(l, ffn1_w_up, ffn1_w_down, ln1_g, ln1_b, w_in, b_in, gm_v_g, gm_v_b,
                          gm_w_s, gm_b_s, dn_conv_w, dn_a_log, dn_dt_bias, dn_norm_w, w_branch_a,
                          w_branch_b, w_out, ln2_g, ln2_b, ffn2_w_up, ffn2_w_down, ln3_g, ln3_b)
        x1, x1s = _ffn_ln(y_p.reshape(bp * tp, D_MODEL), y_s.reshape(bs * ts, D_MODEL), *p['ffn1'],
                          alpha, FFN_ROWS)
        x2, c_p, s_p = _mix_prompt(x1.reshape(bp, tp, D_MODEL), p, alpha, PROMPT_ROWS, PROMPT_SEQS)
        x1s = jnp.pad(x1s.reshape(bs, ts, D_MODEL), ((0, 0), (0, ROW_TILE - ts), (0, 0)))
        cs_pad = jnp.pad(state_conv[l], ((0, 0), (ROW_TILE - (DN_CONV - 1), 0), (0, 0)))
        x2s, vrows, c_s, s_s = _mix_sample(x1s.reshape(bs * ROW_TILE, D_MODEL),
                                           cs_pad.reshape(bs * ROW_TILE, QKV),
                                           state_ssm[l], p, alpha, SAMPLE_SEQS, ts)
        y_p, y_s = _ffn_ln(x2.reshape(bp * tp, D_MODEL), x2s.reshape(bs * ts, D_MODEL), *p['ffn2'],
                           alpha, FFN_ROWS)
        y_p, y_s = y_p.reshape(bp, tp, D_MODEL), y_s.reshape(bs, ts, D_MODEL)
        conv_p.append(c_p)
        ssm_p.append(s_p)
        conv_s.append(c_s)
        ssm_s.append(s_s)
        v_s.append(vrows)
    return (y_p, y_s, jnp.stack(conv_p), jnp.stack(ssm_p), jnp.stack(conv_s), jnp.stack(ssm_s),
            jnp.stack(v_s))
```
